```python
import math
import jax
import jax.numpy as jnp
from jax import lax
import numpy as np

D_MODEL = 1024
BATCH = 8
SEQ = 2048
DEPTH = 2
DEC_BATCH = 128
DEC_SEQ = 1
PAST_LEN = 16384
PAGE_SIZE = 128

GLA_HEADS = 4
GLA_DK = 64
GLA_DV = 128
GLA_LOWRANK = 16
GLA_TAU = 16.0
GLA_CHUNK = 64
GLA_QK = GLA_HEADS * GLA_DK
GLA_V = GLA_HEADS * GLA_DV
ML_HEADS = 4
ML_D = 128
ML_CHUNK = 64
ML_W = ML_HEADS * ML_D
S5_GROUP = 16
S5_GROUPS = 32
S5_P = 64
S5_WIDTH = S5_GROUP * S5_GROUPS
MEM_LEN = 256
X_HEADS = 4
X_DH = D_MODEL // X_HEADS
D_FF = 2816
CONV_W = 3
EPS = 1e-6
NEG_BIG = -1e30
IN_SPLIT_SIZES = (GLA_QK, GLA_QK, GLA_V, GLA_V, GLA_LOWRANK,
                  ML_W, ML_W, ML_W, ML_W, ML_HEADS, ML_HEADS,
                  S5_WIDTH, D_MODEL, D_MODEL, D_MODEL)
W_IN_COLS = sum(IN_SPLIT_SIZES)

kernel_name = 'hybrid_gla_mlstm_s5_decoder_step'


def rmsnorm(x, g):
    xf = x.astype(jnp.float32)
    r = lax.rsqrt(jnp.mean(xf * xf, axis=-1, keepdims=True) + EPS)
    return (xf * r).astype(x.dtype) * g


def head_rmsnorm(x, g):
    B, T, H, d = x.shape
    r = lax.rsqrt(jnp.mean(x * x, axis=-1, keepdims=True) + EPS)
    return (x * r).reshape(B, T, H * d) * g


def split_in(z):
    idx = np.cumsum(IN_SPLIT_SIZES)[:-1].tolist()
    return jnp.split(z, idx, axis=-1)


def pad_time(x, L, value):
    pad = (-x.shape[1]) % L
    widths = [(0, 0), (0, pad)] + [(0, 0)] * (x.ndim - 2)
    return jnp.pad(x, widths, constant_values=value)


def to_chunks(x, L):
    B, Tp = x.shape[0], x.shape[1]
    x = x.reshape((B, Tp // L, L) + x.shape[2:])
    return jnp.moveaxis(x, (1, 3), (0, 2))


def from_chunks(o, T):
    o = jnp.moveaxis(o, (0, 2), (1, 3))
    B, nc, L = o.shape[:3]
    return o.reshape((B, nc * L) + o.shape[3:])[:, :T]


def gla_chunked(q, k, v, g, s0):
    T = q.shape[1]
    L = min(GLA_CHUNK, T)
    qc, kc, vc, gc = (to_chunks(pad_time(a, L, 0.0), L) for a in (q, k, v, g))
    mask = jnp.tril(jnp.ones((L, L), dtype=bool))

    def step(S, inp):
        qi, ki, vi, gi = inp
        b = jnp.cumsum(gi, axis=2)
        diff = b[:, :, :, None, :] - b[:, :, None, :, :]
        decay = jnp.exp(jnp.where(mask[:, :, None], diff, -jnp.inf))
        att = jnp.einsum('bhid,bhjd,bhijd->bhij', qi, ki, decay)
        o = (jnp.einsum('bhij,bhjv->bhiv', att, vi)
             + jnp.einsum('bhid,bhdv->bhiv', qi * jnp.exp(b), S))
        b_end = b[:, :, -1]
        k_dec = ki * jnp.exp(b_end[:, :, None, :] - b)
        S_new = jnp.exp(b_end)[..., None] * S + jnp.einsum('bhjd,bhjv->bhdv', k_dec, vi)
        return S_new, o

    S_fin, o = lax.scan(step, s0, (qc, kc, vc, gc))
    return from_chunks(o, T), S_fin


def mlstm_chunked(q, k, v, log_i, log_f, c0, n0, m0):
    T = q.shape[1]
    L = min(ML_CHUNK, T)
    qc, kc, vc = (to_chunks(pad_time(a, L, 0.0), L) for a in (q, k, v))
    ic = to_chunks(pad_time(log_i, L, NEG_BIG), L)
    fc = to_chunks(pad_time(log_f, L, 0.0), L)
    mask = jnp.tril(jnp.ones((L, L), dtype=bool))

    def step(carry, inp):
        C, n, m = carry
        qi, ki, vi, li, lf = inp
        b = jnp.cumsum(lf, axis=-1)
        w_inter = b + m[..., None]
        w_intra = jnp.where(mask, b[..., :, None] - b[..., None, :] + li[..., None, :], -jnp.inf)
        m_tok = jnp.maximum(w_inter, jnp.max(w_intra, axis=-1))
        a_inter = jnp.exp(w_inter - m_tok)
        s = jnp.einsum('bhid,bhjd->bhij', qi, ki) * jnp.exp(w_intra - m_tok[..., None])
        num = (jnp.einsum('bhij,bhjv->bhiv', s, vi)
               + a_inter[..., None] * jnp.einsum('bhid,bhdv->bhiv', qi, C))
        den = jnp.sum(s, axis=-1) + a_inter * jnp.einsum('bhid,bhd->bhi', qi, n)
        h = num / jnp.maximum(jnp.abs(den), jnp.exp(-m_tok))[..., None]
        b_end = b[..., -1]
        w_end_inter = b_end + m
        w_end = b_end[..., None] - b + li
        m_new = jnp.maximum(w_end_inter, jnp.max(w_end, axis=-1))
        e_inter = jnp.exp(w_end_inter - m_new)
        e_j = jnp.exp(w_end - m_new[..., None])
        C_new = e_inter[..., None, None] * C + jnp.einsum('bhj,bhjd,bhjv->bhdv', e_j, ki, vi)
        n_new = e_inter[..., None] * n + jnp.einsum('bhj,bhjd->bhd', e_j, ki)
        return (C_new, n_new, m_new), h

    (C, n, m), h = lax.scan(step, (c0, n0, m0), (qc, kc, vc, ic, fc))
    return from_chunks(h, T), C, n, m


def s5_ssm(u, h0_re, h0_im, lam_re, lam_im, log_dt, b_re, b_im, c_re, c_im, d, w_glu):
    B, T, _ = u.shape
    ug = u.reshape(B, T, S5_GROUPS, S5_GROUP)
    dt = jnp.exp(log_dt)[:, None]
    mag = jnp.exp(lam_re * dt)
    lb_re = mag * jnp.cos(lam_im * dt)
    lb_im = mag * jnp.sin(lam_im * dt)
    nr = lb_re - 1.0
    den = lam_re * lam_re + lam_im * lam_im
    f_re = (nr * lam_re + lb_im * lam_im) / den
    f_im = (lb_im * lam_re - nr * lam_im) / den
    bb_re = f_re[..., None] * b_re - f_im[..., None] * b_im
    bb_im = f_re[..., None] * b_im + f_im[..., None] * b_re
    bu_re = jnp.einsum('btgc,gpc->btgp', ug, bb_re)
    bu_im = jnp.einsum('btgc,gpc->btgp', ug, bb_im)
    bu_re = bu_re.at[:, 0].add(lb_re * h0_re - lb_im * h0_im)
    bu_im = bu_im.at[:, 0].add(lb_re * h0_im + lb_im * h0_re)
    a_re = jnp.broadcast_to(lb_re, bu_re.shape)
    a_im = jnp.broadcast_to(lb_im, bu_im.shape)

    def combine(e1, e2):
        a1r, a1i, b1r, b1i = e1
        a2r, a2i, b2r, b2i = e2
        return (a1r * a2r - a1i * a2i, a1r * a2i + a1i * a2r,
                a2r * b1r - a2i * b1i + b2r, a2r * b1i + a2i * b1r + b2i)

    _, _, hr, hi = lax.associative_scan(combine, (a_re, a_im, bu_re, bu_im), axis=1)
    y = (jnp.einsum('btgp,gcp->btgc', hr, c_re) - jnp.einsum('btgp,gcp->btgc', hi, c_im)
         + d.reshape(S5_GROUPS, S5_GROUP) * ug).reshape(B, T, S5_WIDTH)
    y = jax.nn.gelu(y)
    y = y * jax.nn.sigmoid(y @ w_glu)
    return y, hr[:, -1], hi[:, -1]


def mixer_block(hn, s_gla, s_c, s_n, s_m, s_re, s_im, P, l):
    B, T, _ = hn.shape
    f32 = jnp.float32
    (gq, gk, gv, gr, ga, mq, mk, mv, mo, mi, mf, su, za, zb, zc) = split_in(hn @ P['w_in'][l])
    q = gq.astype(f32).reshape(B, T, GLA_HEADS, GLA_DK) * GLA_DK ** -0.5
    k = gk.astype(f32).reshape(B, T, GLA_HEADS, GLA_DK)
    v = gv.astype(f32).reshape(B, T, GLA_HEADS, GLA_DV)
    a_pre = (ga @ P['w_gla_alpha'][l] + P['b_gla_alpha'][l]).astype(f32)
    g = (jax.nn.log_sigmoid(a_pre) / GLA_TAU).reshape(B, T, GLA_HEADS, GLA_DK)
    o_a, s_gla_new = gla_chunked(q, k, v, g, s_gla.astype(f32))
    y_a = head_rmsnorm(o_a, P['gla_head_norm'][l]) * jax.nn.silu(gr.astype(f32))
    q = mq.astype(f32).reshape(B, T, ML_HEADS, ML_D) * ML_D ** -0.5
    k = mk.astype(f32).reshape(B, T, ML_HEADS, ML_D)
    v = mv.astype(f32).reshape(B, T, ML_HEADS, ML_D)
    log_i = (mi + P['b_mlstm_i'][l]).astype(f32)
    log_f = jax.nn.log_sigmoid((mf + P['b_mlstm_f'][l]).astype(f32))
    h_b, c_new, n_new, m_new = mlstm_chunked(q, k, v, log_i, log_f, s_c.astype(f32),
                                             s_n.astype(f32), s_m.astype(f32))
    y_b = head_rmsnorm(h_b, P['mlstm_head_norm'][l]) * jax.nn.sigmoid(mo.astype(f32))
    y_c, re_new, im_new = s5_ssm(su.astype(f32), s_re.astype(f32), s_im.astype(f32),
                                 P['s5_lam_re'][l], P['s5_lam_im'][l], P['s5_log_dt'][l],
                                 P['s5_b_re'][l], P['s5_b_im'][l], P['s5_c_re'][l],
                                 P['s5_c_im'][l], P['s5_d'][l], P['s5_w_glu'][l])
    dt = hn.dtype
    merged = (jax.nn.sigmoid(za) * (y_a.astype(dt) @ P['w_branch_a'][l])
              + jax.nn.sigmoid(zb) * (y_b.astype(dt) @ P['w_branch_b'][l])
              + jax.nn.sigmoid(zc) * (y_c.astype(dt) @ P['w_branch_c'][l]))
    return merged @ P['w_out'][l], (s_gla_new, c_new, n_new, m_new, re_new, im_new)


def memory_kv(mem, g, w_k, w_v):
    B, M, _ = mem.shape
    mn = rmsnorm(mem, g)
    return ((mn @ w_k).reshape(B, M, X_HEADS, X_DH), (mn @ w_v).reshape(B, M, X_HEADS, X_DH))


def cross_attn(hn, mk, mv, w_q, w_o):
    B, T, _ = hn.shape
    q = (hn @ w_q).reshape(B, T, X_HEADS, X_DH)
    s = jnp.einsum('bthd,bmhd->bhtm', q, mk).astype(jnp.float32) * X_DH ** -0.5
    p = jax.nn.softmax(s, axis=-1).astype(mv.dtype)
    o = jnp.einsum('bhtm,bmhd->bthd', p, mv).reshape(B, T, D_MODEL)
    return o @ w_o


def conv_ffn(hn, buf, w_up, conv_w, conv_b, w_down):
    T = hn.shape[1]
    up = hn @ w_up
    full = jnp.concatenate([buf.astype(up.dtype), up], axis=1)
    conv = sum(full[:, j:j + T] * conv_w[j] for j in range(CONV_W)) + conv_b
    a, gt = jnp.split(conv, 2, axis=-1)
    return (a * jax.nn.silu(gt)) @ w_down, full[:, T:]


def run_trunk(x, mem_k, mem_v, states, P):
    new_states = []
    for l in range(DEPTH):
        s_gla, s_c, s_n, s_m, s_re, s_im, s_conv = states[l]
        y, mix_state = mixer_block(rmsnorm(x, P['norm_mix'][l]), s_gla, s_c, s_n, s_m, s_re, s_im, P, l)
        x = x + y
        x = x + cross_attn(rmsnorm(x, P['norm_cross'][l]), mem_k[l], mem_v[l], P['w_cq'][l], P['w_co'][l])
        f, conv_state = conv_ffn(rmsnorm(x, P['norm_ffn'][l]), s_conv, P['w_ffn_up'][l],
                                 P['ffn_conv_w'][l], P['ffn_conv_b'][l], P['w_ffn_down'][l])
        x = x + f
        new_states.append(mix_state + (conv_state,))
    y = rmsnorm(x, P['norm_final'])
    stacked = [jnp.stack([new_states[l][i] for l in range(DEPTH)]) for i in range(7)]
    return y, stacked


def setup_inputs(seed: int = 0) -> dict:
    key = jax.random.key(seed)
    keys = list(jax.random.split(key, 64))
    f32 = jnp.float32

    def nrm(shape, scale):
        return jax.random.normal(keys.pop(), shape, f32) * scale

    def gain(shape):
        return 1.0 + 0.02 * jax.random.normal(keys.pop(), shape, f32)

    modes = jnp.arange(S5_P, dtype=f32)
    return {
        'x_prompt': nrm((BATCH, SEQ, D_MODEL), 1.0),
        'x_sample': nrm((DEC_BATCH, DEC_SEQ, D_MODEL), 1.0),
        'mem_prompt': nrm((BATCH, MEM_LEN, D_MODEL), 1.0),
        'cache_mem_k': nrm((DEPTH, DEC_BATCH, MEM_LEN, X_HEADS, X_DH), 1.0),
        'cache_mem_v': nrm((DEPTH, DEC_BATCH, MEM_LEN, X_HEADS, X_DH), 1.0),
        'state_gla': nrm((DEPTH, DEC_BATCH, GLA_HEADS, GLA_DK, GLA_DV), 0.5),
        'state_mlstm_c': nrm((DEPTH, DEC_BATCH, ML_HEADS, ML_D, ML_D), 0.5),
        'state_mlstm_n': nrm((DEPTH, DEC_BATCH, ML_HEADS, ML_D), 0.5),
        'state_mlstm_m': nrm((DEPTH, DEC_BATCH, ML_HEADS), 1.0),
        'state_s5_re': nrm((DEPTH, DEC_BATCH, S5_GROUPS, S5_P), 0.1),
        'state_s5_im': nrm((DEPTH, DEC_BATCH, S5_GROUPS, S5_P), 0.1),
        'state_ffn_conv': nrm((DEPTH, DEC_BATCH, CONV_W - 1, 2 * D_FF), 1.0),
        'norm_mix': gain((DEPTH, D_MODEL)),
        'w_in': nrm((DEPTH, D_MODEL, W_IN_COLS), D_MODEL ** -0.5),
        'w_gla_alpha': nrm((DEPTH, GLA_LOWRANK, GLA_QK), GLA_LOWRANK ** -0.5),
        'b_gla_alpha': nrm((DEPTH, GLA_QK), 0.1),
        'gla_head_norm': gain((DEPTH, GLA_V)),
        'b_mlstm_i': nrm((DEPTH, ML_HEADS), 0.1),
        'b_mlstm_f': jnp.linspace(3.0, 6.0, ML_HEADS, dtype=f32)[None, :] + nrm((DEPTH, ML_HEADS), 0.1),
        'mlstm_head_norm': gain((DEPTH, ML_W)),
        's5_lam_re': -0.5 + nrm((DEPTH, S5_GROUPS, S5_P), 0.01),
        's5_lam_im': math.pi * modes + nrm((DEPTH, S5_GROUPS, S5_P), 0.01),
        's5_log_dt': jax.random.uniform(keys.pop(), (DEPTH, S5_GROUPS), f32, math.log(1e-3), math.log(1e-1)),
        's5_b_re': nrm((DEPTH, S5_GROUPS, S5_P, S5_GROUP), (2 * S5_GROUP) ** -0.5),
        's5_b_im': nrm((DEPTH, S5_GROUPS, S5_P, S5_GROUP), (2 * S5_GROUP) ** -0.5),
        's5_c_re': nrm((DEPTH, S5_GROUPS, S5_GROUP, S5_P), S5_P ** -0.5),
        's5_c_im': nrm((DEPTH, S5_GROUPS, S5_GROUP, S5_P), S5_P ** -0.5),
        's5_d': nrm((DEPTH, S5_WIDTH), 1.0),
        's5_w_glu': nrm((DEPTH, S5_WIDTH, S5_WIDTH), S5_WIDTH ** -0.5),
        'w_branch_a': nrm((DEPTH, GLA_V, D_MODEL), GLA_V ** -0.5),
        'w_branch_b': nrm((DEPTH, ML_W, D_MODEL), ML_W ** -0.5),
        'w_branch_c': nrm((DEPTH, S5_WIDTH, D_MODEL), S5_WIDTH ** -0.5),
        'w_out': nrm((DEPTH, D_MODEL, D_MODEL), D_MODEL ** -0.5),
        'norm_cross': gain((DEPTH, D_MODEL)),
        'norm_mem': gain((DEPTH, D_MODEL)),
        'w_cq': nrm((DEPTH, D_MODEL, D_MODEL), D_MODEL ** -0.5),
        'w_ck': nrm((DEPTH, D_MODEL, D_MODEL), D_MODEL ** -0.5),
        'w_cv': nrm((DEPTH, D_MODEL, D_MODEL), D_MODEL ** -0.5),
        'w_co': nrm((DEPTH, D_MODEL, D_MODEL), D_MODEL ** -0.5),
        'norm_ffn': gain((DEPTH, D_MODEL)),
        'w_ffn_up': nrm((DEPTH, D_MODEL, 2 * D_FF), D_MODEL ** -0.5),
        'ffn_conv_w': nrm((DEPTH, CONV_W, 2 * D_FF), CONV_W ** -0.5),
        'ffn_conv_b': nrm((DEPTH, 2 * D_FF), 0.02),
        'w_ffn_down': nrm((DEPTH, D_FF, D_MODEL), D_FF ** -0.5),
        'norm_final': gain((D_MODEL,)),
    }


def reference(x_prompt, x_sample, mem_prompt, cache_mem_k, cache_mem_v, state_gla, state_mlstm_c,
              state_mlstm_n, state_mlstm_m, state_s5_re, state_s5_im, state_ffn_conv,
              norm_mix, w_in, w_gla_alpha, b_gla_alpha, gla_head_norm, b_mlstm_i, b_mlstm_f,
              mlstm_head_norm, s5_lam_re, s5_lam_im, s5_log_dt, s5_b_re, s5_b_im, s5_c_re, s5_c_im,
              s5_d, s5_w_glu, w_branch_a, w_branch_b, w_branch_c, w_out, norm_cross, norm_mem,
              w_cq, w_ck, w_cv, w_co, norm_ffn, w_ffn_up, ffn_conv_w, ffn_conv_b, w_ffn_down,
              norm_final):
    P = dict(norm_mix=norm_mix, w_in=w_in, w_gla_alpha=w_gla_alpha, b_gla_alpha=b_gla_alpha,
             gla_head_norm=gla_head_norm, b_mlstm_i=b_mlstm_i, b_mlstm_f=b_mlstm_f,
             mlstm_head_norm=mlstm_head_norm, s5_lam_re=s5_lam_re, s5_lam_im=s5_lam_im,
             s5_log_dt=s5_log_dt, s5_b_re=s5_b_re, s5_b_im=s5_b_im, s5_c_re=s5_c_re,
             s5_c_im=s5_c_im, s5_d=s5_d, s5_w_glu=s5_w_glu, w_branch_a=w_branch_a,
             w_branch_b=w_branch_b, w_branch_c=w_branch_c, w_out=w_out, norm_cross=norm_cross,
             w_cq=w_cq, w_co=w_co, norm_ffn=norm_ffn, w_ffn_up=w_ffn_up, ffn_conv_w=ffn_conv_w,
             ffn_conv_b=ffn_conv_b, w_ffn_down=w_ffn_down, norm_final=norm_final)
    f32 = jnp.float32
    Bp = x_prompt.shape[0]
    zero_state = (jnp.zeros((Bp, GLA_HEADS, GLA_DK, GLA_DV), f32),
                  jnp.zeros((Bp, ML_HEADS, ML_D, ML_D), f32),
                  jnp.zeros((Bp, ML_HEADS, ML_D), f32),
                  jnp.zeros((Bp, ML_HEADS), f32),
                  jnp.zeros((Bp, S5_GROUPS, S5_P), f32),
                  jnp.zeros((Bp, S5_GROUPS, S5_P), f32),
                  jnp.zeros((Bp, CONV_W - 1, 2 * D_FF), x_prompt.dtype))
    mem_kv = [memory_kv(mem_prompt, norm_mem[l], w_ck[l], w_cv[l]) for l in range(DEPTH)]
    p_mem_k = jnp.stack([kv[0] for kv in mem_kv])
    p_mem_v = jnp.stack([kv[1] for kv in mem_kv])
    y_prompt, (p_gla, p_c, p_n, p_m, p_re, p_im, p_conv) = run_trunk(
        x_prompt, p_mem_k, p_mem_v, [zero_state] * DEPTH, P)
    sample_states = [(state_gla[l], state_mlstm_c[l], state_mlstm_n[l], state_mlstm_m[l],
                      state_s5_re[l], state_s5_im[l], state_ffn_conv[l]) for l in range(DEPTH)]
    y_sample, (s_gla, s_c, s_n, s_m, s_re, s_im, s_conv) = run_trunk(
        x_sample, cache_mem_k, cache_mem_v, sample_states, P)
    return (y_prompt, y_sample, p_gla, p_c, p_n, p_m, p_re, p_im, p_conv, p_mem_k, p_mem_v,
            s_gla, s_c, s_n, s_m, s_re, s_im, s_conv)
```

```python
import functools

import jax
import jax.numpy as jnp
from jax import lax
from jax.experimental import pallas as pl
from jax.experimental.pallas import tpu as pltpu

F32 = jnp.float32
BF16 = jnp.bfloat16

D_MODEL = 1024
GLA_HEADS, GLA_DK, GLA_DV = 4, 64, 128
GLA_QK = GLA_HEADS * GLA_DK
GLA_V = GLA_HEADS * GLA_DV
GLA_LOWRANK = 16
GLA_TAU = 16.0
ML_HEADS, ML_D = 4, 128
ML_W = ML_HEADS * ML_D
S5_GROUP, S5_GROUPS, S5_P = 16, 32, 64
S5_WIDTH = S5_GROUP * S5_GROUPS
S5_MODES = S5_GROUPS * S5_P
S5_NB = 4
S5_GB = S5_GROUPS // S5_NB
S5_MB = S5_MODES // S5_NB
S5_UB = S5_WIDTH // S5_NB
MEM_LEN = 256
X_HEADS = 4
X_DH = D_MODEL // X_HEADS
D_FF = 2816
CONV_W = 3
EPS = 1e-6

CHUNK = 64
LANES = 128
FF_CHUNK = 256
N_FF_CHUNKS = D_FF // FF_CHUNK

Z_MAIN = 7168
Z_QK, Z_GV, Z_GR, Z_MQ, Z_MK, Z_MV, Z_MO, Z_SU = range(8)
Z_ZA, Z_ZB, Z_ZC = 4, 5, 6
SM_MI = GLA_LOWRANK
SM_MF = GLA_LOWRANK + ML_HEADS

GLA_SAFE_DECAY = 80.0

VMEM_LIMIT = 56 * 1024 * 1024


def _cparams(sem):
    return pltpu.CompilerParams(dimension_semantics=sem, vmem_limit_bytes=VMEM_LIMIT)


def _dot(a, b):
    return jnp.dot(a, b, preferred_element_type=F32)


def _dot_nt(a, b):
    return lax.dot_general(a, b, (((1,), (1,)), ((), ())), preferred_element_type=F32)


def _dot_tn(a, b):
    return lax.dot_general(a, b, (((0,), (0,)), ((), ())), preferred_element_type=F32)


def _sigmoid(x):
    return 1.0 / (1.0 + jnp.exp(-x))


def _log_sigmoid(x):
    return jnp.minimum(x, 0.0) - jnp.log(1.0 + jnp.exp(-jnp.abs(x)))


def _gelu_tanh(x):
    return 0.5 * x * (1.0 + jnp.tanh(0.7978845608028654 * (x + 0.044715 * x * x * x)))


def _rms_rows(x, g):
    r = lax.rsqrt(jnp.mean(x * x, axis=-1, keepdims=True) + EPS)
    return (x * r) * g


def _dot_exact01(a01, x):
    hi = x.astype(BF16)
    r1 = x - hi.astype(F32)
    mid = r1.astype(BF16)
    lo = (r1 - mid.astype(F32)).astype(BF16)
    return _dot(a01, hi) + _dot(a01, mid) + _dot(a01, lo)


def _row_to_col(row, n):
    eye = (lax.broadcasted_iota(jnp.int32, (n, n), 0) == lax.broadcasted_iota(jnp.int32, (n, n), 1))
    return jnp.sum(jnp.where(eye, jnp.broadcast_to(row, (n, n)), 0.0), axis=1, keepdims=True)


def _norm_mm_kernel(x_ref, g_ref, w_ref, o_ref, hn_ref):
    @pl.when(pl.program_id(1) == 0)
    def _():
        hn_ref[...] = _rms_rows(x_ref[...], g_ref[...]).astype(BF16)
    o_ref[...] = _dot(hn_ref[...], w_ref[...])


def _norm_mm_small_kernel(x_ref, g_ref, w_ref, ws_ref, o_ref, os_ref, hn_ref):
    @pl.when(pl.program_id(1) == 0)
    def _():
        hn = _rms_rows(x_ref[...], g_ref[...]).astype(BF16)
        hn_ref[...] = hn
        os_ref[...] = _dot(hn, ws_ref[...])
    o_ref[...] = _dot(hn_ref[...], w_ref[...])


def _norm_matmul(x, g, w, w_small=None, *, bm, bn):
    n, d = x.shape
    c = w.shape[1]
    grid = (n // bm, c // bn)
    x_spec = pl.BlockSpec((bm, d), lambda i, j: (i, 0))
    g_spec = pl.BlockSpec((1, d), lambda i, j: (0, 0))
    w_spec = pl.BlockSpec((d, bn), lambda i, j: (0, j))
    o_spec = pl.BlockSpec((bm, bn), lambda i, j: (i, j))
    scratch = [pltpu.VMEM((bm, d), BF16)]
    if w_small is None:
        return pl.pallas_call(
            _norm_mm_kernel, grid=grid, in_specs=[x_spec, g_spec, w_spec], out_specs=o_spec,
            out_shape=jax.ShapeDtypeStruct((n, c), F32), scratch_shapes=scratch,
            compiler_params=_cparams(("arbitrary", "arbitrary")), name="norm_matmul",
        )(x, g, w)
    cs = w_small.shape[1]
    return pl.pallas_call(
        _norm_mm_small_kernel, grid=grid,
        in_specs=[x_spec, g_spec, w_spec, pl.BlockSpec((d, cs), lambda i, j: (0, 0))],
        out_specs=[o_spec, pl.BlockSpec((bm, cs), lambda i, j: (i, 0))],
        out_shape=[jax.ShapeDtypeStruct((n, c), F32), jax.ShapeDtypeStruct((n, cs), F32)],
        scratch_shapes=scratch,
        compiler_params=_cparams(("arbitrary", "arbitrary")), name="norm_matmul_in",
    )(x, g, w, w_small)


def _head_norm(oh, gn_h):
    r = lax.rsqrt(jnp.mean(oh * oh, axis=-1, keepdims=True) + EPS)
    return oh * r * gn_h


def _gla_log_decay(small, wa_ref, ba_ref):
    a_pre = _dot(small.astype(BF16), wa_ref[...]) + ba_ref[...]
    return _log_sigmoid(a_pre) * (1.0 / GLA_TAU)


def _gla_prompt_kernel(qk_ref, v_ref, gr_ref, sm_ref, wa_ref, ba_ref, gn_ref,
                       y_ref, st_ref, state_ref, b_ref, *, n_chunks):
    t = pl.program_id(1)
    L = CHUNK

    @pl.when(t == 0)
    def _():
        state_ref[...] = jnp.zeros_like(state_ref)

    ri = lax.broadcasted_iota(jnp.int32, (L, L), 0)
    ci = lax.broadcasted_iota(jnp.int32, (L, L), 1)
    tril = jnp.where(ri >= ci, 1.0, 0.0).astype(BF16)
    rw = lax.broadcasted_iota(jnp.int32, (L, GLA_HEADS * L), 0)
    cw = lax.broadcasted_iota(jnp.int32, (L, GLA_HEADS * L), 1)
    causal_wide = (cw % L) <= rw
    kr = lax.broadcasted_iota(jnp.int32, (GLA_HEADS * L, GLA_QK), 0)
    kc = lax.broadcasted_iota(jnp.int32, (GLA_HEADS * L, GLA_QK), 1)
    kk_mask = (kr // L) == (kc // GLA_DK)
    vr = lax.broadcasted_iota(jnp.int32, (GLA_HEADS * L, GLA_V), 0)
    vc = lax.broadcasted_iota(jnp.int32, (GLA_HEADS * L, GLA_V), 1)
    vv_mask = (vr // L) == (vc // GLA_DV)
    sr = lax.broadcasted_iota(jnp.int32, (GLA_V, GLA_QK), 0)
    sc = lax.broadcasted_iota(jnp.int32, (GLA_V, GLA_QK), 1)
    st_mask = (sr // GLA_DV) == (sc // GLA_DK)
    hr = lax.broadcasted_iota(jnp.int32, (GLA_QK, LANES), 0)
    hc = lax.broadcasted_iota(jnp.int32, (GLA_QK, LANES), 1)
    head_ones = jnp.where((hr // GLA_DK) == hc, 1.0, 0.0).astype(BF16)
    pr = lax.broadcasted_iota(jnp.int32, (LANES, GLA_HEADS * L), 0)
    pc = lax.broadcasted_iota(jnp.int32, (LANES, GLA_HEADS * L), 1)

    def chunk(c, carry):
        r0 = pl.multiple_of(c * L, L)
        rows = pl.ds(r0, L)
        qk = qk_ref[rows, :]
        q = qk[:, :GLA_QK] * (GLA_DK ** -0.5)
        k = qk[:, GLA_QK:]
        v = v_ref[rows, :]
        g = _gla_log_decay(sm_ref[rows, :], wa_ref, ba_ref)
        b = _dot_exact01(tril, g)
        b_end = b[L - 1:L, :]
        qe = q * jnp.exp(b)
        k_dec = k * jnp.exp(b_end - b)
        qe16 = qe.astype(BF16)

        def fast_att(_):
            k_til = k * jnp.exp(-b)
            kk = jnp.where(kk_mask, jnp.concatenate([k_til] * GLA_HEADS, axis=0), 0.0)
            return _dot_nt(qe16, kk.astype(BF16))

        def direct_att(_):
            b_ref[...] = b

            def col(j, acc):
                kj = qk_ref[pl.ds(r0 + j, 1), GLA_QK:]
                bj = b_ref[pl.ds(j, 1), :]
                prod = q * kj * jnp.exp(jnp.minimum(b - bj, 0.0))
                red = _dot(prod.astype(BF16), head_ones)
                place = jnp.where((pc == pr * L + j) & (pr < GLA_HEADS), 1.0, 0.0).astype(BF16)
                return acc + _dot(red.astype(BF16), place)

            return lax.fori_loop(0, L, col, jnp.zeros((L, GLA_HEADS * L), F32))

        safe = jnp.max(-b_end) <= GLA_SAFE_DECAY
        att = lax.cond(safe, fast_att, direct_att, 0)
        att = jnp.where(causal_wide, att, 0.0)

        vv = jnp.where(vv_mask, jnp.concatenate([v] * GLA_HEADS, axis=0), 0.0)
        st = state_ref[...]
        o = _dot(att.astype(BF16), vv.astype(BF16)) + _dot_nt(qe16, st.astype(BF16))
        upd = _dot_tn(v.astype(BF16), k_dec.astype(BF16))
        state_ref[...] = jnp.where(st_mask, st * jnp.exp(b_end) + upd, 0.0)

        gr = gr_ref[rows, :]
        for h in range(GLA_HEADS):
            sl = slice(h * GLA_DV, (h + 1) * GLA_DV)
            gate = gr[:, sl]
            yh = _head_norm(o[:, sl], gn_ref[:, sl]) * (gate * _sigmoid(gate))
            y_ref[rows, sl] = yh.astype(BF16)
        return carry

    lax.fori_loop(0, n_chunks, chunk, 0)

    @pl.when(t == pl.num_programs(1) - 1)
    def _():
        st_ref[0] = state_ref[...]


def _gla_prompt(z, zs, wa, ba, gn, *, batch, seq, tt):
    nt = seq // tt
    row = lambda b, t: b * nt + t
    blk = lambda j: pl.BlockSpec((tt, 512), lambda b, t: (row(b, t), j))
    const = lambda shape: pl.BlockSpec(shape, lambda b, t: (0,) * len(shape))
    return pl.pallas_call(
        functools.partial(_gla_prompt_kernel, n_chunks=tt // CHUNK),
        grid=(batch, nt),
        in_specs=[blk(Z_QK), blk(Z_GV), blk(Z_GR),
                  pl.BlockSpec((tt, LANES), lambda b, t: (row(b, t), 0)),
                  const((LANES, GLA_QK)), const((1, GLA_QK)), const((1, GLA_V))],
        out_specs=[pl.BlockSpec((tt, GLA_V), lambda b, t: (row(b, t), 0)),
                   pl.BlockSpec((1, GLA_V, GLA_QK), lambda b, t: (b, 0, 0))],
        out_shape=[jax.ShapeDtypeStruct((batch * seq, GLA_V), BF16),
                   jax.ShapeDtypeStruct((batch, GLA_V, GLA_QK), F32)],
        scratch_shapes=[pltpu.VMEM((GLA_V, GLA_QK), F32), pltpu.VMEM((CHUNK, GLA_QK), F32)],
        compiler_params=_cparams(("arbitrary", "arbitrary")), name="gla_prompt",
    )(z, z, z, zs, wa, ba, gn)


def _mlstm_prompt_kernel(q_ref, k_ref, v_ref, og_ref, sm_ref, gb_ref, gn_ref,
                         y_ref, c_out, n_out, m_out, c_ref, n_ref, m_ref, *, n_chunks):
    t = pl.program_id(1)
    L = CHUNK

    @pl.when(t == 0)
    def _():
        c_ref[...] = jnp.zeros_like(c_ref)
        n_ref[...] = jnp.zeros_like(n_ref)
        m_ref[...] = jnp.zeros_like(m_ref)

    ri = lax.broadcasted_iota(jnp.int32, (L, L), 0)
    ci = lax.broadcasted_iota(jnp.int32, (L, L), 1)
    lower = ri >= ci
    eye = ri == ci

    def chunk(c, carry):
        r0 = pl.multiple_of(c * L, L)
        rows = pl.ds(r0, L)
        sm = sm_ref[rows, :] + gb_ref[...]
        lf_all = _log_sigmoid(sm)
        for h in range(ML_HEADS):
            sl = slice(h * ML_D, (h + 1) * ML_D)
            li_col = sm[:, SM_MI + h:SM_MI + h + 1]
            lf_col = lf_all[:, SM_MF + h:SM_MF + h + 1]
            lf_b = jnp.broadcast_to(lf_col, (L, L))
            li_row = jnp.sum(jnp.where(eye, jnp.broadcast_to(li_col, (L, L)), 0.0), axis=0, keepdims=True)
            lf_row = jnp.sum(jnp.where(eye, lf_b, 0.0), axis=0, keepdims=True)
            b_row = jnp.sum(jnp.where(ri <= ci, lf_b, 0.0), axis=0, keepdims=True)
            b_col = jnp.sum(jnp.where(lower, jnp.broadcast_to(lf_row, (L, L)), 0.0), axis=1, keepdims=True)
            m_prev = m_ref[h:h + 1, 0:1]
            w = jnp.where(lower, b_col - b_row + li_row, -jnp.inf)
            m_tok = jnp.maximum(b_col + m_prev, jnp.max(w, axis=1, keepdims=True))
            a_inter = jnp.exp(b_col + m_prev - m_tok)
            qh = q_ref[rows, sl] * (ML_D ** -0.5)
            kh = k_ref[rows, sl]
            vh = v_ref[rows, sl]
            qh16, kh16, vh16 = qh.astype(BF16), kh.astype(BF16), vh.astype(BF16)
            s = _dot_nt(qh16, kh16) * jnp.exp(w - m_tok)
            c_h = c_ref[sl, :]
            n_h = n_ref[h:h + 1, :]
            num = _dot(s.astype(BF16), vh16) + a_inter * _dot(qh16, c_h.astype(BF16))
            den = jnp.sum(s, axis=1, keepdims=True) + a_inter * jnp.sum(qh * n_h, axis=1, keepdims=True)
            hh = num / jnp.maximum(jnp.abs(den), jnp.exp(-m_tok))
            og = og_ref[rows, sl]
            y_ref[rows, sl] = (_head_norm(hh, gn_ref[:, sl]) * _sigmoid(og)).astype(BF16)
            b_end = b_col[L - 1:L, :]
            w_end = b_end - b_col + li_col
            m_new = jnp.maximum(b_end + m_prev, jnp.max(w_end, axis=0, keepdims=True))
            e_inter = jnp.exp(b_end + m_prev - m_new)
            kd = kh * jnp.exp(w_end - m_new)
            c_ref[sl, :] = e_inter * c_h + _dot_tn(kd.astype(BF16), vh16)
            n_ref[h:h + 1, :] = e_inter * n_h + jnp.sum(kd, axis=0, keepdims=True)
            m_ref[h:h + 1, :] = jnp.broadcast_to(m_new, (1, LANES))
        return carry

    lax.fori_loop(0, n_chunks, chunk, 0)

    @pl.when(t == pl.num_programs(1) - 1)
    def _():
        c_out[0] = c_ref[...]
        n_out[0] = n_ref[...]
        m_out[0] = m_ref[...]


def _mlstm_prompt(z, zs, gbias, gn, *, batch, seq, tt):
    nt = seq // tt
    row = lambda b, t: b * nt + t
    blk = lambda j: pl.BlockSpec((tt, 512), lambda b, t: (row(b, t), j))
    const = lambda shape: pl.BlockSpec(shape, lambda b, t: (0,) * len(shape))
    return pl.pallas_call(
        functools.partial(_mlstm_prompt_kernel, n_chunks=tt // CHUNK),
        grid=(batch, nt),
        in_specs=[blk(Z_MQ), blk(Z_MK), blk(Z_MV), blk(Z_MO),
                  pl.BlockSpec((tt, LANES), lambda b, t: (row(b, t), 0)),
                  const((1, LANES)), const((1, ML_W))],
        out_specs=[pl.BlockSpec((tt, ML_W), lambda b, t: (row(b, t), 0)),
                   pl.BlockSpec((1, ML_W, ML_D), lambda b, t: (b, 0, 0)),
                   pl.BlockSpec((1, 8, LANES), lambda b, t: (b, 0, 0)),
                   pl.BlockSpec((1, 8, LANES), lambda b, t: (b, 0, 0))],
        out_shape=[jax.ShapeDtypeStruct((batch * seq, ML_W), BF16),
                   jax.ShapeDtypeStruct((batch, ML_W, ML_D), F32),
                   jax.ShapeDtypeStruct((batch, 8, LANES), F32),
                   jax.ShapeDtypeStruct((batch, 8, LANES), F32)],
        scratch_shapes=[pltpu.VMEM((ML_W, ML_D), F32), pltpu.VMEM((8, LANES), F32),
                        pltpu.VMEM((8, LANES), F32)],
        compiler_params=_cparams(("arbitrary", "arbitrary")), name="mlstm_prompt",
    )(z, z, z, z, zs, gbias, gn)


def _s5_prep_kernel(lre_ref, lim_ref, ldt_ref, ctr_ref, cti_ref, par_ref, cf_ref):
    lam_re = lre_ref[0]
    lam_im = lim_ref[0]
    dt = jnp.exp(ldt_ref[0])
    mag = jnp.exp(lam_re * dt)
    lb_re = mag * jnp.cos(lam_im * dt)
    lb_im = mag * jnp.sin(lam_im * dt)
    nr = lb_re - 1.0
    den = lam_re * lam_re + lam_im * lam_im
    f_re = (nr * lam_re + lb_im * lam_im) / den
    f_im = (lb_im * lam_re - nr * lam_im) / den
    par_ref[0] = jnp.concatenate([lb_re, lb_im, f_re, f_im, jnp.zeros((4, S5_MB), F32)], axis=0)
    ct_re = ctr_ref[0]
    ct_im = cti_ref[0]
    cf_ref[0, :, :S5_MB] = (ct_re * f_re - ct_im * f_im).astype(BF16)
    cf_ref[0, :, S5_MB:] = (-(ct_re * f_im + ct_im * f_re)).astype(BF16)


def _s5_prep(lam_re, lam_im, log_dt, ct_re, ct_im):
    vec = pl.BlockSpec((1, 1, S5_MB), lambda n: (n, 0, 0))
    mat = pl.BlockSpec((1, S5_UB, S5_MB), lambda n: (n, 0, 0))
    return pl.pallas_call(
        _s5_prep_kernel, grid=(S5_NB,),
        in_specs=[vec, vec, vec, mat, mat],
        out_specs=[pl.BlockSpec((1, 8, S5_MB), lambda n: (n, 0, 0)),
                   pl.BlockSpec((1, S5_UB, 2 * S5_MB), lambda n: (n, 0, 0))],
        out_shape=[jax.ShapeDtypeStruct((S5_NB, 8, S5_MB), F32),
                   jax.ShapeDtypeStruct((S5_NB, S5_UB, 2 * S5_MB), BF16)],
        compiler_params=_cparams(("arbitrary",)), name="s5_prep",
    )(lam_re, lam_im, log_dt, ct_re, ct_im)


def _s5_prompt_kernel(u_ref, w_ref, cf_ref, par_ref, y_ref, h_out, x_ref, h_ref, *, batch, tt):
    t_blk = pl.program_id(1)

    @pl.when(t_blk == 0)
    def _():
        h_ref[...] = jnp.zeros_like(h_ref)

    nlb = S5_MB // LANES
    u = u_ref[...].reshape(batch * tt, S5_UB).astype(BF16)
    x = _dot(u, w_ref[0])
    for j in range(2 * nlb):
        x_ref[j] = x[:, j * LANES:(j + 1) * LANES]

    par = par_ref[0]
    lr = [jnp.broadcast_to(par[0:1, j * LANES:(j + 1) * LANES], (batch, LANES)) for j in range(nlb)]
    li = [jnp.broadcast_to(par[1:2, j * LANES:(j + 1) * LANES], (batch, LANES)) for j in range(nlb)]

    def step(t, carry):
        rows = pl.ds(t, batch, stride=tt)
        new = []
        for j in range(nlb):
            hr, hi = carry[j], carry[nlb + j]
            nr = lr[j] * hr - li[j] * hi + x_ref[j, rows, :]
            ni = lr[j] * hi + li[j] * hr + x_ref[nlb + j, rows, :]
            x_ref[j, rows, :] = nr
            x_ref[nlb + j, rows, :] = ni
            new.append((nr, ni))
        return tuple(n[0] for n in new) + tuple(n[1] for n in new)

    h0 = tuple(h_ref[:, j * LANES:(j + 1) * LANES] for j in range(2 * nlb))
    hfin = lax.fori_loop(0, tt, step, h0, unroll=8)
    for j in range(2 * nlb):
        h_ref[:, j * LANES:(j + 1) * LANES] = hfin[j]
    hr = jnp.concatenate(hfin[:nlb], axis=1)
    hi = jnp.concatenate(hfin[nlb:], axis=1)

    hall = jnp.concatenate([x_ref[j].astype(BF16) for j in range(2 * nlb)], axis=1)
    y = _dot_nt(hall, cf_ref[0])
    y_ref[...] = y.reshape(batch, tt, S5_UB)

    @pl.when(t_blk == pl.num_programs(1) - 1)
    def _():
        fr = par[2:3, :]
        fi = par[3:4, :]
        h_out[0, :, :S5_MB] = fr * hr - fi * hi
        h_out[0, :, S5_MB:] = fr * hi + fi * hr


def _s5_prompt(z3, w_blk, cfold, par, *, batch, seq, tt):
    nt = seq // tt
    u_col0 = Z_SU * 512 // S5_UB
    return pl.pallas_call(
        functools.partial(_s5_prompt_kernel, batch=batch, tt=tt),
        grid=(S5_NB, nt),
        in_specs=[pl.BlockSpec((batch, tt, S5_UB), lambda n, t: (0, t, u_col0 + n)),
                  pl.BlockSpec((1, S5_UB, 2 * S5_MB), lambda n, t: (n, 0, 0)),
                  pl.BlockSpec((1, S5_UB, 2 * S5_MB), lambda n, t: (n, 0, 0)),
                  pl.BlockSpec((1, 8, S5_MB), lambda n, t: (n, 0, 0))],
        out_specs=[pl.BlockSpec((batch, tt, S5_UB), lambda n, t: (0, t, n)),
                   pl.BlockSpec((1, batch, 2 * S5_MB), lambda n, t: (n, 0, 0))],
        out_shape=[jax.ShapeDtypeStruct((batch, seq, S5_WIDTH), F32),
                   jax.ShapeDtypeStruct((S5_NB, batch, 2 * S5_MB), F32)],
        scratch_shapes=[pltpu.VMEM((2 * S5_MB // LANES, batch * tt, LANES), F32),
                        pltpu.VMEM((batch, 2 * S5_MB), F32)],
        compiler_params=_cparams(("arbitrary", "arbitrary")), name="s5_prompt",
    )(z3, w_blk, cfold, par)


def _s5_decode_kernel(u_ref, w_ref, ctr_ref, cti_ref, par_ref, hr_ref, hi_ref, y_ref, hro_ref, hio_ref):
    x = _dot(u_ref[...].astype(BF16), w_ref[0])
    xr, xi = x[:, :S5_MB], x[:, S5_MB:]
    par = par_ref[0]
    lr, li, fr, fi = par[0:1, :], par[1:2, :], par[2:3, :], par[3:4, :]
    h0r, h0i = hr_ref[...], hi_ref[...]
    hr = lr * h0r - li * h0i + (fr * xr - fi * xi)
    hi = lr * h0i + li * h0r + (fr * xi + fi * xr)
    hro_ref[...] = hr
    hio_ref[...] = hi
    y_ref[...] = _dot_nt(hr.astype(BF16), ctr_ref[0]) - _dot_nt(hi.astype(BF16), cti_ref[0])


def _s5_decode(z, w_blk, ct_re, ct_im, par, h_re, h_im):
    rows = z.shape[0]
    u_col0 = Z_SU * 512 // S5_UB
    hspec = pl.BlockSpec((rows, S5_MB), lambda n: (0, n))
    mat = pl.BlockSpec((1, S5_UB, S5_MB), lambda n: (n, 0, 0))
    return pl.pallas_call(
        _s5_decode_kernel, grid=(S5_NB,),
        in_specs=[pl.BlockSpec((rows, S5_UB), lambda n: (0, u_col0 + n)),
                  pl.BlockSpec((1, S5_UB, 2 * S5_MB), lambda n: (n, 0, 0)),
                  mat, mat,
                  pl.BlockSpec((1, 8, S5_MB), lambda n: (n, 0, 0)), hspec, hspec],
        out_specs=[pl.BlockSpec((rows, S5_UB), lambda n: (0, n)), hspec, hspec],
        out_shape=[jax.ShapeDtypeStruct((rows, S5_WIDTH), F32),
                   jax.ShapeDtypeStruct((rows, S5_MODES), F32),
                   jax.ShapeDtypeStruct((rows, S5_MODES), F32)],
        compiler_params=_cparams(("arbitrary",)), name="s5_decode",
    )(z, w_blk, ct_re, ct_im, par, h_re, h_im)


def _gla_decode_kernel(qk_ref, v_ref, gr_ref, sm_ref, wa_ref, ba_ref, gn_ref, s_ref,
                       y_ref, so_ref, eg_ref, o_ref, *, bb):
    g = _gla_log_decay(sm_ref[...], wa_ref, ba_ref)
    eg_ref[...] = jnp.exp(g)

    def per_row(b, carry):
        row = pl.ds(b, 1)
        eg_r = eg_ref[row, :]
        qk_r = qk_ref[row, :]
        v_r = v_ref[row, :]
        q_r = qk_r[:, :GLA_QK] * (GLA_DK ** -0.5)
        k_r = qk_r[:, GLA_QK:]
        o_heads = []
        for h in range(GLA_HEADS):
            ks = slice(h * GLA_DK, (h + 1) * GLA_DK)
            vs = slice(h * GLA_DV, (h + 1) * GLA_DV)
            s_new = _row_to_col(eg_r[:, ks], GLA_DK) * s_ref[b, h] + _row_to_col(k_r[:, ks], GLA_DK) * v_r[:, vs]
            so_ref[b, h] = s_new
            o_heads.append(jnp.sum(_row_to_col(q_r[:, ks], GLA_DK) * s_new, axis=0, keepdims=True))
        o_ref[row, :] = jnp.concatenate(o_heads, axis=1)
        return carry

    lax.fori_loop(0, bb, per_row, 0)
    o = o_ref[...]
    gr = gr_ref[...]
    for h in range(GLA_HEADS):
        vs = slice(h * GLA_DV, (h + 1) * GLA_DV)
        gate = gr[:, vs]
        y_ref[:, vs] = (_head_norm(o[:, vs], gn_ref[:, vs]) * (gate * _sigmoid(gate))).astype(BF16)


def _gla_decode(z, zs, wa, ba, gn, state, *, bb):
    rows = z.shape[0]
    blk = lambda j: pl.BlockSpec((bb, 512), lambda i: (i, j))
    const = lambda shape: pl.BlockSpec(shape, lambda i: (0,) * len(shape))
    sspec = pl.BlockSpec((bb, GLA_HEADS, GLA_DK, GLA_DV), lambda i: (i, 0, 0, 0))
    return pl.pallas_call(
        functools.partial(_gla_decode_kernel, bb=bb), grid=(rows // bb,),
        in_specs=[blk(Z_QK), blk(Z_GV), blk(Z_GR), pl.BlockSpec((bb, LANES), lambda i: (i, 0)),
                  const((LANES, GLA_QK)), const((1, GLA_QK)), const((1, GLA_V)), sspec],
        out_specs=[pl.BlockSpec((bb, GLA_V), lambda i: (i, 0)), sspec],
        out_shape=[jax.ShapeDtypeStruct((rows, GLA_V), BF16),
                   jax.ShapeDtypeStruct(state.shape, F32)],
        scratch_shapes=[pltpu.VMEM((bb, GLA_QK), F32), pltpu.VMEM((bb, GLA_V), F32)],
        compiler_params=_cparams(("arbitrary",)), name="gla_decode",
    )(z, z, z, zs, wa, ba, gn, state)


def _mlstm_decode_kernel(q_ref, k_ref, v_ref, og_ref, sm_ref, gb_ref, gn_ref, c_ref, n_ref, m_ref,
                         y_ref, co_ref, no_ref, mo_ref, li_ref, lf_ref, h_ref, *, bb):
    sm = sm_ref[...] + gb_ref[...]
    li_ref[...] = sm
    lf_ref[...] = _log_sigmoid(sm)
    lane = lax.broadcasted_iota(jnp.int32, (1, LANES), 1)

    def per_row(b, carry):
        row = pl.ds(b, 1)
        li_r = li_ref[row, :]
        lf_r = lf_ref[row, :]
        m_r = m_ref[row, :]
        q_all = q_ref[row, :] * (ML_D ** -0.5)
        k_all = k_ref[row, :]
        v_all = v_ref[row, :]
        n_all = n_ref[b]
        m_new_row = jnp.zeros((1, LANES), F32)
        h_heads, n_heads = [], []
        for h in range(ML_HEADS):
            sl = slice(h * ML_D, (h + 1) * ML_D)
            li = li_r[:, SM_MI + h:SM_MI + h + 1]
            lf = lf_r[:, SM_MF + h:SM_MF + h + 1]
            m_prev = m_r[:, h:h + 1]
            m_new = jnp.maximum(lf + m_prev, li)
            a = jnp.exp(lf + m_prev - m_new)
            e = jnp.exp(li - m_new)
            k_r = k_all[:, sl]
            q_r = q_all[:, sl]
            c_new = a * c_ref[b, h] + (_row_to_col(k_r, ML_D) * e) * v_all[:, sl]
            n_new = a * n_all[h:h + 1, :] + e * k_r
            co_ref[b, h] = c_new
            n_heads.append(n_new)
            num = jnp.sum(_row_to_col(q_r, ML_D) * c_new, axis=0, keepdims=True)
            den = jnp.sum(q_r * n_new, axis=1, keepdims=True)
            h_heads.append(num / jnp.maximum(jnp.abs(den), jnp.exp(-m_new)))
            m_new_row = jnp.where(lane == h, m_new, m_new_row)
        no_ref[b] = jnp.concatenate(n_heads, axis=0)
        h_ref[row, :] = jnp.concatenate(h_heads, axis=1)
        mo_ref[row, :] = m_new_row
        return carry

    lax.fori_loop(0, bb, per_row, 0)
    hh = h_ref[...]
    og = og_ref[...]
    for h in range(ML_HEADS):
        sl = slice(h * ML_D, (h + 1) * ML_D)
        y_ref[:, sl] = (_head_norm(hh[:, sl], gn_ref[:, sl]) * _sigmoid(og[:, sl])).astype(BF16)


def _mlstm_decode(z, zs, gbias, gn, c, n, m_pad, *, bb):
    rows = z.shape[0]
    blk = lambda j: pl.BlockSpec((bb, 512), lambda i: (i, j))
    const = lambda shape: pl.BlockSpec(shape, lambda i: (0,) * len(shape))
    cspec = pl.BlockSpec((bb, ML_HEADS, ML_D, ML_D), lambda i: (i, 0, 0, 0))
    nspec = pl.BlockSpec((bb, ML_HEADS, ML_D), lambda i: (i, 0, 0))
    mspec = pl.BlockSpec((bb, LANES), lambda i: (i, 0))
    return pl.pallas_call(
        functools.partial(_mlstm_decode_kernel, bb=bb), grid=(rows // bb,),
        in_specs=[blk(Z_MQ), blk(Z_MK), blk(Z_MV), blk(Z_MO), mspec,
                  const((1, LANES)), const((1, ML_W)), cspec, nspec, mspec],
        out_specs=[pl.BlockSpec((bb, ML_W), lambda i: (i, 0)), cspec, nspec, mspec],
        out_shape=[jax.ShapeDtypeStruct((rows, ML_W), BF16),
                   jax.ShapeDtypeStruct(c.shape, F32), jax.ShapeDtypeStruct(n.shape, F32),
                   jax.ShapeDtypeStruct((rows, LANES), F32)],
        scratch_shapes=[pltpu.VMEM((bb, LANES), F32), pltpu.VMEM((bb, LANES), F32),
                        pltpu.VMEM((bb, ML_W), F32)],
        compiler_params=_cparams(("arbitrary",)), name="mlstm_decode",
    )(z, z, z, z, zs, gbias, gn, c, n, m_pad)


def _merge_kernel(ya_ref, yb_ref, yp_ref, u_ref, za_ref, zb_ref, zc_ref, x_ref,
                  d_ref, wg_ref, wa_ref, wb_ref, wc_ref, wo_ref, o_ref):
    yc = _gelu_tanh(yp_ref[...] + d_ref[...] * u_ref[...])
    yc = yc * _sigmoid(_dot(yc.astype(BF16), wg_ref[...]))
    m = _sigmoid(za_ref[...]) * _dot(ya_ref[...], wa_ref[...])
    m = m + _sigmoid(zb_ref[...]) * _dot(yb_ref[...], wb_ref[...])
    m = m + _sigmoid(zc_ref[...]) * _dot(yc.astype(BF16), wc_ref[...])
    o_ref[...] = x_ref[...] + _dot(m.astype(BF16), wo_ref[...])


def _merge(ya, yb, ypre, z, x, d, wg, wa, wb, wc, wo, *, bm):
    n = x.shape[0]
    r512 = pl.BlockSpec((bm, 512), lambda i: (i, 0))
    zblk = lambda j: pl.BlockSpec((bm, D_MODEL), lambda i: (i, j))
    const = lambda shape: pl.BlockSpec(shape, lambda i: (0, 0))
    return pl.pallas_call(
        _merge_kernel, grid=(n // bm,),
        in_specs=[r512, r512, r512, pl.BlockSpec((bm, 512), lambda i: (i, Z_SU)),
                  zblk(Z_ZA), zblk(Z_ZB), zblk(Z_ZC), pl.BlockSpec((bm, D_MODEL), lambda i: (i, 0)),
                  const((1, S5_WIDTH)), const((S5_WIDTH, S5_WIDTH)),
                  const((GLA_V, D_MODEL)), const((ML_W, D_MODEL)), const((S5_WIDTH, D_MODEL)),
                  const((D_MODEL, D_MODEL))],
        out_specs=pl.BlockSpec((bm, D_MODEL), lambda i: (i, 0)),
        out_shape=jax.ShapeDtypeStruct((n, D_MODEL), F32),
        compiler_params=_cparams(("arbitrary",)), name="merge",
    )(ya, yb, ypre, z, z, z, z, x, d, wg, wa, wb, wc, wo)


def _cross_prompt_kernel(x_ref, g_ref, wq_ref, mk_ref, mv_ref, wo_ref, o_ref):
    x = x_ref[...]
    q = _dot(_rms_rows(x, g_ref[...]).astype(BF16), wq_ref[...])
    heads = []
    for h in range(X_HEADS):
        sl = slice(h * X_DH, (h + 1) * X_DH)
        s = _dot_nt(q[:, sl].astype(BF16), mk_ref[:, sl].astype(BF16)) * (X_DH ** -0.5)
        s = s - jnp.max(s, axis=-1, keepdims=True)
        p = jnp.exp(s)
        p = p / jnp.sum(p, axis=-1, keepdims=True)
        heads.append(_dot(p.astype(BF16), mv_ref[:, sl].astype(BF16)))
    o = jnp.concatenate(heads, axis=-1)
    o_ref[...] = x + _dot(o.astype(BF16), wo_ref[...])


def _cross_prompt(x, g, wq, mem_k, mem_v, wo, *, batch, seq, tq):
    nt = seq // tq
    xspec = pl.BlockSpec((tq, D_MODEL), lambda b, t: (b * nt + t, 0))
    const = lambda shape: pl.BlockSpec(shape, lambda b, t: (0, 0))
    mspec = pl.BlockSpec((MEM_LEN, D_MODEL), lambda b, t: (b, 0))
    return pl.pallas_call(
        _cross_prompt_kernel, grid=(batch, nt),
        in_specs=[xspec, const((1, D_MODEL)), const((D_MODEL, D_MODEL)), mspec, mspec,
                  const((D_MODEL, D_MODEL))],
        out_specs=xspec, out_shape=jax.ShapeDtypeStruct(x.shape, F32),
        compiler_params=_cparams(("arbitrary", "arbitrary")), name="cross_prompt",
    )(x, g, wq, mem_k, mem_v, wo)


def _cross_decode_kernel(x_ref, g_ref, wq_ref, mk_ref, mv_ref, wo_ref, o_ref, att_ref, *, bb):
    x = x_ref[...]
    q = _dot(_rms_rows(x, g_ref[...]).astype(BF16), wq_ref[...])
    for b in range(bb):
        prod = mk_ref[b] * q[b:b + 1, :]
        mv = mv_ref[b]
        for h in range(X_HEADS):
            sl = slice(h * X_DH, (h + 1) * X_DH)
            s = jnp.sum(prod[:, sl], axis=1, keepdims=True) * (X_DH ** -0.5)
            p = jnp.exp(s - jnp.max(s, axis=0, keepdims=True))
            p = p / jnp.sum(p, axis=0, keepdims=True)
            att_ref[b:b + 1, sl] = jnp.sum(p * mv[:, sl], axis=0, keepdims=True)
    o_ref[...] = x + _dot(att_ref[...].astype(BF16), wo_ref[...])


def _cross_decode(x, g, wq, mem_k, mem_v, wo, *, bb):
    rows = x.shape[0]
    xspec = pl.BlockSpec((bb, D_MODEL), lambda i: (i, 0))
    const = lambda shape: pl.BlockSpec(shape, lambda i: (0, 0))
    mspec = pl.BlockSpec((bb, MEM_LEN, D_MODEL), lambda i: (i, 0, 0))
    return pl.pallas_call(
        functools.partial(_cross_decode_kernel, bb=bb), grid=(rows // bb,),
        in_specs=[xspec, const((1, D_MODEL)), const((D_MODEL, D_MODEL)), mspec, mspec,
                  const((D_MODEL, D_MODEL))],
        out_specs=xspec, out_shape=jax.ShapeDtypeStruct(x.shape, F32),
        scratch_shapes=[pltpu.VMEM((bb, D_MODEL), F32)],
        compiler_params=_cparams(("arbitrary",)), name="cross_decode",
    )(x, g, wq, mem_k, mem_v, wo)


def _ffn_prompt_kernel(x_ref, g_ref, wup_ref, cw_ref, cb_ref, wdn_ref, gf_ref,
                       o_ref, st_ref, hn_ref, halo_ref, *, tt, final_norm):
    t = pl.program_id(1)

    @pl.when(t == 0)
    def _():
        halo_ref[...] = jnp.zeros_like(halo_ref)

    x = x_ref[...]
    hn_ref[...] = _rms_rows(x, g_ref[...]).astype(BF16)
    rid = lax.broadcasted_iota(jnp.int32, (tt, FF_CHUNK), 0)

    def conv(col0):
        cols = slice(col0, col0 + FF_CHUNK)
        up = _dot(hn_ref[...], wup_ref[:, cols])
        p2 = halo_ref[0:1, cols]
        p1 = halo_ref[1:2, cols]
        sh1 = jnp.where(rid == 0, p1, pltpu.roll(up, 1, axis=0))
        sh2 = jnp.where(rid == 0, p2, jnp.where(rid == 1, p1, pltpu.roll(up, 2, axis=0)))
        halo_ref[0:2, cols] = up[tt - 2:tt, :]
        return sh2 * cw_ref[0:1, cols] + sh1 * cw_ref[1:2, cols] + up * cw_ref[2:3, cols] + cb_ref[:, cols]

    acc = x
    for c in range(N_FF_CHUNKS):
        a = conv(c * FF_CHUNK)
        gt = conv(D_FF + c * FF_CHUNK)
        act = a * (gt * _sigmoid(gt))
        acc = acc + _dot(act.astype(BF16), wdn_ref[c * FF_CHUNK:(c + 1) * FF_CHUNK, :])
    if final_norm:
        acc = _rms_rows(acc, gf_ref[...])
    o_ref[...] = acc

    @pl.when(t == pl.num_programs(1) - 1)
    def _():
        st_ref[0] = halo_ref[0:2, :]


def _ffn_prompt(x, g, wup, cw, cb, wdn, gf, *, batch, seq, tt, final_norm):
    nt = seq // tt
    xspec = pl.BlockSpec((tt, D_MODEL), lambda b, t: (b * nt + t, 0))
    const = lambda shape: pl.BlockSpec(shape, lambda b, t: (0, 0), pipeline_mode=pl.Buffered(1))
    return pl.pallas_call(
        functools.partial(_ffn_prompt_kernel, tt=tt, final_norm=final_norm), grid=(batch, nt),
        in_specs=[xspec, const((1, D_MODEL)), const((D_MODEL, 2 * D_FF)), const((CONV_W, 2 * D_FF)),
                  const((1, 2 * D_FF)), const((D_FF, D_MODEL)), const((1, D_MODEL))],
        out_specs=[xspec, pl.BlockSpec((1, CONV_W - 1, 2 * D_FF), lambda b, t: (b, 0, 0))],
        out_shape=[jax.ShapeDtypeStruct(x.shape, F32),
                   jax.ShapeDtypeStruct((batch, CONV_W - 1, 2 * D_FF), F32)],
        scratch_shapes=[pltpu.VMEM((tt, D_MODEL), BF16), pltpu.VMEM((8, 2 * D_FF), F32)],
        compiler_params=_cparams(("arbitrary", "arbitrary")), name="ffn_prompt",
    )(x, g, wup, cw, cb, wdn, gf)


def _ffn_decode_kernel(x_ref, g_ref, wa_ref, wg_ref, cwa_ref, cwg_ref, cba_ref, cbg_ref,
                       b0a_ref, b0g_ref, b1a_ref, b1g_ref, wdn_ref, gf_ref,
                       o_ref, upa_ref, upg_ref, hn_ref, acc_ref, *, final_norm):
    c = pl.program_id(0)

    @pl.when(c == 0)
    def _():
        x = x_ref[...]
        hn_ref[...] = _rms_rows(x, g_ref[...]).astype(BF16)
        acc_ref[...] = x

    def conv(w_ref, cw_ref, cb_ref, b0_ref, b1_ref, up_ref):
        up = _dot(hn_ref[...], w_ref[...])
        up_ref[...] = up
        return b0_ref[...] * cw_ref[0:1, :] + b1_ref[...] * cw_ref[1:2, :] + up * cw_ref[2:3, :] + cb_ref[...]

    a = conv(wa_ref, cwa_ref, cba_ref, b0a_ref, b1a_ref, upa_ref)
    gt = conv(wg_ref, cwg_ref, cbg_ref, b0g_ref, b1g_ref, upg_ref)
    act = a * (gt * _sigmoid(gt))
    acc_ref[...] += _dot(act.astype(BF16), wdn_ref[...])

    @pl.when(c == pl.num_programs(0) - 1)
    def _():
        acc = acc_ref[...]
        if final_norm:
            acc = _rms_rows(acc, gf_ref[...])
        o_ref[...] = acc


def _ffn_decode(x, g, wup, cw, cb, wdn, gf, buf0, buf1, *, final_norm):
    rows = x.shape[0]
    fc = FF_CHUNK
    full = lambda shape: pl.BlockSpec(shape, lambda c: (0, 0))
    a_col = lambda r: pl.BlockSpec((r, fc), lambda c: (0, c))
    g_col = lambda r: pl.BlockSpec((r, fc), lambda c: (0, N_FF_CHUNKS + c))
    return pl.pallas_call(
        functools.partial(_ffn_decode_kernel, final_norm=final_norm), grid=(N_FF_CHUNKS,),
        in_specs=[full((rows, D_MODEL)), full((1, D_MODEL)),
                  a_col(D_MODEL), g_col(D_MODEL), a_col(CONV_W), g_col(CONV_W), a_col(1), g_col(1),
                  a_col(rows), g_col(rows), a_col(rows), g_col(rows),
                  pl.BlockSpec((fc, D_MODEL), lambda c: (c, 0)), full((1, D_MODEL))],
        out_specs=[full((rows, D_MODEL)), a_col(rows), a_col(rows)],
        out_shape=[jax.ShapeDtypeStruct(x.shape, F32),
                   jax.ShapeDtypeStruct((rows, D_FF), F32),
                   jax.ShapeDtypeStruct((rows, D_FF), F32)],
        scratch_shapes=[pltpu.VMEM((rows, D_MODEL), BF16), pltpu.VMEM((rows, D_MODEL), F32)],
        compiler_params=_cparams(("arbitrary",)), name="ffn_decode",
    )(x, g, wup, wup, cw, cw, cb, cb, buf0, buf0, buf1, buf1, wdn, gf)


def _layer_weights(P, l):
    w_in = P['w_in'][l]
    w_main = jnp.concatenate([w_in[:, 0:1536], w_in[:, 1552:3600], w_in[:, 3608:7192]], axis=1).astype(BF16)
    w_small = jnp.concatenate([w_in[:, 1536:1552], w_in[:, 3600:3608],
                               jnp.zeros((D_MODEL, LANES - GLA_LOWRANK - 2 * ML_HEADS), F32)], axis=1).astype(BF16)
    wa = jnp.zeros((LANES, GLA_QK), F32).at[:GLA_LOWRANK].set(P['w_gla_alpha'][l]).astype(BF16)
    gbias = jnp.zeros((1, LANES), F32)
    gbias = gbias.at[0, SM_MI:SM_MI + ML_HEADS].set(P['b_mlstm_i'][l])
    gbias = gbias.at[0, SM_MF:SM_MF + ML_HEADS].set(P['b_mlstm_f'][l])
    eye = jnp.eye(S5_GB, dtype=F32)
    w_re = jnp.einsum('ngpc,gh->ngchp', P['s5_b_re'][l].reshape(S5_NB, S5_GB, S5_P, S5_GROUP), eye)
    w_im = jnp.einsum('ngpc,gh->ngchp', P['s5_b_im'][l].reshape(S5_NB, S5_GB, S5_P, S5_GROUP), eye)
    w_blk = jnp.concatenate([w_re.reshape(S5_NB, S5_UB, S5_MB), w_im.reshape(S5_NB, S5_UB, S5_MB)],
                            axis=-1).astype(BF16)
    ct_re = jnp.einsum('ngcp,gh->ngchp', P['s5_c_re'][l].reshape(S5_NB, S5_GB, S5_GROUP, S5_P), eye)
    ct_im = jnp.einsum('ngcp,gh->ngchp', P['s5_c_im'][l].reshape(S5_NB, S5_GB, S5_GROUP, S5_P), eye)
    ct_re = ct_re.reshape(S5_NB, S5_UB, S5_MB)
    ct_im = ct_im.reshape(S5_NB, S5_UB, S5_MB)
    lam_re = P['s5_lam_re'][l].reshape(S5_NB, 1, S5_MB)
    lam_im = P['s5_lam_im'][l].reshape(S5_NB, 1, S5_MB)
    log_dt = jnp.broadcast_to(P['s5_log_dt'][l][:, None], (S5_GROUPS, S5_P)).reshape(S5_NB, 1, S5_MB)
    par, cfold = _s5_prep(lam_re, lam_im, log_dt, ct_re, ct_im)
    row = lambda a: a.reshape(1, -1)
    return dict(
        norm_mix=row(P['norm_mix'][l]), w_main=w_main, w_small=w_small, wa=wa,
        ba=row(P['b_gla_alpha'][l]), gla_norm=row(P['gla_head_norm'][l]), gbias=gbias,
        ml_norm=row(P['mlstm_head_norm'][l]), w_blk=w_blk, ct_re=ct_re.astype(BF16),
        ct_im=ct_im.astype(BF16), par=par, cfold=cfold, s5_d=row(P['s5_d'][l]),
        w_glu=P['s5_w_glu'][l].astype(BF16), w_a=P['w_branch_a'][l].astype(BF16),
        w_b=P['w_branch_b'][l].astype(BF16), w_c=P['w_branch_c'][l].astype(BF16),
        w_out=P['w_out'][l].astype(BF16), norm_cross=row(P['norm_cross'][l]),
        w_cq=P['w_cq'][l].astype(BF16), w_co=P['w_co'][l].astype(BF16),
        norm_ffn=row(P['norm_ffn'][l]), w_up=P['w_ffn_up'][l].astype(BF16),
        conv_w=P['ffn_conv_w'][l], conv_b=row(P['ffn_conv_b'][l]),
        w_down=P['w_ffn_down'][l].astype(BF16), norm_final=row(P['norm_final']),
    )


def _prompt_trunk(x_prompt, mem_k, mem_v, W, *, tt_mix, tt_s5, bm, tq, tt_ffn):
    batch, seq, _ = x_prompt.shape
    depth = len(W)
    x = x_prompt.reshape(batch * seq, D_MODEL)
    outs = []
    for l, w in enumerate(W):
        z, zs = _norm_matmul(x, w['norm_mix'], w['w_main'], w['w_small'], bm=min(bm * 2, batch * seq), bn=1024)
        ya, st = _gla_prompt(z, zs, w['wa'], w['ba'], w['gla_norm'], batch=batch, seq=seq, tt=tt_mix)
        yb, c, n, m = _mlstm_prompt(z, zs, w['gbias'], w['ml_norm'], batch=batch, seq=seq, tt=tt_mix)
        ypre, hfin = _s5_prompt(z.reshape(batch, seq, Z_MAIN), w['w_blk'], w['cfold'], w['par'],
                                batch=batch, seq=seq, tt=tt_s5)
        x = _merge(ya, yb, ypre.reshape(batch * seq, S5_WIDTH), z, x, w['s5_d'], w['w_glu'],
                   w['w_a'], w['w_b'], w['w_c'], w['w_out'], bm=bm)
        x = _cross_prompt(x, w['norm_cross'], w['w_cq'], mem_k[l], mem_v[l], w['w_co'],
                          batch=batch, seq=seq, tq=tq)
        x, conv = _ffn_prompt(x, w['norm_ffn'], w['w_up'], w['conv_w'], w['conv_b'], w['w_down'],
                              w['norm_final'], batch=batch, seq=seq, tt=tt_ffn,
                              final_norm=(l == depth - 1))
        st4 = st.reshape(batch, GLA_HEADS, GLA_DV, GLA_HEADS, GLA_DK)
        gla = jnp.stack([st4[:, h, :, h, :] for h in range(GLA_HEADS)], axis=1).transpose(0, 1, 3, 2)
        h4 = hfin.reshape(S5_NB, batch, 2, S5_GB, S5_P)
        s5_re = h4[:, :, 0].transpose(1, 0, 2, 3).reshape(batch, S5_GROUPS, S5_P)
        s5_im = h4[:, :, 1].transpose(1, 0, 2, 3).reshape(batch, S5_GROUPS, S5_P)
        outs.append((gla, c.reshape(batch, ML_HEADS, ML_D, ML_D), n[:, :ML_HEADS, :],
                     m[:, :ML_HEADS, 0], s5_re, s5_im, conv))
    stacked = [jnp.stack([outs[l][i] for l in range(depth)]) for i in range(7)]
    return x.reshape(batch, seq, D_MODEL), stacked


def _sample_trunk(x_sample, cache_k, cache_v, states, W):
    rows = x_sample.shape[0]
    depth = len(W)
    x = x_sample.reshape(rows, D_MODEL)
    s_gla, s_c, s_n, s_m, s_re, s_im, s_conv = states
    outs = []
    for l, w in enumerate(W):
        z, zs = _norm_matmul(x, w['norm_mix'], w['w_main'], w['w_small'], bm=rows, bn=1024)
        ya, gla = _gla_decode(z, zs, w['wa'], w['ba'], w['gla_norm'], s_gla[l], bb=16)
        m_pad = jnp.pad(s_m[l], ((0, 0), (0, LANES - ML_HEADS)))
        yb, c, n, m = _mlstm_decode(z, zs, w['gbias'], w['ml_norm'], s_c[l], s_n[l], m_pad, bb=16)
        ypre, h_re, h_im = _s5_decode(z, w['w_blk'], w['ct_re'], w['ct_im'], w['par'],
                                      s_re[l].reshape(rows, S5_MODES), s_im[l].reshape(rows, S5_MODES))
        x = _merge(ya, yb, ypre, z, x, w['s5_d'], w['w_glu'], w['w_a'], w['w_b'], w['w_c'], w['w_out'],
                   bm=rows)
        x = _cross_decode(x, w['norm_cross'], w['w_cq'],
                          cache_k[l].reshape(rows, MEM_LEN, D_MODEL), cache_v[l].reshape(rows, MEM_LEN, D_MODEL),
                          w['w_co'], bb=8)
        x, up_a, up_g = _ffn_decode(x, w['norm_ffn'], w['w_up'], w['conv_w'], w['conv_b'], w['w_down'],
                                    w['norm_final'], s_conv[l][:, 0, :], s_conv[l][:, 1, :],
                                    final_norm=(l == depth - 1))
        conv = jnp.stack([s_conv[l][:, 1, :], jnp.concatenate([up_a, up_g], axis=1)], axis=1)
        outs.append((gla, c, n, m[:, :ML_HEADS], h_re.reshape(rows, S5_GROUPS, S5_P),
                     h_im.reshape(rows, S5_GROUPS, S5_P), conv))
    stacked = [jnp.stack([outs[l][i] for l in range(depth)]) for i in range(7)]
    return x.reshape(rows, 1, D_MODEL), stacked


def kernel(x_prompt, x_sample, mem_prompt, cache_mem_k, cache_mem_v, state_gla, state_mlstm_c, state_mlstm_n, state_mlstm_m, state_s5_re, state_s5_im, state_ffn_conv, norm_mix, w_in, w_gla_alpha, b_gla_alpha, gla_head_norm, b_mlstm_i, b_mlstm_f, mlstm_head_norm, s5_lam_re, s5_lam_im, s5_log_dt, s5_b_re, s5_b_im, s5_c_re, s5_c_im, s5_d, s5_w_glu, w_branch_a, w_branch_b, w_branch_c, w_out, norm_cross, norm_mem, w_cq, w_ck, w_cv, w_co, norm_ffn, w_ffn_up, ffn_conv_w, ffn_conv_b, w_ffn_down, norm_final):
    P = dict(norm_mix=norm_mix, w_in=w_in, w_gla_alpha=w_gla_alpha, b_gla_alpha=b_gla_alpha,
             gla_head_norm=gla_head_norm, b_mlstm_i=b_mlstm_i, b_mlstm_f=b_mlstm_f,
             mlstm_head_norm=mlstm_head_norm, s5_lam_re=s5_lam_re, s5_lam_im=s5_lam_im,
             s5_log_dt=s5_log_dt, s5_b_re=s5_b_re, s5_b_im=s5_b_im, s5_c_re=s5_c_re,
             s5_c_im=s5_c_im, s5_d=s5_d, s5_w_glu=s5_w_glu, w_branch_a=w_branch_a,
             w_branch_b=w_branch_b, w_branch_c=w_branch_c, w_out=w_out, norm_cross=norm_cross,
             w_cq=w_cq, w_co=w_co, norm_ffn=norm_ffn, w_ffn_up=w_ffn_up, ffn_conv_w=ffn_conv_w,
             ffn_conv_b=ffn_conv_b, w_ffn_down=w_ffn_down, norm_final=norm_final)
    depth = w_in.shape[0]
    W = [_layer_weights(P, l) for l in range(depth)]
    batch, mem_len, _ = mem_prompt.shape
    mem2 = mem_prompt.reshape(batch * mem_len, D_MODEL)
    mem_k, mem_v = [], []
    for l in range(depth):
        w_kv = jnp.concatenate([w_ck[l], w_cv[l]], axis=1).astype(BF16)
        kv = _norm_matmul(mem2, norm_mem[l].reshape(1, -1), w_kv, bm=1024, bn=1024)
        mem_k.append(kv[:, :D_MODEL])
        mem_v.append(kv[:, D_MODEL:])
    y_prompt, p_states = _prompt_trunk(x_prompt, mem_k, mem_v, W, tt_mix=256, tt_s5=256, bm=512, tq=512,
                                       tt_ffn=512)
    p_mem_k = jnp.stack(mem_k).reshape(depth, batch, mem_len, X_HEADS, X_DH)
    p_mem_v = jnp.stack(mem_v).reshape(depth, batch, mem_len, X_HEADS, X_DH)
    y_sample, s_states = _sample_trunk(
        x_sample, cache_mem_k, cache_mem_v,
        (state_gla, state_mlstm_c, state_mlstm_n, state_mlstm_m, state_s5_re, state_s5_im, state_ffn_conv), W)
    return (y_prompt, y_sample, *p_states, p_mem_k, p_mem_v, *s_states)
```

```python
import functools

import jax
import jax.numpy as jnp
from jax import lax
from jax.experimental import pallas as pl
from jax.experimental.pallas import tpu as pltpu

F32 = jnp.float32
BF16 = jnp.bfloat16

D_MODEL = 1024
GLA_HEADS, GLA_DK, GLA_DV = 4, 64, 128
GLA_QK = GLA_HEADS * GLA_DK
GLA_V = GLA_HEADS * GLA_DV
GLA_LOWRANK = 16
GLA_TAU = 16.0
ML_HEADS, ML_D = 4, 128
ML_W = ML_HEADS * ML_D
S5_GROUP, S5_GROUPS, S5_P = 16, 32, 64
S5_WIDTH = S5_GROUP * S5_GROUPS
S5_MODES = S5_GROUPS * S5_P
S5_NB = 4
S5_GB = S5_GROUPS // S5_NB
S5_MB = S5_MODES // S5_NB
S5_UB = S5_WIDTH // S5_NB
MEM_LEN = 256
X_HEADS = 4
X_DH = D_MODEL // X_HEADS
D_FF = 2816
CONV_W = 3
EPS = 1e-6

CHUNK = 64
LANES = 128
FF_CHUNK = 256
N_FF_CHUNKS = D_FF // FF_CHUNK

Z_MAIN = 7168
Z_QK, Z_GV, Z_GR, Z_MQ, Z_MK, Z_MV, Z_MO, Z_SU = range(8)
Z_ZA, Z_ZB, Z_ZC = 4, 5, 6
SM_MI = GLA_LOWRANK
SM_MF = GLA_LOWRANK + ML_HEADS

GLA_SAFE_DECAY = 80.0

VMEM_LIMIT = 56 * 1024 * 1024


def _cparams(sem):
    return pltpu.CompilerParams(dimension_semantics=sem, vmem_limit_bytes=VMEM_LIMIT)


def _dot(a, b):
    return jnp.dot(a, b, preferred_element_type=F32)


def _dot_nt(a, b):
    return lax.dot_general(a, b, (((1,), (1,)), ((), ())), preferred_element_type=F32)


def _dot_tn(a, b):
    return lax.dot_general(a, b, (((0,), (0,)), ((), ())), preferred_element_type=F32)


def _sigmoid(x):
    return 1.0 / (1.0 + jnp.exp(-x))


def _log_sigmoid(x):
    return jnp.minimum(x, 0.0) - jnp.log(1.0 + jnp.exp(-jnp.abs(x)))


def _gelu_tanh(x):
    return 0.5 * x * (1.0 + jnp.tanh(0.7978845608028654 * (x + 0.044715 * x * x * x)))


def _rms_rows(x, g):
    r = lax.rsqrt(jnp.mean(x * x, axis=-1, keepdims=True) + EPS)
    return (x * r) * g


def _dot_exact01(a01, x):
    hi = x.astype(BF16)
    r1 = x - hi.astype(F32)
    mid = r1.astype(BF16)
    lo = (r1 - mid.astype(F32)).astype(BF16)
    return _dot(a01, hi) + _dot(a01, mid) + _dot(a01, lo)


def _row_to_col(row, n):
    eye = (lax.broadcasted_iota(jnp.int32, (n, n), 0) == lax.broadcasted_iota(jnp.int32, (n, n), 1))
    return jnp.sum(jnp.where(eye, jnp.broadcast_to(row, (n, n)), 0.0), axis=1, keepdims=True)


def _norm_mm_kernel(x_ref, g_ref, w_ref, o_ref, hn_ref):
    @pl.when(pl.program_id(1) == 0)
    def _():
        hn_ref[...] = _rms_rows(x_ref[...], g_ref[...]).astype(BF16)
    o_ref[...] = _dot(hn_ref[...], w_ref[...])


def _norm_mm_small_kernel(x_ref, g_ref, w_ref, ws_ref, o_ref, os_ref, hn_ref):
    @pl.when(pl.program_id(1) == 0)
    def _():
        hn = _rms_rows(x_ref[...], g_ref[...]).astype(BF16)
        hn_ref[...] = hn
        os_ref[...] = _dot(hn, ws_ref[...])
    o_ref[...] = _dot(hn_ref[...], w_ref[...])


def _norm_matmul(x, g, w, w_small=None, *, bm, bn):
    n, d = x.shape
    c = w.shape[1]
    grid = (n // bm, c // bn)
    x_spec = pl.BlockSpec((bm, d), lambda i, j: (i, 0))
    g_spec = pl.BlockSpec((1, d), lambda i, j: (0, 0))
    w_spec = pl.BlockSpec((d, bn), lambda i, j: (0, j))
    o_spec = pl.BlockSpec((bm, bn), lambda i, j: (i, j))
    scratch = [pltpu.VMEM((bm, d), BF16)]
    if w_small is None:
        return pl.pallas_call(
            _norm_mm_kernel, grid=grid, in_specs=[x_spec, g_spec, w_spec], out_specs=o_spec,
            out_shape=jax.ShapeDtypeStruct((n, c), F32), scratch_shapes=scratch,
            compiler_params=_cparams(("arbitrary", "arbitrary")), name="norm_matmul",
        )(x, g, w)
    cs = w_small.shape[1]
    return pl.pallas_call(
        _norm_mm_small_kernel, grid=grid,
        in_specs=[x_spec, g_spec, w_spec, pl.BlockSpec((d, cs), lambda i, j: (0, 0))],
        out_specs=[o_spec, pl.BlockSpec((bm, cs), lambda i, j: (i, 0))],
        out_shape=[jax.ShapeDtypeStruct((n, c), F32), jax.ShapeDtypeStruct((n, cs), F32)],
        scratch_shapes=scratch,
        compiler_params=_cparams(("arbitrary", "arbitrary")), name="norm_matmul_in",
    )(x, g, w, w_small)


def _head_norm(oh, gn_h):
    r = lax.rsqrt(jnp.mean(oh * oh, axis=-1, keepdims=True) + EPS)
    return oh * r * gn_h


def _gla_log_decay(small, wa_ref, ba_ref):
    a_pre = _dot(small.astype(BF16), wa_ref[...]) + ba_ref[...]
    return _log_sigmoid(a_pre) * (1.0 / GLA_TAU)


def _gla_prompt_kernel(qk_ref, v_ref, gr_ref, sm_ref, wa_ref, ba_ref, gn_ref,
                       y_ref, st_ref, state_ref, b_ref, *, n_chunks):
    t = pl.program_id(1)
    L = CHUNK

    @pl.when(t == 0)
    def _():
        state_ref[...] = jnp.zeros_like(state_ref)

    ri = lax.broadcasted_iota(jnp.int32, (L, L), 0)
    ci = lax.broadcasted_iota(jnp.int32, (L, L), 1)
    tril = jnp.where(ri >= ci, 1.0, 0.0).astype(BF16)
    rw = lax.broadcasted_iota(jnp.int32, (L, GLA_HEADS * L), 0)
    cw = lax.broadcasted_iota(jnp.int32, (L, GLA_HEADS * L), 1)
    causal_wide = (cw % L) <= rw
    kr = lax.broadcasted_iota(jnp.int32, (GLA_HEADS * L, GLA_QK), 0)
    kc = lax.broadcasted_iota(jnp.int32, (GLA_HEADS * L, GLA_QK), 1)
    kk_mask = (kr // L) == (kc // GLA_DK)
    vr = lax.broadcasted_iota(jnp.int32, (GLA_HEADS * L, GLA_V), 0)
    vc = lax.broadcasted_iota(jnp.int32, (GLA_HEADS * L, GLA_V), 1)
    vv_mask = (vr // L) == (vc // GLA_DV)
    sr = lax.broadcasted_iota(jnp.int32, (GLA_V, GLA_QK), 0)
    sc = lax.broadcasted_iota(jnp.int32, (GLA_V, GLA_QK), 1)
    st_mask = (sr // GLA_DV) == (sc // GLA_DK)
    hr = lax.broadcasted_iota(jnp.int32, (GLA_QK, LANES), 0)
    hc = lax.broadcasted_iota(jnp.int32, (GLA_QK, LANES), 1)
    head_ones = jnp.where((hr // GLA_DK) == hc, 1.0, 0.0).astype(BF16)
    pr = lax.broadcasted_iota(jnp.int32, (LANES, GLA_HEADS * L), 0)
    pc = lax.broadcasted_iota(jnp.int32, (LANES, GLA_HEADS * L), 1)

    def chunk(c, carry):
        r0 = pl.multiple_of(c * L, L)
        rows = pl.ds(r0, L)
        qk = qk_ref[rows, :]
        q = qk[:, :GLA_QK] * (GLA_DK ** -0.5)
        k = qk[:, GLA_QK:]
        v = v_ref[rows, :]
        g = _gla_log_decay(sm_ref[rows, :], wa_ref, ba_ref)
        b = _dot_exact01(tril, g)
        b_end = b[L - 1:L, :]
        qe = q * jnp.exp(b)
        k_dec = k * jnp.exp(b_end - b)
        qe16 = qe.astype(BF16)

        def fast_att(_):
            k_til = k * jnp.exp(-b)
            kk = jnp.where(kk_mask, jnp.concatenate([k_til] * GLA_HEADS, axis=0), 0.0)
            return _dot_nt(qe16, kk.astype(BF16))

        def direct_att(_):
            b_ref[...] = b

            def col(j, acc):
                kj = qk_ref[pl.ds(r0 + j, 1), GLA_QK:]
                bj = b_ref[pl.ds(j, 1), :]
                prod = q * kj * jnp.exp(jnp.minimum(b - bj, 0.0))
                red = _dot(prod.astype(BF16), head_ones)
                place = jnp.where((pc == pr * L + j) & (pr < GLA_HEADS), 1.0, 0.0).astype(BF16)
                return acc + _dot(red.astype(BF16), place)

            return lax.fori_loop(0, L, col, jnp.zeros((L, GLA_HEADS * L), F32))

        safe = jnp.max(-b_end) <= GLA_SAFE_DECAY
        att = lax.cond(safe, fast_att, direct_att, 0)
        att = jnp.where(causal_wide, att, 0.0)

        vv = jnp.where(vv_mask, jnp.concatenate([v] * GLA_HEADS, axis=0), 0.0)
        st = state_ref[...]
        o = _dot(att.astype(BF16), vv.astype(BF16)) + _dot_nt(qe16, st.astype(BF16))
        upd = _dot_tn(v.astype(BF16), k_dec.astype(BF16))
        state_ref[...] = jnp.where(st_mask, st * jnp.exp(b_end) + upd, 0.0)

        gr = gr_ref[rows, :]
        for h in range(GLA_HEADS):
            sl = slice(h * GLA_DV, (h + 1) * GLA_DV)
            gate = gr[:, sl]
            yh = _head_norm(o[:, sl], gn_ref[:, sl]) * (gate * _sigmoid(gate))
            y_ref[rows, sl] = yh.astype(BF16)
        return carry

    lax.fori_loop(0, n_chunks, chunk, 0)

    @pl.when(t == pl.num_programs(1) - 1)
    def _():
        st_ref[0] = state_ref[...]


def _gla_prompt(z, zs, wa, ba, gn, *, batch, seq, tt):
    nt = seq // tt
    row = lambda b, t: b * nt + t
    blk = lambda j: pl.BlockSpec((tt, 512), lambda b, t: (row(b, t), j))
    const = lambda shape: pl.BlockSpec(shape, lambda b, t: (0,) * len(shape))
    return pl.pallas_call(
        functools.partial(_gla_prompt_kernel, n_chunks=tt // CHUNK),
        grid=(batch, nt),
        in_specs=[blk(Z_QK), blk(Z_GV), blk(Z_GR),
                  pl.BlockSpec((tt, LANES), lambda b, t: (row(b, t), 0)),
                  const((LANES, GLA_QK)), const((1, GLA_QK)), const((1, GLA_V))],
        out_specs=[pl.BlockSpec((tt, GLA_V), lambda b, t: (row(b, t), 0)),
                   pl.BlockSpec((1, GLA_V, GLA_QK), lambda b, t: (b, 0, 0))],
        out_shape=[jax.ShapeDtypeStruct((batch * seq, GLA_V), BF16),
                   jax.ShapeDtypeStruct((batch, GLA_V, GLA_QK), F32)],
        scratch_shapes=[pltpu.VMEM((GLA_V, GLA_QK), F32), pltpu.VMEM((CHUNK, GLA_QK), F32)],
        compiler_params=_cparams(("arbitrary", "arbitrary")), name="gla_prompt",
    )(z, z, z, zs, wa, ba, gn)


def _mlstm_prompt_kernel(q_ref, k_ref, v_ref, og_ref, sm_ref, gb_ref, gn_ref,
                         y_ref, c_out, n_out, m_out, c_ref, n_ref, m_ref, *, n_chunks):
    t = pl.program_id(1)
    L = CHUNK

    @pl.when(t == 0)
    def _():
        c_ref[...] = jnp.zeros_like(c_ref)
        n_ref[...] = jnp.zeros_like(n_ref)
        m_ref[...] = jnp.zeros_like(m_ref)

    ri = lax.broadcasted_iota(jnp.int32, (L, L), 0)
    ci = lax.broadcasted_iota(jnp.int32, (L, L), 1)
    lower = ri >= ci
    eye = ri == ci

    def chunk(c, carry):
        r0 = pl.multiple_of(c * L, L)
        rows = pl.ds(r0, L)
        sm = sm_ref[rows, :] + gb_ref[...]
        lf_all = _log_sigmoid(sm)
        for h in range(ML_HEADS):
            sl = slice(h * ML_D, (h + 1) * ML_D)
            li_col = sm[:, SM_MI + h:SM_MI + h + 1]
            lf_col = lf_all[:, SM_MF + h:SM_MF + h + 1]
            lf_b = jnp.broadcast_to(lf_col, (L, L))
            li_row = jnp.sum(jnp.where(eye, jnp.broadcast_to(li_col, (L, L)), 0.0), axis=0, keepdims=True)
            lf_row = jnp.sum(jnp.where(eye, lf_b, 0.0), axis=0, keepdims=True)
            b_row = jnp.sum(jnp.where(ri <= ci, lf_b, 0.0), axis=0, keepdims=True)
            b_col = jnp.sum(jnp.where(lower, jnp.broadcast_to(lf_row, (L, L)), 0.0), axis=1, keepdims=True)
            m_prev = m_ref[h:h + 1, 0:1]
            w = jnp.where(lower, b_col - b_row + li_row, -jnp.inf)
            m_tok = jnp.maximum(b_col + m_prev, jnp.max(w, axis=1, keepdims=True))
            a_inter = jnp.exp(b_col + m_prev - m_tok)
            qh = q_ref[rows, sl] * (ML_D ** -0.5)
            kh = k_ref[rows, sl]
            vh = v_ref[rows, sl]
            qh16, kh16, vh16 = qh.astype(BF16), kh.astype(BF16), vh.astype(BF16)
            s = _dot_nt(qh16, kh16) * jnp.exp(w - m_tok)
            c_h = c_ref[sl, :]
            n_h = n_ref[h:h + 1, :]
            num = _dot(s.astype(BF16), vh16) + a_inter * _dot(qh16, c_h.astype(BF16))
            den = jnp.sum(s, axis=1, keepdims=True) + a_inter * jnp.sum(qh * n_h, axis=1, keepdims=True)
            hh = num / jnp.maximum(jnp.abs(den), jnp.exp(-m_tok))
            og = og_ref[rows, sl]
            y_ref[rows, sl] = (_head_norm(hh, gn_ref[:, sl]) * _sigmoid(og)).astype(BF16)
            b_end = b_col[L - 1:L, :]
            w_end = b_end - b_col + li_col
            m_new = jnp.maximum(b_end + m_prev, jnp.max(w_end, axis=0, keepdims=True))
            e_inter = jnp.exp(b_end + m_prev - m_new)
            kd = kh * jnp.exp(w_end - m_new)
            c_ref[sl, :] = e_inter * c_h + _dot_tn(kd.astype(BF16), vh16)
            n_ref[h:h + 1, :] = e_inter * n_h + jnp.sum(kd, axis=0, keepdims=True)
            m_ref[h:h + 1, :] = jnp.broadcast_to(m_new, (1, LANES))
        return carry

    lax.fori_loop(0, n_chunks, chunk, 0)

    @pl.when(t == pl.num_programs(1) - 1)
    def _():
        c_out[0] = c_ref[...]
        n_out[0] = n_ref[...]
        m_out[0] = m_ref[...]


def _mlstm_prompt(z, zs, gbias, gn, *, batch, seq, tt):
    nt = seq // tt
    row = lambda b, t: b * nt + t
    blk = lambda j: pl.BlockSpec((tt, 512), lambda b, t: (row(b, t), j))
    const = lambda shape: pl.BlockSpec(shape, lambda b, t: (0,) * len(shape))
    return pl.pallas_call(
        functools.partial(_mlstm_prompt_kernel, n_chunks=tt // CHUNK),
        grid=(batch, nt),
        in_specs=[blk(Z_MQ), blk(Z_MK), blk(Z_MV), blk(Z_MO),
                  pl.BlockSpec((tt, LANES), lambda b, t: (row(b, t), 0)),
                  const((1, LANES)), const((1, ML_W))],
        out_specs=[pl.BlockSpec((tt, ML_W), lambda b, t: (row(b, t), 0)),
                   pl.BlockSpec((1, ML_W, ML_D), lambda b, t: (b, 0, 0)),
                   pl.BlockSpec((1, 8, LANES), lambda b, t: (b, 0, 0)),
                   pl.BlockSpec((1, 8, LANES), lambda b, t: (b, 0, 0))],
        out_shape=[jax.ShapeDtypeStruct((batch * seq, ML_W), BF16),
                   jax.ShapeDtypeStruct((batch, ML_W, ML_D), F32),
                   jax.ShapeDtypeStruct((batch, 8, LANES), F32),
                   jax.ShapeDtypeStruct((batch, 8, LANES), F32)],
        scratch_shapes=[pltpu.VMEM((ML_W, ML_D), F32), pltpu.VMEM((8, LANES), F32),
                        pltpu.VMEM((8, LANES), F32)],
        compiler_params=_cparams(("arbitrary", "arbitrary")), name="mlstm_prompt",
    )(z, z, z, z, zs, gbias, gn)


def _s5_prep_kernel(lre_ref, lim_ref, ldt_ref, ctr_ref, cti_ref, par_ref, cf_ref):
    lam_re = lre_ref[0]
    lam_im = lim_ref[0]
    dt = jnp.exp(ldt_ref[0])
    mag = jnp.exp(lam_re * dt)
    lb_re = mag * jnp.cos(lam_im * dt)
    lb_im = mag * jnp.sin(lam_im * dt)
    nr = lb_re - 1.0
    den = lam_re * lam_re + lam_im * lam_im
    f_re = (nr * lam_re + lb_im * lam_im) / den
    f_im = (lb_im * lam_re - nr * lam_im) / den
    par_ref[0] = jnp.concatenate([lb_re, lb_im, f_re, f_im, jnp.zeros((4, S5_MB), F32)], axis=0)
    ct_re = ctr_ref[0]
    ct_im = cti_ref[0]
    cf_ref[0, :, :S5_MB] = (ct_re * f_re - ct_im * f_im).astype(BF16)
    cf_ref[0, :, S5_MB:] = (-(ct_re * f_im + ct_im * f_re)).astype(BF16)


def _s5_prep(lam_re, lam_im, log_dt, ct_re, ct_im):
    vec = pl.BlockSpec((1, 1, S5_MB), lambda n: (n, 0, 0))
    mat = pl.BlockSpec((1, S5_UB, S5_MB), lambda n: (n, 0, 0))
    return pl.pallas_call(
        _s5_prep_kernel, grid=(S5_NB,),
        in_specs=[vec, vec, vec, mat, mat],
        out_specs=[pl.BlockSpec((1, 8, S5_MB), lambda n: (n, 0, 0)),
                   pl.BlockSpec((1, S5_UB, 2 * S5_MB), lambda n: (n, 0, 0))],
        out_shape=[jax.ShapeDtypeStruct((S5_NB, 8, S5_MB), F32),
                   jax.ShapeDtypeStruct((S5_NB, S5_UB, 2 * S5_MB), BF16)],
        compiler_params=_cparams(("arbitrary",)), name="s5_prep",
    )(lam_re, lam_im, log_dt, ct_re, ct_im)


def _s5_prompt_kernel(u_ref, w_ref, cf_ref, par_ref, y_ref, h_out, x_ref, h_ref, ub_ref, ut_ref, *, batch, tt):
    t_blk = pl.program_id(1)

    @pl.when(t_blk == 0)
    def _():
        h_ref[...] = jnp.zeros_like(h_ref)

    nlb = S5_MB // LANES
    ub_ref[...] = u_ref[...].reshape(batch * tt, S5_UB)

    def interleave(t, carry):
        ut_ref[pl.ds(pl.multiple_of(t * batch, batch), batch), :] = ub_ref[pl.ds(t, batch, stride=tt), :]
        return carry

    lax.fori_loop(0, tt, interleave, 0, unroll=8)
    x = _dot(ut_ref[...].astype(BF16), w_ref[0])
    for j in range(2 * nlb):
        x_ref[j] = x[:, j * LANES:(j + 1) * LANES]

    par = par_ref[0]
    lr = [jnp.broadcast_to(par[0:1, j * LANES:(j + 1) * LANES], (batch, LANES)) for j in range(nlb)]
    li = [jnp.broadcast_to(par[1:2, j * LANES:(j + 1) * LANES], (batch, LANES)) for j in range(nlb)]

    def step(t, carry):
        rows = pl.ds(pl.multiple_of(t * batch, batch), batch)
        new = []
        for j in range(nlb):
            hr, hi = carry[j], carry[nlb + j]
            nr = lr[j] * hr - li[j] * hi + x_ref[j, rows, :]
            ni = lr[j] * hi + li[j] * hr + x_ref[nlb + j, rows, :]
            x_ref[j, rows, :] = nr
            x_ref[nlb + j, rows, :] = ni
            new.append((nr, ni))
        return tuple(n[0] for n in new) + tuple(n[1] for n in new)

    h0 = tuple(h_ref[:, j * LANES:(j + 1) * LANES] for j in range(2 * nlb))
    hfin = lax.fori_loop(0, tt, step, h0, unroll=8)
    for j in range(2 * nlb):
        h_ref[:, j * LANES:(j + 1) * LANES] = hfin[j]
    hr = jnp.concatenate(hfin[:nlb], axis=1)
    hi = jnp.concatenate(hfin[nlb:], axis=1)

    hall = jnp.concatenate([x_ref[j].astype(BF16) for j in range(2 * nlb)], axis=1)
    ut_ref[...] = _dot_nt(hall, cf_ref[0])
    for b in range(batch):
        y_ref[b] = ut_ref[pl.ds(b, tt, stride=batch), :]

    @pl.when(t_blk == pl.num_programs(1) - 1)
    def _():
        fr = par[2:3, :]
        fi = par[3:4, :]
        h_out[0, :, :S5_MB] = fr * hr - fi * hi
        h_out[0, :, S5_MB:] = fr * hi + fi * hr


def _s5_prompt(z3, w_blk, cfold, par, *, batch, seq, tt):
    nt = seq // tt
    u_col0 = Z_SU * 512 // S5_UB
    return pl.pallas_call(
        functools.partial(_s5_prompt_kernel, batch=batch, tt=tt),
        grid=(S5_NB, nt),
        in_specs=[pl.BlockSpec((batch, tt, S5_UB), lambda n, t: (0, t, u_col0 + n)),
                  pl.BlockSpec((1, S5_UB, 2 * S5_MB), lambda n, t: (n, 0, 0)),
                  pl.BlockSpec((1, S5_UB, 2 * S5_MB), lambda n, t: (n, 0, 0)),
                  pl.BlockSpec((1, 8, S5_MB), lambda n, t: (n, 0, 0))],
        out_specs=[pl.BlockSpec((batch, tt, S5_UB), lambda n, t: (0, t, n)),
                   pl.BlockSpec((1, batch, 2 * S5_MB), lambda n, t: (n, 0, 0))],
        out_shape=[jax.ShapeDtypeStruct((batch, seq, S5_WIDTH), F32),
                   jax.ShapeDtypeStruct((S5_NB, batch, 2 * S5_MB), F32)],
        scratch_shapes=[pltpu.VMEM((2 * S5_MB // LANES, batch * tt, LANES), F32),
                        pltpu.VMEM((batch, 2 * S5_MB), F32),
                        pltpu.VMEM((batch * tt, S5_UB), F32), pltpu.VMEM((batch * tt, S5_UB), F32)],
        compiler_params=_cparams(("arbitrary", "arbitrary")), name="s5_prompt",
    )(z3, w_blk, cfold, par)


def _s5_decode_kernel(u_ref, w_ref, ctr_ref, cti_ref, par_ref, hr_ref, hi_ref, y_ref, hro_ref, hio_ref):
    x = _dot(u_ref[...].astype(BF16), w_ref[0])
    xr, xi = x[:, :S5_MB], x[:, S5_MB:]
    par = par_ref[0]
    lr, li, fr, fi = par[0:1, :], par[1:2, :], par[2:3, :], par[3:4, :]
    h0r, h0i = hr_ref[...], hi_ref[...]
    hr = lr * h0r - li * h0i + (fr * xr - fi * xi)
    hi = lr * h0i + li * h0r + (fr * xi + fi * xr)
    hro_ref[...] = hr
    hio_ref[...] = hi
    y_ref[...] = _dot_nt(hr.astype(BF16), ctr_ref[0]) - _dot_nt(hi.astype(BF16), cti_ref[0])


def _s5_decode(z, w_blk, ct_re, ct_im, par, h_re, h_im):
    rows = z.shape[0]
    u_col0 = Z_SU * 512 // S5_UB
    hspec = pl.BlockSpec((rows, S5_MB), lambda n: (0, n))
    mat = pl.BlockSpec((1, S5_UB, S5_MB), lambda n: (n, 0, 0))
    return pl.pallas_call(
        _s5_decode_kernel, grid=(S5_NB,),
        in_specs=[pl.BlockSpec((rows, S5_UB), lambda n: (0, u_col0 + n)),
                  pl.BlockSpec((1, S5_UB, 2 * S5_MB), lambda n: (n, 0, 0)),
                  mat, mat,
                  pl.BlockSpec((1, 8, S5_MB), lambda n: (n, 0, 0)), hspec, hspec],
        out_specs=[pl.BlockSpec((rows, S5_UB), lambda n: (0, n)), hspec, hspec],
        out_shape=[jax.ShapeDtypeStruct((rows, S5_WIDTH), F32),
                   jax.ShapeDtypeStruct((rows, S5_MODES), F32),
                   jax.ShapeDtypeStruct((rows, S5_MODES), F32)],
        compiler_params=_cparams(("arbitrary",)), name="s5_decode",
    )(z, w_blk, ct_re, ct_im, par, h_re, h_im)


def _gla_decode_kernel(qk_ref, v_ref, gr_ref, sm_ref, wa_ref, ba_ref, gn_ref, s_ref,
                       y_ref, so_ref, eg_ref, o_ref, *, bb):
    g = _gla_log_decay(sm_ref[...], wa_ref, ba_ref)
    eg_ref[...] = jnp.exp(g)

    def per_row(b, carry):
        row = pl.ds(b, 1)
        eg_r = eg_ref[row, :]
        qk_r = qk_ref[row, :]
        v_r = v_ref[row, :]
        q_r = qk_r[:, :GLA_QK] * (GLA_DK ** -0.5)
        k_r = qk_r[:, GLA_QK:]
        o_heads = []
        for h in range(GLA_HEADS):
            ks = slice(h * GLA_DK, (h + 1) * GLA_DK)
            vs = slice(h * GLA_DV, (h + 1) * GLA_DV)
            s_new = _row_to_col(eg_r[:, ks], GLA_DK) * s_ref[b, h] + _row_to_col(k_r[:, ks], GLA_DK) * v_r[:, vs]
            so_ref[b, h] = s_new
            o_heads.append(jnp.sum(_row_to_col(q_r[:, ks], GLA_DK) * s_new, axis=0, keepdims=True))
        o_ref[row, :] = jnp.concatenate(o_heads, axis=1)
        return carry

    lax.fori_loop(0, bb, per_row, 0)
    o = o_ref[...]
    gr = gr_ref[...]
    for h in range(GLA_HEADS):
        vs = slice(h * GLA_DV, (h + 1) * GLA_DV)
        gate = gr[:, vs]
        y_ref[:, vs] = (_head_norm(o[:, vs], gn_ref[:, vs]) * (gate * _sigmoid(gate))).astype(BF16)


def _gla_decode(z, zs, wa, ba, gn, states, *, layer, bb):
    rows = z.shape[0]
    blk = lambda j: pl.BlockSpec((bb, 512), lambda i: (i, j))
    const = lambda shape: pl.BlockSpec(shape, lambda i: (0,) * len(shape))
    sblock = (bb, GLA_HEADS, GLA_DK, GLA_DV)
    return pl.pallas_call(
        functools.partial(_gla_decode_kernel, bb=bb), grid=(rows // bb,),
        in_specs=[blk(Z_QK), blk(Z_GV), blk(Z_GR), pl.BlockSpec((bb, LANES), lambda i: (i, 0)),
                  const((LANES, GLA_QK)), const((1, GLA_QK)), const((1, GLA_V)),
                  pl.BlockSpec((None,) + sblock, lambda i: (layer, i, 0, 0, 0))],
        out_specs=[pl.BlockSpec((bb, GLA_V), lambda i: (i, 0)),
                   pl.BlockSpec(sblock, lambda i: (i, 0, 0, 0))],
        out_shape=[jax.ShapeDtypeStruct((rows, GLA_V), BF16),
                   jax.ShapeDtypeStruct(states.shape[1:], F32)],
        scratch_shapes=[pltpu.VMEM((bb, GLA_QK), F32), pltpu.VMEM((bb, GLA_V), F32)],
        compiler_params=_cparams(("arbitrary",)), name="gla_decode",
    )(z, z, z, zs, wa, ba, gn, states)


def _mlstm_decode_kernel(q_ref, k_ref, v_ref, og_ref, sm_ref, gb_ref, gn_ref, c_ref, n_ref, m_ref,
                         y_ref, co_ref, no_ref, mo_ref, li_ref, lf_ref, h_ref, *, bb):
    sm = sm_ref[...] + gb_ref[...]
    li_ref[...] = sm
    lf_ref[...] = _log_sigmoid(sm)
    lane = lax.broadcasted_iota(jnp.int32, (1, LANES), 1)

    def per_row(b, carry):
        row = pl.ds(b, 1)
        li_r = li_ref[row, :]
        lf_r = lf_ref[row, :]
        m_r = m_ref[row, :]
        q_all = q_ref[row, :] * (ML_D ** -0.5)
        k_all = k_ref[row, :]
        v_all = v_ref[row, :]
        n_all = n_ref[b]
        m_new_row = jnp.zeros((1, LANES), F32)
        h_heads, n_heads = [], []
        for h in range(ML_HEADS):
            sl = slice(h * ML_D, (h + 1) * ML_D)
            li = li_r[:, SM_MI + h:SM_MI + h + 1]
            lf = lf_r[:, SM_MF + h:SM_MF + h + 1]
            m_prev = m_r[:, h:h + 1]
            m_new = jnp.maximum(lf + m_prev, li)
            a = jnp.exp(lf + m_prev - m_new)
            e = jnp.exp(li - m_new)
            k_r = k_all[:, sl]
            q_r = q_all[:, sl]
            c_new = a * c_ref[b, h] + (_row_to_col(k_r, ML_D) * e) * v_all[:, sl]
            n_new = a * n_all[h:h + 1, :] + e * k_r
            co_ref[b, h] = c_new
            n_heads.append(n_new)
            num = jnp.sum(_row_to_col(q_r, ML_D) * c_new, axis=0, keepdims=True)
            den = jnp.sum(q_r * n_new, axis=1, keepdims=True)
            h_heads.append(num / jnp.maximum(jnp.abs(den), jnp.exp(-m_new)))
            m_new_row = jnp.where(lane == h, m_new, m_new_row)
        no_ref[b] = jnp.concatenate(n_heads, axis=0)
        h_ref[row, :] = jnp.concatenate(h_heads, axis=1)
        mo_ref[row, :] = m_new_row
        return carry

    lax.fori_loop(0, bb, per_row, 0)
    hh = h_ref[...]
    og = og_ref[...]
    for h in range(ML_HEADS):
        sl = slice(h * ML_D, (h + 1) * ML_D)
        y_ref[:, sl] = (_head_norm(hh[:, sl], gn_ref[:, sl]) * _sigmoid(og[:, sl])).astype(BF16)


def _mlstm_decode(z, zs, gbias, gn, c_all, n_all, m_pad, *, layer, bb):
    rows = z.shape[0]
    blk = lambda j: pl.BlockSpec((bb, 512), lambda i: (i, j))
    const = lambda shape: pl.BlockSpec(shape, lambda i: (0,) * len(shape))
    cblock = (bb, ML_HEADS, ML_D, ML_D)
    nblock = (bb, ML_HEADS, ML_D)
    cspec = pl.BlockSpec(cblock, lambda i: (i, 0, 0, 0))
    nspec = pl.BlockSpec(nblock, lambda i: (i, 0, 0))
    mspec = pl.BlockSpec((bb, LANES), lambda i: (i, 0))
    return pl.pallas_call(
        functools.partial(_mlstm_decode_kernel, bb=bb), grid=(rows // bb,),
        in_specs=[blk(Z_MQ), blk(Z_MK), blk(Z_MV), blk(Z_MO), mspec,
                  const((1, LANES)), const((1, ML_W)),
                  pl.BlockSpec((None,) + cblock, lambda i: (layer, i, 0, 0, 0)),
                  pl.BlockSpec((None,) + nblock, lambda i: (layer, i, 0, 0)), mspec],
        out_specs=[pl.BlockSpec((bb, ML_W), lambda i: (i, 0)), cspec, nspec, mspec],
        out_shape=[jax.ShapeDtypeStruct((rows, ML_W), BF16),
                   jax.ShapeDtypeStruct(c_all.shape[1:], F32), jax.ShapeDtypeStruct(n_all.shape[1:], F32),
                   jax.ShapeDtypeStruct((rows, LANES), F32)],
        scratch_shapes=[pltpu.VMEM((bb, LANES), F32), pltpu.VMEM((bb, LANES), F32),
                        pltpu.VMEM((bb, ML_W), F32)],
        compiler_params=_cparams(("arbitrary",)), name="mlstm_decode",
    )(z, z, z, z, zs, gbias, gn, c_all, n_all, m_pad)


def _merge_kernel(ya_ref, yb_ref, yp_ref, u_ref, za_ref, zb_ref, zc_ref, x_ref,
                  d_ref, wg_ref, wa_ref, wb_ref, wc_ref, wo_ref, o_ref):
    yc = _gelu_tanh(yp_ref[...] + d_ref[...] * u_ref[...])
    yc = yc * _sigmoid(_dot(yc.astype(BF16), wg_ref[...]))
    m = _sigmoid(za_ref[...]) * _dot(ya_ref[...], wa_ref[...])
    m = m + _sigmoid(zb_ref[...]) * _dot(yb_ref[...], wb_ref[...])
    m = m + _sigmoid(zc_ref[...]) * _dot(yc.astype(BF16), wc_ref[...])
    o_ref[...] = x_ref[...] + _dot(m.astype(BF16), wo_ref[...])


def _merge(ya, yb, ypre, z, x, d, wg, wa, wb, wc, wo, *, bm):
    n = x.shape[0]
    r512 = pl.BlockSpec((bm, 512), lambda i: (i, 0))
    zblk = lambda j: pl.BlockSpec((bm, D_MODEL), lambda i: (i, j))
    const = lambda shape: pl.BlockSpec(shape, lambda i: (0, 0))
    return pl.pallas_call(
        _merge_kernel, grid=(n // bm,),
        in_specs=[r512, r512, r512, pl.BlockSpec((bm, 512), lambda i: (i, Z_SU)),
                  zblk(Z_ZA), zblk(Z_ZB), zblk(Z_ZC), pl.BlockSpec((bm, D_MODEL), lambda i: (i, 0)),
                  const((1, S5_WIDTH)), const((S5_WIDTH, S5_WIDTH)),
                  const((GLA_V, D_MODEL)), const((ML_W, D_MODEL)), const((S5_WIDTH, D_MODEL)),
                  const((D_MODEL, D_MODEL))],
        out_specs=pl.BlockSpec((bm, D_MODEL), lambda i: (i, 0)),
        out_shape=jax.ShapeDtypeStruct((n, D_MODEL), F32),
        compiler_params=_cparams(("arbitrary",)), name="merge",
    )(ya, yb, ypre, z, z, z, z, x, d, wg, wa, wb, wc, wo)


def _cross_prompt_kernel(x_ref, g_ref, wq_ref, mk_ref, mv_ref, wo_ref, o_ref):
    x = x_ref[...]
    q = _dot(_rms_rows(x, g_ref[...]).astype(BF16), wq_ref[...])
    heads = []
    for h in range(X_HEADS):
        sl = slice(h * X_DH, (h + 1) * X_DH)
        s = _dot_nt(q[:, sl].astype(BF16), mk_ref[:, sl].astype(BF16)) * (X_DH ** -0.5)
        s = s - jnp.max(s, axis=-1, keepdims=True)
        p = jnp.exp(s)
        p = p / jnp.sum(p, axis=-1, keepdims=True)
        heads.append(_dot(p.astype(BF16), mv_ref[:, sl].astype(BF16)))
    o = jnp.concatenate(heads, axis=-1)
    o_ref[...] = x + _dot(o.astype(BF16), wo_ref[...])


def _cross_prompt(x, g, wq, mem_k, mem_v, wo, *, batch, seq, tq):
    nt = seq // tq
    xspec = pl.BlockSpec((tq, D_MODEL), lambda b, t: (b * nt + t, 0))
    const = lambda shape: pl.BlockSpec(shape, lambda b, t: (0, 0))
    mspec = pl.BlockSpec((MEM_LEN, D_MODEL), lambda b, t: (b, 0))
    return pl.pallas_call(
        _cross_prompt_kernel, grid=(batch, nt),
        in_specs=[xspec, const((1, D_MODEL)), const((D_MODEL, D_MODEL)), mspec, mspec,
                  const((D_MODEL, D_MODEL))],
        out_specs=xspec, out_shape=jax.ShapeDtypeStruct(x.shape, F32),
        compiler_params=_cparams(("arbitrary", "arbitrary")), name="cross_prompt",
    )(x, g, wq, mem_k, mem_v, wo)


def _cross_decode_kernel(x_ref, g_ref, wq_ref, mk_ref, mv_ref, wo_ref, o_ref, att_ref, *, bb):
    x = x_ref[...]
    q = _dot(_rms_rows(x, g_ref[...]).astype(BF16), wq_ref[...])
    for b in range(bb):
        q4 = jnp.concatenate([q[b:b + 1, h * X_DH:(h + 1) * X_DH] for h in range(X_HEADS)], axis=0)
        s = jnp.sum(mk_ref[0, b] * q4[None], axis=-1, keepdims=True) * (X_DH ** -0.5)
        p = jnp.exp(s - jnp.max(s, axis=0, keepdims=True))
        p = p / jnp.sum(p, axis=0, keepdims=True)
        o4 = jnp.sum(p * mv_ref[0, b], axis=0)
        att_ref[b:b + 1, :] = jnp.concatenate([o4[h:h + 1, :] for h in range(X_HEADS)], axis=1)
    o_ref[...] = x + _dot(att_ref[...].astype(BF16), wo_ref[...])


def _cross_decode(x, g, wq, cache_k, cache_v, wo, *, layer, bb):
    rows = x.shape[0]
    xspec = pl.BlockSpec((bb, D_MODEL), lambda i: (i, 0))
    const = lambda shape: pl.BlockSpec(shape, lambda i: (0, 0))
    mspec = pl.BlockSpec((1, bb, MEM_LEN, X_HEADS, X_DH), lambda i: (layer, i, 0, 0, 0))
    return pl.pallas_call(
        functools.partial(_cross_decode_kernel, bb=bb), grid=(rows // bb,),
        in_specs=[xspec, const((1, D_MODEL)), const((D_MODEL, D_MODEL)), mspec, mspec,
                  const((D_MODEL, D_MODEL))],
        out_specs=xspec, out_shape=jax.ShapeDtypeStruct(x.shape, F32),
        scratch_shapes=[pltpu.VMEM((bb, D_MODEL), F32)],
        compiler_params=_cparams(("arbitrary",)), name="cross_decode",
    )(x, g, wq, cache_k, cache_v, wo)


def _ffn_prompt_kernel(x_ref, g_ref, wup_ref, cw_ref, cb_ref, wdn_ref, gf_ref,
                       o_ref, st_ref, hn_ref, halo_ref, *, tt, final_norm):
    t = pl.program_id(1)

    @pl.when(t == 0)
    def _():
        halo_ref[...] = jnp.zeros_like(halo_ref)

    x = x_ref[...]
    hn_ref[...] = _rms_rows(x, g_ref[...]).astype(BF16)
    rid = lax.broadcasted_iota(jnp.int32, (tt, FF_CHUNK), 0)

    def conv(col0):
        cols = slice(col0, col0 + FF_CHUNK)
        up = _dot(hn_ref[...], wup_ref[:, cols])
        p2 = halo_ref[0:1, cols]
        p1 = halo_ref[1:2, cols]
        sh1 = jnp.where(rid == 0, p1, pltpu.roll(up, 1, axis=0))
        sh2 = jnp.where(rid == 0, p2, jnp.where(rid == 1, p1, pltpu.roll(up, 2, axis=0)))
        halo_ref[0:2, cols] = up[tt - 2:tt, :]
        return sh2 * cw_ref[0:1, cols] + sh1 * cw_ref[1:2, cols] + up * cw_ref[2:3, cols] + cb_ref[:, cols]

    acc = x
    for c in range(N_FF_CHUNKS):
        a = conv(c * FF_CHUNK)
        gt = conv(D_FF + c * FF_CHUNK)
        act = a * (gt * _sigmoid(gt))
        acc = acc + _dot(act.astype(BF16), wdn_ref[c * FF_CHUNK:(c + 1) * FF_CHUNK, :])
    if final_norm:
        acc = _rms_rows(acc, gf_ref[...])
    o_ref[...] = acc

    @pl.when(t == pl.num_programs(1) - 1)
    def _():
        st_ref[0] = halo_ref[0:2, :]


def _ffn_prompt(x, g, wup, cw, cb, wdn, gf, *, batch, seq, tt, final_norm):
    nt = seq // tt
    xspec = pl.BlockSpec((tt, D_MODEL), lambda b, t: (b * nt + t, 0))
    const = lambda shape: pl.BlockSpec(shape, lambda b, t: (0, 0), pipeline_mode=pl.Buffered(1))
    return pl.pallas_call(
        functools.partial(_ffn_prompt_kernel, tt=tt, final_norm=final_norm), grid=(batch, nt),
        in_specs=[xspec, const((1, D_MODEL)), const((D_MODEL, 2 * D_FF)), const((CONV_W, 2 * D_FF)),
                  const((1, 2 * D_FF)), const((D_FF, D_MODEL)), const((1, D_MODEL))],
        out_specs=[xspec, pl.BlockSpec((1, CONV_W - 1, 2 * D_FF), lambda b, t: (b, 0, 0))],
        out_shape=[jax.ShapeDtypeStruct(x.shape, F32),
                   jax.ShapeDtypeStruct((batch, CONV_W - 1, 2 * D_FF), F32)],
        scratch_shapes=[pltpu.VMEM((tt, D_MODEL), BF16), pltpu.VMEM((8, 2 * D_FF), F32)],
        compiler_params=_cparams(("arbitrary", "arbitrary")), name="ffn_prompt",
    )(x, g, wup, cw, cb, wdn, gf)


def _ffn_decode_kernel(x_ref, g_ref, wa_ref, wg_ref, cwa_ref, cwg_ref, cba_ref, cbg_ref,
                       b0a_ref, b0g_ref, b1a_ref, b1g_ref, wdn_ref, gf_ref,
                       o_ref, upa_ref, upg_ref, hn_ref, acc_ref, *, final_norm):
    c = pl.program_id(0)

    @pl.when(c == 0)
    def _():
        x = x_ref[...]
        hn_ref[...] = _rms_rows(x, g_ref[...]).astype(BF16)
        acc_ref[...] = x

    def conv(w_ref, cw_ref, cb_ref, b0_ref, b1_ref, up_ref):
        up = _dot(hn_ref[...], w_ref[...])
        up_ref[...] = up
        return b0_ref[...] * cw_ref[0:1, :] + b1_ref[...] * cw_ref[1:2, :] + up * cw_ref[2:3, :] + cb_ref[...]

    a = conv(wa_ref, cwa_ref, cba_ref, b0a_ref, b1a_ref, upa_ref)
    gt = conv(wg_ref, cwg_ref, cbg_ref, b0g_ref, b1g_ref, upg_ref)
    act = a * (gt * _sigmoid(gt))
    acc_ref[...] += _dot(act.astype(BF16), wdn_ref[...])

    @pl.when(c == pl.num_programs(0) - 1)
    def _():
        acc = acc_ref[...]
        if final_norm:
            acc = _rms_rows(acc, gf_ref[...])
        o_ref[...] = acc


def _ffn_decode(x, g, wup, cw, cb, wdn, gf, buf0, buf1, *, final_norm):
    rows = x.shape[0]
    fc = FF_CHUNK
    full = lambda shape: pl.BlockSpec(shape, lambda c: (0, 0))
    a_col = lambda r: pl.BlockSpec((r, fc), lambda c: (0, c))
    g_col = lambda r: pl.BlockSpec((r, fc), lambda c: (0, N_FF_CHUNKS + c))
    return pl.pallas_call(
        functools.partial(_ffn_decode_kernel, final_norm=final_norm), grid=(N_FF_CHUNKS,),
        in_specs=[full((rows, D_MODEL)), full((1, D_MODEL)),
                  a_col(D_MODEL), g_col(D_MODEL), a_col(CONV_W), g_col(CONV_W), a_col(1), g_col(1),
                  a_col(rows), g_col(rows), a_col(rows), g_col(rows),
                  pl.BlockSpec((fc, D_MODEL), lambda c: (c, 0)), full((1, D_MODEL))],
        out_specs=[full((rows, D_MODEL)), a_col(rows), a_col(rows)],
        out_shape=[jax.ShapeDtypeStruct(x.shape, F32),
                   jax.ShapeDtypeStruct((rows, D_FF), F32),
                   jax.ShapeDtypeStruct((rows, D_FF), F32)],
        scratch_shapes=[pltpu.VMEM((rows, D_MODEL), BF16), pltpu.VMEM((rows, D_MODEL), F32)],
        compiler_params=_cparams(("arbitrary",)), name="ffn_decode",
    )(x, g, wup, wup, cw, cw, cb, cb, buf0, buf0, buf1, buf1, wdn, gf)


def _layer_weights(P, l):
    w_in = P['w_in'][l]
    w_main = jnp.concatenate([w_in[:, 0:1536], w_in[:, 1552:3600], w_in[:, 3608:7192]], axis=1).astype(BF16)
    w_small = jnp.concatenate([w_in[:, 1536:1552], w_in[:, 3600:3608],
                               jnp.zeros((D_MODEL, LANES - GLA_LOWRANK - 2 * ML_HEADS), F32)], axis=1).astype(BF16)
    wa = jnp.zeros((LANES, GLA_QK), F32).at[:GLA_LOWRANK].set(P['w_gla_alpha'][l]).astype(BF16)
    gbias = jnp.zeros((1, LANES), F32)
    gbias = gbias.at[0, SM_MI:SM_MI + ML_HEADS].set(P['b_mlstm_i'][l])
    gbias = gbias.at[0, SM_MF:SM_MF + ML_HEADS].set(P['b_mlstm_f'][l])
    eye = jnp.eye(S5_GB, dtype=F32)
    w_re = jnp.einsum('ngpc,gh->ngchp', P['s5_b_re'][l].reshape(S5_NB, S5_GB, S5_P, S5_GROUP), eye)
    w_im = jnp.einsum('ngpc,gh->ngchp', P['s5_b_im'][l].reshape(S5_NB, S5_GB, S5_P, S5_GROUP), eye)
    w_blk = jnp.concatenate([w_re.reshape(S5_NB, S5_UB, S5_MB), w_im.reshape(S5_NB, S5_UB, S5_MB)],
                            axis=-1).astype(BF16)
    ct_re = jnp.einsum('ngcp,gh->ngchp', P['s5_c_re'][l].reshape(S5_NB, S5_GB, S5_GROUP, S5_P), eye)
    ct_im = jnp.einsum('ngcp,gh->ngchp', P['s5_c_im'][l].reshape(S5_NB, S5_GB, S5_GROUP, S5_P), eye)
    ct_re = ct_re.reshape(S5_NB, S5_UB, S5_MB)
    ct_im = ct_im.reshape(S5_NB, S5_UB, S5_MB)
    lam_re = P['s5_lam_re'][l].reshape(S5_NB, 1, S5_MB)
    lam_im = P['s5_lam_im'][l].reshape(S5_NB, 1, S5_MB)
    log_dt = jnp.broadcast_to(P['s5_log_dt'][l][:, None], (S5_GROUPS, S5_P)).reshape(S5_NB, 1, S5_MB)
    par, cfold = _s5_prep(lam_re, lam_im, log_dt, ct_re, ct_im)
    row = lambda a: a.reshape(1, -1)
    return dict(
        norm_mix=row(P['norm_mix'][l]), w_main=w_main, w_small=w_small, wa=wa,
        ba=row(P['b_gla_alpha'][l]), gla_norm=row(P['gla_head_norm'][l]), gbias=gbias,
        ml_norm=row(P['mlstm_head_norm'][l]), w_blk=w_blk, ct_re=ct_re.astype(BF16),
        ct_im=ct_im.astype(BF16), par=par, cfold=cfold, s5_d=row(P['s5_d'][l]),
        w_glu=P['s5_w_glu'][l].astype(BF16), w_a=P['w_branch_a'][l].astype(BF16),
        w_b=P['w_branch_b'][l].astype(BF16), w_c=P['w_branch_c'][l].astype(BF16),
        w_out=P['w_out'][l].astype(BF16), norm_cross=row(P['norm_cross'][l]),
        w_cq=P['w_cq'][l].astype(BF16), w_co=P['w_co'][l].astype(BF16),
        norm_ffn=row(P['norm_ffn'][l]), w_up=P['w_ffn_up'][l].astype(BF16),
        conv_w=P['ffn_conv_w'][l], conv_b=row(P['ffn_conv_b'][l]),
        w_down=P['w_ffn_down'][l].astype(BF16), norm_final=row(P['norm_final']),
    )


def _prompt_trunk(x_prompt, mem_k, mem_v, W, *, tt_mix, tt_s5, bm, tq, tt_ffn):
    batch, seq, _ = x_prompt.shape
    depth = len(W)
    x = x_prompt.reshape(batch * seq, D_MODEL)
    outs = []
    for l, w in enumerate(W):
        z, zs = _norm_matmul(x, w['norm_mix'], w['w_main'], w['w_small'], bm=min(bm * 2, batch * seq), bn=1024)
        ya, st = _gla_prompt(z, zs, w['wa'], w['ba'], w['gla_norm'], batch=batch, seq=seq, tt=tt_mix)
        yb, c, n, m = _mlstm_prompt(z, zs, w['gbias'], w['ml_norm'], batch=batch, seq=seq, tt=tt_mix)
        ypre, hfin = _s5_prompt(z.reshape(batch, seq, Z_MAIN), w['w_blk'], w['cfold'], w['par'],
                                batch=batch, seq=seq, tt=tt_s5)
        x = _merge(ya, yb, ypre.reshape(batch * seq, S5_WIDTH), z, x, w['s5_d'], w['w_glu'],
                   w['w_a'], w['w_b'], w['w_c'], w['w_out'], bm=bm)
        x = _cross_prompt(x, w['norm_cross'], w['w_cq'], mem_k[l], mem_v[l], w['w_co'],
                          batch=batch, seq=seq, tq=tq)
        x, conv = _ffn_prompt(x, w['norm_ffn'], w['w_up'], w['conv_w'], w['conv_b'], w['w_down'],
                              w['norm_final'], batch=batch, seq=seq, tt=tt_ffn,
                              final_norm=(l == depth - 1))
        st4 = st.reshape(batch, GLA_HEADS, GLA_DV, GLA_HEADS, GLA_DK)
        gla = jnp.stack([st4[:, h, :, h, :] for h in range(GLA_HEADS)], axis=1).transpose(0, 1, 3, 2)
        h4 = hfin.reshape(S5_NB, batch, 2, S5_GB, S5_P)
        s5_re = h4[:, :, 0].transpose(1, 0, 2, 3).reshape(batch, S5_GROUPS, S5_P)
        s5_im = h4[:, :, 1].transpose(1, 0, 2, 3).reshape(batch, S5_GROUPS, S5_P)
        outs.append((gla, c.reshape(batch, ML_HEADS, ML_D, ML_D), n[:, :ML_HEADS, :],
                     m[:, :ML_HEADS, 0], s5_re, s5_im, conv))
    stacked = [jnp.stack([outs[l][i] for l in range(depth)]) for i in range(7)]
    return x.reshape(batch, seq, D_MODEL), stacked


def _sample_trunk(x_sample, cache_k, cache_v, states, W):
    rows = x_sample.shape[0]
    depth = len(W)
    x = x_sample.reshape(rows, D_MODEL)
    s_gla, s_c, s_n, s_m, s_re, s_im, s_conv = states
    outs = []
    for l, w in enumerate(W):
        z, zs = _norm_matmul(x, w['norm_mix'], w['w_main'], w['w_small'], bm=rows, bn=1024)
        ya, gla = _gla_decode(z, zs, w['wa'], w['ba'], w['gla_norm'], s_gla, layer=l, bb=16)
        m_pad = jnp.pad(s_m[l], ((0, 0), (0, LANES - ML_HEADS)))
        yb, c, n, m = _mlstm_decode(z, zs, w['gbias'], w['ml_norm'], s_c, s_n, m_pad, layer=l, bb=16)
        ypre, h_re, h_im = _s5_decode(z, w['w_blk'], w['ct_re'], w['ct_im'], w['par'],
                                      s_re[l].reshape(rows, S5_MODES), s_im[l].reshape(rows, S5_MODES))
        x = _merge(ya, yb, ypre, z, x, w['s5_d'], w['w_glu'], w['w_a'], w['w_b'], w['w_c'], w['w_out'],
                   bm=rows)
        x = _cross_decode(x, w['norm_cross'], w['w_cq'], cache_k, cache_v, w['w_co'], layer=l, bb=8)
        x, up_a, up_g = _ffn_decode(x, w['norm_ffn'], w['w_up'], w['conv_w'], w['conv_b'], w['w_down'],
                                    w['norm_final'], s_conv[l][:, 0, :], s_conv[l][:, 1, :],
                                    final_norm=(l == depth - 1))
        conv = jnp.stack([s_conv[l][:, 1, :], jnp.concatenate([up_a, up_g], axis=1)], axis=1)
        outs.append((gla, c, n, m[:, :ML_HEADS], h_re.reshape(rows, S5_GROUPS, S5_P),
                     h_im.reshape(rows, S5_GROUPS, S5_P), conv))
    stacked = [jnp.stack([outs[l][i] for l in range(depth)]) for i in range(7)]
    return x.reshape(rows, 1, D_MODEL), stacked


def kernel(x_prompt, x_sample, mem_prompt, cache_mem_k, cache_mem_v, state_gla, state_mlstm_c, state_mlstm_n, state_mlstm_m, state_s5_re, state_s5_im, state_ffn_conv, norm_mix, w_in, w_gla_alpha, b_gla_alpha, gla_head_norm, b_mlstm_i, b_mlstm_f, mlstm_head_norm, s5_lam_re, s5_lam_im, s5_log_dt, s5_b_re, s5_b_im, s5_c_re, s5_c_im, s5_d, s5_w_glu, w_branch_a, w_branch_b, w_branch_c, w_out, norm_cross, norm_mem, w_cq, w_ck, w_cv, w_co, norm_ffn, w_ffn_up, ffn_conv_w, ffn_conv_b, w_ffn_down, norm_final):
    P = dict(norm_mix=norm_mix, w_in=w_in, w_gla_alpha=w_gla_alpha, b_gla_alpha=b_gla_alpha,
             gla_head_norm=gla_head_norm, b_mlstm_i=b_mlstm_i, b_mlstm_f=b_mlstm_f,
             mlstm_head_norm=mlstm_head_norm, s5_lam_re=s5_lam_re, s5_lam_im=s5_lam_im,
             s5_log_dt=s5_log_dt, s5_b_re=s5_b_re, s5_b_im=s5_b_im, s5_c_re=s5_c_re,
             s5_c_im=s5_c_im, s5_d=s5_d, s5_w_glu=s5_w_glu, w_branch_a=w_branch_a,
             w_branch_b=w_branch_b, w_branch_c=w_branch_c, w_out=w_out, norm_cross=norm_cross,
             w_cq=w_cq, w_co=w_co, norm_ffn=norm_ffn, w_ffn_up=w_ffn_up, ffn_conv_w=ffn_conv_w,
             ffn_conv_b=ffn_conv_b, w_ffn_down=w_ffn_down, norm_final=norm_final)
    depth = w_in.shape[0]
    W = [_layer_weights(P, l) for l in range(depth)]
    batch, mem_len, _ = mem_prompt.shape
    mem2 = mem_prompt.reshape(batch * mem_len, D_MODEL)
    mem_k, mem_v = [], []
    for l in range(depth):
        w_kv = jnp.concatenate([w_ck[l], w_cv[l]], axis=1).astype(BF16)
        kv = _norm_matmul(mem2, norm_mem[l].reshape(1, -1), w_kv, bm=1024, bn=1024)
        mem_k.append(kv[:, :D_MODEL])
        mem_v.append(kv[:, D_MODEL:])
    y_prompt, p_states = _prompt_trunk(x_prompt, mem_k, mem_v, W, tt_mix=256, tt_s5=256, bm=512, tq=512,
                                       tt_ffn=512)
    p_mem_k = jnp.stack(mem_k).reshape(depth, batch, mem_len, X_HEADS, X_DH)
    p_mem_v = jnp.stack(mem_v).reshape(depth, batch, mem_len, X_HEADS, X_DH)
    y_sample, s_states = _sample_trunk(
        x_sample, cache_mem_k, cache_mem_v,
        (state_gla, state_mlstm_c, state_mlstm_n, state_mlstm_m, state_s5_re, state_s5_im, state_ffn_conv), W)
    return (y_prompt, y_sample, *p_states, p_mem_k, p_mem_v, *s_states)
```

```python
import functools

import jax
import jax.numpy as jnp
from jax import lax
from jax.experimental import pallas as pl
from jax.experimental.pallas import tpu as pltpu

F32 = jnp.float32
BF16 = jnp.bfloat16

D_MODEL = 1024
GLA_HEADS, GLA_DK, GLA_DV = 4, 64, 128
GLA_QK = GLA_HEADS * GLA_DK
GLA_V = GLA_HEADS * GLA_DV
GLA_LOWRANK = 16
GLA_TAU = 16.0
ML_HEADS, ML_D = 4, 128
ML_W = ML_HEADS * ML_D
S5_GROUP, S5_GROUPS, S5_P = 16, 32, 64
S5_WIDTH = S5_GROUP * S5_GROUPS
S5_MODES = S5_GROUPS * S5_P
S5_NB = 4
S5_GB = S5_GROUPS // S5_NB
S5_MB = S5_MODES // S5_NB
S5_UB = S5_WIDTH // S5_NB
MEM_LEN = 256
X_HEADS = 4
X_DH = D_MODEL // X_HEADS
D_FF = 2816
CONV_W = 3
EPS = 1e-6

CHUNK = 256
LANES = 128
FF_CHUNK = 256
N_FF_CHUNKS = D_FF // FF_CHUNK

Z_MAIN = 7168
Z_QK, Z_GV, Z_GR, Z_MQ, Z_MK, Z_MV, Z_MO, Z_SU = range(8)
Z_ZA, Z_ZB, Z_ZC = 4, 5, 6
SM_MI = GLA_LOWRANK
SM_MF = GLA_LOWRANK + ML_HEADS

GLA_SAFE_DECAY = 80.0

VMEM_LIMIT = 56 * 1024 * 1024


def _cparams(sem):
    return pltpu.CompilerParams(dimension_semantics=sem, vmem_limit_bytes=VMEM_LIMIT)


def _dot(a, b):
    return jnp.dot(a, b, preferred_element_type=F32)


def _dot_nt(a, b):
    return lax.dot_general(a, b, (((1,), (1,)), ((), ())), preferred_element_type=F32)


def _dot_tn(a, b):
    return lax.dot_general(a, b, (((0,), (0,)), ((), ())), preferred_element_type=F32)


def _sigmoid(x):
    return 1.0 / (1.0 + jnp.exp(-x))


def _log_sigmoid(x):
    return jnp.minimum(x, 0.0) - jnp.log(1.0 + jnp.exp(-jnp.abs(x)))


def _gelu_tanh(x):
    return 0.5 * x * (1.0 + jnp.tanh(0.7978845608028654 * (x + 0.044715 * x * x * x)))


def _rms_rows(x, g):
    r = lax.rsqrt(jnp.mean(x * x, axis=-1, keepdims=True) + EPS)
    return (x * r) * g


def _dot_exact01(a01, x):
    hi = x.astype(BF16)
    r1 = x - hi.astype(F32)
    mid = r1.astype(BF16)
    lo = (r1 - mid.astype(F32)).astype(BF16)
    return _dot(a01, hi) + _dot(a01, mid) + _dot(a01, lo)


def _row_to_col(row, n):
    eye = (lax.broadcasted_iota(jnp.int32, (n, n), 0) == lax.broadcasted_iota(jnp.int32, (n, n), 1))
    return jnp.sum(jnp.where(eye, jnp.broadcast_to(row, (n, n)), 0.0), axis=1, keepdims=True)


def _norm_mm_kernel(x_ref, g_ref, w_ref, o_ref, hn_ref):
    @pl.when(pl.program_id(1) == 0)
    def _():
        hn_ref[...] = _rms_rows(x_ref[...], g_ref[...]).astype(BF16)
    o_ref[...] = _dot(hn_ref[...], w_ref[...]).astype(o_ref.dtype)


def _norm_mm_small_kernel(x_ref, g_ref, w_ref, ws_ref, o_ref, os_ref, hn_ref):
    @pl.when(pl.program_id(1) == 0)
    def _():
        hn = _rms_rows(x_ref[...], g_ref[...]).astype(BF16)
        hn_ref[...] = hn
        os_ref[...] = _dot(hn, ws_ref[...])
    o_ref[...] = _dot(hn_ref[...], w_ref[...]).astype(o_ref.dtype)


def _norm_matmul(x, g, w, w_small=None, *, bm, bn, out_dtype=F32):
    n, d = x.shape
    c = w.shape[1]
    grid = (n // bm, c // bn)
    x_spec = pl.BlockSpec((bm, d), lambda i, j: (i, 0))
    g_spec = pl.BlockSpec((1, d), lambda i, j: (0, 0))
    w_spec = pl.BlockSpec((d, bn), lambda i, j: (0, j))
    o_spec = pl.BlockSpec((bm, bn), lambda i, j: (i, j))
    scratch = [pltpu.VMEM((bm, d), BF16)]
    if w_small is None:
        return pl.pallas_call(
            _norm_mm_kernel, grid=grid, in_specs=[x_spec, g_spec, w_spec], out_specs=o_spec,
            out_shape=jax.ShapeDtypeStruct((n, c), out_dtype), scratch_shapes=scratch,
            compiler_params=_cparams(("arbitrary", "arbitrary")), name="norm_matmul",
        )(x, g, w)
    cs = w_small.shape[1]
    return pl.pallas_call(
        _norm_mm_small_kernel, grid=grid,
        in_specs=[x_spec, g_spec, w_spec, pl.BlockSpec((d, cs), lambda i, j: (0, 0))],
        out_specs=[o_spec, pl.BlockSpec((bm, cs), lambda i, j: (i, 0))],
        out_shape=[jax.ShapeDtypeStruct((n, c), out_dtype), jax.ShapeDtypeStruct((n, cs), F32)],
        scratch_shapes=scratch,
        compiler_params=_cparams(("arbitrary", "arbitrary")), name="norm_matmul_in",
    )(x, g, w, w_small)


def _head_norm(oh, gn_h):
    r = lax.rsqrt(jnp.mean(oh * oh, axis=-1, keepdims=True) + EPS)
    return oh * r * gn_h


def _gla_log_decay(small, wa_ref, ba_ref):
    a_pre = _dot(small.astype(BF16), wa_ref[...]) + ba_ref[...]
    return _log_sigmoid(a_pre) * (1.0 / GLA_TAU)


def _gla_prompt_kernel(qk_ref, v_ref, gr_ref, sm_ref, wa_ref, ba_ref, gn_ref,
                       y_ref, st_ref, state_ref, b_ref, k_ref, *, n_chunks):
    t = pl.program_id(1)
    L = CHUNK

    @pl.when(t == 0)
    def _():
        state_ref[...] = jnp.zeros_like(state_ref)

    ri = lax.broadcasted_iota(jnp.int32, (L, L), 0)
    ci = lax.broadcasted_iota(jnp.int32, (L, L), 1)
    tril = jnp.where(ri >= ci, 1.0, 0.0).astype(BF16)
    rw = lax.broadcasted_iota(jnp.int32, (L, GLA_HEADS * L), 0)
    cw = lax.broadcasted_iota(jnp.int32, (L, GLA_HEADS * L), 1)
    causal_wide = (cw % L) <= rw
    kr = lax.broadcasted_iota(jnp.int32, (GLA_HEADS * L, GLA_QK), 0)
    kc = lax.broadcasted_iota(jnp.int32, (GLA_HEADS * L, GLA_QK), 1)
    kk_mask = (kr // L) == (kc // GLA_DK)
    vr = lax.broadcasted_iota(jnp.int32, (GLA_HEADS * L, GLA_V), 0)
    vc = lax.broadcasted_iota(jnp.int32, (GLA_HEADS * L, GLA_V), 1)
    vv_mask = (vr // L) == (vc // GLA_DV)
    sr = lax.broadcasted_iota(jnp.int32, (GLA_V, GLA_QK), 0)
    sc = lax.broadcasted_iota(jnp.int32, (GLA_V, GLA_QK), 1)
    st_mask = (sr // GLA_DV) == (sc // GLA_DK)
    hr = lax.broadcasted_iota(jnp.int32, (GLA_QK, LANES), 0)
    hc = lax.broadcasted_iota(jnp.int32, (GLA_QK, LANES), 1)
    head_ones = jnp.where((hr // GLA_DK) == hc, 1.0, 0.0).astype(BF16)
    pr = lax.broadcasted_iota(jnp.int32, (LANES, GLA_HEADS * L), 0)
    pc = lax.broadcasted_iota(jnp.int32, (LANES, GLA_HEADS * L), 1)

    def chunk(c, carry):
        r0 = pl.multiple_of(c * L, L)
        rows = pl.ds(r0, L)
        qk = qk_ref[rows, :].astype(F32)
        q = qk[:, :GLA_QK] * (GLA_DK ** -0.5)
        k = qk[:, GLA_QK:]
        v16 = v_ref[rows, :].astype(BF16)
        g = _gla_log_decay(sm_ref[rows, :], wa_ref, ba_ref)
        b = _dot_exact01(tril, g)
        b_end = b[L - 1:L, :]
        qe = q * jnp.exp(b)
        k_dec = k * jnp.exp(b_end - b)
        qe16 = qe.astype(BF16)

        def fast_att(_):
            k_til = k * jnp.exp(-b)
            kk = jnp.where(kk_mask, jnp.concatenate([k_til] * GLA_HEADS, axis=0), 0.0)
            return _dot_nt(qe16, kk.astype(BF16))

        def direct_att(_):
            b_ref[...] = b
            k_ref[...] = k

            def col(j, acc):
                kj = k_ref[pl.ds(j, 1), :]
                bj = b_ref[pl.ds(j, 1), :]
                prod = q * kj * jnp.exp(jnp.minimum(b - bj, 0.0))
                red = _dot(prod.astype(BF16), head_ones)
                place = jnp.where((pc == pr * L + j) & (pr < GLA_HEADS), 1.0, 0.0).astype(BF16)
                return acc + _dot(red.astype(BF16), place)

            return lax.fori_loop(0, L, col, jnp.zeros((L, GLA_HEADS * L), F32))

        safe = jnp.max(-b_end) <= GLA_SAFE_DECAY
        att = lax.cond(safe, fast_att, direct_att, 0)
        att = jnp.where(causal_wide, att, 0.0)

        vv = jnp.where(vv_mask, jnp.concatenate([v16] * GLA_HEADS, axis=0), jnp.zeros((), BF16))
        st = state_ref[...]
        o = _dot(att.astype(BF16), vv) + _dot_nt(qe16, st.astype(BF16))
        upd = _dot_tn(v16, k_dec.astype(BF16))
        state_ref[...] = jnp.where(st_mask, st * jnp.exp(b_end) + upd, 0.0)

        gr = gr_ref[rows, :].astype(F32)
        for h in range(GLA_HEADS):
            sl = slice(h * GLA_DV, (h + 1) * GLA_DV)
            gate = gr[:, sl]
            yh = _head_norm(o[:, sl], gn_ref[:, sl]) * (gate * _sigmoid(gate))
            y_ref[rows, sl] = yh.astype(BF16)
        return carry

    lax.fori_loop(0, n_chunks, chunk, 0)

    @pl.when(t == pl.num_programs(1) - 1)
    def _():
        st_ref[0] = state_ref[...]


def _gla_prompt(z, zs, wa, ba, gn, *, batch, seq, tt):
    nt = seq // tt
    row = lambda b, t: b * nt + t
    blk = lambda j: pl.BlockSpec((tt, 512), lambda b, t: (row(b, t), j))
    const = lambda shape: pl.BlockSpec(shape, lambda b, t: (0,) * len(shape))
    return pl.pallas_call(
        functools.partial(_gla_prompt_kernel, n_chunks=tt // CHUNK),
        grid=(batch, nt),
        in_specs=[blk(Z_QK), blk(Z_GV), blk(Z_GR),
                  pl.BlockSpec((tt, LANES), lambda b, t: (row(b, t), 0)),
                  const((LANES, GLA_QK)), const((1, GLA_QK)), const((1, GLA_V))],
        out_specs=[pl.BlockSpec((tt, GLA_V), lambda b, t: (row(b, t), 0)),
                   pl.BlockSpec((1, GLA_V, GLA_QK), lambda b, t: (b, 0, 0))],
        out_shape=[jax.ShapeDtypeStruct((batch * seq, GLA_V), BF16),
                   jax.ShapeDtypeStruct((batch, GLA_V, GLA_QK), F32)],
        scratch_shapes=[pltpu.VMEM((GLA_V, GLA_QK), F32), pltpu.VMEM((CHUNK, GLA_QK), F32),
                        pltpu.VMEM((CHUNK, GLA_QK), F32)],
        compiler_params=_cparams(("arbitrary", "arbitrary")), name="gla_prompt",
    )(z, z, z, zs, wa, ba, gn)


def _mlstm_prompt_kernel(q_ref, k_ref, v_ref, og_ref, sm_ref, gb_ref, gn_ref,
                         y_ref, c_out, n_out, m_out, c_ref, n_ref, m_ref, *, n_chunks):
    t = pl.program_id(1)
    L = CHUNK

    @pl.when(t == 0)
    def _():
        c_ref[...] = jnp.zeros_like(c_ref)
        n_ref[...] = jnp.zeros_like(n_ref)
        m_ref[...] = jnp.zeros_like(m_ref)

    ri = lax.broadcasted_iota(jnp.int32, (L, L), 0)
    ci = lax.broadcasted_iota(jnp.int32, (L, L), 1)
    lower = ri >= ci
    eye = ri == ci

    def chunk(c, carry):
        r0 = pl.multiple_of(c * L, L)
        rows = pl.ds(r0, L)
        sm = sm_ref[rows, :] + gb_ref[...]
        lf_all = _log_sigmoid(sm)
        for h in range(ML_HEADS):
            sl = slice(h * ML_D, (h + 1) * ML_D)
            li_col = sm[:, SM_MI + h:SM_MI + h + 1]
            lf_col = lf_all[:, SM_MF + h:SM_MF + h + 1]
            lf_b = jnp.broadcast_to(lf_col, (L, L))
            li_row = jnp.sum(jnp.where(eye, jnp.broadcast_to(li_col, (L, L)), 0.0), axis=0, keepdims=True)
            lf_row = jnp.sum(jnp.where(eye, lf_b, 0.0), axis=0, keepdims=True)
            b_row = jnp.sum(jnp.where(ri <= ci, lf_b, 0.0), axis=0, keepdims=True)
            b_col = jnp.sum(jnp.where(lower, jnp.broadcast_to(lf_row, (L, L)), 0.0), axis=1, keepdims=True)
            m_prev = m_ref[h:h + 1, 0:1]
            w = jnp.where(lower, b_col - b_row + li_row, -jnp.inf)
            m_tok = jnp.maximum(b_col + m_prev, jnp.max(w, axis=1, keepdims=True))
            a_inter = jnp.exp(b_col + m_prev - m_tok)
            qh = q_ref[rows, sl].astype(F32) * (ML_D ** -0.5)
            kh16 = k_ref[rows, sl].astype(BF16)
            kh = kh16.astype(F32)
            qh16, vh16 = qh.astype(BF16), v_ref[rows, sl].astype(BF16)
            s = _dot_nt(qh16, kh16) * jnp.exp(w - m_tok)
            c_h = c_ref[sl, :]
            n_h = n_ref[h:h + 1, :]
            num = _dot(s.astype(BF16), vh16) + a_inter * _dot(qh16, c_h.astype(BF16))
            den = jnp.sum(s, axis=1, keepdims=True) + a_inter * jnp.sum(qh * n_h, axis=1, keepdims=True)
            hh = num / jnp.maximum(jnp.abs(den), jnp.exp(-m_tok))
            og = og_ref[rows, sl].astype(F32)
            y_ref[rows, sl] = (_head_norm(hh, gn_ref[:, sl]) * _sigmoid(og)).astype(BF16)
            b_end = b_col[L - 1:L, :]
            w_end = b_end - b_col + li_col
            m_new = jnp.maximum(b_end + m_prev, jnp.max(w_end, axis=0, keepdims=True))
            e_inter = jnp.exp(b_end + m_prev - m_new)
            kd = kh * jnp.exp(w_end - m_new)
            c_ref[sl, :] = e_inter * c_h + _dot_tn(kd.astype(BF16), vh16)
            n_ref[h:h + 1, :] = e_inter * n_h + jnp.sum(kd, axis=0, keepdims=True)
            m_ref[h:h + 1, :] = jnp.broadcast_to(m_new, (1, LANES))
        return carry

    lax.fori_loop(0, n_chunks, chunk, 0)

    @pl.when(t == pl.num_programs(1) - 1)
    def _():
        c_out[0] = c_ref[...]
        n_out[0] = n_ref[...]
        m_out[0] = m_ref[...]


def _mlstm_prompt(z, zs, gbias, gn, *, batch, seq, tt):
    nt = seq // tt
    row = lambda b, t: b * nt + t
    blk = lambda j: pl.BlockSpec((tt, 512), lambda b, t: (row(b, t), j))
    const = lambda shape: pl.BlockSpec(shape, lambda b, t: (0,) * len(shape))
    return pl.pallas_call(
        functools.partial(_mlstm_prompt_kernel, n_chunks=tt // CHUNK),
        grid=(batch, nt),
        in_specs=[blk(Z_MQ), blk(Z_MK), blk(Z_MV), blk(Z_MO),
                  pl.BlockSpec((tt, LANES), lambda b, t: (row(b, t), 0)),
                  const((1, LANES)), const((1, ML_W))],
        out_specs=[pl.BlockSpec((tt, ML_W), lambda b, t: (row(b, t), 0)),
                   pl.BlockSpec((1, ML_W, ML_D), lambda b, t: (b, 0, 0)),
                   pl.BlockSpec((1, 8, LANES), lambda b, t: (b, 0, 0)),
                   pl.BlockSpec((1, 8, LANES), lambda b, t: (b, 0, 0))],
        out_shape=[jax.ShapeDtypeStruct((batch * seq, ML_W), BF16),
                   jax.ShapeDtypeStruct((batch, ML_W, ML_D), F32),
                   jax.ShapeDtypeStruct((batch, 8, LANES), F32),
                   jax.ShapeDtypeStruct((batch, 8, LANES), F32)],
        scratch_shapes=[pltpu.VMEM((ML_W, ML_D), F32), pltpu.VMEM((8, LANES), F32),
                        pltpu.VMEM((8, LANES), F32)],
        compiler_params=_cparams(("arbitrary", "arbitrary")), name="mlstm_prompt",
    )(z, z, z, z, zs, gbias, gn)


def _s5_prep_kernel(lre_ref, lim_ref, ldt_ref, ctr_ref, cti_ref, par_ref, cf_ref):
    lam_re = lre_ref[0]
    lam_im = lim_ref[0]
    dt = jnp.exp(ldt_ref[0])
    mag = jnp.exp(lam_re * dt)
    lb_re = mag * jnp.cos(lam_im * dt)
    lb_im = mag * jnp.sin(lam_im * dt)
    nr = lb_re - 1.0
    den = lam_re * lam_re + lam_im * lam_im
    f_re = (nr * lam_re + lb_im * lam_im) / den
    f_im = (lb_im * lam_re - nr * lam_im) / den
    par_ref[0] = jnp.concatenate([lb_re, lb_im, f_re, f_im, jnp.zeros((4, S5_MB), F32)], axis=0)
    ct_re = ctr_ref[0]
    ct_im = cti_ref[0]
    cf_ref[0, :, :S5_MB] = (ct_re * f_re - ct_im * f_im).astype(BF16)
    cf_ref[0, :, S5_MB:] = (-(ct_re * f_im + ct_im * f_re)).astype(BF16)


def _s5_prep(lam_re, lam_im, log_dt, ct_re, ct_im):
    vec = pl.BlockSpec((1, 1, S5_MB), lambda n: (n, 0, 0))
    mat = pl.BlockSpec((1, S5_UB, S5_MB), lambda n: (n, 0, 0))
    return pl.pallas_call(
        _s5_prep_kernel, grid=(S5_NB,),
        in_specs=[vec, vec, vec, mat, mat],
        out_specs=[pl.BlockSpec((1, 8, S5_MB), lambda n: (n, 0, 0)),
                   pl.BlockSpec((1, S5_UB, 2 * S5_MB), lambda n: (n, 0, 0))],
        out_shape=[jax.ShapeDtypeStruct((S5_NB, 8, S5_MB), F32),
                   jax.ShapeDtypeStruct((S5_NB, S5_UB, 2 * S5_MB), BF16)],
        compiler_params=_cparams(("arbitrary",)), name="s5_prep",
    )(lam_re, lam_im, log_dt, ct_re, ct_im)


def _s5_prompt_kernel(u_ref, w_ref, cf_ref, par_ref, y_ref, h_out, x_ref, h_ref, ub_ref, ut_ref, *, batch, tt):
    t_blk = pl.program_id(1)

    @pl.when(t_blk == 0)
    def _():
        h_ref[...] = jnp.zeros_like(h_ref)

    nlb = S5_MB // LANES
    ub_ref[...] = u_ref[...].astype(F32).reshape(batch * tt, S5_UB)

    def interleave(t, carry):
        ut_ref[pl.ds(pl.multiple_of(t * batch, batch), batch), :] = ub_ref[pl.ds(t, batch, stride=tt), :]
        return carry

    lax.fori_loop(0, tt, interleave, 0, unroll=8)
    x = _dot(ut_ref[...].astype(BF16), w_ref[0])
    for j in range(2 * nlb):
        x_ref[j] = x[:, j * LANES:(j + 1) * LANES]

    par = par_ref[0]
    lr = [jnp.broadcast_to(par[0:1, j * LANES:(j + 1) * LANES], (batch, LANES)) for j in range(nlb)]
    li = [jnp.broadcast_to(par[1:2, j * LANES:(j + 1) * LANES], (batch, LANES)) for j in range(nlb)]

    def step(t, carry):
        rows = pl.ds(pl.multiple_of(t * batch, batch), batch)
        new = []
        for j in range(nlb):
            hr, hi = carry[j], carry[nlb + j]
            nr = lr[j] * hr - li[j] * hi + x_ref[j, rows, :]
            ni = lr[j] * hi + li[j] * hr + x_ref[nlb + j, rows, :]
            x_ref[j, rows, :] = nr
            x_ref[nlb + j, rows, :] = ni
            new.append((nr, ni))
        return tuple(n[0] for n in new) + tuple(n[1] for n in new)

    h0 = tuple(h_ref[:, j * LANES:(j + 1) * LANES] for j in range(2 * nlb))
    hfin = lax.fori_loop(0, tt, step, h0, unroll=8)
    for j in range(2 * nlb):
        h_ref[:, j * LANES:(j + 1) * LANES] = hfin[j]
    hr = jnp.concatenate(hfin[:nlb], axis=1)
    hi = jnp.concatenate(hfin[nlb:], axis=1)

    hall = jnp.concatenate([x_ref[j].astype(BF16) for j in range(2 * nlb)], axis=1)
    ut_ref[...] = _dot_nt(hall, cf_ref[0])
    for b in range(batch):
        y_ref[b] = ut_ref[pl.ds(b, tt, stride=batch), :]

    @pl.when(t_blk == pl.num_programs(1) - 1)
    def _():
        fr = par[2:3, :]
        fi = par[3:4, :]
        h_out[0, :, :S5_MB] = fr * hr - fi * hi
        h_out[0, :, S5_MB:] = fr * hi + fi * hr


def _s5_prompt(z3, w_blk, cfold, par, *, batch, seq, tt):
    nt = seq // tt
    u_col0 = Z_SU * 512 // S5_UB
    return pl.pallas_call(
        functools.partial(_s5_prompt_kernel, batch=batch, tt=tt),
        grid=(S5_NB, nt),
        in_specs=[pl.BlockSpec((batch, tt, S5_UB), lambda n, t: (0, t, u_col0 + n)),
                  pl.BlockSpec((1, S5_UB, 2 * S5_MB), lambda n, t: (n, 0, 0)),
                  pl.BlockSpec((1, S5_UB, 2 * S5_MB), lambda n, t: (n, 0, 0)),
                  pl.BlockSpec((1, 8, S5_MB), lambda n, t: (n, 0, 0))],
        out_specs=[pl.BlockSpec((batch, tt, S5_UB), lambda n, t: (0, t, n)),
                   pl.BlockSpec((1, batch, 2 * S5_MB), lambda n, t: (n, 0, 0))],
        out_shape=[jax.ShapeDtypeStruct((batch, seq, S5_WIDTH), F32),
                   jax.ShapeDtypeStruct((S5_NB, batch, 2 * S5_MB), F32)],
        scratch_shapes=[pltpu.VMEM((2 * S5_MB // LANES, batch * tt, LANES), F32),
                        pltpu.VMEM((batch, 2 * S5_MB), F32),
                        pltpu.VMEM((batch * tt, S5_UB), F32), pltpu.VMEM((batch * tt, S5_UB), F32)],
        compiler_params=_cparams(("arbitrary", "arbitrary")), name="s5_prompt",
    )(z3, w_blk, cfold, par)


def _s5_decode_kernel(u_ref, w_ref, ctr_ref, cti_ref, par_ref, hr_ref, hi_ref, y_ref, hro_ref, hio_ref):
    x = _dot(u_ref[...].astype(BF16), w_ref[0])
    xr, xi = x[:, :S5_MB], x[:, S5_MB:]
    par = par_ref[0]
    lr, li, fr, fi = par[0:1, :], par[1:2, :], par[2:3, :], par[3:4, :]
    h0r, h0i = hr_ref[...], hi_ref[...]
    hr = lr * h0r - li * h0i + (fr * xr - fi * xi)
    hi = lr * h0i + li * h0r + (fr * xi + fi * xr)
    hro_ref[...] = hr
    hio_ref[...] = hi
    y_ref[...] = _dot_nt(hr.astype(BF16), ctr_ref[0]) - _dot_nt(hi.astype(BF16), cti_ref[0])


def _s5_decode(z, w_blk, ct_re, ct_im, par, h_re, h_im):
    rows = z.shape[0]
    u_col0 = Z_SU * 512 // S5_UB
    hspec = pl.BlockSpec((rows, S5_MB), lambda n: (0, n))
    mat = pl.BlockSpec((1, S5_UB, S5_MB), lambda n: (n, 0, 0))
    return pl.pallas_call(
        _s5_decode_kernel, grid=(S5_NB,),
        in_specs=[pl.BlockSpec((rows, S5_UB), lambda n: (0, u_col0 + n)),
                  pl.BlockSpec((1, S5_UB, 2 * S5_MB), lambda n: (n, 0, 0)),
                  mat, mat,
                  pl.BlockSpec((1, 8, S5_MB), lambda n: (n, 0, 0)), hspec, hspec],
        out_specs=[pl.BlockSpec((rows, S5_UB), lambda n: (0, n)), hspec, hspec],
        out_shape=[jax.ShapeDtypeStruct((rows, S5_WIDTH), F32),
                   jax.ShapeDtypeStruct((rows, S5_MODES), F32),
                   jax.ShapeDtypeStruct((rows, S5_MODES), F32)],
        compiler_params=_cparams(("arbitrary",)), name="s5_decode",
    )(z, w_blk, ct_re, ct_im, par, h_re, h_im)


def _gla_decode_kernel(qk_ref, v_ref, gr_ref, sm_ref, wa_ref, ba_ref, gn_ref, s_ref,
                       y_ref, so_ref, eg_ref, o_ref, *, bb):
    g = _gla_log_decay(sm_ref[...], wa_ref, ba_ref)
    eg_ref[...] = jnp.exp(g)

    def per_row(b, carry):
        row = pl.ds(b, 1)
        eg_r = eg_ref[row, :]
        qk_r = qk_ref[row, :]
        v_r = v_ref[row, :]
        q_r = qk_r[:, :GLA_QK] * (GLA_DK ** -0.5)
        k_r = qk_r[:, GLA_QK:]
        o_heads = []
        for h in range(GLA_HEADS):
            ks = slice(h * GLA_DK, (h + 1) * GLA_DK)
            vs = slice(h * GLA_DV, (h + 1) * GLA_DV)
            s_new = _row_to_col(eg_r[:, ks], GLA_DK) * s_ref[b, h] + _row_to_col(k_r[:, ks], GLA_DK) * v_r[:, vs]
            so_ref[b, h] = s_new
            o_heads.append(jnp.sum(_row_to_col(q_r[:, ks], GLA_DK) * s_new, axis=0, keepdims=True))
        o_ref[row, :] = jnp.concatenate(o_heads, axis=1)
        return carry

    lax.fori_loop(0, bb, per_row, 0)
    o = o_ref[...]
    gr = gr_ref[...]
    for h in range(GLA_HEADS):
        vs = slice(h * GLA_DV, (h + 1) * GLA_DV)
        gate = gr[:, vs]
        y_ref[:, vs] = (_head_norm(o[:, vs], gn_ref[:, vs]) * (gate * _sigmoid(gate))).astype(BF16)


def _gla_decode(z, zs, wa, ba, gn, states, *, layer, bb):
    rows = z.shape[0]
    blk = lambda j: pl.BlockSpec((bb, 512), lambda i: (i, j))
    const = lambda shape: pl.BlockSpec(shape, lambda i: (0,) * len(shape))
    sblock = (bb, GLA_HEADS, GLA_DK, GLA_DV)
    return pl.pallas_call(
        functools.partial(_gla_decode_kernel, bb=bb), grid=(rows // bb,),
        in_specs=[blk(Z_QK), blk(Z_GV), blk(Z_GR), pl.BlockSpec((bb, LANES), lambda i: (i, 0)),
                  const((LANES, GLA_QK)), const((1, GLA_QK)), const((1, GLA_V)),
                  pl.BlockSpec((None,) + sblock, lambda i: (layer, i, 0, 0, 0))],
        out_specs=[pl.BlockSpec((bb, GLA_V), lambda i: (i, 0)),
                   pl.BlockSpec(sblock, lambda i: (i, 0, 0, 0))],
        out_shape=[jax.ShapeDtypeStruct((rows, GLA_V), BF16),
                   jax.ShapeDtypeStruct(states.shape[1:], F32)],
        scratch_shapes=[pltpu.VMEM((bb, GLA_QK), F32), pltpu.VMEM((bb, GLA_V), F32)],
        compiler_params=_cparams(("arbitrary",)), name="gla_decode",
    )(z, z, z, zs, wa, ba, gn, states)


def _mlstm_decode_kernel(q_ref, k_ref, v_ref, og_ref, sm_ref, gb_ref, gn_ref, c_ref, n_ref, m_ref,
                         y_ref, co_ref, no_ref, mo_ref, li_ref, lf_ref, h_ref, *, bb):
    sm = sm_ref[...] + gb_ref[...]
    li_ref[...] = sm
    lf_ref[...] = _log_sigmoid(sm)
    lane = lax.broadcasted_iota(jnp.int32, (1, LANES), 1)

    def per_row(b, carry):
        row = pl.ds(b, 1)
        li_r = li_ref[row, :]
        lf_r = lf_ref[row, :]
        m_r = m_ref[row, :]
        q_all = q_ref[row, :] * (ML_D ** -0.5)
        k_all = k_ref[row, :]
        v_all = v_ref[row, :]
        n_all = n_ref[b]
        m_new_row = jnp.zeros((1, LANES), F32)
        h_heads, n_heads = [], []
        for h in range(ML_HEADS):
            sl = slice(h * ML_D, (h + 1) * ML_D)
            li = li_r[:, SM_MI + h:SM_MI + h + 1]
            lf = lf_r[:, SM_MF + h:SM_MF + h + 1]
            m_prev = m_r[:, h:h + 1]
            m_new = jnp.maximum(lf + m_prev, li)
            a = jnp.exp(lf + m_prev - m_new)
            e = jnp.exp(li - m_new)
            k_r = k_all[:, sl]
            q_r = q_all[:, sl]
            c_new = a * c_ref[b, h] + (_row_to_col(k_r, ML_D) * e) * v_all[:, sl]
            n_new = a * n_all[h:h + 1, :] + e * k_r
            co_ref[b, h] = c_new
            n_heads.append(n_new)
            num = jnp.sum(_row_to_col(q_r, ML_D) * c_new, axis=0, keepdims=True)
            den = jnp.sum(q_r * n_new, axis=1, keepdims=True)
            h_heads.append(num / jnp.maximum(jnp.abs(den), jnp.exp(-m_new)))
            m_new_row = jnp.where(lane == h, m_new, m_new_row)
        no_ref[b] = jnp.concatenate(n_heads, axis=0)
        h_ref[row, :] = jnp.concatenate(h_heads, axis=1)
        mo_ref[row, :] = m_new_row
        return carry

    lax.fori_loop(0, bb, per_row, 0)
    hh = h_ref[...]
    og = og_ref[...]
    for h in range(ML_HEADS):
        sl = slice(h * ML_D, (h + 1) * ML_D)
        y_ref[:, sl] = (_head_norm(hh[:, sl], gn_ref[:, sl]) * _sigmoid(og[:, sl])).astype(BF16)


def _mlstm_decode(z, zs, gbias, gn, c_all, n_all, m_pad, *, layer, bb):
    rows = z.shape[0]
    blk = lambda j: pl.BlockSpec((bb, 512), lambda i: (i, j))
    const = lambda shape: pl.BlockSpec(shape, lambda i: (0,) * len(shape))
    cblock = (bb, ML_HEADS, ML_D, ML_D)
    nblock = (bb, ML_HEADS, ML_D)
    cspec = pl.BlockSpec(cblock, lambda i: (i, 0, 0, 0))
    nspec = pl.BlockSpec(nblock, lambda i: (i, 0, 0))
    mspec = pl.BlockSpec((bb, LANES), lambda i: (i, 0))
    return pl.pallas_call(
        functools.partial(_mlstm_decode_kernel, bb=bb), grid=(rows // bb,),
        in_specs=[blk(Z_MQ), blk(Z_MK), blk(Z_MV), blk(Z_MO), mspec,
                  const((1, LANES)), const((1, ML_W)),
                  pl.BlockSpec((None,) + cblock, lambda i: (layer, i, 0, 0, 0)),
                  pl.BlockSpec((None,) + nblock, lambda i: (layer, i, 0, 0)), mspec],
        out_specs=[pl.BlockSpec((bb, ML_W), lambda i: (i, 0)), cspec, nspec, mspec],
        out_shape=[jax.ShapeDtypeStruct((rows, ML_W), BF16),
                   jax.ShapeDtypeStruct(c_all.shape[1:], F32), jax.ShapeDtypeStruct(n_all.shape[1:], F32),
                   jax.ShapeDtypeStruct((rows, LANES), F32)],
        scratch_shapes=[pltpu.VMEM((bb, LANES), F32), pltpu.VMEM((bb, LANES), F32),
                        pltpu.VMEM((bb, ML_W), F32)],
        compiler_params=_cparams(("arbitrary",)), name="mlstm_decode",
    )(z, z, z, z, zs, gbias, gn, c_all, n_all, m_pad)


def _merge_kernel(ya_ref, yb_ref, yp_ref, u_ref, za_ref, zb_ref, zc_ref, x_ref,
                  d_ref, wg_ref, wa_ref, wb_ref, wc_ref, wo_ref, o_ref):
    yc = _gelu_tanh(yp_ref[...] + d_ref[...] * u_ref[...].astype(F32))
    yc = yc * _sigmoid(_dot(yc.astype(BF16), wg_ref[...]))
    m = _sigmoid(za_ref[...].astype(F32)) * _dot(ya_ref[...], wa_ref[...])
    m = m + _sigmoid(zb_ref[...].astype(F32)) * _dot(yb_ref[...], wb_ref[...])
    m = m + _sigmoid(zc_ref[...].astype(F32)) * _dot(yc.astype(BF16), wc_ref[...])
    o_ref[...] = x_ref[...] + _dot(m.astype(BF16), wo_ref[...])


def _merge(ya, yb, ypre, z, x, d, wg, wa, wb, wc, wo, *, bm):
    n = x.shape[0]
    r512 = pl.BlockSpec((bm, 512), lambda i: (i, 0))
    zblk = lambda j: pl.BlockSpec((bm, D_MODEL), lambda i: (i, j))
    const = lambda shape: pl.BlockSpec(shape, lambda i: (0, 0))
    return pl.pallas_call(
        _merge_kernel, grid=(n // bm,),
        in_specs=[r512, r512, r512, pl.BlockSpec((bm, 512), lambda i: (i, Z_SU)),
                  zblk(Z_ZA), zblk(Z_ZB), zblk(Z_ZC), pl.BlockSpec((bm, D_MODEL), lambda i: (i, 0)),
                  const((1, S5_WIDTH)), const((S5_WIDTH, S5_WIDTH)),
                  const((GLA_V, D_MODEL)), const((ML_W, D_MODEL)), const((S5_WIDTH, D_MODEL)),
                  const((D_MODEL, D_MODEL))],
        out_specs=pl.BlockSpec((bm, D_MODEL), lambda i: (i, 0)),
        out_shape=jax.ShapeDtypeStruct((n, D_MODEL), F32),
        compiler_params=_cparams(("arbitrary",)), name="merge",
    )(ya, yb, ypre, z, z, z, z, x, d, wg, wa, wb, wc, wo)


def _cross_prompt_kernel(x_ref, g_ref, wq_ref, mk_ref, mv_ref, wo_ref, o_ref):
    x = x_ref[...]
    q = _dot(_rms_rows(x, g_ref[...]).astype(BF16), wq_ref[...])
    heads = []
    for h in range(X_HEADS):
        sl = slice(h * X_DH, (h + 1) * X_DH)
        s = _dot_nt(q[:, sl].astype(BF16), mk_ref[:, sl].astype(BF16)) * (X_DH ** -0.5)
        s = s - jnp.max(s, axis=-1, keepdims=True)
        p = jnp.exp(s)
        p = p / jnp.sum(p, axis=-1, keepdims=True)
        heads.append(_dot(p.astype(BF16), mv_ref[:, sl].astype(BF16)))
    o = jnp.concatenate(heads, axis=-1)
    o_ref[...] = x + _dot(o.astype(BF16), wo_ref[...])


def _cross_prompt(x, g, wq, mem_k, mem_v, wo, *, batch, seq, tq):
    nt = seq // tq
    xspec = pl.BlockSpec((tq, D_MODEL), lambda b, t: (b * nt + t, 0))
    const = lambda shape: pl.BlockSpec(shape, lambda b, t: (0, 0))
    mspec = pl.BlockSpec((MEM_LEN, D_MODEL), lambda b, t: (b, 0))
    return pl.pallas_call(
        _cross_prompt_kernel, grid=(batch, nt),
        in_specs=[xspec, const((1, D_MODEL)), const((D_MODEL, D_MODEL)), mspec, mspec,
                  const((D_MODEL, D_MODEL))],
        out_specs=xspec, out_shape=jax.ShapeDtypeStruct(x.shape, F32),
        compiler_params=_cparams(("arbitrary", "arbitrary")), name="cross_prompt",
    )(x, g, wq, mem_k, mem_v, wo)


def _cross_decode_kernel(x_ref, g_ref, wq_ref, mk_ref, mv_ref, wo_ref, o_ref, att_ref, *, bb):
    x = x_ref[...]
    q = _dot(_rms_rows(x, g_ref[...]).astype(BF16), wq_ref[...])
    for b in range(bb):
        q4 = jnp.concatenate([q[b:b + 1, h * X_DH:(h + 1) * X_DH] for h in range(X_HEADS)], axis=0)
        s = jnp.sum(mk_ref[0, b] * q4[None], axis=-1, keepdims=True) * (X_DH ** -0.5)
        p = jnp.exp(s - jnp.max(s, axis=0, keepdims=True))
        p = p / jnp.sum(p, axis=0, keepdims=True)
        o4 = jnp.sum(p * mv_ref[0, b], axis=0)
        att_ref[b:b + 1, :] = jnp.concatenate([o4[h:h + 1, :] for h in range(X_HEADS)], axis=1)
    o_ref[...] = x + _dot(att_ref[...].astype(BF16), wo_ref[...])


def _cross_decode(x, g, wq, cache_k, cache_v, wo, *, layer, bb):
    rows = x.shape[0]
    xspec = pl.BlockSpec((bb, D_MODEL), lambda i: (i, 0))
    const = lambda shape: pl.BlockSpec(shape, lambda i: (0, 0))
    mspec = pl.BlockSpec((1, bb, MEM_LEN, X_HEADS, X_DH), lambda i: (layer, i, 0, 0, 0))
    return pl.pallas_call(
        functools.partial(_cross_decode_kernel, bb=bb), grid=(rows // bb,),
        in_specs=[xspec, const((1, D_MODEL)), const((D_MODEL, D_MODEL)), mspec, mspec,
                  const((D_MODEL, D_MODEL))],
        out_specs=xspec, out_shape=jax.ShapeDtypeStruct(x.shape, F32),
        scratch_shapes=[pltpu.VMEM((bb, D_MODEL), F32)],
        compiler_params=_cparams(("arbitrary",)), name="cross_decode",
    )(x, g, wq, cache_k, cache_v, wo)


def _ffn_prompt_kernel(x_ref, g_ref, wup_ref, cw_ref, cb_ref, wdn_ref, gf_ref,
                       o_ref, st_ref, hn_ref, halo_ref, *, tt, final_norm):
    t = pl.program_id(1)

    @pl.when(t == 0)
    def _():
        halo_ref[...] = jnp.zeros_like(halo_ref)

    x = x_ref[...]
    hn_ref[...] = _rms_rows(x, g_ref[...]).astype(BF16)
    rid = lax.broadcasted_iota(jnp.int32, (8, FF_CHUNK), 0)

    def conv(col0):
        cols = slice(col0, col0 + FF_CHUNK)
        up = _dot(hn_ref[...], wup_ref[:, cols])
        p2 = halo_ref[0:1, cols]
        p1 = halo_ref[1:2, cols]
        r1 = pltpu.roll(up, 1, axis=0)
        r2 = pltpu.roll(up, 2, axis=0)
        sh1 = jnp.concatenate([jnp.where(rid == 0, p1, r1[0:8]), r1[8:]], axis=0)
        sh2 = jnp.concatenate([jnp.where(rid == 0, p2, jnp.where(rid == 1, p1, r2[0:8])), r2[8:]], axis=0)
        halo_ref[0:2, cols] = up[tt - 2:tt, :]
        return sh2 * cw_ref[0:1, cols] + sh1 * cw_ref[1:2, cols] + up * cw_ref[2:3, cols] + cb_ref[:, cols]

    acc = x
    for c in range(N_FF_CHUNKS):
        a = conv(c * FF_CHUNK)
        gt = conv(D_FF + c * FF_CHUNK)
        act = a * (gt * _sigmoid(gt))
        acc = acc + _dot(act.astype(BF16), wdn_ref[c * FF_CHUNK:(c + 1) * FF_CHUNK, :])
    if final_norm:
        acc = _rms_rows(acc, gf_ref[...])
    o_ref[...] = acc

    @pl.when(t == pl.num_programs(1) - 1)
    def _():
        st_ref[0] = halo_ref[0:2, :]


def _ffn_prompt(x, g, wup, cw, cb, wdn, gf, *, batch, seq, tt, final_norm):
    nt = seq // tt
    xspec = pl.BlockSpec((tt, D_MODEL), lambda b, t: (b * nt + t, 0))
    const = lambda shape: pl.BlockSpec(shape, lambda b, t: (0, 0), pipeline_mode=pl.Buffered(1))
    return pl.pallas_call(
        functools.partial(_ffn_prompt_kernel, tt=tt, final_norm=final_norm), grid=(batch, nt),
        in_specs=[xspec, const((1, D_MODEL)), const((D_MODEL, 2 * D_FF)), const((CONV_W, 2 * D_FF)),
                  const((1, 2 * D_FF)), const((D_FF, D_MODEL)), const((1, D_MODEL))],
        out_specs=[xspec, pl.BlockSpec((1, CONV_W - 1, 2 * D_FF), lambda b, t: (b, 0, 0))],
        out_shape=[jax.ShapeDtypeStruct(x.shape, F32),
                   jax.ShapeDtypeStruct((batch, CONV_W - 1, 2 * D_FF), F32)],
        scratch_shapes=[pltpu.VMEM((tt, D_MODEL), BF16), pltpu.VMEM((8, 2 * D_FF), F32)],
        compiler_params=_cparams(("arbitrary", "arbitrary")), name="ffn_prompt",
    )(x, g, wup, cw, cb, wdn, gf)


def _ffn_decode_kernel(x_ref, g_ref, wa_ref, wg_ref, cwa_ref, cwg_ref, cba_ref, cbg_ref,
                       b0a_ref, b0g_ref, b1a_ref, b1g_ref, wdn_ref, gf_ref,
                       o_ref, upa_ref, upg_ref, hn_ref, acc_ref, *, final_norm):
    c = pl.program_id(0)

    @pl.when(c == 0)
    def _():
        x = x_ref[...]
        hn_ref[...] = _rms_rows(x, g_ref[...]).astype(BF16)
        acc_ref[...] = x

    def conv(w_ref, cw_ref, cb_ref, b0_ref, b1_ref, up_ref):
        up = _dot(hn_ref[...], w_ref[...])
        up_ref[...] = up
        return b0_ref[...] * cw_ref[0:1, :] + b1_ref[...] * cw_ref[1:2, :] + up * cw_ref[2:3, :] + cb_ref[...]

    a = conv(wa_ref, cwa_ref, cba_ref, b0a_ref, b1a_ref, upa_ref)
    gt = conv(wg_ref, cwg_ref, cbg_ref, b0g_ref, b1g_ref, upg_ref)
    act = a * (gt * _sigmoid(gt))
    acc_ref[...] += _dot(act.astype(BF16), wdn_ref[...])

    @pl.when(c == pl.num_programs(0) - 1)
    def _():
        acc = acc_ref[...]
        if final_norm:
            acc = _rms_rows(acc, gf_ref[...])
        o_ref[...] = acc


def _ffn_decode(x, g, wup, cw, cb, wdn, gf, buf0, buf1, *, final_norm):
    rows = x.shape[0]
    fc = FF_CHUNK
    full = lambda shape: pl.BlockSpec(shape, lambda c: (0, 0))
    a_col = lambda r: pl.BlockSpec((r, fc), lambda c: (0, c))
    g_col = lambda r: pl.BlockSpec((r, fc), lambda c: (0, N_FF_CHUNKS + c))
    return pl.pallas_call(
        functools.partial(_ffn_decode_kernel, final_norm=final_norm), grid=(N_FF_CHUNKS,),
        in_specs=[full((rows, D_MODEL)), full((1, D_MODEL)),
                  a_col(D_MODEL), g_col(D_MODEL), a_col(CONV_W), g_col(CONV_W), a_col(1), g_col(1),
                  a_col(rows), g_col(rows), a_col(rows), g_col(rows),
                  pl.BlockSpec((fc, D_MODEL), lambda c: (c, 0)), full((1, D_MODEL))],
        out_specs=[full((rows, D_MODEL)), a_col(rows), a_col(rows)],
        out_shape=[jax.ShapeDtypeStruct(x.shape, F32),
                   jax.ShapeDtypeStruct((rows, D_FF), F32),
                   jax.ShapeDtypeStruct((rows, D_FF), F32)],
        scratch_shapes=[pltpu.VMEM((rows, D_MODEL), BF16), pltpu.VMEM((rows, D_MODEL), F32)],
        compiler_params=_cparams(("arbitrary",)), name="ffn_decode",
    )(x, g, wup, wup, cw, cw, cb, cb, buf0, buf0, buf1, buf1, wdn, gf)


def _layer_weights(P, l):
    w_in = P['w_in'][l]
    w_main = jnp.concatenate([w_in[:, 0:1536], w_in[:, 1552:3600], w_in[:, 3608:7192]], axis=1).astype(BF16)
    w_small = jnp.concatenate([w_in[:, 1536:1552], w_in[:, 3600:3608],
                               jnp.zeros((D_MODEL, LANES - GLA_LOWRANK - 2 * ML_HEADS), F32)], axis=1).astype(BF16)
    wa = jnp.zeros((LANES, GLA_QK), F32).at[:GLA_LOWRANK].set(P['w_gla_alpha'][l]).astype(BF16)
    gbias = jnp.zeros((1, LANES), F32)
    gbias = gbias.at[0, SM_MI:SM_MI + ML_HEADS].set(P['b_mlstm_i'][l])
    gbias = gbias.at[0, SM_MF:SM_MF + ML_HEADS].set(P['b_mlstm_f'][l])
    eye = jnp.eye(S5_GB, dtype=F32)
    w_re = jnp.einsum('ngpc,gh->ngchp', P['s5_b_re'][l].reshape(S5_NB, S5_GB, S5_P, S5_GROUP), eye)
    w_im = jnp.einsum('ngpc,gh->ngchp', P['s5_b_im'][l].reshape(S5_NB, S5_GB, S5_P, S5_GROUP), eye)
    w_blk = jnp.concatenate([w_re.reshape(S5_NB, S5_UB, S5_MB), w_im.reshape(S5_NB, S5_UB, S5_MB)],
                            axis=-1).astype(BF16)
    ct_re = jnp.einsum('ngcp,gh->ngchp', P['s5_c_re'][l].reshape(S5_NB, S5_GB, S5_GROUP, S5_P), eye)
    ct_im = jnp.einsum('ngcp,gh->ngchp', P['s5_c_im'][l].reshape(S5_NB, S5_GB, S5_GROUP, S5_P), eye)
    ct_re = ct_re.reshape(S5_NB, S5_UB, S5_MB)
    ct_im = ct_im.reshape(S5_NB, S5_UB, S5_MB)
    lam_re = P['s5_lam_re'][l].reshape(S5_NB, 1, S5_MB)
    lam_im = P['s5_lam_im'][l].reshape(S5_NB, 1, S5_MB)
    log_dt = jnp.broadcast_to(P['s5_log_dt'][l][:, None], (S5_GROUPS, S5_P)).reshape(S5_NB, 1, S5_MB)
    par, cfold = _s5_prep(lam_re, lam_im, log_dt, ct_re, ct_im)
    row = lambda a: a.reshape(1, -1)
    return dict(
        norm_mix=row(P['norm_mix'][l]), w_main=w_main, w_small=w_small, wa=wa,
        ba=row(P['b_gla_alpha'][l]), gla_norm=row(P['gla_head_norm'][l]), gbias=gbias,
        ml_norm=row(P['mlstm_head_norm'][l]), w_blk=w_blk, ct_re=ct_re.astype(BF16),
        ct_im=ct_im.astype(BF16), par=par, cfold=cfold, s5_d=row(P['s5_d'][l]),
        w_glu=P['s5_w_glu'][l].astype(BF16), w_a=P['w_branch_a'][l].astype(BF16),
        w_b=P['w_branch_b'][l].astype(BF16), w_c=P['w_branch_c'][l].astype(BF16),
        w_out=P['w_out'][l].astype(BF16), norm_cross=row(P['norm_cross'][l]),
        w_cq=P['w_cq'][l].astype(BF16), w_co=P['w_co'][l].astype(BF16),
        norm_ffn=row(P['norm_ffn'][l]), w_up=P['w_ffn_up'][l].astype(BF16),
        conv_w=P['ffn_conv_w'][l], conv_b=row(P['ffn_conv_b'][l]),
        w_down=P['w_ffn_down'][l].astype(BF16), norm_final=row(P['norm_final']),
    )


def _prompt_trunk(x_prompt, mem_k, mem_v, W, *, bm_in, tt_mix, tt_s5, bm, tq, tt_ffn):
    batch, seq, _ = x_prompt.shape
    depth = len(W)
    x = x_prompt.reshape(batch * seq, D_MODEL)
    outs = []
    for l, w in enumerate(W):
        z, zs = _norm_matmul(x, w['norm_mix'], w['w_main'], w['w_small'], bm=bm_in, bn=1024, out_dtype=BF16)
        ya, st = _gla_prompt(z, zs, w['wa'], w['ba'], w['gla_norm'], batch=batch, seq=seq, tt=tt_mix)
        yb, c, n, m = _mlstm_prompt(z, zs, w['gbias'], w['ml_norm'], batch=batch, seq=seq, tt=tt_mix)
        ypre, hfin = _s5_prompt(z.reshape(batch, seq, Z_MAIN), w['w_blk'], w['cfold'], w['par'],
                                batch=batch, seq=seq, tt=tt_s5)
        x = _merge(ya, yb, ypre.reshape(batch * seq, S5_WIDTH), z, x, w['s5_d'], w['w_glu'],
                   w['w_a'], w['w_b'], w['w_c'], w['w_out'], bm=bm)
        x = _cross_prompt(x, w['norm_cross'], w['w_cq'], mem_k[l], mem_v[l], w['w_co'],
                          batch=batch, seq=seq, tq=tq)
        x, conv = _ffn_prompt(x, w['norm_ffn'], w['w_up'], w['conv_w'], w['conv_b'], w['w_down'],
                              w['norm_final'], batch=batch, seq=seq, tt=tt_ffn,
                              final_norm=(l == depth - 1))
        st4 = st.reshape(batch, GLA_HEADS, GLA_DV, GLA_HEADS, GLA_DK)
        gla = jnp.stack([st4[:, h, :, h, :] for h in range(GLA_HEADS)], axis=1).transpose(0, 1, 3, 2)
        h4 = hfin.reshape(S5_NB, batch, 2, S5_GB, S5_P)
        s5_re = h4[:, :, 0].transpose(1, 0, 2, 3).reshape(batch, S5_GROUPS, S5_P)
        s5_im = h4[:, :, 1].transpose(1, 0, 2, 3).reshape(batch, S5_GROUPS, S5_P)
        outs.append((gla, c.reshape(batch, ML_HEADS, ML_D, ML_D), n[:, :ML_HEADS, :],
                     m[:, :ML_HEADS, 0], s5_re, s5_im, conv))
    stacked = [jnp.stack([outs[l][i] for l in range(depth)]) for i in range(7)]
    return x.reshape(batch, seq, D_MODEL), stacked


def _sample_trunk(x_sample, cache_k, cache_v, states, W):
    rows = x_sample.shape[0]
    depth = len(W)
    x = x_sample.reshape(rows, D_MODEL)
    s_gla, s_c, s_n, s_m, s_re, s_im, s_conv = states
    outs = []
    for l, w in enumerate(W):
        z, zs = _norm_matmul(x, w['norm_mix'], w['w_main'], w['w_small'], bm=rows, bn=1024)
        ya, gla = _gla_decode(z, zs, w['wa'], w['ba'], w['gla_norm'], s_gla, layer=l, bb=16)
        m_pad = jnp.pad(s_m[l], ((0, 0), (0, LANES - ML_HEADS)))
        yb, c, n, m = _mlstm_decode(z, zs, w['gbias'], w['ml_norm'], s_c, s_n, m_pad, layer=l, bb=16)
        ypre, h_re, h_im = _s5_decode(z, w['w_blk'], w['ct_re'], w['ct_im'], w['par'],
                                      s_re[l].reshape(rows, S5_MODES), s_im[l].reshape(rows, S5_MODES))
        x = _merge(ya, yb, ypre, z, x, w['s5_d'], w['w_glu'], w['w_a'], w['w_b'], w['w_c'], w['w_out'],
                   bm=rows)
        x = _cross_decode(x, w['norm_cross'], w['w_cq'], cache_k, cache_v, w['w_co'], layer=l, bb=8)
        x, up_a, up_g = _ffn_decode(x, w['norm_ffn'], w['w_up'], w['conv_w'], w['conv_b'], w['w_down'],
                                    w['norm_final'], s_conv[l][:, 0, :], s_conv[l][:, 1, :],
                                    final_norm=(l == depth - 1))
        conv = jnp.stack([s_conv[l][:, 1, :], jnp.concatenate([up_a, up_g], axis=1)], axis=1)
        outs.append((gla, c, n, m[:, :ML_HEADS], h_re.reshape(rows, S5_GROUPS, S5_P),
                     h_im.reshape(rows, S5_GROUPS, S5_P), conv))
    stacked = [jnp.stack([outs[l][i] for l in range(depth)]) for i in range(7)]
    return x.reshape(rows, 1, D_MODEL), stacked


def kernel(x_prompt, x_sample, mem_prompt, cache_mem_k, cache_mem_v, state_gla, state_mlstm_c, state_mlstm_n, state_mlstm_m, state_s5_re, state_s5_im, state_ffn_conv, norm_mix, w_in, w_gla_alpha, b_gla_alpha, gla_head_norm, b_mlstm_i, b_mlstm_f, mlstm_head_norm, s5_lam_re, s5_lam_im, s5_log_dt, s5_b_re, s5_b_im, s5_c_re, s5_c_im, s5_d, s5_w_glu, w_branch_a, w_branch_b, w_branch_c, w_out, norm_cross, norm_mem, w_cq, w_ck, w_cv, w_co, norm_ffn, w_ffn_up, ffn_conv_w, ffn_conv_b, w_ffn_down, norm_final):
    P = dict(norm_mix=norm_mix, w_in=w_in, w_gla_alpha=w_gla_alpha, b_gla_alpha=b_gla_alpha,
             gla_head_norm=gla_head_norm, b_mlstm_i=b_mlstm_i, b_mlstm_f=b_mlstm_f,
             mlstm_head_norm=mlstm_head_norm, s5_lam_re=s5_lam_re, s5_lam_im=s5_lam_im,
             s5_log_dt=s5_log_dt, s5_b_re=s5_b_re, s5_b_im=s5_b_im, s5_c_re=s5_c_re,
             s5_c_im=s5_c_im, s5_d=s5_d, s5_w_glu=s5_w_glu, w_branch_a=w_branch_a,
             w_branch_b=w_branch_b, w_branch_c=w_branch_c, w_out=w_out, norm_cross=norm_cross,
             w_cq=w_cq, w_co=w_co, norm_ffn=norm_ffn, w_ffn_up=w_ffn_up, ffn_conv_w=ffn_conv_w,
             ffn_conv_b=ffn_conv_b, w_ffn_down=w_ffn_down, norm_final=norm_final)
    depth = w_in.shape[0]
    W = [_layer_weights(P, l) for l in range(depth)]
    batch, mem_len, _ = mem_prompt.shape
    mem2 = mem_prompt.reshape(batch * mem_len, D_MODEL)
    mem_k, mem_v = [], []
    for l in range(depth):
        w_kv = jnp.concatenate([w_ck[l], w_cv[l]], axis=1).astype(BF16)
        kv = _norm_matmul(mem2, norm_mem[l].reshape(1, -1), w_kv, bm=1024, bn=1024)
        mem_k.append(kv[:, :D_MODEL])
        mem_v.append(kv[:, D_MODEL:])
    y_prompt, p_states = _prompt_trunk(x_prompt, mem_k, mem_v, W, bm_in=2048, tt_mix=256, tt_s5=256, bm=512, tq=512,
                                       tt_ffn=512)
    p_mem_k = jnp.stack(mem_k).reshape(depth, batch, mem_len, X_HEADS, X_DH)
    p_mem_v = jnp.stack(mem_v).reshape(depth, batch, mem_len, X_HEADS, X_DH)
    y_sample, s_states = _sample_trunk(
        x_sample, cache_mem_k, cache_mem_v,
        (state_gla, state_mlstm_c, state_mlstm_n, state_mlstm_m, state_s5_re, state_s5_im, state_ffn_conv), W)
    return (y_prompt, y_sample, *p_states, p_mem_k, p_mem_v, *s_states)
```

```python
import functools

import jax
import jax.numpy as jnp
from jax import lax
from jax.experimental import pallas as pl
from jax.experimental.pallas import tpu as pltpu

F32 = jnp.float32
BF16 = jnp.bfloat16

D_MODEL = 1024
GLA_HEADS, GLA_DK, GLA_DV = 4, 64, 128
GLA_QK = GLA_HEADS * GLA_DK
GLA_V = GLA_HEADS * GLA_DV
GLA_LOWRANK = 16
GLA_TAU = 16.0
ML_HEADS, ML_D = 4, 128
ML_W = ML_HEADS * ML_D
S5_GROUP, S5_GROUPS, S5_P = 16, 32, 64
S5_WIDTH = S5_GROUP * S5_GROUPS
S5_MODES = S5_GROUPS * S5_P
S5_NB = 4
S5_GB = S5_GROUPS // S5_NB
S5_MB = S5_MODES // S5_NB
S5_UB = S5_WIDTH // S5_NB
MEM_LEN = 256
X_HEADS = 4
X_DH = D_MODEL // X_HEADS
D_FF = 2816
CONV_W = 3
EPS = 1e-6

CHUNK = 256
LANES = 128
FF_CHUNK = 256
N_FF_CHUNKS = D_FF // FF_CHUNK

Z_MAIN = 7168
Z_QK, Z_GV, Z_GR, Z_MQ, Z_MK, Z_MV, Z_MO, Z_SU = range(8)
Z_ZA, Z_ZB, Z_ZC = 4, 5, 6
SM_MI = GLA_LOWRANK
SM_MF = GLA_LOWRANK + ML_HEADS

GLA_SAFE_DECAY = 80.0

VMEM_LIMIT = 56 * 1024 * 1024


def _cparams(sem):
    return pltpu.CompilerParams(dimension_semantics=sem, vmem_limit_bytes=VMEM_LIMIT)


def _dot(a, b):
    return jnp.dot(a, b, preferred_element_type=F32)


def _dot_nt(a, b):
    return lax.dot_general(a, b, (((1,), (1,)), ((), ())), preferred_element_type=F32)


def _dot_tn(a, b):
    return lax.dot_general(a, b, (((0,), (0,)), ((), ())), preferred_element_type=F32)


def _sigmoid(x):
    return 1.0 / (1.0 + jnp.exp(-x))


def _log_sigmoid(x):
    return jnp.minimum(x, 0.0) - jnp.log(1.0 + jnp.exp(-jnp.abs(x)))


def _gelu_tanh(x):
    return 0.5 * x * (1.0 + jnp.tanh(0.7978845608028654 * (x + 0.044715 * x * x * x)))


def _rms_rows(x, g):
    r = lax.rsqrt(jnp.mean(x * x, axis=-1, keepdims=True) + EPS)
    return (x * r) * g


def _dot_exact01(a01, x):
    hi = x.astype(BF16)
    r1 = x - hi.astype(F32)
    mid = r1.astype(BF16)
    lo = (r1 - mid.astype(F32)).astype(BF16)
    return _dot(a01, hi) + _dot(a01, mid) + _dot(a01, lo)


def _dot_exact01_rhs(x, b01):
    hi = x.astype(BF16)
    r1 = x - hi.astype(F32)
    mid = r1.astype(BF16)
    lo = (r1 - mid.astype(F32)).astype(BF16)
    return _dot(hi, b01) + _dot(mid, b01) + _dot(lo, b01)


def _row_to_col(row, n):
    eye = (lax.broadcasted_iota(jnp.int32, (n, n), 0) == lax.broadcasted_iota(jnp.int32, (n, n), 1))
    return jnp.sum(jnp.where(eye, jnp.broadcast_to(row, (n, n)), 0.0), axis=1, keepdims=True)


def _norm_mm_small_kernel(x_ref, g_ref, w_ref, ws_ref, o_ref, os_ref, hn_ref):
    @pl.when(pl.program_id(1) == 0)
    def _():
        hn = _rms_rows(x_ref[...], g_ref[...]).astype(BF16)
        hn_ref[...] = hn
        os_ref[...] = _dot(hn, ws_ref[...])
    o_ref[...] = _dot(hn_ref[...], w_ref[...]).astype(o_ref.dtype)


def _norm_matmul(x, g, w, w_small, *, bm, bn, out_dtype=F32):
    n, d = x.shape
    c = w.shape[1]
    grid = (n // bm, c // bn)
    x_spec = pl.BlockSpec((bm, d), lambda i, j: (i, 0))
    g_spec = pl.BlockSpec((1, d), lambda i, j: (0, 0))
    w_spec = pl.BlockSpec((d, bn), lambda i, j: (0, j))
    o_spec = pl.BlockSpec((bm, bn), lambda i, j: (i, j))
    scratch = [pltpu.VMEM((bm, d), BF16)]
    cs = w_small.shape[1]
    return pl.pallas_call(
        _norm_mm_small_kernel, grid=grid,
        in_specs=[x_spec, g_spec, w_spec, pl.BlockSpec((d, cs), lambda i, j: (0, 0))],
        out_specs=[o_spec, pl.BlockSpec((bm, cs), lambda i, j: (i, 0))],
        out_shape=[jax.ShapeDtypeStruct((n, c), out_dtype), jax.ShapeDtypeStruct((n, cs), F32)],
        scratch_shapes=scratch,
        compiler_params=_cparams(("arbitrary", "arbitrary")), name="norm_matmul_in",
    )(x, g, w, w_small)


def _head_norm(oh, gn_h):
    r = lax.rsqrt(jnp.mean(oh * oh, axis=-1, keepdims=True) + EPS)
    return oh * r * gn_h


def _head_norm_mxu(oh, gn_h, ones_blk):
    sq = oh * oh
    hi = sq.astype(BF16)
    lo = (sq - hi.astype(F32)).astype(BF16)
    ms = (_dot(hi, ones_blk) + _dot(lo, ones_blk)) * (1.0 / GLA_DV)
    return oh * lax.rsqrt(ms + EPS) * gn_h


def _gla_log_decay(small, wa_ref, ba_ref):
    a_pre = _dot(small.astype(BF16), wa_ref[...]) + ba_ref[...]
    return _log_sigmoid(a_pre) * (1.0 / GLA_TAU)


def _gla_prompt_kernel(qk_ref, v_ref, gr_ref, sm_ref, wa_ref, ba_ref, gn_ref,
                       y_ref, st_ref, state_ref, b_ref, k_ref, *, n_chunks, group):
    t = pl.program_id(1)
    L = CHUNK

    @pl.when(t == 0)
    def _():
        state_ref[...] = jnp.zeros_like(state_ref)

    ri = lax.broadcasted_iota(jnp.int32, (L, L), 0)
    ci = lax.broadcasted_iota(jnp.int32, (L, L), 1)
    tril = jnp.where(ri >= ci, 1.0, 0.0).astype(BF16)
    rw = lax.broadcasted_iota(jnp.int32, (L, GLA_HEADS * L), 0)
    cw = lax.broadcasted_iota(jnp.int32, (L, GLA_HEADS * L), 1)
    causal_wide = (cw % L) <= rw
    kr = lax.broadcasted_iota(jnp.int32, (GLA_HEADS * L, GLA_QK), 0)
    kc = lax.broadcasted_iota(jnp.int32, (GLA_HEADS * L, GLA_QK), 1)
    kk_mask = (kr // L) == (kc // GLA_DK)
    vr = lax.broadcasted_iota(jnp.int32, (GLA_HEADS * L, GLA_V), 0)
    vc = lax.broadcasted_iota(jnp.int32, (GLA_HEADS * L, GLA_V), 1)
    vv_mask = (vr // L) == (vc // GLA_DV)
    sr = lax.broadcasted_iota(jnp.int32, (GLA_V, GLA_QK), 0)
    sc = lax.broadcasted_iota(jnp.int32, (GLA_V, GLA_QK), 1)
    st_mask = (sr // GLA_DV) == (sc // GLA_DK)
    hr = lax.broadcasted_iota(jnp.int32, (GLA_QK, LANES), 0)
    hc = lax.broadcasted_iota(jnp.int32, (GLA_QK, LANES), 1)
    head_ones = jnp.where((hr // GLA_DK) == hc, 1.0, 0.0).astype(BF16)
    pr = lax.broadcasted_iota(jnp.int32, (LANES, GLA_HEADS * L), 0)
    pc = lax.broadcasted_iota(jnp.int32, (LANES, GLA_HEADS * L), 1)

    def one(e, c):
        r0 = pl.multiple_of(c * L, L)
        rows = pl.ds(r0, L)
        qk = qk_ref[e, rows, :].astype(F32)
        q = qk[:, :GLA_QK] * (GLA_DK ** -0.5)
        k = qk[:, GLA_QK:]
        v16 = v_ref[e, rows, :].astype(BF16)
        g = _gla_log_decay(sm_ref[e, rows, :], wa_ref, ba_ref)
        b = _dot_exact01(tril, g)
        b_end = b[L - 1:L, :]
        qe = q * jnp.exp(b)
        k_dec = k * jnp.exp(b_end - b)
        qe16 = qe.astype(BF16)

        def fast_att(_):
            k_til = k * jnp.exp(-b)
            kk = jnp.where(kk_mask, jnp.concatenate([k_til] * GLA_HEADS, axis=0), 0.0)
            return _dot_nt(qe16, kk.astype(BF16))

        def direct_att(_):
            b_ref[...] = b
            k_ref[...] = k

            def col(j, acc):
                kj = k_ref[pl.ds(j, 1), :]
                bj = b_ref[pl.ds(j, 1), :]
                prod = q * kj * jnp.exp(jnp.minimum(b - bj, 0.0))
                red = _dot(prod.astype(BF16), head_ones)
                place = jnp.where((pc == pr * L + j) & (pr < GLA_HEADS), 1.0, 0.0).astype(BF16)
                return acc + _dot(red.astype(BF16), place)

            return lax.fori_loop(0, L, col, jnp.zeros((L, GLA_HEADS * L), F32))

        safe = jnp.max(-b_end) <= GLA_SAFE_DECAY
        att = lax.cond(safe, fast_att, direct_att, 0)
        att = jnp.where(causal_wide, att, 0.0)

        vv = jnp.where(vv_mask, jnp.concatenate([v16] * GLA_HEADS, axis=0), jnp.zeros((), BF16))
        st = state_ref[e]
        o = _dot(att.astype(BF16), vv) + _dot_nt(qe16, st.astype(BF16))
        upd = _dot_tn(v16, k_dec.astype(BF16))
        state_ref[e] = jnp.where(st_mask, st * jnp.exp(b_end) + upd, 0.0)

        gr = gr_ref[e, rows, :].astype(F32)
        for h in range(GLA_HEADS):
            sl = slice(h * GLA_DV, (h + 1) * GLA_DV)
            gate = gr[:, sl]
            yh = _head_norm(o[:, sl], gn_ref[:, sl]) * (gate * _sigmoid(gate))
            y_ref[e, rows, sl] = yh.astype(BF16)

    def chunk(c, carry):
        for e in range(group):
            one(e, c)
        return carry

    lax.fori_loop(0, n_chunks, chunk, 0)

    @pl.when(t == pl.num_programs(1) - 1)
    def _():
        st_ref[...] = state_ref[...]


def _gla_prompt(z3, zs3, wa, ba, gn, *, tt, group):
    batch, seq, _ = z3.shape
    blk = lambda j: pl.BlockSpec((group, tt, 512), lambda b, t: (b, t, j))
    const = lambda shape: pl.BlockSpec(shape, lambda b, t: (0,) * len(shape))
    return pl.pallas_call(
        functools.partial(_gla_prompt_kernel, n_chunks=tt // CHUNK, group=group),
        grid=(batch // group, seq // tt),
        in_specs=[blk(Z_QK), blk(Z_GV), blk(Z_GR),
                  pl.BlockSpec((group, tt, LANES), lambda b, t: (b, t, 0)),
                  const((LANES, GLA_QK)), const((1, GLA_QK)), const((1, GLA_V))],
        out_specs=[pl.BlockSpec((group, tt, GLA_V), lambda b, t: (b, t, 0)),
                   pl.BlockSpec((group, GLA_V, GLA_QK), lambda b, t: (b, 0, 0))],
        out_shape=[jax.ShapeDtypeStruct((batch, seq, GLA_V), BF16),
                   jax.ShapeDtypeStruct((batch, GLA_V, GLA_QK), F32)],
        scratch_shapes=[pltpu.VMEM((group, GLA_V, GLA_QK), F32), pltpu.VMEM((CHUNK, GLA_QK), F32),
                        pltpu.VMEM((CHUNK, GLA_QK), F32)],
        compiler_params=_cparams(("arbitrary", "arbitrary")), name="gla_prompt",
    )(z3, z3, z3, zs3, wa, ba, gn)


def _mlstm_prompt_kernel(q_ref, k_ref, v_ref, og_ref, sm_ref, gb_ref, gn_ref,
                         y_ref, c_out, n_out, m_out, c_ref, n_ref, m_ref, *, n_chunks, group):
    t = pl.program_id(1)
    L = CHUNK

    @pl.when(t == 0)
    def _():
        c_ref[...] = jnp.zeros_like(c_ref)
        n_ref[...] = jnp.zeros_like(n_ref)
        m_ref[...] = jnp.zeros_like(m_ref)

    ri = lax.broadcasted_iota(jnp.int32, (L, L), 0)
    ci = lax.broadcasted_iota(jnp.int32, (L, L), 1)
    lower = ri >= ci
    tril = jnp.where(lower, 1.0, 0.0).astype(BF16)
    triu = jnp.where(ri <= ci, 1.0, 0.0).astype(BF16)
    sr = lax.broadcasted_iota(jnp.int32, (LANES, ML_W), 0)
    sh = lax.broadcasted_iota(jnp.int32, (LANES, ML_W), 1) // ML_D
    sel_f = jnp.where(sr == SM_MF + sh, 1.0, 0.0).astype(BF16)
    sel_i = jnp.where(sr == SM_MI + sh, 1.0, 0.0).astype(BF16)
    ones_l = jnp.ones((L, LANES), BF16)
    ones_d = jnp.ones((ML_D, ML_D), BF16)

    def one(e, c):
        r0 = pl.multiple_of(c * L, L)
        rows = pl.ds(r0, L)
        sm = sm_ref[e, rows, :] + gb_ref[...]
        lf_all = _log_sigmoid(sm)
        sm_t = sm.T
        b_cols = _dot_exact01(tril, lf_all)
        b_rows = _dot_exact01_rhs(_log_sigmoid(sm_t), triu)
        b_wide = _dot_exact01_rhs(b_cols, sel_f)
        li_wide = _dot_exact01_rhs(sm, sel_i)
        for h in range(ML_HEADS):
            sl = slice(h * ML_D, (h + 1) * ML_D)
            b_tok = b_wide[:, sl]
            li_tok = li_wide[:, sl]
            li_row = sm_t[SM_MI + h:SM_MI + h + 1, :]
            b_row = b_rows[SM_MF + h:SM_MF + h + 1, :]
            m_prev = m_ref[e, h:h + 1, :]
            w = jnp.where(lower, jnp.concatenate([b_tok] * (L // LANES), axis=1) - b_row + li_row, -jnp.inf)
            m_tok = jnp.maximum(b_tok + m_prev, jnp.max(w, axis=1, keepdims=True))
            a_inter = jnp.exp(b_tok + m_prev - m_tok)
            qh = q_ref[e, rows, sl].astype(F32) * (ML_D ** -0.5)
            kh16 = k_ref[e, rows, sl].astype(BF16)
            kh = kh16.astype(F32)
            qh16, vh16 = qh.astype(BF16), v_ref[e, rows, sl].astype(BF16)
            s = _dot_nt(qh16, kh16) * jnp.exp(w - jnp.concatenate([m_tok] * (L // LANES), axis=1))
            c_h = c_ref[e, sl, :]
            n_h = n_ref[e, h:h + 1, :]
            s16 = s.astype(BF16)
            s_lo = (s - s16.astype(F32)).astype(BF16)
            row_sum = _dot(s16, ones_l) + _dot(s_lo, ones_l)
            q_n = _dot_nt(qh16, jnp.broadcast_to(n_h, (LANES, ML_D)).astype(BF16))
            num = _dot(s16, vh16) + a_inter * _dot(qh16, c_h.astype(BF16))
            den = row_sum + a_inter * q_n
            hh = num / jnp.maximum(jnp.abs(den), jnp.exp(-m_tok))
            og = og_ref[e, rows, sl].astype(F32)
            y_ref[e, rows, sl] = (_head_norm_mxu(hh, gn_ref[:, sl], ones_d) * _sigmoid(og)).astype(BF16)
            b_end = b_tok[L - 1:L, :]
            w_end = b_end - b_tok + li_tok
            m_new = jnp.maximum(b_end + m_prev, jnp.max(w_end, axis=0, keepdims=True))
            e_inter = jnp.exp(b_end + m_prev - m_new)
            kd = kh * jnp.exp(w_end - m_new)
            c_ref[e, sl, :] = e_inter * c_h + _dot_tn(kd.astype(BF16), vh16)
            n_ref[e, h:h + 1, :] = e_inter * n_h + jnp.sum(kd, axis=0, keepdims=True)
            m_ref[e, h:h + 1, :] = m_new

    def chunk(c, carry):
        for e in range(group):
            one(e, c)
        return carry

    lax.fori_loop(0, n_chunks, chunk, 0)

    @pl.when(t == pl.num_programs(1) - 1)
    def _():
        c_out[...] = c_ref[...]
        n_out[...] = n_ref[...]
        m_out[...] = m_ref[...]


def _mlstm_prompt(z3, zs3, gbias, gn, *, tt, group):
    batch, seq, _ = z3.shape
    blk = lambda j: pl.BlockSpec((group, tt, 512), lambda b, t: (b, t, j))
    const = lambda shape: pl.BlockSpec(shape, lambda b, t: (0,) * len(shape))
    state = lambda shape: pl.BlockSpec((group,) + shape, lambda b, t: (b, 0, 0))
    return pl.pallas_call(
        functools.partial(_mlstm_prompt_kernel, n_chunks=tt // CHUNK, group=group),
        grid=(batch // group, seq // tt),
        in_specs=[blk(Z_MQ), blk(Z_MK), blk(Z_MV), blk(Z_MO),
                  pl.BlockSpec((group, tt, LANES), lambda b, t: (b, t, 0)),
                  const((1, LANES)), const((1, ML_W))],
        out_specs=[pl.BlockSpec((group, tt, ML_W), lambda b, t: (b, t, 0)),
                   state((ML_W, ML_D)), state((8, LANES)), state((8, LANES))],
        out_shape=[jax.ShapeDtypeStruct((batch, seq, ML_W), BF16),
                   jax.ShapeDtypeStruct((batch, ML_W, ML_D), F32),
                   jax.ShapeDtypeStruct((batch, 8, LANES), F32),
                   jax.ShapeDtypeStruct((batch, 8, LANES), F32)],
        scratch_shapes=[pltpu.VMEM((group, ML_W, ML_D), F32), pltpu.VMEM((group, 8, LANES), F32),
                        pltpu.VMEM((group, 8, LANES), F32)],
        compiler_params=_cparams(("arbitrary", "arbitrary")), name="mlstm_prompt",
    )(z3, z3, z3, z3, zs3, gbias, gn)


def _s5_prep_kernel(lre_ref, lim_ref, ldt_ref, ctr_ref, cti_ref, par_ref, cf_ref):
    lam_re = lre_ref[0]
    lam_im = lim_ref[0]
    dt = jnp.exp(ldt_ref[0])
    mag = jnp.exp(lam_re * dt)
    lb_re = mag * jnp.cos(lam_im * dt)
    lb_im = mag * jnp.sin(lam_im * dt)
    nr = lb_re - 1.0
    den = lam_re * lam_re + lam_im * lam_im
    f_re = (nr * lam_re + lb_im * lam_im) / den
    f_im = (lb_im * lam_re - nr * lam_im) / den
    par_ref[0] = jnp.concatenate([lb_re, lb_im, f_re, f_im, jnp.zeros((4, S5_MB), F32)], axis=0)
    ct_re = ctr_ref[0]
    ct_im = cti_ref[0]
    cf_ref[0, :, :S5_MB] = (ct_re * f_re - ct_im * f_im).astype(BF16)
    cf_ref[0, :, S5_MB:] = (-(ct_re * f_im + ct_im * f_re)).astype(BF16)


def _s5_prep(lam_re, lam_im, log_dt, ct_re, ct_im):
    vec = pl.BlockSpec((1, 1, S5_MB), lambda n: (n, 0, 0))
    mat = pl.BlockSpec((1, S5_UB, S5_MB), lambda n: (n, 0, 0))
    return pl.pallas_call(
        _s5_prep_kernel, grid=(S5_NB,),
        in_specs=[vec, vec, vec, mat, mat],
        out_specs=[pl.BlockSpec((1, 8, S5_MB), lambda n: (n, 0, 0)),
                   pl.BlockSpec((1, S5_UB, 2 * S5_MB), lambda n: (n, 0, 0))],
        out_shape=[jax.ShapeDtypeStruct((S5_NB, 8, S5_MB), F32),
                   jax.ShapeDtypeStruct((S5_NB, S5_UB, 2 * S5_MB), BF16)],
        compiler_params=_cparams(("arbitrary",)), name="s5_prep",
    )(lam_re, lam_im, log_dt, ct_re, ct_im)


def _s5_prompt_kernel(u_ref, w_ref, cf_ref, par_ref, y_ref, h_out, x_ref, h_ref, ub_ref, ut_ref, *, batch, tt):
    t_blk = pl.program_id(1)

    @pl.when(t_blk == 0)
    def _():
        h_ref[...] = jnp.zeros_like(h_ref)

    nlb = S5_MB // LANES
    ub_ref[...] = u_ref[...].astype(F32).reshape(batch * tt, S5_UB)

    def interleave(t, carry):
        ut_ref[pl.ds(pl.multiple_of(t * batch, batch), batch), :] = ub_ref[pl.ds(t, batch, stride=tt), :]
        return carry

    lax.fori_loop(0, tt, interleave, 0, unroll=8)
    x = _dot(ut_ref[...].astype(BF16), w_ref[0])
    for j in range(2 * nlb):
        x_ref[j] = x[:, j * LANES:(j + 1) * LANES]

    par = par_ref[0]
    lr = [jnp.broadcast_to(par[0:1, j * LANES:(j + 1) * LANES], (batch, LANES)) for j in range(nlb)]
    li = [jnp.broadcast_to(par[1:2, j * LANES:(j + 1) * LANES], (batch, LANES)) for j in range(nlb)]

    def step(t, carry):
        rows = pl.ds(pl.multiple_of(t * batch, batch), batch)
        new = []
        for j in range(nlb):
            hr, hi = carry[j], carry[nlb + j]
            nr = lr[j] * hr - li[j] * hi + x_ref[j, rows, :]
            ni = lr[j] * hi + li[j] * hr + x_ref[nlb + j, rows, :]
            x_ref[j, rows, :] = nr
            x_ref[nlb + j, rows, :] = ni
            new.append((nr, ni))
        return tuple(n[0] for n in new) + tuple(n[1] for n in new)

    h0 = tuple(h_ref[:, j * LANES:(j + 1) * LANES] for j in range(2 * nlb))
    hfin = lax.fori_loop(0, tt, step, h0, unroll=8)
    for j in range(2 * nlb):
        h_ref[:, j * LANES:(j + 1) * LANES] = hfin[j]
    hr = jnp.concatenate(hfin[:nlb], axis=1)
    hi = jnp.concatenate(hfin[nlb:], axis=1)

    hall = jnp.concatenate([x_ref[j].astype(BF16) for j in range(2 * nlb)], axis=1)
    ut_ref[...] = _dot_nt(hall, cf_ref[0])
    for b in range(batch):
        y_ref[b] = ut_ref[pl.ds(b, tt, stride=batch), :]

    @pl.when(t_blk == pl.num_programs(1) - 1)
    def _():
        fr = par[2:3, :]
        fi = par[3:4, :]
        h_out[0, :, :S5_MB] = fr * hr - fi * hi
        h_out[0, :, S5_MB:] = fr * hi + fi * hr


def _s5_prompt(z3, w_blk, cfold, par, *, batch, seq, tt):
    nt = seq // tt
    u_col0 = Z_SU * 512 // S5_UB
    return pl.pallas_call(
        functools.partial(_s5_prompt_kernel, batch=batch, tt=tt),
        grid=(S5_NB, nt),
        in_specs=[pl.BlockSpec((batch, tt, S5_UB), lambda n, t: (0, t, u_col0 + n)),
                  pl.BlockSpec((1, S5_UB, 2 * S5_MB), lambda n, t: (n, 0, 0)),
                  pl.BlockSpec((1, S5_UB, 2 * S5_MB), lambda n, t: (n, 0, 0)),
                  pl.BlockSpec((1, 8, S5_MB), lambda n, t: (n, 0, 0))],
        out_specs=[pl.BlockSpec((batch, tt, S5_UB), lambda n, t: (0, t, n)),
                   pl.BlockSpec((1, batch, 2 * S5_MB), lambda n, t: (n, 0, 0))],
        out_shape=[jax.ShapeDtypeStruct((batch, seq, S5_WIDTH), F32),
                   jax.ShapeDtypeStruct((S5_NB, batch, 2 * S5_MB), F32)],
        scratch_shapes=[pltpu.VMEM((2 * S5_MB // LANES, batch * tt, LANES), F32),
                        pltpu.VMEM((batch, 2 * S5_MB), F32),
                        pltpu.VMEM((batch * tt, S5_UB), F32), pltpu.VMEM((batch * tt, S5_UB), F32)],
        compiler_params=_cparams(("arbitrary", "arbitrary")), name="s5_prompt",
    )(z3, w_blk, cfold, par)


def _s5_decode_kernel(u_ref, w_ref, ctr_ref, cti_ref, par_ref, hr_ref, hi_ref, y_ref, hro_ref, hio_ref):
    x = _dot(u_ref[...].astype(BF16), w_ref[0])
    xr, xi = x[:, :S5_MB], x[:, S5_MB:]
    par = par_ref[0]
    lr, li, fr, fi = par[0:1, :], par[1:2, :], par[2:3, :], par[3:4, :]
    h0r, h0i = hr_ref[...], hi_ref[...]
    hr = lr * h0r - li * h0i + (fr * xr - fi * xi)
    hi = lr * h0i + li * h0r + (fr * xi + fi * xr)
    hro_ref[...] = hr
    hio_ref[...] = hi
    y_ref[...] = _dot_nt(hr.astype(BF16), ctr_ref[0]) - _dot_nt(hi.astype(BF16), cti_ref[0])


def _s5_decode(z, w_blk, ct_re, ct_im, par, h_re, h_im):
    rows = z.shape[0]
    u_col0 = Z_SU * 512 // S5_UB
    hspec = pl.BlockSpec((rows, S5_MB), lambda n: (0, n))
    mat = pl.BlockSpec((1, S5_UB, S5_MB), lambda n: (n, 0, 0))
    return pl.pallas_call(
        _s5_decode_kernel, grid=(S5_NB,),
        in_specs=[pl.BlockSpec((rows, S5_UB), lambda n: (0, u_col0 + n)),
                  pl.BlockSpec((1, S5_UB, 2 * S5_MB), lambda n: (n, 0, 0)),
                  mat, mat,
                  pl.BlockSpec((1, 8, S5_MB), lambda n: (n, 0, 0)), hspec, hspec],
        out_specs=[pl.BlockSpec((rows, S5_UB), lambda n: (0, n)), hspec, hspec],
        out_shape=[jax.ShapeDtypeStruct((rows, S5_WIDTH), F32),
                   jax.ShapeDtypeStruct((rows, S5_MODES), F32),
                   jax.ShapeDtypeStruct((rows, S5_MODES), F32)],
        compiler_params=_cparams(("arbitrary",)), name="s5_decode",
    )(z, w_blk, ct_re, ct_im, par, h_re, h_im)


def _gla_decode_kernel(qk_ref, v_ref, gr_ref, sm_ref, wa_ref, ba_ref, gn_ref, s_ref,
                       y_ref, so_ref, eg_ref, o_ref, *, bb):
    g = _gla_log_decay(sm_ref[...], wa_ref, ba_ref)
    eg_ref[...] = jnp.exp(g)

    def per_row(b, carry):
        row = pl.ds(b, 1)
        eg_r = eg_ref[row, :]
        qk_r = qk_ref[row, :]
        v_r = v_ref[row, :]
        q_r = qk_r[:, :GLA_QK] * (GLA_DK ** -0.5)
        k_r = qk_r[:, GLA_QK:]
        o_heads = []
        for h in range(GLA_HEADS):
            ks = slice(h * GLA_DK, (h + 1) * GLA_DK)
            vs = slice(h * GLA_DV, (h + 1) * GLA_DV)
            s_new = _row_to_col(eg_r[:, ks], GLA_DK) * s_ref[b, h] + _row_to_col(k_r[:, ks], GLA_DK) * v_r[:, vs]
            so_ref[b, h] = s_new
            o_heads.append(jnp.sum(_row_to_col(q_r[:, ks], GLA_DK) * s_new, axis=0, keepdims=True))
        o_ref[row, :] = jnp.concatenate(o_heads, axis=1)
        return carry

    lax.fori_loop(0, bb, per_row, 0)
    o = o_ref[...]
    gr = gr_ref[...]
    for h in range(GLA_HEADS):
        vs = slice(h * GLA_DV, (h + 1) * GLA_DV)
        gate = gr[:, vs]
        y_ref[:, vs] = (_head_norm(o[:, vs], gn_ref[:, vs]) * (gate * _sigmoid(gate))).astype(BF16)


def _gla_decode(z, zs, wa, ba, gn, states, *, layer, bb):
    rows = z.shape[0]
    blk = lambda j: pl.BlockSpec((bb, 512), lambda i: (i, j))
    const = lambda shape: pl.BlockSpec(shape, lambda i: (0,) * len(shape))
    sblock = (bb, GLA_HEADS, GLA_DK, GLA_DV)
    return pl.pallas_call(
        functools.partial(_gla_decode_kernel, bb=bb), grid=(rows // bb,),
        in_specs=[blk(Z_QK), blk(Z_GV), blk(Z_GR), pl.BlockSpec((bb, LANES), lambda i: (i, 0)),
                  const((LANES, GLA_QK)), const((1, GLA_QK)), const((1, GLA_V)),
                  pl.BlockSpec((None,) + sblock, lambda i: (layer, i, 0, 0, 0))],
        out_specs=[pl.BlockSpec((bb, GLA_V), lambda i: (i, 0)),
                   pl.BlockSpec(sblock, lambda i: (i, 0, 0, 0))],
        out_shape=[jax.ShapeDtypeStruct((rows, GLA_V), BF16),
                   jax.ShapeDtypeStruct(states.shape[1:], F32)],
        scratch_shapes=[pltpu.VMEM((bb, GLA_QK), F32), pltpu.VMEM((bb, GLA_V), F32)],
        compiler_params=_cparams(("arbitrary",)), name="gla_decode",
    )(z, z, z, zs, wa, ba, gn, states)


def _mlstm_decode_kernel(q_ref, k_ref, v_ref, og_ref, sm_ref, gb_ref, gn_ref, c_ref, n_ref, m_ref,
                         y_ref, co_ref, no_ref, mo_ref, li_ref, lf_ref, h_ref, *, bb):
    sm = sm_ref[...] + gb_ref[...]
    li_ref[...] = sm
    lf_ref[...] = _log_sigmoid(sm)
    lane = lax.broadcasted_iota(jnp.int32, (1, LANES), 1)

    def per_row(b, carry):
        row = pl.ds(b, 1)
        li_r = li_ref[row, :]
        lf_r = lf_ref[row, :]
        m_r = m_ref[row, :]
        q_all = q_ref[row, :] * (ML_D ** -0.5)
        k_all = k_ref[row, :]
        v_all = v_ref[row, :]
        n_all = n_ref[b]
        m_new_row = jnp.zeros((1, LANES), F32)
        h_heads, n_heads = [], []
        for h in range(ML_HEADS):
            sl = slice(h * ML_D, (h + 1) * ML_D)
            li = li_r[:, SM_MI + h:SM_MI + h + 1]
            lf = lf_r[:, SM_MF + h:SM_MF + h + 1]
            m_prev = m_r[:, h:h + 1]
            m_new = jnp.maximum(lf + m_prev, li)
            a = jnp.exp(lf + m_prev - m_new)
            e = jnp.exp(li - m_new)
            k_r = k_all[:, sl]
            q_r = q_all[:, sl]
            c_new = a * c_ref[b, h] + (_row_to_col(k_r, ML_D) * e) * v_all[:, sl]
            n_new = a * n_all[h:h + 1, :] + e * k_r
            co_ref[b, h] = c_new
            n_heads.append(n_new)
            num = jnp.sum(_row_to_col(q_r, ML_D) * c_new, axis=0, keepdims=True)
            den = jnp.sum(q_r * n_new, axis=1, keepdims=True)
            h_heads.append(num / jnp.maximum(jnp.abs(den), jnp.exp(-m_new)))
            m_new_row = jnp.where(lane == h, m_new, m_new_row)
        no_ref[b] = jnp.concatenate(n_heads, axis=0)
        h_ref[row, :] = jnp.concatenate(h_heads, axis=1)
        mo_ref[row, :] = m_new_row
        return carry

    lax.fori_loop(0, bb, per_row, 0)
    hh = h_ref[...]
    og = og_ref[...]
    for h in range(ML_HEADS):
        sl = slice(h * ML_D, (h + 1) * ML_D)
        y_ref[:, sl] = (_head_norm(hh[:, sl], gn_ref[:, sl]) * _sigmoid(og[:, sl])).astype(BF16)


def _mlstm_decode(z, zs, gbias, gn, c_all, n_all, m_pad, *, layer, bb):
    rows = z.shape[0]
    blk = lambda j: pl.BlockSpec((bb, 512), lambda i: (i, j))
    const = lambda shape: pl.BlockSpec(shape, lambda i: (0,) * len(shape))
    cblock = (bb, ML_HEADS, ML_D, ML_D)
    nblock = (bb, ML_HEADS, ML_D)
    cspec = pl.BlockSpec(cblock, lambda i: (i, 0, 0, 0))
    nspec = pl.BlockSpec(nblock, lambda i: (i, 0, 0))
    mspec = pl.BlockSpec((bb, LANES), lambda i: (i, 0))
    return pl.pallas_call(
        functools.partial(_mlstm_decode_kernel, bb=bb), grid=(rows // bb,),
        in_specs=[blk(Z_MQ), blk(Z_MK), blk(Z_MV), blk(Z_MO), mspec,
                  const((1, LANES)), const((1, ML_W)),
                  pl.BlockSpec((None,) + cblock, lambda i: (layer, i, 0, 0, 0)),
                  pl.BlockSpec((None,) + nblock, lambda i: (layer, i, 0, 0)), mspec],
        out_specs=[pl.BlockSpec((bb, ML_W), lambda i: (i, 0)), cspec, nspec, mspec],
        out_shape=[jax.ShapeDtypeStruct((rows, ML_W), BF16),
                   jax.ShapeDtypeStruct(c_all.shape[1:], F32), jax.ShapeDtypeStruct(n_all.shape[1:], F32),
                   jax.ShapeDtypeStruct((rows, LANES), F32)],
        scratch_shapes=[pltpu.VMEM((bb, LANES), F32), pltpu.VMEM((bb, LANES), F32),
                        pltpu.VMEM((bb, ML_W), F32)],
        compiler_params=_cparams(("arbitrary",)), name="mlstm_decode",
    )(z, z, z, z, zs, gbias, gn, c_all, n_all, m_pad)


def _merge_kernel(ya_ref, yb_ref, yp_ref, u_ref, za_ref, zb_ref, zc_ref, x_ref,
                  d_ref, wg_ref, wa_ref, wb_ref, wc_ref, wo_ref, o_ref):
    yc = _gelu_tanh(yp_ref[...] + d_ref[...] * u_ref[...].astype(F32))
    yc = yc * _sigmoid(_dot(yc.astype(BF16), wg_ref[...]))
    m = _sigmoid(za_ref[...].astype(F32)) * _dot(ya_ref[...], wa_ref[...])
    m = m + _sigmoid(zb_ref[...].astype(F32)) * _dot(yb_ref[...], wb_ref[...])
    m = m + _sigmoid(zc_ref[...].astype(F32)) * _dot(yc.astype(BF16), wc_ref[...])
    o_ref[...] = x_ref[...] + _dot(m.astype(BF16), wo_ref[...])


def _merge(ya, yb, ypre, z, x, d, wg, wa, wb, wc, wo, *, bm):
    n = x.shape[0]
    r512 = pl.BlockSpec((bm, 512), lambda i: (i, 0))
    zblk = lambda j: pl.BlockSpec((bm, D_MODEL), lambda i: (i, j))
    const = lambda shape: pl.BlockSpec(shape, lambda i: (0, 0))
    return pl.pallas_call(
        _merge_kernel, grid=(n // bm,),
        in_specs=[r512, r512, r512, pl.BlockSpec((bm, 512), lambda i: (i, Z_SU)),
                  zblk(Z_ZA), zblk(Z_ZB), zblk(Z_ZC), pl.BlockSpec((bm, D_MODEL), lambda i: (i, 0)),
                  const((1, S5_WIDTH)), const((S5_WIDTH, S5_WIDTH)),
                  const((GLA_V, D_MODEL)), const((ML_W, D_MODEL)), const((S5_WIDTH, D_MODEL)),
                  const((D_MODEL, D_MODEL))],
        out_specs=pl.BlockSpec((bm, D_MODEL), lambda i: (i, 0)),
        out_shape=jax.ShapeDtypeStruct((n, D_MODEL), F32),
        compiler_params=_cparams(("arbitrary",)), name="merge",
    )(ya, yb, ypre, z, z, z, z, x, d, wg, wa, wb, wc, wo)


def _cross_prompt_kernel(x_ref, g_ref, wq_ref, mk_ref, mv_ref, wo_ref, o_ref):
    x = x_ref[...]
    q = _dot(_rms_rows(x, g_ref[...]).astype(BF16), wq_ref[...])
    heads = []
    for h in range(X_HEADS):
        sl = slice(h * X_DH, (h + 1) * X_DH)
        s = _dot_nt(q[:, sl].astype(BF16), mk_ref[:, sl].astype(BF16)) * (X_DH ** -0.5)
        s = s - jnp.max(s, axis=-1, keepdims=True)
        p = jnp.exp(s)
        p = p / jnp.sum(p, axis=-1, keepdims=True)
        heads.append(_dot(p.astype(BF16), mv_ref[:, sl].astype(BF16)))
    o = jnp.concatenate(heads, axis=-1)
    o_ref[...] = x + _dot(o.astype(BF16), wo_ref[...])


def _memory_kv_kernel(x_ref, g_ref, wk_ref, wv_ref, k_ref, v_ref, hn_ref):
    @pl.when(pl.program_id(1) == 0)
    def _():
        hn_ref[...] = _rms_rows(x_ref[...], g_ref[...]).astype(BF16)
    k_ref[...] = _dot(hn_ref[...], wk_ref[...])
    v_ref[...] = _dot(hn_ref[...], wv_ref[...])


def _memory_kv(mem, g, wk, wv, *, bn=512):
    rows, d = mem.shape
    depth = wk.shape[0]
    wspec = pl.BlockSpec((None, d, bn), lambda l, j: (l, 0, j))
    ospec = pl.BlockSpec((None, rows, bn), lambda l, j: (l, 0, j))
    out = jax.ShapeDtypeStruct((depth, rows, d), F32)
    return pl.pallas_call(
        _memory_kv_kernel, grid=(depth, d // bn),
        in_specs=[pl.BlockSpec((rows, d), lambda l, j: (0, 0)),
                  pl.BlockSpec((None, 1, d), lambda l, j: (l, 0, 0)), wspec, wspec],
        out_specs=[ospec, ospec], out_shape=[out, out],
        scratch_shapes=[pltpu.VMEM((rows, d), BF16)],
        compiler_params=_cparams(("arbitrary", "arbitrary")), name="memory_kv",
    )(mem, g, wk, wv)


def _cross_prompt(x, g, wq, mem_k, mem_v, wo, *, layer, batch, seq, tq):
    nt = seq // tq
    xspec = pl.BlockSpec((tq, D_MODEL), lambda b, t: (b * nt + t, 0))
    const = lambda shape: pl.BlockSpec(shape, lambda b, t: (0, 0))
    mspec = pl.BlockSpec((None, MEM_LEN, D_MODEL), lambda b, t: (layer, b, 0))
    return pl.pallas_call(
        _cross_prompt_kernel, grid=(batch, nt),
        in_specs=[xspec, const((1, D_MODEL)), const((D_MODEL, D_MODEL)), mspec, mspec,
                  const((D_MODEL, D_MODEL))],
        out_specs=xspec, out_shape=jax.ShapeDtypeStruct(x.shape, F32),
        compiler_params=_cparams(("arbitrary", "arbitrary")), name="cross_prompt",
    )(x, g, wq, mem_k, mem_v, wo)


def _cross_decode_kernel(x_ref, g_ref, wq_ref, mk_ref, mv_ref, wo_ref, o_ref, att_ref, *, bb):
    x = x_ref[...]
    q = _dot(_rms_rows(x, g_ref[...]).astype(BF16), wq_ref[...])
    for b in range(bb):
        q4 = jnp.concatenate([q[b:b + 1, h * X_DH:(h + 1) * X_DH] for h in range(X_HEADS)], axis=0)
        s = jnp.sum(mk_ref[0, b] * q4[None], axis=-1, keepdims=True) * (X_DH ** -0.5)
        p = jnp.exp(s - jnp.max(s, axis=0, keepdims=True))
        p = p / jnp.sum(p, axis=0, keepdims=True)
        o4 = jnp.sum(p * mv_ref[0, b], axis=0)
        att_ref[b:b + 1, :] = jnp.concatenate([o4[h:h + 1, :] for h in range(X_HEADS)], axis=1)
    o_ref[...] = x + _dot(att_ref[...].astype(BF16), wo_ref[...])


def _cross_decode(x, g, wq, cache_k, cache_v, wo, *, layer, bb):
    rows = x.shape[0]
    xspec = pl.BlockSpec((bb, D_MODEL), lambda i: (i, 0))
    const = lambda shape: pl.BlockSpec(shape, lambda i: (0, 0))
    mspec = pl.BlockSpec((1, bb, MEM_LEN, X_HEADS, X_DH), lambda i: (layer, i, 0, 0, 0))
    return pl.pallas_call(
        functools.partial(_cross_decode_kernel, bb=bb), grid=(rows // bb,),
        in_specs=[xspec, const((1, D_MODEL)), const((D_MODEL, D_MODEL)), mspec, mspec,
                  const((D_MODEL, D_MODEL))],
        out_specs=xspec, out_shape=jax.ShapeDtypeStruct(x.shape, F32),
        scratch_shapes=[pltpu.VMEM((bb, D_MODEL), F32)],
        compiler_params=_cparams(("arbitrary",)), name="cross_decode",
    )(x, g, wq, cache_k, cache_v, wo)


def _ffn_prompt_kernel(x_ref, g_ref, wup_ref, cw_ref, cb_ref, wdn_ref, gf_ref,
                       o_ref, st_ref, hn_ref, halo_ref, up_ref, *, tt, final_norm):
    t = pl.program_id(1)

    @pl.when(t == 0)
    def _():
        halo_ref[...] = jnp.zeros_like(halo_ref)

    x = x_ref[...]
    hn_ref[...] = _rms_rows(x, g_ref[...]).astype(BF16)
    def project(c):
        slot = c % 2
        cols = slice(c * 2 * FF_CHUNK, (c + 1) * 2 * FF_CHUNK)
        up = _dot(hn_ref[...], wup_ref[:, cols])
        up_ref[slot, 6:8, :] = halo_ref[0:2, cols]
        up_ref[slot, 8:8 + tt, :] = up
        halo_ref[0:2, cols] = up[tt - 2:tt, :]

    def gated(c):
        slot = c % 2
        cols = slice(c * 2 * FF_CHUNK, (c + 1) * 2 * FF_CHUNK)
        cv = (up_ref[slot, 6:6 + tt, :] * cw_ref[0:1, cols] + up_ref[slot, 7:7 + tt, :] * cw_ref[1:2, cols]
              + up_ref[slot, 8:8 + tt, :] * cw_ref[2:3, cols] + cb_ref[:, cols])
        gt = cv[:, FF_CHUNK:]
        return (cv[:, :FF_CHUNK] * (gt * _sigmoid(gt))).astype(BF16)

    acc = x
    project(0)
    for c in range(N_FF_CHUNKS):
        if c + 1 < N_FF_CHUNKS:
            project(c + 1)
        acc = acc + _dot(gated(c), wdn_ref[c * FF_CHUNK:(c + 1) * FF_CHUNK, :])
    if final_norm:
        acc = _rms_rows(acc, gf_ref[...])
    o_ref[...] = acc

    @pl.when(t == pl.num_programs(1) - 1)
    def _():
        st_ref[0] = halo_ref[0:2, :]


def _ffn_prompt(x, g, wup, cw, cb, wdn, gf, *, batch, seq, tt, final_norm):
    nt = seq // tt
    xspec = pl.BlockSpec((tt, D_MODEL), lambda b, t: (b * nt + t, 0))
    const = lambda shape: pl.BlockSpec(shape, lambda b, t: (0, 0), pipeline_mode=pl.Buffered(1))
    return pl.pallas_call(
        functools.partial(_ffn_prompt_kernel, tt=tt, final_norm=final_norm), grid=(batch, nt),
        in_specs=[xspec, const((1, D_MODEL)), const((D_MODEL, 2 * D_FF)), const((CONV_W, 2 * D_FF)),
                  const((1, 2 * D_FF)), const((D_FF, D_MODEL)), const((1, D_MODEL))],
        out_specs=[xspec, pl.BlockSpec((1, CONV_W - 1, 2 * D_FF), lambda b, t: (b, 0, 0))],
        out_shape=[jax.ShapeDtypeStruct(x.shape, F32),
                   jax.ShapeDtypeStruct((batch, CONV_W - 1, 2 * D_FF), F32)],
        scratch_shapes=[pltpu.VMEM((tt, D_MODEL), BF16), pltpu.VMEM((8, 2 * D_FF), F32),
                        pltpu.VMEM((2, tt + 8, 2 * FF_CHUNK), F32)],
        compiler_params=_cparams(("arbitrary", "arbitrary")), name="ffn_prompt",
    )(x, g, wup, cw, cb, wdn, gf)


def _ffn_decode_kernel(x_ref, g_ref, w_ref, cw_ref, cb_ref, b0_ref, b1_ref, wdn_ref, gf_ref,
                       o_ref, up_ref, hn_ref, acc_ref, *, final_norm):
    c = pl.program_id(0)

    @pl.when(c == 0)
    def _():
        x = x_ref[...]
        hn_ref[...] = _rms_rows(x, g_ref[...]).astype(BF16)
        acc_ref[...] = x

    up = _dot(hn_ref[...], w_ref[...])
    up_ref[...] = up
    cv = b0_ref[...] * cw_ref[0:1, :] + b1_ref[...] * cw_ref[1:2, :] + up * cw_ref[2:3, :] + cb_ref[...]
    gt = cv[:, FF_CHUNK:]
    act = cv[:, :FF_CHUNK] * (gt * _sigmoid(gt))
    acc_ref[...] += _dot(act.astype(BF16), wdn_ref[...])

    @pl.when(c == pl.num_programs(0) - 1)
    def _():
        acc = acc_ref[...]
        if final_norm:
            acc = _rms_rows(acc, gf_ref[...])
        o_ref[...] = acc


def _ffn_decode(x, g, wup, cw, cb, wdn, gf, buf0, buf1, *, final_norm):
    rows = x.shape[0]
    full = lambda shape: pl.BlockSpec(shape, lambda c: (0, 0))
    col = lambda r: pl.BlockSpec((r, 2 * FF_CHUNK), lambda c: (0, c))
    return pl.pallas_call(
        functools.partial(_ffn_decode_kernel, final_norm=final_norm), grid=(N_FF_CHUNKS,),
        in_specs=[full((rows, D_MODEL)), full((1, D_MODEL)),
                  col(D_MODEL), col(CONV_W), col(1), col(rows), col(rows),
                  pl.BlockSpec((FF_CHUNK, D_MODEL), lambda c: (c, 0)), full((1, D_MODEL))],
        out_specs=[full((rows, D_MODEL)), col(rows)],
        out_shape=[jax.ShapeDtypeStruct(x.shape, F32),
                   jax.ShapeDtypeStruct((rows, 2 * D_FF), F32)],
        scratch_shapes=[pltpu.VMEM((rows, D_MODEL), BF16), pltpu.VMEM((rows, D_MODEL), F32)],
        compiler_params=_cparams(("arbitrary",)), name="ffn_decode",
    )(x, g, wup, cw, cb, buf0, buf1, wdn, gf)


def _ff_interleave(a):
    lead = a.shape[:-1]
    a = a.reshape(lead + (2, N_FF_CHUNKS, FF_CHUNK))
    return jnp.swapaxes(a, -3, -2).reshape(lead + (2 * D_FF,))


def _ff_deinterleave(a):
    lead = a.shape[:-1]
    a = a.reshape(lead + (N_FF_CHUNKS, 2, FF_CHUNK))
    return jnp.swapaxes(a, -3, -2).reshape(lead + (2 * D_FF,))


def _layer_weights(P, l):
    w_in = P['w_in'][l]
    w_main = jnp.concatenate([w_in[:, 0:1536], w_in[:, 1552:3600], w_in[:, 3608:7192]], axis=1).astype(BF16)
    w_small = jnp.concatenate([w_in[:, 1536:1552], w_in[:, 3600:3608],
                               jnp.zeros((D_MODEL, LANES - GLA_LOWRANK - 2 * ML_HEADS), F32)], axis=1).astype(BF16)
    wa = jnp.zeros((LANES, GLA_QK), F32).at[:GLA_LOWRANK].set(P['w_gla_alpha'][l]).astype(BF16)
    gbias = jnp.zeros((1, LANES), F32)
    gbias = gbias.at[0, SM_MI:SM_MI + ML_HEADS].set(P['b_mlstm_i'][l])
    gbias = gbias.at[0, SM_MF:SM_MF + ML_HEADS].set(P['b_mlstm_f'][l])
    eye = jnp.eye(S5_GB, dtype=F32)
    w_re = jnp.einsum('ngpc,gh->ngchp', P['s5_b_re'][l].reshape(S5_NB, S5_GB, S5_P, S5_GROUP), eye)
    w_im = jnp.einsum('ngpc,gh->ngchp', P['s5_b_im'][l].reshape(S5_NB, S5_GB, S5_P, S5_GROUP), eye)
    w_blk = jnp.concatenate([w_re.reshape(S5_NB, S5_UB, S5_MB), w_im.reshape(S5_NB, S5_UB, S5_MB)],
                            axis=-1).astype(BF16)
    ct_re = jnp.einsum('ngcp,gh->ngchp', P['s5_c_re'][l].reshape(S5_NB, S5_GB, S5_GROUP, S5_P), eye)
    ct_im = jnp.einsum('ngcp,gh->ngchp', P['s5_c_im'][l].reshape(S5_NB, S5_GB, S5_GROUP, S5_P), eye)
    ct_re = ct_re.reshape(S5_NB, S5_UB, S5_MB)
    ct_im = ct_im.reshape(S5_NB, S5_UB, S5_MB)
    lam_re = P['s5_lam_re'][l].reshape(S5_NB, 1, S5_MB)
    lam_im = P['s5_lam_im'][l].reshape(S5_NB, 1, S5_MB)
    log_dt = jnp.broadcast_to(P['s5_log_dt'][l][:, None], (S5_GROUPS, S5_P)).reshape(S5_NB, 1, S5_MB)
    par, cfold = _s5_prep(lam_re, lam_im, log_dt, ct_re, ct_im)
    row = lambda a: a.reshape(1, -1)
    return dict(
        norm_mix=row(P['norm_mix'][l]), w_main=w_main, w_small=w_small, wa=wa,
        ba=row(P['b_gla_alpha'][l]), gla_norm=row(P['gla_head_norm'][l]), gbias=gbias,
        ml_norm=row(P['mlstm_head_norm'][l]), w_blk=w_blk, ct_re=ct_re.astype(BF16),
        ct_im=ct_im.astype(BF16), par=par, cfold=cfold, s5_d=row(P['s5_d'][l]),
        w_glu=P['s5_w_glu'][l].astype(BF16), w_a=P['w_branch_a'][l].astype(BF16),
        w_b=P['w_branch_b'][l].astype(BF16), w_c=P['w_branch_c'][l].astype(BF16),
        w_out=P['w_out'][l].astype(BF16), norm_cross=row(P['norm_cross'][l]),
        w_cq=P['w_cq'][l].astype(BF16), w_co=P['w_co'][l].astype(BF16),
        norm_ffn=row(P['norm_ffn'][l]), w_up=_ff_interleave(P['w_ffn_up'][l]).astype(BF16),
        conv_w=_ff_interleave(P['ffn_conv_w'][l]), conv_b=_ff_interleave(row(P['ffn_conv_b'][l])),
        w_down=P['w_ffn_down'][l].astype(BF16), norm_final=row(P['norm_final']),
    )


def _prompt_trunk(x_prompt, mem_k, mem_v, W, *, bm_in, tt_mix, gla_group, ml_group, tt_s5, bm, tq, tt_ffn):
    batch, seq, _ = x_prompt.shape
    depth = len(W)
    x = x_prompt.reshape(batch * seq, D_MODEL)
    outs = []
    for l, w in enumerate(W):
        z, zs = _norm_matmul(x, w['norm_mix'], w['w_main'], w['w_small'], bm=bm_in, bn=1024, out_dtype=BF16)
        z3 = z.reshape(batch, seq, Z_MAIN)
        zs3 = zs.reshape(batch, seq, LANES)
        ya, st = _gla_prompt(z3, zs3, w['wa'], w['ba'], w['gla_norm'], tt=tt_mix, group=gla_group)
        yb, c, n, m = _mlstm_prompt(z3, zs3, w['gbias'], w['ml_norm'], tt=tt_mix, group=ml_group)
        ypre, hfin = _s5_prompt(z3, w['w_blk'], w['cfold'], w['par'], batch=batch, seq=seq, tt=tt_s5)
        x = _merge(ya.reshape(batch * seq, GLA_V), yb.reshape(batch * seq, ML_W),
                   ypre.reshape(batch * seq, S5_WIDTH), z, x, w['s5_d'], w['w_glu'],
                   w['w_a'], w['w_b'], w['w_c'], w['w_out'], bm=bm)
        x = _cross_prompt(x, w['norm_cross'], w['w_cq'], mem_k, mem_v, w['w_co'],
                          layer=l, batch=batch, seq=seq, tq=tq)
        x, conv = _ffn_prompt(x, w['norm_ffn'], w['w_up'], w['conv_w'], w['conv_b'], w['w_down'],
                              w['norm_final'], batch=batch, seq=seq, tt=tt_ffn,
                              final_norm=(l == depth - 1))
        conv = _ff_deinterleave(conv)
        st4 = st.reshape(batch, GLA_HEADS, GLA_DV, GLA_HEADS, GLA_DK)
        gla = jnp.stack([st4[:, h, :, h, :] for h in range(GLA_HEADS)], axis=1).transpose(0, 1, 3, 2)
        h4 = hfin.reshape(S5_NB, batch, 2, S5_GB, S5_P)
        s5_re = h4[:, :, 0].transpose(1, 0, 2, 3).reshape(batch, S5_GROUPS, S5_P)
        s5_im = h4[:, :, 1].transpose(1, 0, 2, 3).reshape(batch, S5_GROUPS, S5_P)
        outs.append((gla, c.reshape(batch, ML_HEADS, ML_D, ML_D), n[:, :ML_HEADS, :],
                     m[:, :ML_HEADS, 0], s5_re, s5_im, conv))
    stacked = [jnp.stack([outs[l][i] for l in range(depth)]) for i in range(7)]
    return x.reshape(batch, seq, D_MODEL), stacked


def _sample_trunk(x_sample, cache_k, cache_v, states, W):
    rows = x_sample.shape[0]
    depth = len(W)
    x = x_sample.reshape(rows, D_MODEL)
    s_gla, s_c, s_n, s_m, s_re, s_im, s_conv = states
    outs = []
    for l, w in enumerate(W):
        z, zs = _norm_matmul(x, w['norm_mix'], w['w_main'], w['w_small'], bm=rows, bn=1024)
        ya, gla = _gla_decode(z, zs, w['wa'], w['ba'], w['gla_norm'], s_gla, layer=l, bb=16)
        m_pad = jnp.pad(s_m[l], ((0, 0), (0, LANES - ML_HEADS)))
        yb, c, n, m = _mlstm_decode(z, zs, w['gbias'], w['ml_norm'], s_c, s_n, m_pad, layer=l, bb=16)
        ypre, h_re, h_im = _s5_decode(z, w['w_blk'], w['ct_re'], w['ct_im'], w['par'],
                                      s_re[l].reshape(rows, S5_MODES), s_im[l].reshape(rows, S5_MODES))
        x = _merge(ya, yb, ypre, z, x, w['s5_d'], w['w_glu'], w['w_a'], w['w_b'], w['w_c'], w['w_out'],
                   bm=rows)
        x = _cross_decode(x, w['norm_cross'], w['w_cq'], cache_k, cache_v, w['w_co'], layer=l, bb=8)
        buf = _ff_interleave(s_conv[l])
        x, up = _ffn_decode(x, w['norm_ffn'], w['w_up'], w['conv_w'], w['conv_b'], w['w_down'],
                            w['norm_final'], buf[:, 0, :], buf[:, 1, :], final_norm=(l == depth - 1))
        conv = jnp.stack([s_conv[l][:, 1, :], _ff_deinterleave(up)], axis=1)
        outs.append((gla, c, n, m[:, :ML_HEADS], h_re.reshape(rows, S5_GROUPS, S5_P),
                     h_im.reshape(rows, S5_GROUPS, S5_P), conv))
    stacked = [jnp.stack([outs[l][i] for l in range(depth)]) for i in range(7)]
    return x.reshape(rows, 1, D_MODEL), stacked


def kernel(x_prompt, x_sample, mem_prompt, cache_mem_k, cache_mem_v, state_gla, state_mlstm_c, state_mlstm_n, state_mlstm_m, state_s5_re, state_s5_im, state_ffn_conv, norm_mix, w_in, w_gla_alpha, b_gla_alpha, gla_head_norm, b_mlstm_i, b_mlstm_f, mlstm_head_norm, s5_lam_re, s5_lam_im, s5_log_dt, s5_b_re, s5_b_im, s5_c_re, s5_c_im, s5_d, s5_w_glu, w_branch_a, w_branch_b, w_branch_c, w_out, norm_cross, norm_mem, w_cq, w_ck, w_cv, w_co, norm_ffn, w_ffn_up, ffn_conv_w, ffn_conv_b, w_ffn_down, norm_final):
    P = dict(norm_mix=norm_mix, w_in=w_in, w_gla_alpha=w_gla_alpha, b_gla_alpha=b_gla_alpha,
             gla_head_norm=gla_head_norm, b_mlstm_i=b_mlstm_i, b_mlstm_f=b_mlstm_f,
             mlstm_head_norm=mlstm_head_norm, s5_lam_re=s5_lam_re, s5_lam_im=s5_lam_im,
             s5_log_dt=s5_log_dt, s5_b_re=s5_b_re, s5_b_im=s5_b_im, s5_c_re=s5_c_re,
             s5_c_im=s5_c_im, s5_d=s5_d, s5_w_glu=s5_w_glu, w_branch_a=w_branch_a,
             w_branch_b=w_branch_b, w_branch_c=w_branch_c, w_out=w_out, norm_cross=norm_cross,
             w_cq=w_cq, w_co=w_co, norm_ffn=norm_ffn, w_ffn_up=w_ffn_up, ffn_conv_w=ffn_conv_w,
             ffn_conv_b=ffn_conv_b, w_ffn_down=w_ffn_down, norm_final=norm_final)
    depth = w_in.shape[0]
    W = [_layer_weights(P, l) for l in range(depth)]
    batch, mem_len, _ = mem_prompt.shape
    mem2 = mem_prompt.reshape(batch * mem_len, D_MODEL)
    mem_k, mem_v = _memory_kv(mem2, norm_mem.reshape(depth, 1, D_MODEL), w_ck.astype(BF16), w_cv.astype(BF16))
    y_prompt, p_states = _prompt_trunk(x_prompt, mem_k, mem_v, W, bm_in=2048, tt_mix=256, gla_group=2, ml_group=2, tt_s5=256, bm=512, tq=512,
                                       tt_ffn=512)
    p_mem_k = mem_k.reshape(depth, batch, mem_len, X_HEADS, X_DH)
    p_mem_v = mem_v.reshape(depth, batch, mem_len, X_HEADS, X_DH)
    y_sample, s_states = _sample_trunk(
        x_sample, cache_mem_k, cache_mem_v,
        (state_gla, state_mlstm_c, state_mlstm_n, state_mlstm_m, state_s5_re, state_s5_im, state_ffn_conv), W)
    return (y_prompt, y_sample, *p_states, p_mem_k, p_mem_v, *s_states)
```

```python
import functools

import jax
import jax.numpy as jnp
from jax import lax
from jax.experimental import pallas as pl
from jax.experimental.pallas import tpu as pltpu

F32 = jnp.float32
BF16 = jnp.bfloat16

D_MODEL = 1024
GLA_HEADS, GLA_DK, GLA_DV = 4, 64, 128
GLA_QK = GLA_HEADS * GLA_DK
GLA_V = GLA_HEADS * GLA_DV
GLA_LOWRANK = 16
GLA_TAU = 16.0
ML_HEADS, ML_D = 4, 128
ML_W = ML_HEADS * ML_D
S5_GROUP, S5_GROUPS, S5_P = 16, 32, 64
S5_WIDTH = S5_GROUP * S5_GROUPS
S5_MODES = S5_GROUPS * S5_P
S5_NB = 4
S5_GB = S5_GROUPS // S5_NB
S5_MB = S5_MODES // S5_NB
S5_UB = S5_WIDTH // S5_NB
MEM_LEN = 256
X_HEADS = 4
X_DH = D_MODEL // X_HEADS
D_FF = 2816
CONV_W = 3
EPS = 1e-6

CHUNK = 256
LANES = 128
FF_CHUNK = 256
N_FF_CHUNKS = D_FF // FF_CHUNK

Z_MAIN = 7168
Z_QK, Z_GV, Z_GR, Z_MQ, Z_MK, Z_MV, Z_MO, Z_SU = range(8)
Z_ZA, Z_ZB, Z_ZC = 4, 5, 6
SM_MI = GLA_LOWRANK
SM_MF = GLA_LOWRANK + ML_HEADS

GLA_SAFE_DECAY = 80.0

VMEM_LIMIT = 56 * 1024 * 1024


def _cparams(sem):
    return pltpu.CompilerParams(dimension_semantics=sem, vmem_limit_bytes=VMEM_LIMIT)


def _dot(a, b):
    return jnp.dot(a, b, preferred_element_type=F32)


def _dot_nt(a, b):
    return lax.dot_general(a, b, (((1,), (1,)), ((), ())), preferred_element_type=F32)


def _dot_tn(a, b):
    return lax.dot_general(a, b, (((0,), (0,)), ((), ())), preferred_element_type=F32)


def _sigmoid(x):
    return 1.0 / (1.0 + jnp.exp(-x))


def _log_sigmoid(x):
    return jnp.minimum(x, 0.0) - jnp.log(1.0 + jnp.exp(-jnp.abs(x)))


def _gelu_tanh(x):
    return 0.5 * x * (1.0 + jnp.tanh(0.7978845608028654 * (x + 0.044715 * x * x * x)))


def _rms_rows(x, g):
    r = lax.rsqrt(jnp.mean(x * x, axis=-1, keepdims=True) + EPS)
    return (x * r) * g


def _dot_exact01(a01, x):
    hi = x.astype(BF16)
    r1 = x - hi.astype(F32)
    mid = r1.astype(BF16)
    lo = (r1 - mid.astype(F32)).astype(BF16)
    return _dot(a01, hi) + _dot(a01, mid) + _dot(a01, lo)


def _dot_exact01_rhs(x, b01):
    hi = x.astype(BF16)
    r1 = x - hi.astype(F32)
    mid = r1.astype(BF16)
    lo = (r1 - mid.astype(F32)).astype(BF16)
    return _dot(hi, b01) + _dot(mid, b01) + _dot(lo, b01)


def _row_to_col(row, n):
    eye = (lax.broadcasted_iota(jnp.int32, (n, n), 0) == lax.broadcasted_iota(jnp.int32, (n, n), 1))
    return jnp.sum(jnp.where(eye, jnp.broadcast_to(row, (n, n)), 0.0), axis=1, keepdims=True)


def _norm_mm_small_kernel(x_ref, g_ref, w_ref, ws_ref, o_ref, os_ref, hn_ref):
    @pl.when(pl.program_id(1) == 0)
    def _():
        hn = _rms_rows(x_ref[...], g_ref[...]).astype(BF16)
        hn_ref[...] = hn
        os_ref[...] = _dot(hn, ws_ref[...])
    o_ref[...] = _dot(hn_ref[...], w_ref[...]).astype(o_ref.dtype)


def _norm_matmul(x, g, w, w_small, *, bm, bn, out_dtype=F32):
    n, d = x.shape
    c = w.shape[1]
    grid = (n // bm, c // bn)
    x_spec = pl.BlockSpec((bm, d), lambda i, j: (i, 0))
    g_spec = pl.BlockSpec((1, d), lambda i, j: (0, 0))
    w_spec = pl.BlockSpec((d, bn), lambda i, j: (0, j))
    o_spec = pl.BlockSpec((bm, bn), lambda i, j: (i, j))
    scratch = [pltpu.VMEM((bm, d), BF16)]
    cs = w_small.shape[1]
    return pl.pallas_call(
        _norm_mm_small_kernel, grid=grid,
        in_specs=[x_spec, g_spec, w_spec, pl.BlockSpec((d, cs), lambda i, j: (0, 0))],
        out_specs=[o_spec, pl.BlockSpec((bm, cs), lambda i, j: (i, 0))],
        out_shape=[jax.ShapeDtypeStruct((n, c), out_dtype), jax.ShapeDtypeStruct((n, cs), F32)],
        scratch_shapes=scratch,
        compiler_params=_cparams(("arbitrary", "arbitrary")), name="norm_matmul_in",
    )(x, g, w, w_small)


def _head_norm(oh, gn_h):
    r = lax.rsqrt(jnp.mean(oh * oh, axis=-1, keepdims=True) + EPS)
    return oh * r * gn_h


def _head_norm_mxu(oh, gn_h, ones_blk):
    sq = oh * oh
    hi = sq.astype(BF16)
    lo = (sq - hi.astype(F32)).astype(BF16)
    ms = (_dot(hi, ones_blk) + _dot(lo, ones_blk)) * (1.0 / GLA_DV)
    return oh * lax.rsqrt(ms + EPS) * gn_h


def _gla_log_decay(small, wa_ref, ba_ref):
    a_pre = _dot(small.astype(BF16), wa_ref[...]) + ba_ref[...]
    return _log_sigmoid(a_pre) * (1.0 / GLA_TAU)


def _gla_prompt_kernel(qk_ref, v_ref, gr_ref, sm_ref, wa_ref, ba_ref, gn_ref,
                       y_ref, st_ref, state_ref, b_ref, k_ref, *, n_chunks, group):
    t = pl.program_id(1)
    L = CHUNK

    @pl.when(t == 0)
    def _():
        state_ref[...] = jnp.zeros_like(state_ref)

    ri = lax.broadcasted_iota(jnp.int32, (L, L), 0)
    ci = lax.broadcasted_iota(jnp.int32, (L, L), 1)
    tril = jnp.where(ri >= ci, 1.0, 0.0).astype(BF16)
    rw = lax.broadcasted_iota(jnp.int32, (L, GLA_HEADS * L), 0)
    cw = lax.broadcasted_iota(jnp.int32, (L, GLA_HEADS * L), 1)
    causal_wide = (cw % L) <= rw
    kr = lax.broadcasted_iota(jnp.int32, (GLA_HEADS * L, GLA_QK), 0)
    kc = lax.broadcasted_iota(jnp.int32, (GLA_HEADS * L, GLA_QK), 1)
    kk_mask = (kr // L) == (kc // GLA_DK)
    vr = lax.broadcasted_iota(jnp.int32, (GLA_HEADS * L, GLA_V), 0)
    vc = lax.broadcasted_iota(jnp.int32, (GLA_HEADS * L, GLA_V), 1)
    vv_mask = (vr // L) == (vc // GLA_DV)
    sr = lax.broadcasted_iota(jnp.int32, (GLA_V, GLA_QK), 0)
    sc = lax.broadcasted_iota(jnp.int32, (GLA_V, GLA_QK), 1)
    st_mask = (sr // GLA_DV) == (sc // GLA_DK)
    hr = lax.broadcasted_iota(jnp.int32, (GLA_QK, LANES), 0)
    hc = lax.broadcasted_iota(jnp.int32, (GLA_QK, LANES), 1)
    head_ones = jnp.where((hr // GLA_DK) == hc, 1.0, 0.0).astype(BF16)
    pr = lax.broadcasted_iota(jnp.int32, (LANES, GLA_HEADS * L), 0)
    pc = lax.broadcasted_iota(jnp.int32, (LANES, GLA_HEADS * L), 1)

    def one(e, c):
        r0 = pl.multiple_of(c * L, L)
        rows = pl.ds(r0, L)
        qk = qk_ref[e, rows, :].astype(F32)
        q = qk[:, :GLA_QK] * (GLA_DK ** -0.5)
        k = qk[:, GLA_QK:]
        v16 = v_ref[e, rows, :].astype(BF16)
        g = _gla_log_decay(sm_ref[e, rows, :], wa_ref, ba_ref)
        b = _dot_exact01(tril, g)
        b_end = b[L - 1:L, :]
        qe = q * jnp.exp(b)
        k_dec = k * jnp.exp(b_end - b)
        qe16 = qe.astype(BF16)

        def fast_att(_):
            k_til = k * jnp.exp(-b)
            kk = jnp.where(kk_mask, jnp.concatenate([k_til] * GLA_HEADS, axis=0), 0.0)
            return _dot_nt(qe16, kk.astype(BF16))

        def direct_att(_):
            b_ref[...] = b
            k_ref[...] = k

            def col(j, acc):
                kj = k_ref[pl.ds(j, 1), :]
                bj = b_ref[pl.ds(j, 1), :]
                prod = q * kj * jnp.exp(jnp.minimum(b - bj, 0.0))
                red = _dot(prod.astype(BF16), head_ones)
                place = jnp.where((pc == pr * L + j) & (pr < GLA_HEADS), 1.0, 0.0).astype(BF16)
                return acc + _dot(red.astype(BF16), place)

            return lax.fori_loop(0, L, col, jnp.zeros((L, GLA_HEADS * L), F32))

        safe = jnp.max(-b_end) <= GLA_SAFE_DECAY
        att = lax.cond(safe, fast_att, direct_att, 0)
        att = jnp.where(causal_wide, att, 0.0)

        vv = jnp.where(vv_mask, jnp.concatenate([v16] * GLA_HEADS, axis=0), jnp.zeros((), BF16))
        st = state_ref[e]
        o = _dot(att.astype(BF16), vv) + _dot_nt(qe16, st.astype(BF16))
        upd = _dot_tn(v16, k_dec.astype(BF16))
        st_new = jnp.where(st_mask, st * jnp.exp(b_end) + upd, 0.0)

        gr = gr_ref[e, rows, :].astype(F32)
        ys = []
        for h in range(GLA_HEADS):
            sl = slice(h * GLA_DV, (h + 1) * GLA_DV)
            gate = gr[:, sl]
            ys.append((_head_norm(o[:, sl], gn_ref[:, sl]) * (gate * _sigmoid(gate))).astype(BF16))
        return jnp.concatenate(ys, axis=1), st_new

    def chunk(c, carry):
        res = [one(e, c) for e in range(group)]
        rows = pl.ds(pl.multiple_of(c * L, L), L)
        y_ref[:, rows, :] = jnp.stack([r[0] for r in res])
        state_ref[...] = jnp.stack([r[1] for r in res])
        return carry

    lax.fori_loop(0, n_chunks, chunk, 0)

    @pl.when(t == pl.num_programs(1) - 1)
    def _():
        st_ref[...] = state_ref[...]


def _gla_prompt(z3, zs3, wa, ba, gn, *, tt, group):
    batch, seq, _ = z3.shape
    blk = lambda j: pl.BlockSpec((group, tt, 512), lambda b, t: (b, t, j))
    const = lambda shape: pl.BlockSpec(shape, lambda b, t: (0,) * len(shape))
    return pl.pallas_call(
        functools.partial(_gla_prompt_kernel, n_chunks=tt // CHUNK, group=group),
        grid=(batch // group, seq // tt),
        in_specs=[blk(Z_QK), blk(Z_GV), blk(Z_GR),
                  pl.BlockSpec((group, tt, LANES), lambda b, t: (b, t, 0)),
                  const((LANES, GLA_QK)), const((1, GLA_QK)), const((1, GLA_V))],
        out_specs=[pl.BlockSpec((group, tt, GLA_V), lambda b, t: (b, t, 0)),
                   pl.BlockSpec((group, GLA_V, GLA_QK), lambda b, t: (b, 0, 0))],
        out_shape=[jax.ShapeDtypeStruct((batch, seq, GLA_V), BF16),
                   jax.ShapeDtypeStruct((batch, GLA_V, GLA_QK), F32)],
        scratch_shapes=[pltpu.VMEM((group, GLA_V, GLA_QK), F32), pltpu.VMEM((CHUNK, GLA_QK), F32),
                        pltpu.VMEM((CHUNK, GLA_QK), F32)],
        compiler_params=_cparams(("arbitrary", "arbitrary")), name="gla_prompt",
    )(z3, z3, z3, zs3, wa, ba, gn)


def _mlstm_prompt_kernel(q_ref, k_ref, v_ref, og_ref, sm_ref, gb_ref, gn_ref,
                         y_ref, c_out, n_out, m_out, c_ref, n_ref, m_ref, *, n_chunks, group):
    t = pl.program_id(1)
    L = CHUNK

    @pl.when(t == 0)
    def _():
        c_ref[...] = jnp.zeros_like(c_ref)
        n_ref[...] = jnp.zeros_like(n_ref)
        m_ref[...] = jnp.zeros_like(m_ref)

    ri = lax.broadcasted_iota(jnp.int32, (L, L), 0)
    ci = lax.broadcasted_iota(jnp.int32, (L, L), 1)
    lower = ri >= ci
    tril = jnp.where(lower, 1.0, 0.0).astype(BF16)
    triu = jnp.where(ri <= ci, 1.0, 0.0).astype(BF16)
    sr = lax.broadcasted_iota(jnp.int32, (LANES, ML_W), 0)
    sh = lax.broadcasted_iota(jnp.int32, (LANES, ML_W), 1) // ML_D
    sel_f = jnp.where(sr == SM_MF + sh, 1.0, 0.0).astype(BF16)
    sel_i = jnp.where(sr == SM_MI + sh, 1.0, 0.0).astype(BF16)
    ones_l = jnp.ones((L, LANES), BF16)
    ones_d = jnp.ones((ML_D, ML_D), BF16)

    def one(e, c):
        r0 = pl.multiple_of(c * L, L)
        rows = pl.ds(r0, L)
        sm = sm_ref[e, rows, :] + gb_ref[...]
        lf_all = _log_sigmoid(sm)
        sm_t = sm.T
        b_cols = _dot_exact01(tril, lf_all)
        b_rows = _dot_exact01_rhs(_log_sigmoid(sm_t), triu)
        b_wide = _dot_exact01_rhs(b_cols, sel_f)
        li_wide = _dot_exact01_rhs(sm, sel_i)
        ys, cs, ns, ms = [], [], [], []
        for h in range(ML_HEADS):
            sl = slice(h * ML_D, (h + 1) * ML_D)
            b_tok = b_wide[:, sl]
            li_tok = li_wide[:, sl]
            li_row = sm_t[SM_MI + h:SM_MI + h + 1, :]
            b_row = b_rows[SM_MF + h:SM_MF + h + 1, :]
            m_prev = m_ref[e, h:h + 1, :]
            w = jnp.where(lower, jnp.concatenate([b_tok] * (L // LANES), axis=1) - b_row + li_row, -jnp.inf)
            m_tok = jnp.maximum(b_tok + m_prev, jnp.max(w, axis=1, keepdims=True))
            a_inter = jnp.exp(b_tok + m_prev - m_tok)
            qh = q_ref[e, rows, sl].astype(F32) * (ML_D ** -0.5)
            kh16 = k_ref[e, rows, sl].astype(BF16)
            kh = kh16.astype(F32)
            qh16, vh16 = qh.astype(BF16), v_ref[e, rows, sl].astype(BF16)
            s = _dot_nt(qh16, kh16) * jnp.exp(w - jnp.concatenate([m_tok] * (L // LANES), axis=1))
            c_h = c_ref[e, sl, :]
            n_h = n_ref[e, h:h + 1, :]
            s16 = s.astype(BF16)
            s_lo = (s - s16.astype(F32)).astype(BF16)
            row_sum = _dot(s16, ones_l) + _dot(s_lo, ones_l)
            q_n = _dot_nt(qh16, jnp.broadcast_to(n_h, (LANES, ML_D)).astype(BF16))
            num = _dot(s16, vh16) + a_inter * _dot(qh16, c_h.astype(BF16))
            den = row_sum + a_inter * q_n
            hh = num / jnp.maximum(jnp.abs(den), jnp.exp(-m_tok))
            og = og_ref[e, rows, sl].astype(F32)
            ys.append((_head_norm_mxu(hh, gn_ref[:, sl], ones_d) * _sigmoid(og)).astype(BF16))
            b_end = b_tok[L - 1:L, :]
            w_end = b_end - b_tok + li_tok
            m_new = jnp.maximum(b_end + m_prev, jnp.max(w_end, axis=0, keepdims=True))
            e_inter = jnp.exp(b_end + m_prev - m_new)
            kd = kh * jnp.exp(w_end - m_new)
            cs.append(e_inter * c_h + _dot_tn(kd.astype(BF16), vh16))
            ns.append(e_inter * n_h + jnp.sum(kd, axis=0, keepdims=True))
            ms.append(m_new)
        pad = [jnp.zeros((8 - ML_HEADS, LANES), F32)]
        return (jnp.concatenate(ys, axis=1), jnp.concatenate(cs, axis=0),
                jnp.concatenate(ns + pad, axis=0), jnp.concatenate(ms + pad, axis=0))

    def chunk(c, carry):
        res = [one(e, c) for e in range(group)]
        rows = pl.ds(pl.multiple_of(c * L, L), L)
        y_ref[:, rows, :] = jnp.stack([r[0] for r in res])
        c_ref[...] = jnp.stack([r[1] for r in res])
        n_ref[...] = jnp.stack([r[2] for r in res])
        m_ref[...] = jnp.stack([r[3] for r in res])
        return carry

    lax.fori_loop(0, n_chunks, chunk, 0)

    @pl.when(t == pl.num_programs(1) - 1)
    def _():
        c_out[...] = c_ref[...]
        n_out[...] = n_ref[...]
        m_out[...] = m_ref[...]


def _mlstm_prompt(z3, zs3, gbias, gn, *, tt, group):
    batch, seq, _ = z3.shape
    blk = lambda j: pl.BlockSpec((group, tt, 512), lambda b, t: (b, t, j))
    const = lambda shape: pl.BlockSpec(shape, lambda b, t: (0,) * len(shape))
    state = lambda shape: pl.BlockSpec((group,) + shape, lambda b, t: (b, 0, 0))
    return pl.pallas_call(
        functools.partial(_mlstm_prompt_kernel, n_chunks=tt // CHUNK, group=group),
        grid=(batch // group, seq // tt),
        in_specs=[blk(Z_MQ), blk(Z_MK), blk(Z_MV), blk(Z_MO),
                  pl.BlockSpec((group, tt, LANES), lambda b, t: (b, t, 0)),
                  const((1, LANES)), const((1, ML_W))],
        out_specs=[pl.BlockSpec((group, tt, ML_W), lambda b, t: (b, t, 0)),
                   state((ML_W, ML_D)), state((8, LANES)), state((8, LANES))],
        out_shape=[jax.ShapeDtypeStruct((batch, seq, ML_W), BF16),
                   jax.ShapeDtypeStruct((batch, ML_W, ML_D), F32),
                   jax.ShapeDtypeStruct((batch, 8, LANES), F32),
                   jax.ShapeDtypeStruct((batch, 8, LANES), F32)],
        scratch_shapes=[pltpu.VMEM((group, ML_W, ML_D), F32), pltpu.VMEM((group, 8, LANES), F32),
                        pltpu.VMEM((group, 8, LANES), F32)],
        compiler_params=_cparams(("arbitrary", "arbitrary")), name="mlstm_prompt",
    )(z3, z3, z3, z3, zs3, gbias, gn)


def _s5_prep_kernel(lre_ref, lim_ref, ldt_ref, ctr_ref, cti_ref, par_ref, cf_ref):
    lam_re = lre_ref[0]
    lam_im = lim_ref[0]
    dt = jnp.exp(ldt_ref[0])
    mag = jnp.exp(lam_re * dt)
    lb_re = mag * jnp.cos(lam_im * dt)
    lb_im = mag * jnp.sin(lam_im * dt)
    nr = lb_re - 1.0
    den = lam_re * lam_re + lam_im * lam_im
    f_re = (nr * lam_re + lb_im * lam_im) / den
    f_im = (lb_im * lam_re - nr * lam_im) / den
    par_ref[0] = jnp.concatenate([lb_re, lb_im, f_re, f_im, jnp.zeros((4, S5_MB), F32)], axis=0)
    ct_re = ctr_ref[0]
    ct_im = cti_ref[0]
    cf_ref[0, :, :S5_MB] = (ct_re * f_re - ct_im * f_im).astype(BF16)
    cf_ref[0, :, S5_MB:] = (-(ct_re * f_im + ct_im * f_re)).astype(BF16)


def _s5_prep(lam_re, lam_im, log_dt, ct_re, ct_im):
    vec = pl.BlockSpec((1, 1, S5_MB), lambda n: (n, 0, 0))
    mat = pl.BlockSpec((1, S5_UB, S5_MB), lambda n: (n, 0, 0))
    return pl.pallas_call(
        _s5_prep_kernel, grid=(S5_NB,),
        in_specs=[vec, vec, vec, mat, mat],
        out_specs=[pl.BlockSpec((1, 8, S5_MB), lambda n: (n, 0, 0)),
                   pl.BlockSpec((1, S5_UB, 2 * S5_MB), lambda n: (n, 0, 0))],
        out_shape=[jax.ShapeDtypeStruct((S5_NB, 8, S5_MB), F32),
                   jax.ShapeDtypeStruct((S5_NB, S5_UB, 2 * S5_MB), BF16)],
        compiler_params=_cparams(("arbitrary",)), name="s5_prep",
    )(lam_re, lam_im, log_dt, ct_re, ct_im)


def _s5_prompt_kernel(u_ref, w_ref, cf_ref, par_ref, y_ref, h_out, x_ref, h_ref, ub_ref, ut_ref, *, batch, tt):
    t_blk = pl.program_id(1)

    @pl.when(t_blk == 0)
    def _():
        h_ref[...] = jnp.zeros_like(h_ref)

    nlb = S5_MB // LANES
    ub_ref[...] = u_ref[...].astype(F32).reshape(batch * tt, S5_UB)

    def interleave(t, carry):
        ut_ref[pl.ds(pl.multiple_of(t * batch, batch), batch), :] = ub_ref[pl.ds(t, batch, stride=tt), :]
        return carry

    lax.fori_loop(0, tt, interleave, 0, unroll=8)
    x = _dot(ut_ref[...].astype(BF16), w_ref[0])
    for j in range(2 * nlb):
        x_ref[j] = x[:, j * LANES:(j + 1) * LANES]

    par = par_ref[0]
    lr = [jnp.broadcast_to(par[0:1, j * LANES:(j + 1) * LANES], (batch, LANES)) for j in range(nlb)]
    li = [jnp.broadcast_to(par[1:2, j * LANES:(j + 1) * LANES], (batch, LANES)) for j in range(nlb)]

    def step(t, carry):
        rows = pl.ds(pl.multiple_of(t * batch, batch), batch)
        new = []
        for j in range(nlb):
            hr, hi = carry[j], carry[nlb + j]
            nr = lr[j] * hr - li[j] * hi + x_ref[j, rows, :]
            ni = lr[j] * hi + li[j] * hr + x_ref[nlb + j, rows, :]
            x_ref[j, rows, :] = nr
            x_ref[nlb + j, rows, :] = ni
            new.append((nr, ni))
        return tuple(n[0] for n in new) + tuple(n[1] for n in new)

    h0 = tuple(h_ref[:, j * LANES:(j + 1) * LANES] for j in range(2 * nlb))
    hfin = lax.fori_loop(0, tt, step, h0, unroll=8)
    for j in range(2 * nlb):
        h_ref[:, j * LANES:(j + 1) * LANES] = hfin[j]
    hr = jnp.concatenate(hfin[:nlb], axis=1)
    hi = jnp.concatenate(hfin[nlb:], axis=1)

    hall = jnp.concatenate([x_ref[j].astype(BF16) for j in range(2 * nlb)], axis=1)
    ut_ref[...] = _dot_nt(hall, cf_ref[0])
    for b in range(batch):
        y_ref[b] = ut_ref[pl.ds(b, tt, stride=batch), :]

    @pl.when(t_blk == pl.num_programs(1) - 1)
    def _():
        fr = par[2:3, :]
        fi = par[3:4, :]
        h_out[0, :, :S5_MB] = fr * hr - fi * hi
        h_out[0, :, S5_MB:] = fr * hi + fi * hr


def _s5_prompt(z3, w_blk, cfold, par, *, batch, seq, tt):
    nt = seq // tt
    u_col0 = Z_SU * 512 // S5_UB
    return pl.pallas_call(
        functools.partial(_s5_prompt_kernel, batch=batch, tt=tt),
        grid=(S5_NB, nt),
        in_specs=[pl.BlockSpec((batch, tt, S5_UB), lambda n, t: (0, t, u_col0 + n)),
                  pl.BlockSpec((1, S5_UB, 2 * S5_MB), lambda n, t: (n, 0, 0)),
                  pl.BlockSpec((1, S5_UB, 2 * S5_MB), lambda n, t: (n, 0, 0)),
                  pl.BlockSpec((1, 8, S5_MB), lambda n, t: (n, 0, 0))],
        out_specs=[pl.BlockSpec((batch, tt, S5_UB), lambda n, t: (0, t, n)),
                   pl.BlockSpec((1, batch, 2 * S5_MB), lambda n, t: (n, 0, 0))],
        out_shape=[jax.ShapeDtypeStruct((batch, seq, S5_WIDTH), F32),
                   jax.ShapeDtypeStruct((S5_NB, batch, 2 * S5_MB), F32)],
        scratch_shapes=[pltpu.VMEM((2 * S5_MB // LANES, batch * tt, LANES), F32),
                        pltpu.VMEM((batch, 2 * S5_MB), F32),
                        pltpu.VMEM((batch * tt, S5_UB), F32), pltpu.VMEM((batch * tt, S5_UB), F32)],
        compiler_params=_cparams(("arbitrary", "arbitrary")), name="s5_prompt",
    )(z3, w_blk, cfold, par)


def _s5_decode_kernel(u_ref, w_ref, ctr_ref, cti_ref, par_ref, hr_ref, hi_ref, y_ref, hro_ref, hio_ref):
    x = _dot(u_ref[...].astype(BF16), w_ref[0])
    xr, xi = x[:, :S5_MB], x[:, S5_MB:]
    par = par_ref[0]
    lr, li, fr, fi = par[0:1, :], par[1:2, :], par[2:3, :], par[3:4, :]
    h0r, h0i = hr_ref[...], hi_ref[...]
    hr = lr * h0r - li * h0i + (fr * xr - fi * xi)
    hi = lr * h0i + li * h0r + (fr * xi + fi * xr)
    hro_ref[...] = hr
    hio_ref[...] = hi
    y_ref[...] = _dot_nt(hr.astype(BF16), ctr_ref[0]) - _dot_nt(hi.astype(BF16), cti_ref[0])


def _s5_decode(z, w_blk, ct_re, ct_im, par, h_re, h_im):
    rows = z.shape[0]
    u_col0 = Z_SU * 512 // S5_UB
    hspec = pl.BlockSpec((rows, S5_MB), lambda n: (0, n))
    mat = pl.BlockSpec((1, S5_UB, S5_MB), lambda n: (n, 0, 0))
    return pl.pallas_call(
        _s5_decode_kernel, grid=(S5_NB,),
        in_specs=[pl.BlockSpec((rows, S5_UB), lambda n: (0, u_col0 + n)),
                  pl.BlockSpec((1, S5_UB, 2 * S5_MB), lambda n: (n, 0, 0)),
                  mat, mat,
                  pl.BlockSpec((1, 8, S5_MB), lambda n: (n, 0, 0)), hspec, hspec],
        out_specs=[pl.BlockSpec((rows, S5_UB), lambda n: (0, n)), hspec, hspec],
        out_shape=[jax.ShapeDtypeStruct((rows, S5_WIDTH), F32),
                   jax.ShapeDtypeStruct((rows, S5_MODES), F32),
                   jax.ShapeDtypeStruct((rows, S5_MODES), F32)],
        compiler_params=_cparams(("arbitrary",)), name="s5_decode",
    )(z, w_blk, ct_re, ct_im, par, h_re, h_im)


def _gla_decode_kernel(qk_ref, v_ref, gr_ref, sm_ref, wa_ref, ba_ref, gn_ref, s_ref,
                       y_ref, so_ref, *, bb):
    eg = jnp.exp(_gla_log_decay(sm_ref[...], wa_ref, ba_ref))
    qk = qk_ref[...]
    q = qk[:, :GLA_QK] * (GLA_DK ** -0.5)
    k = qk[:, GLA_QK:]
    v = v_ref[...]

    def per_row(b):
        row = slice(b, b + 1)
        s_old = s_ref[b].reshape(GLA_QK, GLA_DV)
        v_rows = jnp.concatenate(
            [jnp.broadcast_to(v[row, h * GLA_DV:(h + 1) * GLA_DV], (GLA_DK, GLA_DV)) for h in range(GLA_HEADS)],
            axis=0)
        s_new = _row_to_col(eg[row], GLA_QK) * s_old + _row_to_col(k[row], GLA_QK) * v_rows
        qs = (_row_to_col(q[row], GLA_QK) * s_new).reshape(GLA_HEADS, GLA_DK, GLA_DV)
        o4 = jnp.sum(qs, axis=1)
        o_row = jnp.concatenate([o4[h:h + 1, :] for h in range(GLA_HEADS)], axis=1)
        return s_new.reshape(GLA_HEADS, GLA_DK, GLA_DV), o_row

    res = [per_row(b) for b in range(bb)]
    so_ref[...] = jnp.stack([r[0] for r in res])
    o = jnp.concatenate([r[1] for r in res], axis=0)
    gr = gr_ref[...]
    for h in range(GLA_HEADS):
        vs = slice(h * GLA_DV, (h + 1) * GLA_DV)
        gate = gr[:, vs]
        y_ref[:, vs] = (_head_norm(o[:, vs], gn_ref[:, vs]) * (gate * _sigmoid(gate))).astype(BF16)


def _gla_decode(z, zs, wa, ba, gn, states, *, layer, bb):
    rows = z.shape[0]
    blk = lambda j: pl.BlockSpec((bb, 512), lambda i: (i, j))
    const = lambda shape: pl.BlockSpec(shape, lambda i: (0,) * len(shape))
    sblock = (bb, GLA_HEADS, GLA_DK, GLA_DV)
    return pl.pallas_call(
        functools.partial(_gla_decode_kernel, bb=bb), grid=(rows // bb,),
        in_specs=[blk(Z_QK), blk(Z_GV), blk(Z_GR), pl.BlockSpec((bb, LANES), lambda i: (i, 0)),
                  const((LANES, GLA_QK)), const((1, GLA_QK)), const((1, GLA_V)),
                  pl.BlockSpec((None,) + sblock, lambda i: (layer, i, 0, 0, 0))],
        out_specs=[pl.BlockSpec((bb, GLA_V), lambda i: (i, 0)),
                   pl.BlockSpec(sblock, lambda i: (i, 0, 0, 0))],
        out_shape=[jax.ShapeDtypeStruct((rows, GLA_V), BF16),
                   jax.ShapeDtypeStruct(states.shape[1:], F32)],
        compiler_params=_cparams(("arbitrary",)), name="gla_decode",
    )(z, z, z, zs, wa, ba, gn, states)


def _mlstm_decode_kernel(q_ref, k_ref, v_ref, og_ref, sm_ref, gb_ref, gn_ref, c_ref, n_ref, m_ref,
                         y_ref, co_ref, no_ref, mo_ref, *, bb):
    li_blk = sm_ref[...] + gb_ref[...]
    lf_blk = _log_sigmoid(li_blk)
    m_blk = m_ref[...]
    q_blk = q_ref[...] * (ML_D ** -0.5)
    k_blk = k_ref[...]
    v_blk = v_ref[...]
    lane = lax.broadcasted_iota(jnp.int32, (1, LANES), 1)

    def per_row(b):
        row = slice(b, b + 1)
        li_r, lf_r, m_r = li_blk[row], lf_blk[row], m_blk[row]
        q_all, k_all, v_all = q_blk[row], k_blk[row], v_blk[row]
        n_all = n_ref[b]
        m_new_row = jnp.zeros((1, LANES), F32)
        h_heads, n_heads, c_heads = [], [], []
        for h in range(ML_HEADS):
            sl = slice(h * ML_D, (h + 1) * ML_D)
            li = li_r[:, SM_MI + h:SM_MI + h + 1]
            lf = lf_r[:, SM_MF + h:SM_MF + h + 1]
            m_prev = m_r[:, h:h + 1]
            m_new = jnp.maximum(lf + m_prev, li)
            a = jnp.exp(lf + m_prev - m_new)
            e = jnp.exp(li - m_new)
            k_r = k_all[:, sl]
            q_r = q_all[:, sl]
            c_new = a * c_ref[b, h] + (_row_to_col(k_r, ML_D) * e) * v_all[:, sl]
            n_new = a * n_all[h:h + 1, :] + e * k_r
            c_heads.append(c_new)
            n_heads.append(n_new)
            num = jnp.sum(_row_to_col(q_r, ML_D) * c_new, axis=0, keepdims=True)
            den = jnp.sum(q_r * n_new, axis=1, keepdims=True)
            h_heads.append(num / jnp.maximum(jnp.abs(den), jnp.exp(-m_new)))
            m_new_row = jnp.where(lane == h, m_new, m_new_row)
        return (jnp.stack(c_heads), jnp.concatenate(n_heads, axis=0),
                jnp.concatenate(h_heads, axis=1), m_new_row)

    res = [per_row(b) for b in range(bb)]
    co_ref[...] = jnp.stack([r[0] for r in res])
    no_ref[...] = jnp.stack([r[1] for r in res])
    mo_ref[...] = jnp.concatenate([r[3] for r in res], axis=0)
    hh = jnp.concatenate([r[2] for r in res], axis=0)
    og = og_ref[...]
    for h in range(ML_HEADS):
        sl = slice(h * ML_D, (h + 1) * ML_D)
        y_ref[:, sl] = (_head_norm(hh[:, sl], gn_ref[:, sl]) * _sigmoid(og[:, sl])).astype(BF16)


def _mlstm_decode(z, zs, gbias, gn, c_all, n_all, m_pad, *, layer, bb):
    rows = z.shape[0]
    blk = lambda j: pl.BlockSpec((bb, 512), lambda i: (i, j))
    const = lambda shape: pl.BlockSpec(shape, lambda i: (0,) * len(shape))
    cblock = (bb, ML_HEADS, ML_D, ML_D)
    nblock = (bb, ML_HEADS, ML_D)
    cspec = pl.BlockSpec(cblock, lambda i: (i, 0, 0, 0))
    nspec = pl.BlockSpec(nblock, lambda i: (i, 0, 0))
    mspec = pl.BlockSpec((bb, LANES), lambda i: (i, 0))
    return pl.pallas_call(
        functools.partial(_mlstm_decode_kernel, bb=bb), grid=(rows // bb,),
        in_specs=[blk(Z_MQ), blk(Z_MK), blk(Z_MV), blk(Z_MO), mspec,
                  const((1, LANES)), const((1, ML_W)),
                  pl.BlockSpec((None,) + cblock, lambda i: (layer, i, 0, 0, 0)),
                  pl.BlockSpec((None,) + nblock, lambda i: (layer, i, 0, 0)), mspec],
        out_specs=[pl.BlockSpec((bb, ML_W), lambda i: (i, 0)), cspec, nspec, mspec],
        out_shape=[jax.ShapeDtypeStruct((rows, ML_W), BF16),
                   jax.ShapeDtypeStruct(c_all.shape[1:], F32), jax.ShapeDtypeStruct(n_all.shape[1:], F32),
                   jax.ShapeDtypeStruct((rows, LANES), F32)],
        compiler_params=_cparams(("arbitrary",)), name="mlstm_decode",
    )(z, z, z, z, zs, gbias, gn, c_all, n_all, m_pad)


def _merge_kernel(ya_ref, yb_ref, yp_ref, u_ref, za_ref, zb_ref, zc_ref, x_ref,
                  d_ref, wg_ref, wa_ref, wb_ref, wc_ref, wo_ref, o_ref):
    yc = _gelu_tanh(yp_ref[...] + d_ref[...] * u_ref[...].astype(F32))
    yc = yc * _sigmoid(_dot(yc.astype(BF16), wg_ref[...]))
    m = _sigmoid(za_ref[...].astype(F32)) * _dot(ya_ref[...], wa_ref[...])
    m = m + _sigmoid(zb_ref[...].astype(F32)) * _dot(yb_ref[...], wb_ref[...])
    m = m + _sigmoid(zc_ref[...].astype(F32)) * _dot(yc.astype(BF16), wc_ref[...])
    o_ref[...] = x_ref[...] + _dot(m.astype(BF16), wo_ref[...])


def _merge(ya, yb, ypre, z, x, d, wg, wa, wb, wc, wo, *, bm):
    n = x.shape[0]
    r512 = pl.BlockSpec((bm, 512), lambda i: (i, 0))
    zblk = lambda j: pl.BlockSpec((bm, D_MODEL), lambda i: (i, j))
    const = lambda shape: pl.BlockSpec(shape, lambda i: (0, 0))
    return pl.pallas_call(
        _merge_kernel, grid=(n // bm,),
        in_specs=[r512, r512, r512, pl.BlockSpec((bm, 512), lambda i: (i, Z_SU)),
                  zblk(Z_ZA), zblk(Z_ZB), zblk(Z_ZC), pl.BlockSpec((bm, D_MODEL), lambda i: (i, 0)),
                  const((1, S5_WIDTH)), const((S5_WIDTH, S5_WIDTH)),
                  const((GLA_V, D_MODEL)), const((ML_W, D_MODEL)), const((S5_WIDTH, D_MODEL)),
                  const((D_MODEL, D_MODEL))],
        out_specs=pl.BlockSpec((bm, D_MODEL), lambda i: (i, 0)),
        out_shape=jax.ShapeDtypeStruct((n, D_MODEL), F32),
        compiler_params=_cparams(("arbitrary",)), name="merge",
    )(ya, yb, ypre, z, z, z, z, x, d, wg, wa, wb, wc, wo)


def _cross_prompt_kernel(x_ref, g_ref, wq_ref, mk_ref, mv_ref, wo_ref, o_ref):
    x = x_ref[...]
    q = _dot(_rms_rows(x, g_ref[...]).astype(BF16), wq_ref[...])
    heads = []
    for h in range(X_HEADS):
        sl = slice(h * X_DH, (h + 1) * X_DH)
        s = _dot_nt(q[:, sl].astype(BF16), mk_ref[:, sl].astype(BF16)) * (X_DH ** -0.5)
        s = s - jnp.max(s, axis=-1, keepdims=True)
        p = jnp.exp(s)
        p = p / jnp.sum(p, axis=-1, keepdims=True)
        heads.append(_dot(p.astype(BF16), mv_ref[:, sl].astype(BF16)))
    o = jnp.concatenate(heads, axis=-1)
    o_ref[...] = x + _dot(o.astype(BF16), wo_ref[...])


def _memory_kv_kernel(x_ref, g_ref, wk_ref, wv_ref, k_ref, v_ref, hn_ref):
    @pl.when(pl.program_id(1) == 0)
    def _():
        hn_ref[...] = _rms_rows(x_ref[...], g_ref[...]).astype(BF16)
    k_ref[...] = _dot(hn_ref[...], wk_ref[...])
    v_ref[...] = _dot(hn_ref[...], wv_ref[...])


def _memory_kv(mem, g, wk, wv, *, bn=512):
    rows, d = mem.shape
    depth = wk.shape[0]
    wspec = pl.BlockSpec((None, d, bn), lambda l, j: (l, 0, j))
    ospec = pl.BlockSpec((None, rows, bn), lambda l, j: (l, 0, j))
    out = jax.ShapeDtypeStruct((depth, rows, d), F32)
    return pl.pallas_call(
        _memory_kv_kernel, grid=(depth, d // bn),
        in_specs=[pl.BlockSpec((rows, d), lambda l, j: (0, 0)),
                  pl.BlockSpec((None, 1, d), lambda l, j: (l, 0, 0)), wspec, wspec],
        out_specs=[ospec, ospec], out_shape=[out, out],
        scratch_shapes=[pltpu.VMEM((rows, d), BF16)],
        compiler_params=_cparams(("arbitrary", "arbitrary")), name="memory_kv",
    )(mem, g, wk, wv)


def _cross_prompt(x, g, wq, mem_k, mem_v, wo, *, layer, batch, seq, tq):
    nt = seq // tq
    xspec = pl.BlockSpec((tq, D_MODEL), lambda b, t: (b * nt + t, 0))
    const = lambda shape: pl.BlockSpec(shape, lambda b, t: (0, 0))
    mspec = pl.BlockSpec((None, MEM_LEN, D_MODEL), lambda b, t: (layer, b, 0))
    return pl.pallas_call(
        _cross_prompt_kernel, grid=(batch, nt),
        in_specs=[xspec, const((1, D_MODEL)), const((D_MODEL, D_MODEL)), mspec, mspec,
                  const((D_MODEL, D_MODEL))],
        out_specs=xspec, out_shape=jax.ShapeDtypeStruct(x.shape, F32),
        compiler_params=_cparams(("arbitrary", "arbitrary")), name="cross_prompt",
    )(x, g, wq, mem_k, mem_v, wo)


def _cross_decode_kernel(x_ref, g_ref, wq_ref, mk_ref, mv_ref, wo_ref, o_ref, att_ref, *, bb):
    x = x_ref[...]
    q = _dot(_rms_rows(x, g_ref[...]).astype(BF16), wq_ref[...])
    for b in range(bb):
        q4 = jnp.concatenate([q[b:b + 1, h * X_DH:(h + 1) * X_DH] for h in range(X_HEADS)], axis=0)
        s = jnp.sum(mk_ref[0, b] * q4[None], axis=-1, keepdims=True) * (X_DH ** -0.5)
        p = jnp.exp(s - jnp.max(s, axis=0, keepdims=True))
        p = p / jnp.sum(p, axis=0, keepdims=True)
        o4 = jnp.sum(p * mv_ref[0, b], axis=0)
        att_ref[b:b + 1, :] = jnp.concatenate([o4[h:h + 1, :] for h in range(X_HEADS)], axis=1)
    o_ref[...] = x + _dot(att_ref[...].astype(BF16), wo_ref[...])


def _cross_decode(x, g, wq, cache_k, cache_v, wo, *, layer, bb):
    rows = x.shape[0]
    xspec = pl.BlockSpec((bb, D_MODEL), lambda i: (i, 0))
    const = lambda shape: pl.BlockSpec(shape, lambda i: (0, 0))
    mspec = pl.BlockSpec((1, bb, MEM_LEN, X_HEADS, X_DH), lambda i: (layer, i, 0, 0, 0))
    return pl.pallas_call(
        functools.partial(_cross_decode_kernel, bb=bb), grid=(rows // bb,),
        in_specs=[xspec, const((1, D_MODEL)), const((D_MODEL, D_MODEL)), mspec, mspec,
                  const((D_MODEL, D_MODEL))],
        out_specs=xspec, out_shape=jax.ShapeDtypeStruct(x.shape, F32),
        scratch_shapes=[pltpu.VMEM((bb, D_MODEL), F32)],
        compiler_params=_cparams(("arbitrary",)), name="cross_decode",
    )(x, g, wq, cache_k, cache_v, wo)


def _ffn_prompt_kernel(x_ref, g_ref, wup_ref, cw_ref, cb_ref, wdn_ref, gf_ref,
                       o_ref, st_ref, hn_ref, halo_ref, up_ref, *, tt, final_norm):
    t = pl.program_id(1)

    @pl.when(t == 0)
    def _():
        halo_ref[...] = jnp.zeros_like(halo_ref)

    x = x_ref[...]
    hn_ref[...] = _rms_rows(x, g_ref[...]).astype(BF16)
    def project(c):
        slot = c % 2
        for part, col0 in enumerate((c * FF_CHUNK, D_FF + c * FF_CHUNK)):
            cols = slice(col0, col0 + FF_CHUNK)
            lanes = slice(part * FF_CHUNK, (part + 1) * FF_CHUNK)
            up = _dot(hn_ref[...], wup_ref[:, cols])
            up_ref[slot, 6:8, lanes] = halo_ref[0:2, cols]
            up_ref[slot, 8:8 + tt, lanes] = up
            halo_ref[0:2, cols] = up[tt - 2:tt, :]

    def gated(c):
        slot = c % 2
        a_cols = slice(c * FF_CHUNK, (c + 1) * FF_CHUNK)
        g_cols = slice(D_FF + c * FF_CHUNK, D_FF + (c + 1) * FF_CHUNK)
        cw = jnp.concatenate([cw_ref[:, a_cols], cw_ref[:, g_cols]], axis=1)
        cb = jnp.concatenate([cb_ref[:, a_cols], cb_ref[:, g_cols]], axis=1)
        cv = (up_ref[slot, 6:6 + tt, :] * cw[0:1] + up_ref[slot, 7:7 + tt, :] * cw[1:2]
              + up_ref[slot, 8:8 + tt, :] * cw[2:3] + cb)
        gt = cv[:, FF_CHUNK:]
        return (cv[:, :FF_CHUNK] * (gt * _sigmoid(gt))).astype(BF16)

    acc = x
    project(0)
    for c in range(N_FF_CHUNKS):
        if c + 1 < N_FF_CHUNKS:
            project(c + 1)
        acc = acc + _dot(gated(c), wdn_ref[c * FF_CHUNK:(c + 1) * FF_CHUNK, :])
    if final_norm:
        acc = _rms_rows(acc, gf_ref[...])
    o_ref[...] = acc

    @pl.when(t == pl.num_programs(1) - 1)
    def _():
        st_ref[0] = halo_ref[0:2, :]


def _ffn_prompt(x, g, wup, cw, cb, wdn, gf, *, batch, seq, tt, final_norm):
    nt = seq // tt
    xspec = pl.BlockSpec((tt, D_MODEL), lambda b, t: (b * nt + t, 0))
    const = lambda shape: pl.BlockSpec(shape, lambda b, t: (0, 0), pipeline_mode=pl.Buffered(1))
    return pl.pallas_call(
        functools.partial(_ffn_prompt_kernel, tt=tt, final_norm=final_norm), grid=(batch, nt),
        in_specs=[xspec, const((1, D_MODEL)), const((D_MODEL, 2 * D_FF)), const((CONV_W, 2 * D_FF)),
                  const((1, 2 * D_FF)), const((D_FF, D_MODEL)), const((1, D_MODEL))],
        out_specs=[xspec, pl.BlockSpec((1, CONV_W - 1, 2 * D_FF), lambda b, t: (b, 0, 0))],
        out_shape=[jax.ShapeDtypeStruct(x.shape, F32),
                   jax.ShapeDtypeStruct((batch, CONV_W - 1, 2 * D_FF), F32)],
        scratch_shapes=[pltpu.VMEM((tt, D_MODEL), BF16), pltpu.VMEM((8, 2 * D_FF), F32),
                        pltpu.VMEM((2, tt + 8, 2 * FF_CHUNK), F32)],
        compiler_params=_cparams(("arbitrary", "arbitrary")), name="ffn_prompt",
    )(x, g, wup, cw, cb, wdn, gf)


def _ffn_decode_kernel(x_ref, g_ref, wa_ref, wg_ref, cwa_ref, cwg_ref, cba_ref, cbg_ref,
                       b0a_ref, b0g_ref, b1a_ref, b1g_ref, wdn_ref, gf_ref,
                       o_ref, upa_ref, upg_ref, hn_ref, acc_ref, *, final_norm):
    c = pl.program_id(0)

    @pl.when(c == 0)
    def _():
        x = x_ref[...]
        hn_ref[...] = _rms_rows(x, g_ref[...]).astype(BF16)
        acc_ref[...] = x

    def conv(w_ref, cw_ref, cb_ref, b0_ref, b1_ref, up_ref):
        up = _dot(hn_ref[...], w_ref[...])
        up_ref[...] = up
        return b0_ref[...] * cw_ref[0:1, :] + b1_ref[...] * cw_ref[1:2, :] + up * cw_ref[2:3, :] + cb_ref[...]

    a = conv(wa_ref, cwa_ref, cba_ref, b0a_ref, b1a_ref, upa_ref)
    gt = conv(wg_ref, cwg_ref, cbg_ref, b0g_ref, b1g_ref, upg_ref)
    act = a * (gt * _sigmoid(gt))
    acc_ref[...] += _dot(act.astype(BF16), wdn_ref[...])

    @pl.when(c == pl.num_programs(0) - 1)
    def _():
        acc = acc_ref[...]
        if final_norm:
            acc = _rms_rows(acc, gf_ref[...])
        o_ref[...] = acc


def _ffn_decode(x, g, wup, cw, cb, wdn, gf, buf0, buf1, *, final_norm):
    rows = x.shape[0]
    fc = FF_CHUNK
    full = lambda shape: pl.BlockSpec(shape, lambda c: (0, 0))
    a_col = lambda r: pl.BlockSpec((r, fc), lambda c: (0, c))
    g_col = lambda r: pl.BlockSpec((r, fc), lambda c: (0, N_FF_CHUNKS + c))
    return pl.pallas_call(
        functools.partial(_ffn_decode_kernel, final_norm=final_norm), grid=(N_FF_CHUNKS,),
        in_specs=[full((rows, D_MODEL)), full((1, D_MODEL)),
                  a_col(D_MODEL), g_col(D_MODEL), a_col(CONV_W), g_col(CONV_W), a_col(1), g_col(1),
                  a_col(rows), g_col(rows), a_col(rows), g_col(rows),
                  pl.BlockSpec((fc, D_MODEL), lambda c: (c, 0)), full((1, D_MODEL))],
        out_specs=[full((rows, D_MODEL)), a_col(rows), a_col(rows)],
        out_shape=[jax.ShapeDtypeStruct(x.shape, F32),
                   jax.ShapeDtypeStruct((rows, D_FF), F32),
                   jax.ShapeDtypeStruct((rows, D_FF), F32)],
        scratch_shapes=[pltpu.VMEM((rows, D_MODEL), BF16), pltpu.VMEM((rows, D_MODEL), F32)],
        compiler_params=_cparams(("arbitrary",)), name="ffn_decode",
    )(x, g, wup, wup, cw, cw, cb, cb, buf0, buf0, buf1, buf1, wdn, gf)


def _layer_weights(P, l):
    w_in = P['w_in'][l]
    w_main = jnp.concatenate([w_in[:, 0:1536], w_in[:, 1552:3600], w_in[:, 3608:7192]], axis=1).astype(BF16)
    w_small = jnp.concatenate([w_in[:, 1536:1552], w_in[:, 3600:3608],
                               jnp.zeros((D_MODEL, LANES - GLA_LOWRANK - 2 * ML_HEADS), F32)], axis=1).astype(BF16)
    wa = jnp.zeros((LANES, GLA_QK), F32).at[:GLA_LOWRANK].set(P['w_gla_alpha'][l]).astype(BF16)
    gbias = jnp.zeros((1, LANES), F32)
    gbias = gbias.at[0, SM_MI:SM_MI + ML_HEADS].set(P['b_mlstm_i'][l])
    gbias = gbias.at[0, SM_MF:SM_MF + ML_HEADS].set(P['b_mlstm_f'][l])
    eye = jnp.eye(S5_GB, dtype=F32)
    w_re = jnp.einsum('ngpc,gh->ngchp', P['s5_b_re'][l].reshape(S5_NB, S5_GB, S5_P, S5_GROUP), eye)
    w_im = jnp.einsum('ngpc,gh->ngchp', P['s5_b_im'][l].reshape(S5_NB, S5_GB, S5_P, S5_GROUP), eye)
    w_blk = jnp.concatenate([w_re.reshape(S5_NB, S5_UB, S5_MB), w_im.reshape(S5_NB, S5_UB, S5_MB)],
                            axis=-1).astype(BF16)
    ct_re = jnp.einsum('ngcp,gh->ngchp', P['s5_c_re'][l].reshape(S5_NB, S5_GB, S5_GROUP, S5_P), eye)
    ct_im = jnp.einsum('ngcp,gh->ngchp', P['s5_c_im'][l].reshape(S5_NB, S5_GB, S5_GROUP, S5_P), eye)
    ct_re = ct_re.reshape(S5_NB, S5_UB, S5_MB)
    ct_im = ct_im.reshape(S5_NB, S5_UB, S5_MB)
    lam_re = P['s5_lam_re'][l].reshape(S5_NB, 1, S5_MB)
    lam_im = P['s5_lam_im'][l].reshape(S5_NB, 1, S5_MB)
    log_dt = jnp.broadcast_to(P['s5_log_dt'][l][:, None], (S5_GROUPS, S5_P)).reshape(S5_NB, 1, S5_MB)
    par, cfold = _s5_prep(lam_re, lam_im, log_dt, ct_re, ct_im)
    row = lambda a: a.reshape(1, -1)
    return dict(
        norm_mix=row(P['norm_mix'][l]), w_main=w_main, w_small=w_small, wa=wa,
        ba=row(P['b_gla_alpha'][l]), gla_norm=row(P['gla_head_norm'][l]), gbias=gbias,
        ml_norm=row(P['mlstm_head_norm'][l]), w_blk=w_blk, ct_re=ct_re.astype(BF16),
        ct_im=ct_im.astype(BF16), par=par, cfold=cfold, s5_d=row(P['s5_d'][l]),
        w_glu=P['s5_w_glu'][l].astype(BF16), w_a=P['w_branch_a'][l].astype(BF16),
        w_b=P['w_branch_b'][l].astype(BF16), w_c=P['w_branch_c'][l].astype(BF16),
        w_out=P['w_out'][l].astype(BF16), norm_cross=row(P['norm_cross'][l]),
        w_cq=P['w_cq'][l].astype(BF16), w_co=P['w_co'][l].astype(BF16),
        norm_ffn=row(P['norm_ffn'][l]), w_up=P['w_ffn_up'][l].astype(BF16),
        conv_w=P['ffn_conv_w'][l], conv_b=row(P['ffn_conv_b'][l]),
        w_down=P['w_ffn_down'][l].astype(BF16), norm_final=row(P['norm_final']),
    )


def _prompt_trunk(x_prompt, mem_k, mem_v, W, *, bm_in, tt_mix, gla_group, ml_group, tt_s5, bm, tq, tt_ffn):
    batch, seq, _ = x_prompt.shape
    depth = len(W)
    x = x_prompt.reshape(batch * seq, D_MODEL)
    outs = []
    for l, w in enumerate(W):
        z, zs = _norm_matmul(x, w['norm_mix'], w['w_main'], w['w_small'], bm=bm_in, bn=1024, out_dtype=BF16)
        z3 = z.reshape(batch, seq, Z_MAIN)
        zs3 = zs.reshape(batch, seq, LANES)
        ya, st = _gla_prompt(z3, zs3, w['wa'], w['ba'], w['gla_norm'], tt=tt_mix, group=gla_group)
        yb, c, n, m = _mlstm_prompt(z3, zs3, w['gbias'], w['ml_norm'], tt=tt_mix, group=ml_group)
        ypre, hfin = _s5_prompt(z3, w['w_blk'], w['cfold'], w['par'], batch=batch, seq=seq, tt=tt_s5)
        x = _merge(ya.reshape(batch * seq, GLA_V), yb.reshape(batch * seq, ML_W),
                   ypre.reshape(batch * seq, S5_WIDTH), z, x, w['s5_d'], w['w_glu'],
                   w['w_a'], w['w_b'], w['w_c'], w['w_out'], bm=bm)
        x = _cross_prompt(x, w['norm_cross'], w['w_cq'], mem_k, mem_v, w['w_co'],
                          layer=l, batch=batch, seq=seq, tq=tq)
        x, conv = _ffn_prompt(x, w['norm_ffn'], w['w_up'], w['conv_w'], w['conv_b'], w['w_down'],
                              w['norm_final'], batch=batch, seq=seq, tt=tt_ffn,
                              final_norm=(l == depth - 1))
        st4 = st.reshape(batch, GLA_HEADS, GLA_DV, GLA_HEADS, GLA_DK)
        gla = jnp.stack([st4[:, h, :, h, :] for h in range(GLA_HEADS)], axis=1).transpose(0, 1, 3, 2)
        h4 = hfin.reshape(S5_NB, batch, 2, S5_GB, S5_P)
        s5_re = h4[:, :, 0].transpose(1, 0, 2, 3).reshape(batch, S5_GROUPS, S5_P)
        s5_im = h4[:, :, 1].transpose(1, 0, 2, 3).reshape(batch, S5_GROUPS, S5_P)
        outs.append((gla, c.reshape(batch, ML_HEADS, ML_D, ML_D), n[:, :ML_HEADS, :],
                     m[:, :ML_HEADS, 0], s5_re, s5_im, conv))
    stacked = [jnp.stack([outs[l][i] for l in range(depth)]) for i in range(7)]
    return x.reshape(batch, seq, D_MODEL), stacked


def _sample_trunk(x_sample, cache_k, cache_v, states, W):
    rows = x_sample.shape[0]
    depth = len(W)
    x = x_sample.reshape(rows, D_MODEL)
    s_gla, s_c, s_n, s_m, s_re, s_im, s_conv = states
    outs = []
    for l, w in enumerate(W):
        z, zs = _norm_matmul(x, w['norm_mix'], w['w_main'], w['w_small'], bm=rows, bn=1024)
        ya, gla = _gla_decode(z, zs, w['wa'], w['ba'], w['gla_norm'], s_gla, layer=l, bb=16)
        m_pad = jnp.pad(s_m[l], ((0, 0), (0, LANES - ML_HEADS)))
        yb, c, n, m = _mlstm_decode(z, zs, w['gbias'], w['ml_norm'], s_c, s_n, m_pad, layer=l, bb=16)
        ypre, h_re, h_im = _s5_decode(z, w['w_blk'], w['ct_re'], w['ct_im'], w['par'],
                                      s_re[l].reshape(rows, S5_MODES), s_im[l].reshape(rows, S5_MODES))
        x = _merge(ya, yb, ypre, z, x, w['s5_d'], w['w_glu'], w['w_a'], w['w_b'], w['w_c'], w['w_out'],
                   bm=rows)
        x = _cross_decode(x, w['norm_cross'], w['w_cq'], cache_k, cache_v, w['w_co'], layer=l, bb=8)
        x, up_a, up_g = _ffn_decode(x, w['norm_ffn'], w['w_up'], w['conv_w'], w['conv_b'], w['w_down'],
                                    w['norm_final'], s_conv[l][:, 0, :], s_conv[l][:, 1, :],
                                    final_norm=(l == depth - 1))
        conv = jnp.stack([s_conv[l][:, 1, :], jnp.concatenate([up_a, up_g], axis=1)], axis=1)
        outs.append((gla, c, n, m[:, :ML_HEADS], h_re.reshape(rows, S5_GROUPS, S5_P),
                     h_im.reshape(rows, S5_GROUPS, S5_P), conv))
    stacked = [jnp.stack([outs[l][i] for l in range(depth)]) for i in range(7)]
    return x.reshape(rows, 1, D_MODEL), stacked


def kernel(x_prompt, x_sample, mem_prompt, cache_mem_k, cache_mem_v, state_gla, state_mlstm_c, state_mlstm_n, state_mlstm_m, state_s5_re, state_s5_im, state_ffn_conv, norm_mix, w_in, w_gla_alpha, b_gla_alpha, gla_head_norm, b_mlstm_i, b_mlstm_f, mlstm_head_norm, s5_lam_re, s5_lam_im, s5_log_dt, s5_b_re, s5_b_im, s5_c_re, s5_c_im, s5_d, s5_w_glu, w_branch_a, w_branch_b, w_branch_c, w_out, norm_cross, norm_mem, w_cq, w_ck, w_cv, w_co, norm_ffn, w_ffn_up, ffn_conv_w, ffn_conv_b, w_ffn_down, norm_final):
    P = dict(norm_mix=norm_mix, w_in=w_in, w_gla_alpha=w_gla_alpha, b_gla_alpha=b_gla_alpha,
             gla_head_norm=gla_head_norm, b_mlstm_i=b_mlstm_i, b_mlstm_f=b_mlstm_f,
             mlstm_head_norm=mlstm_head_norm, s5_lam_re=s5_lam_re, s5_lam_im=s5_lam_im,
             s5_log_dt=s5_log_dt, s5_b_re=s5_b_re, s5_b_im=s5_b_im, s5_c_re=s5_c_re,
             s5_c_im=s5_c_im, s5_d=s5_d, s5_w_glu=s5_w_glu, w_branch_a=w_branch_a,
             w_branch_b=w_branch_b, w_branch_c=w_branch_c, w_out=w_out, norm_cross=norm_cross,
             w_cq=w_cq, w_co=w_co, norm_ffn=norm_ffn, w_ffn_up=w_ffn_up, ffn_conv_w=ffn_conv_w,
             ffn_conv_b=ffn_conv_b, w_ffn_down=w_ffn_down, norm_final=norm_final)
    depth = w_in.shape[0]
    W = [_layer_weights(P, l) for l in range(depth)]
    batch, mem_len, _ = mem_prompt.shape
    mem2 = mem_prompt.reshape(batch * mem_len, D_MODEL)
    mem_k, mem_v = _memory_kv(mem2, norm_mem.reshape(depth, 1, D_MODEL), w_ck.astype(BF16), w_cv.astype(BF16))
    y_prompt, p_states = _prompt_trunk(x_prompt, mem_k, mem_v, W, bm_in=2048, tt_mix=256, gla_group=2, ml_group=2, tt_s5=256, bm=512, tq=512,
                                       tt_ffn=512)
    p_mem_k = mem_k.reshape(depth, batch, mem_len, X_HEADS, X_DH)
    p_mem_v = mem_v.reshape(depth, batch, mem_len, X_HEADS, X_DH)
    y_sample, s_states = _sample_trunk(
        x_sample, cache_mem_k, cache_mem_v,
        (state_gla, state_mlstm_c, state_mlstm_n, state_mlstm_m, state_s5_re, state_s5_im, state_ffn_conv), W)
    return (y_prompt, y_sample, *p_states, p_mem_k, p_mem_v, *s_states)
```

```python
import functools

import jax
import jax.numpy as jnp
from jax import lax
from jax.experimental import pallas as pl
from jax.experimental.pallas import tpu as pltpu

F32 = jnp.float32
BF16 = jnp.bfloat16

D_MODEL = 1024
GLA_HEADS, GLA_DK, GLA_DV = 4, 64, 128
GLA_QK = GLA_HEADS * GLA_DK
GLA_V = GLA_HEADS * GLA_DV
GLA_LOWRANK = 16
GLA_TAU = 16.0
ML_HEADS, ML_D = 4, 128
ML_W = ML_HEADS * ML_D
S5_GROUP, S5_GROUPS, S5_P = 16, 32, 64
S5_WIDTH = S5_GROUP * S5_GROUPS
S5_MODES = S5_GROUPS * S5_P
S5_NB = 4
S5_GB = S5_GROUPS // S5_NB
S5_MB = S5_MODES // S5_NB
S5_UB = S5_WIDTH // S5_NB
MEM_LEN = 256
X_HEADS = 4
X_DH = D_MODEL // X_HEADS
D_FF = 2816
CONV_W = 3
EPS = 1e-6

CHUNK = 256
LANES = 128
FF_CHUNK = 256
N_FF_CHUNKS = D_FF // FF_CHUNK

Z_MAIN = 7168
Z_QK, Z_GV, Z_GR, Z_MQ, Z_MK, Z_MV, Z_MO, Z_SU = range(8)
Z_ZA, Z_ZB, Z_ZC = 4, 5, 6
SM_MI = GLA_LOWRANK
SM_MF = GLA_LOWRANK + ML_HEADS

GLA_SAFE_DECAY = 80.0

VMEM_LIMIT = 56 * 1024 * 1024


def _cparams(sem):
    return pltpu.CompilerParams(dimension_semantics=sem, vmem_limit_bytes=VMEM_LIMIT)


def _dot(a, b):
    return jnp.dot(a, b, preferred_element_type=F32)


def _dot_nt(a, b):
    return lax.dot_general(a, b, (((1,), (1,)), ((), ())), preferred_element_type=F32)


def _dot_tn(a, b):
    return lax.dot_general(a, b, (((0,), (0,)), ((), ())), preferred_element_type=F32)


def _sigmoid(x):
    return 1.0 / (1.0 + jnp.exp(-x))


def _log_sigmoid(x):
    return jnp.minimum(x, 0.0) - jnp.log(1.0 + jnp.exp(-jnp.abs(x)))


def _gelu_tanh(x):
    return 0.5 * x * (1.0 + jnp.tanh(0.7978845608028654 * (x + 0.044715 * x * x * x)))


def _rms_rows(x, g):
    r = lax.rsqrt(jnp.mean(x * x, axis=-1, keepdims=True) + EPS)
    return (x * r) * g


def _dot_exact01(a01, x):
    hi = x.astype(BF16)
    r1 = x - hi.astype(F32)
    mid = r1.astype(BF16)
    lo = (r1 - mid.astype(F32)).astype(BF16)
    return _dot(a01, hi) + _dot(a01, mid) + _dot(a01, lo)


def _dot_exact01_rhs(x, b01):
    hi = x.astype(BF16)
    r1 = x - hi.astype(F32)
    mid = r1.astype(BF16)
    lo = (r1 - mid.astype(F32)).astype(BF16)
    return _dot(hi, b01) + _dot(mid, b01) + _dot(lo, b01)


def _row_to_col(row, n):
    eye = (lax.broadcasted_iota(jnp.int32, (n, n), 0) == lax.broadcasted_iota(jnp.int32, (n, n), 1))
    return jnp.sum(jnp.where(eye, jnp.broadcast_to(row, (n, n)), 0.0), axis=1, keepdims=True)


def _norm_mm_small_kernel(x_ref, g_ref, w_ref, ws_ref, o_ref, os_ref, hn_ref):
    @pl.when(pl.program_id(1) == 0)
    def _():
        hn = _rms_rows(x_ref[...], g_ref[...]).astype(BF16)
        hn_ref[...] = hn
        os_ref[...] = _dot(hn, ws_ref[...])
    o_ref[...] = _dot(hn_ref[...], w_ref[...]).astype(o_ref.dtype)


def _norm_matmul(x, g, w, w_small, *, bm, bn, out_dtype=F32):
    n, d = x.shape
    c = w.shape[1]
    grid = (n // bm, c // bn)
    x_spec = pl.BlockSpec((bm, d), lambda i, j: (i, 0))
    g_spec = pl.BlockSpec((1, d), lambda i, j: (0, 0))
    w_spec = pl.BlockSpec((d, bn), lambda i, j: (0, j))
    o_spec = pl.BlockSpec((bm, bn), lambda i, j: (i, j))
    scratch = [pltpu.VMEM((bm, d), BF16)]
    cs = w_small.shape[1]
    return pl.pallas_call(
        _norm_mm_small_kernel, grid=grid,
        in_specs=[x_spec, g_spec, w_spec, pl.BlockSpec((d, cs), lambda i, j: (0, 0))],
        out_specs=[o_spec, pl.BlockSpec((bm, cs), lambda i, j: (i, 0))],
        out_shape=[jax.ShapeDtypeStruct((n, c), out_dtype), jax.ShapeDtypeStruct((n, cs), F32)],
        scratch_shapes=scratch,
        compiler_params=_cparams(("arbitrary", "arbitrary")), name="norm_matmul_in",
    )(x, g, w, w_small)


def _head_norm(oh, gn_h):
    r = lax.rsqrt(jnp.mean(oh * oh, axis=-1, keepdims=True) + EPS)
    return oh * r * gn_h


def _head_norm_mxu(oh, gn_h, ones_blk):
    sq = oh * oh
    hi = sq.astype(BF16)
    lo = (sq - hi.astype(F32)).astype(BF16)
    ms = (_dot(hi, ones_blk) + _dot(lo, ones_blk)) * (1.0 / GLA_DV)
    return oh * lax.rsqrt(ms + EPS) * gn_h


def _gla_log_decay(small, wa_ref, ba_ref):
    a_pre = _dot(small.astype(BF16), wa_ref[...]) + ba_ref[...]
    return _log_sigmoid(a_pre) * (1.0 / GLA_TAU)


def _gla_prompt_kernel(qk_ref, v_ref, gr_ref, sm_ref, wa_ref, ba_ref, gn_ref,
                       y_ref, st_ref, state_ref, b_ref, k_ref, *, n_chunks, group):
    t = pl.program_id(1)
    L = CHUNK

    @pl.when(t == 0)
    def _():
        state_ref[...] = jnp.zeros_like(state_ref)

    ri = lax.broadcasted_iota(jnp.int32, (L, L), 0)
    ci = lax.broadcasted_iota(jnp.int32, (L, L), 1)
    tril = jnp.where(ri >= ci, 1.0, 0.0).astype(BF16)
    rw = lax.broadcasted_iota(jnp.int32, (L, GLA_HEADS * L), 0)
    cw = lax.broadcasted_iota(jnp.int32, (L, GLA_HEADS * L), 1)
    causal_wide = (cw % L) <= rw
    kr = lax.broadcasted_iota(jnp.int32, (GLA_HEADS * L, GLA_QK), 0)
    kc = lax.broadcasted_iota(jnp.int32, (GLA_HEADS * L, GLA_QK), 1)
    kk_mask = (kr // L) == (kc // GLA_DK)
    vr = lax.broadcasted_iota(jnp.int32, (GLA_HEADS * L, GLA_V), 0)
    vc = lax.broadcasted_iota(jnp.int32, (GLA_HEADS * L, GLA_V), 1)
    vv_mask = (vr // L) == (vc // GLA_DV)
    sr = lax.broadcasted_iota(jnp.int32, (GLA_V, GLA_QK), 0)
    sc = lax.broadcasted_iota(jnp.int32, (GLA_V, GLA_QK), 1)
    st_mask = (sr // GLA_DV) == (sc // GLA_DK)
    hr = lax.broadcasted_iota(jnp.int32, (GLA_QK, LANES), 0)
    hc = lax.broadcasted_iota(jnp.int32, (GLA_QK, LANES), 1)
    head_ones = jnp.where((hr // GLA_DK) == hc, 1.0, 0.0).astype(BF16)
    pr = lax.broadcasted_iota(jnp.int32, (LANES, GLA_HEADS * L), 0)
    pc = lax.broadcasted_iota(jnp.int32, (LANES, GLA_HEADS * L), 1)

    def one(e, c):
        r0 = pl.multiple_of(c * L, L)
        rows = pl.ds(r0, L)
        qk = qk_ref[e, rows, :].astype(F32)
        q = qk[:, :GLA_QK] * (GLA_DK ** -0.5)
        k = qk[:, GLA_QK:]
        v16 = v_ref[e, rows, :].astype(BF16)
        g = _gla_log_decay(sm_ref[e, rows, :], wa_ref, ba_ref)
        b = _dot_exact01(tril, g)
        b_end = b[L - 1:L, :]
        qe = q * jnp.exp(b)
        k_dec = k * jnp.exp(b_end - b)
        qe16 = qe.astype(BF16)

        def fast_att(_):
            k_til = k * jnp.exp(-b)
            kk = jnp.where(kk_mask, jnp.concatenate([k_til] * GLA_HEADS, axis=0), 0.0)
            return _dot_nt(qe16, kk.astype(BF16))

        def direct_att(_):
            b_ref[...] = b
            k_ref[...] = k

            def col(j, acc):
                kj = k_ref[pl.ds(j, 1), :]
                bj = b_ref[pl.ds(j, 1), :]
                prod = q * kj * jnp.exp(jnp.minimum(b - bj, 0.0))
                red = _dot(prod.astype(BF16), head_ones)
                place = jnp.where((pc == pr * L + j) & (pr < GLA_HEADS), 1.0, 0.0).astype(BF16)
                return acc + _dot(red.astype(BF16), place)

            return lax.fori_loop(0, L, col, jnp.zeros((L, GLA_HEADS * L), F32))

        safe = jnp.max(-b_end) <= GLA_SAFE_DECAY
        att = lax.cond(safe, fast_att, direct_att, 0)
        att = jnp.where(causal_wide, att, 0.0)

        vv = jnp.where(vv_mask, jnp.concatenate([v16] * GLA_HEADS, axis=0), jnp.zeros((), BF16))
        st = state_ref[e]
        o = _dot(att.astype(BF16), vv) + _dot_nt(qe16, st.astype(BF16))
        upd = _dot_tn(v16, k_dec.astype(BF16))
        st_new = jnp.where(st_mask, st * jnp.exp(b_end) + upd, 0.0)

        gr = gr_ref[e, rows, :].astype(F32)
        ys = []
        for h in range(GLA_HEADS):
            sl = slice(h * GLA_DV, (h + 1) * GLA_DV)
            gate = gr[:, sl]
            ys.append((_head_norm(o[:, sl], gn_ref[:, sl]) * (gate * _sigmoid(gate))).astype(BF16))
        return jnp.concatenate(ys, axis=1), st_new

    def chunk(c, carry):
        res = [one(e, c) for e in range(group)]
        rows = pl.ds(pl.multiple_of(c * L, L), L)
        y_ref[:, rows, :] = jnp.stack([r[0] for r in res])
        state_ref[...] = jnp.stack([r[1] for r in res])
        return carry

    lax.fori_loop(0, n_chunks, chunk, 0)

    @pl.when(t == pl.num_programs(1) - 1)
    def _():
        st_ref[...] = state_ref[...]


def _gla_prompt(z3, zs3, wa, ba, gn, *, tt, group):
    batch, seq, _ = z3.shape
    blk = lambda j: pl.BlockSpec((group, tt, 512), lambda b, t: (b, t, j))
    const = lambda shape: pl.BlockSpec(shape, lambda b, t: (0,) * len(shape))
    return pl.pallas_call(
        functools.partial(_gla_prompt_kernel, n_chunks=tt // CHUNK, group=group),
        grid=(batch // group, seq // tt),
        in_specs=[blk(Z_QK), blk(Z_GV), blk(Z_GR),
                  pl.BlockSpec((group, tt, LANES), lambda b, t: (b, t, 0)),
                  const((LANES, GLA_QK)), const((1, GLA_QK)), const((1, GLA_V))],
        out_specs=[pl.BlockSpec((group, tt, GLA_V), lambda b, t: (b, t, 0)),
                   pl.BlockSpec((group, GLA_V, GLA_QK), lambda b, t: (b, 0, 0))],
        out_shape=[jax.ShapeDtypeStruct((batch, seq, GLA_V), BF16),
                   jax.ShapeDtypeStruct((batch, GLA_V, GLA_QK), F32)],
        scratch_shapes=[pltpu.VMEM((group, GLA_V, GLA_QK), F32), pltpu.VMEM((CHUNK, GLA_QK), F32),
                        pltpu.VMEM((CHUNK, GLA_QK), F32)],
        compiler_params=_cparams(("arbitrary", "arbitrary")), name="gla_prompt",
    )(z3, z3, z3, zs3, wa, ba, gn)


def _mlstm_prompt_kernel(q_ref, k_ref, v_ref, og_ref, sm_ref, gb_ref, gn_ref,
                         y_ref, c_out, n_out, m_out, c_ref, n_ref, m_ref, *, n_chunks, group):
    t = pl.program_id(1)
    L = CHUNK

    @pl.when(t == 0)
    def _():
        c_ref[...] = jnp.zeros_like(c_ref)
        n_ref[...] = jnp.zeros_like(n_ref)
        m_ref[...] = jnp.zeros_like(m_ref)

    ri = lax.broadcasted_iota(jnp.int32, (L, L), 0)
    ci = lax.broadcasted_iota(jnp.int32, (L, L), 1)
    lower = ri >= ci
    tril = jnp.where(lower, 1.0, 0.0).astype(BF16)
    triu = jnp.where(ri <= ci, 1.0, 0.0).astype(BF16)
    sr = lax.broadcasted_iota(jnp.int32, (LANES, ML_W), 0)
    sh = lax.broadcasted_iota(jnp.int32, (LANES, ML_W), 1) // ML_D
    sel_f = jnp.where(sr == SM_MF + sh, 1.0, 0.0).astype(BF16)
    sel_i = jnp.where(sr == SM_MI + sh, 1.0, 0.0).astype(BF16)
    ones_l = jnp.ones((L, LANES), BF16)
    ones_d = jnp.ones((ML_D, ML_D), BF16)

    def one(e, c):
        r0 = pl.multiple_of(c * L, L)
        rows = pl.ds(r0, L)
        sm = sm_ref[e, rows, :] + gb_ref[...]
        lf_all = _log_sigmoid(sm)
        sm_t = sm.T
        b_cols = _dot_exact01(tril, lf_all)
        b_rows = _dot_exact01_rhs(_log_sigmoid(sm_t), triu)
        b_wide = _dot_exact01_rhs(b_cols, sel_f)
        li_wide = _dot_exact01_rhs(sm, sel_i)
        ys, cs, ns, ms = [], [], [], []
        for h in range(ML_HEADS):
            sl = slice(h * ML_D, (h + 1) * ML_D)
            b_tok = b_wide[:, sl]
            li_tok = li_wide[:, sl]
            li_row = sm_t[SM_MI + h:SM_MI + h + 1, :]
            b_row = b_rows[SM_MF + h:SM_MF + h + 1, :]
            m_prev = m_ref[e, h:h + 1, :]
            w = jnp.where(lower, jnp.concatenate([b_tok] * (L // LANES), axis=1) - b_row + li_row, -jnp.inf)
            m_tok = jnp.maximum(b_tok + m_prev, jnp.max(w, axis=1, keepdims=True))
            a_inter = jnp.exp(b_tok + m_prev - m_tok)
            qh = q_ref[e, rows, sl].astype(F32) * (ML_D ** -0.5)
            kh16 = k_ref[e, rows, sl].astype(BF16)
            kh = kh16.astype(F32)
            qh16, vh16 = qh.astype(BF16), v_ref[e, rows, sl].astype(BF16)
            s = _dot_nt(qh16, kh16) * jnp.exp(w - jnp.concatenate([m_tok] * (L // LANES), axis=1))
            c_h = c_ref[e, sl, :]
            n_h = n_ref[e, h:h + 1, :]
            s16 = s.astype(BF16)
            s_lo = (s - s16.astype(F32)).astype(BF16)
            row_sum = _dot(s16, ones_l) + _dot(s_lo, ones_l)
            q_n = _dot_nt(qh16, jnp.broadcast_to(n_h, (LANES, ML_D)).astype(BF16))
            num = _dot(s16, vh16) + a_inter * _dot(qh16, c_h.astype(BF16))
            den = row_sum + a_inter * q_n
            hh = num / jnp.maximum(jnp.abs(den), jnp.exp(-m_tok))
            og = og_ref[e, rows, sl].astype(F32)
            ys.append((_head_norm_mxu(hh, gn_ref[:, sl], ones_d) * _sigmoid(og)).astype(BF16))
            b_end = b_tok[L - 1:L, :]
            w_end = b_end - b_tok + li_tok
            m_new = jnp.maximum(b_end + m_prev, jnp.max(w_end, axis=0, keepdims=True))
            e_inter = jnp.exp(b_end + m_prev - m_new)
            kd = kh * jnp.exp(w_end - m_new)
            cs.append(e_inter * c_h + _dot_tn(kd.astype(BF16), vh16))
            ns.append(e_inter * n_h + jnp.sum(kd, axis=0, keepdims=True))
            ms.append(m_new)
        pad = [jnp.zeros((8 - ML_HEADS, LANES), F32)]
        return (jnp.concatenate(ys, axis=1), jnp.concatenate(cs, axis=0),
                jnp.concatenate(ns + pad, axis=0), jnp.concatenate(ms + pad, axis=0))

    def chunk(c, carry):
        res = [one(e, c) for e in range(group)]
        rows = pl.ds(pl.multiple_of(c * L, L), L)
        y_ref[:, rows, :] = jnp.stack([r[0] for r in res])
        c_ref[...] = jnp.stack([r[1] for r in res])
        n_ref[...] = jnp.stack([r[2] for r in res])
        m_ref[...] = jnp.stack([r[3] for r in res])
        return carry

    lax.fori_loop(0, n_chunks, chunk, 0)

    @pl.when(t == pl.num_programs(1) - 1)
    def _():
        c_out[...] = c_ref[...]
        n_out[...] = n_ref[...]
        m_out[...] = m_ref[...]


def _mlstm_prompt(z3, zs3, gbias, gn, *, tt, group):
    batch, seq, _ = z3.shape
    blk = lambda j: pl.BlockSpec((group, tt, 512), lambda b, t: (b, t, j))
    const = lambda shape: pl.BlockSpec(shape, lambda b, t: (0,) * len(shape))
    state = lambda shape: pl.BlockSpec((group,) + shape, lambda b, t: (b, 0, 0))
    return pl.pallas_call(
        functools.partial(_mlstm_prompt_kernel, n_chunks=tt // CHUNK, group=group),
        grid=(batch // group, seq // tt),
        in_specs=[blk(Z_MQ), blk(Z_MK), blk(Z_MV), blk(Z_MO),
                  pl.BlockSpec((group, tt, LANES), lambda b, t: (b, t, 0)),
                  const((1, LANES)), const((1, ML_W))],
        out_specs=[pl.BlockSpec((group, tt, ML_W), lambda b, t: (b, t, 0)),
                   state((ML_W, ML_D)), state((8, LANES)), state((8, LANES))],
        out_shape=[jax.ShapeDtypeStruct((batch, seq, ML_W), BF16),
                   jax.ShapeDtypeStruct((batch, ML_W, ML_D), F32),
                   jax.ShapeDtypeStruct((batch, 8, LANES), F32),
                   jax.ShapeDtypeStruct((batch, 8, LANES), F32)],
        scratch_shapes=[pltpu.VMEM((group, ML_W, ML_D), F32), pltpu.VMEM((group, 8, LANES), F32),
                        pltpu.VMEM((group, 8, LANES), F32)],
        compiler_params=_cparams(("arbitrary", "arbitrary")), name="mlstm_prompt",
    )(z3, z3, z3, z3, zs3, gbias, gn)


def _s5_prep_kernel(lre_ref, lim_ref, ldt_ref, ctr_ref, cti_ref, par_ref, cf_ref):
    lam_re = lre_ref[0]
    lam_im = lim_ref[0]
    dt = jnp.exp(ldt_ref[0])
    mag = jnp.exp(lam_re * dt)
    lb_re = mag * jnp.cos(lam_im * dt)
    lb_im = mag * jnp.sin(lam_im * dt)
    nr = lb_re - 1.0
    den = lam_re * lam_re + lam_im * lam_im
    f_re = (nr * lam_re + lb_im * lam_im) / den
    f_im = (lb_im * lam_re - nr * lam_im) / den
    par_ref[0] = jnp.concatenate([lb_re, lb_im, f_re, f_im, jnp.zeros((4, S5_MB), F32)], axis=0)
    ct_re = ctr_ref[0]
    ct_im = cti_ref[0]
    cf_ref[0, :, :S5_MB] = (ct_re * f_re - ct_im * f_im).astype(BF16)
    cf_ref[0, :, S5_MB:] = (-(ct_re * f_im + ct_im * f_re)).astype(BF16)


def _s5_prep(lam_re, lam_im, log_dt, ct_re, ct_im):
    vec = pl.BlockSpec((1, 1, S5_MB), lambda n: (n, 0, 0))
    mat = pl.BlockSpec((1, S5_UB, S5_MB), lambda n: (n, 0, 0))
    return pl.pallas_call(
        _s5_prep_kernel, grid=(S5_NB,),
        in_specs=[vec, vec, vec, mat, mat],
        out_specs=[pl.BlockSpec((1, 8, S5_MB), lambda n: (n, 0, 0)),
                   pl.BlockSpec((1, S5_UB, 2 * S5_MB), lambda n: (n, 0, 0))],
        out_shape=[jax.ShapeDtypeStruct((S5_NB, 8, S5_MB), F32),
                   jax.ShapeDtypeStruct((S5_NB, S5_UB, 2 * S5_MB), BF16)],
        compiler_params=_cparams(("arbitrary",)), name="s5_prep",
    )(lam_re, lam_im, log_dt, ct_re, ct_im)


def _s5_prompt_kernel(u_ref, w_ref, cf_ref, par_ref, y_ref, h_out, x_ref, h_ref, ub_ref, ut_ref, *, batch, tt):
    t_blk = pl.program_id(1)

    @pl.when(t_blk == 0)
    def _():
        h_ref[...] = jnp.zeros_like(h_ref)

    nlb = S5_MB // LANES
    ub_ref[...] = u_ref[...].astype(F32).reshape(batch * tt, S5_UB)

    def interleave(t, carry):
        ut_ref[pl.ds(pl.multiple_of(t * batch, batch), batch), :] = ub_ref[pl.ds(t, batch, stride=tt), :]
        return carry

    lax.fori_loop(0, tt, interleave, 0, unroll=8)
    x = _dot(ut_ref[...].astype(BF16), w_ref[0])
    for j in range(2 * nlb):
        x_ref[j] = x[:, j * LANES:(j + 1) * LANES]

    par = par_ref[0]
    lr = [jnp.broadcast_to(par[0:1, j * LANES:(j + 1) * LANES], (batch, LANES)) for j in range(nlb)]
    li = [jnp.broadcast_to(par[1:2, j * LANES:(j + 1) * LANES], (batch, LANES)) for j in range(nlb)]

    def step(t, carry):
        rows = pl.ds(pl.multiple_of(t * batch, batch), batch)
        new = []
        for j in range(nlb):
            hr, hi = carry[j], carry[nlb + j]
            nr = lr[j] * hr - li[j] * hi + x_ref[j, rows, :]
            ni = lr[j] * hi + li[j] * hr + x_ref[nlb + j, rows, :]
            x_ref[j, rows, :] = nr
            x_ref[nlb + j, rows, :] = ni
            new.append((nr, ni))
        return tuple(n[0] for n in new) + tuple(n[1] for n in new)

    h0 = tuple(h_ref[:, j * LANES:(j + 1) * LANES] for j in range(2 * nlb))
    hfin = lax.fori_loop(0, tt, step, h0, unroll=8)
    for j in range(2 * nlb):
        h_ref[:, j * LANES:(j + 1) * LANES] = hfin[j]
    hr = jnp.concatenate(hfin[:nlb], axis=1)
    hi = jnp.concatenate(hfin[nlb:], axis=1)

    hall = jnp.concatenate([x_ref[j].astype(BF16) for j in range(2 * nlb)], axis=1)
    ut_ref[...] = _dot_nt(hall, cf_ref[0])
    for b in range(batch):
        y_ref[b] = ut_ref[pl.ds(b, tt, stride=batch), :]

    @pl.when(t_blk == pl.num_programs(1) - 1)
    def _():
        fr = par[2:3, :]
        fi = par[3:4, :]
        h_out[0, :, :S5_MB] = fr * hr - fi * hi
        h_out[0, :, S5_MB:] = fr * hi + fi * hr


def _s5_prompt(z3, w_blk, cfold, par, *, batch, seq, tt):
    nt = seq // tt
    u_col0 = Z_SU * 512 // S5_UB
    return pl.pallas_call(
        functools.partial(_s5_prompt_kernel, batch=batch, tt=tt),
        grid=(S5_NB, nt),
        in_specs=[pl.BlockSpec((batch, tt, S5_UB), lambda n, t: (0, t, u_col0 + n)),
                  pl.BlockSpec((1, S5_UB, 2 * S5_MB), lambda n, t: (n, 0, 0)),
                  pl.BlockSpec((1, S5_UB, 2 * S5_MB), lambda n, t: (n, 0, 0)),
                  pl.BlockSpec((1, 8, S5_MB), lambda n, t: (n, 0, 0))],
        out_specs=[pl.BlockSpec((batch, tt, S5_UB), lambda n, t: (0, t, n)),
                   pl.BlockSpec((1, batch, 2 * S5_MB), lambda n, t: (n, 0, 0))],
        out_shape=[jax.ShapeDtypeStruct((batch, seq, S5_WIDTH), F32),
                   jax.ShapeDtypeStruct((S5_NB, batch, 2 * S5_MB), F32)],
        scratch_shapes=[pltpu.VMEM((2 * S5_MB // LANES, batch * tt, LANES), F32),
                        pltpu.VMEM((batch, 2 * S5_MB), F32),
                        pltpu.VMEM((batch * tt, S5_UB), F32), pltpu.VMEM((batch * tt, S5_UB), F32)],
        compiler_params=_cparams(("arbitrary", "arbitrary")), name="s5_prompt",
    )(z3, w_blk, cfold, par)


def _s5_decode_kernel(u_ref, w_ref, ctr_ref, cti_ref, par_ref, hr_ref, hi_ref, y_ref, hro_ref, hio_ref):
    x = _dot(u_ref[...].astype(BF16), w_ref[0])
    xr, xi = x[:, :S5_MB], x[:, S5_MB:]
    par = par_ref[0]
    lr, li, fr, fi = par[0:1, :], par[1:2, :], par[2:3, :], par[3:4, :]
    h0r, h0i = hr_ref[...], hi_ref[...]
    hr = lr * h0r - li * h0i + (fr * xr - fi * xi)
    hi = lr * h0i + li * h0r + (fr * xi + fi * xr)
    hro_ref[...] = hr
    hio_ref[...] = hi
    y_ref[...] = _dot_nt(hr.astype(BF16), ctr_ref[0]) - _dot_nt(hi.astype(BF16), cti_ref[0])


def _s5_decode(z, w_blk, ct_re, ct_im, par, h_re, h_im):
    rows = z.shape[0]
    u_col0 = Z_SU * 512 // S5_UB
    hspec = pl.BlockSpec((rows, S5_MB), lambda n: (0, n))
    mat = pl.BlockSpec((1, S5_UB, S5_MB), lambda n: (n, 0, 0))
    return pl.pallas_call(
        _s5_decode_kernel, grid=(S5_NB,),
        in_specs=[pl.BlockSpec((rows, S5_UB), lambda n: (0, u_col0 + n)),
                  pl.BlockSpec((1, S5_UB, 2 * S5_MB), lambda n: (n, 0, 0)),
                  mat, mat,
                  pl.BlockSpec((1, 8, S5_MB), lambda n: (n, 0, 0)), hspec, hspec],
        out_specs=[pl.BlockSpec((rows, S5_UB), lambda n: (0, n)), hspec, hspec],
        out_shape=[jax.ShapeDtypeStruct((rows, S5_WIDTH), F32),
                   jax.ShapeDtypeStruct((rows, S5_MODES), F32),
                   jax.ShapeDtypeStruct((rows, S5_MODES), F32)],
        compiler_params=_cparams(("arbitrary",)), name="s5_decode",
    )(z, w_blk, ct_re, ct_im, par, h_re, h_im)


def _gla_decode_kernel(qk_ref, v_ref, gr_ref, sm_ref, wa_ref, ba_ref, gn_ref, s_ref,
                       y_ref, so_ref, *, bb):
    eg = jnp.exp(_gla_log_decay(sm_ref[...], wa_ref, ba_ref))
    qk = qk_ref[...]
    q = qk[:, :GLA_QK] * (GLA_DK ** -0.5)
    k = qk[:, GLA_QK:]
    v = v_ref[...]

    def per_row(b):
        row = slice(b, b + 1)
        s_old = s_ref[b].reshape(GLA_QK, GLA_DV)
        v_rows = jnp.concatenate(
            [jnp.broadcast_to(v[row, h * GLA_DV:(h + 1) * GLA_DV], (GLA_DK, GLA_DV)) for h in range(GLA_HEADS)],
            axis=0)
        s_new = _row_to_col(eg[row], GLA_QK) * s_old + _row_to_col(k[row], GLA_QK) * v_rows
        qs = (_row_to_col(q[row], GLA_QK) * s_new).reshape(GLA_HEADS, GLA_DK, GLA_DV)
        o4 = jnp.sum(qs, axis=1)
        o_row = jnp.concatenate([o4[h:h + 1, :] for h in range(GLA_HEADS)], axis=1)
        return s_new.reshape(GLA_HEADS, GLA_DK, GLA_DV), o_row

    res = [per_row(b) for b in range(bb)]
    so_ref[...] = jnp.stack([r[0] for r in res])
    o = jnp.concatenate([r[1] for r in res], axis=0)
    gr = gr_ref[...]
    for h in range(GLA_HEADS):
        vs = slice(h * GLA_DV, (h + 1) * GLA_DV)
        gate = gr[:, vs]
        y_ref[:, vs] = (_head_norm(o[:, vs], gn_ref[:, vs]) * (gate * _sigmoid(gate))).astype(BF16)


def _skip_ref(kernel, pos):
    def wrapped(*refs):
        return kernel(*refs[:pos], *refs[pos + 1:])
    return wrapped


def _layer_out(kernel, in_specs, operands, out_specs, out_shapes, stacked_idx, prev, name, grid):
    aliases = {}
    if prev is not None:
        in_specs = in_specs + [pl.BlockSpec(memory_space=pl.ANY)]
        operands = operands + (prev,)
        aliases = {len(operands) - 1: stacked_idx}
        kernel = _skip_ref(kernel, len(operands) - 1)
    return pl.pallas_call(
        kernel, grid=grid, in_specs=in_specs, out_specs=out_specs, out_shape=out_shapes,
        input_output_aliases=aliases, compiler_params=_cparams(("arbitrary",)), name=name,
    )(*operands)


def _gla_decode(z, zs, wa, ba, gn, states, prev, *, layer, bb):
    rows = z.shape[0]
    blk = lambda j: pl.BlockSpec((bb, 512), lambda i: (i, j))
    const = lambda shape: pl.BlockSpec(shape, lambda i: (0,) * len(shape))
    sspec = pl.BlockSpec((None, bb, GLA_HEADS, GLA_DK, GLA_DV), lambda i: (layer, i, 0, 0, 0))
    return _layer_out(
        functools.partial(_gla_decode_kernel, bb=bb),
        [blk(Z_QK), blk(Z_GV), blk(Z_GR), pl.BlockSpec((bb, LANES), lambda i: (i, 0)),
         const((LANES, GLA_QK)), const((1, GLA_QK)), const((1, GLA_V)), sspec],
        (z, z, z, zs, wa, ba, gn, states),
        [pl.BlockSpec((bb, GLA_V), lambda i: (i, 0)), sspec],
        [jax.ShapeDtypeStruct((rows, GLA_V), BF16), jax.ShapeDtypeStruct(states.shape, F32)],
        1, prev, "gla_decode", (rows // bb,))


def _mlstm_decode_kernel(q_ref, k_ref, v_ref, og_ref, sm_ref, gb_ref, gn_ref, c_ref, n_ref, m_ref,
                         y_ref, co_ref, no_ref, mo_ref, *, bb):
    li_blk = sm_ref[...] + gb_ref[...]
    lf_blk = _log_sigmoid(li_blk)
    m_blk = m_ref[...]
    q_blk = q_ref[...] * (ML_D ** -0.5)
    k_blk = k_ref[...]
    v_blk = v_ref[...]
    lane = lax.broadcasted_iota(jnp.int32, (1, LANES), 1)

    def per_row(b):
        row = slice(b, b + 1)
        li_r, lf_r, m_r = li_blk[row], lf_blk[row], m_blk[row]
        q_all, k_all, v_all = q_blk[row], k_blk[row], v_blk[row]
        n_all = n_ref[b]
        m_new_row = jnp.zeros((1, LANES), F32)
        h_heads, n_heads, c_heads = [], [], []
        for h in range(ML_HEADS):
            sl = slice(h * ML_D, (h + 1) * ML_D)
            li = li_r[:, SM_MI + h:SM_MI + h + 1]
            lf = lf_r[:, SM_MF + h:SM_MF + h + 1]
            m_prev = m_r[:, h:h + 1]
            m_new = jnp.maximum(lf + m_prev, li)
            a = jnp.exp(lf + m_prev - m_new)
            e = jnp.exp(li - m_new)
            k_r = k_all[:, sl]
            q_r = q_all[:, sl]
            c_new = a * c_ref[b, h] + (_row_to_col(k_r, ML_D) * e) * v_all[:, sl]
            n_new = a * n_all[h:h + 1, :] + e * k_r
            c_heads.append(c_new)
            n_heads.append(n_new)
            num = jnp.sum(_row_to_col(q_r, ML_D) * c_new, axis=0, keepdims=True)
            den = jnp.sum(q_r * n_new, axis=1, keepdims=True)
            h_heads.append(num / jnp.maximum(jnp.abs(den), jnp.exp(-m_new)))
            m_new_row = jnp.where(lane == h, m_new, m_new_row)
        return (jnp.stack(c_heads), jnp.concatenate(n_heads, axis=0),
                jnp.concatenate(h_heads, axis=1), m_new_row)

    res = [per_row(b) for b in range(bb)]
    co_ref[...] = jnp.stack([r[0] for r in res])
    no_ref[...] = jnp.stack([r[1] for r in res])
    mo_ref[...] = jnp.concatenate([r[3] for r in res], axis=0)
    hh = jnp.concatenate([r[2] for r in res], axis=0)
    og = og_ref[...]
    for h in range(ML_HEADS):
        sl = slice(h * ML_D, (h + 1) * ML_D)
        y_ref[:, sl] = (_head_norm(hh[:, sl], gn_ref[:, sl]) * _sigmoid(og[:, sl])).astype(BF16)


def _mlstm_decode(z, zs, gbias, gn, c_all, n_all, m_pad, prev, *, layer, bb):
    rows = z.shape[0]
    blk = lambda j: pl.BlockSpec((bb, 512), lambda i: (i, j))
    const = lambda shape: pl.BlockSpec(shape, lambda i: (0,) * len(shape))
    nblock = (bb, ML_HEADS, ML_D)
    cspec = pl.BlockSpec((None, bb, ML_HEADS, ML_D, ML_D), lambda i: (layer, i, 0, 0, 0))
    mspec = pl.BlockSpec((bb, LANES), lambda i: (i, 0))
    return _layer_out(
        functools.partial(_mlstm_decode_kernel, bb=bb),
        [blk(Z_MQ), blk(Z_MK), blk(Z_MV), blk(Z_MO), mspec, const((1, LANES)), const((1, ML_W)),
         cspec, pl.BlockSpec((None,) + nblock, lambda i: (layer, i, 0, 0)), mspec],
        (z, z, z, z, zs, gbias, gn, c_all, n_all, m_pad),
        [pl.BlockSpec((bb, ML_W), lambda i: (i, 0)), cspec, pl.BlockSpec(nblock, lambda i: (i, 0, 0)), mspec],
        [jax.ShapeDtypeStruct((rows, ML_W), BF16), jax.ShapeDtypeStruct(c_all.shape, F32),
         jax.ShapeDtypeStruct(n_all.shape[1:], F32), jax.ShapeDtypeStruct((rows, LANES), F32)],
        1, prev, "mlstm_decode", (rows // bb,))


def _merged_residual(ya_ref, yb_ref, yp_ref, u_ref, za_ref, zb_ref, zc_ref, x_ref,
                     d_ref, wg_ref, wa_ref, wb_ref, wc_ref, wo_ref):
    yc = _gelu_tanh(yp_ref[...] + d_ref[...] * u_ref[...].astype(F32))
    yc = yc * _sigmoid(_dot(yc.astype(BF16), wg_ref[...]))
    m = _sigmoid(za_ref[...].astype(F32)) * _dot(ya_ref[...], wa_ref[...])
    m = m + _sigmoid(zb_ref[...].astype(F32)) * _dot(yb_ref[...], wb_ref[...])
    m = m + _sigmoid(zc_ref[...].astype(F32)) * _dot(yc.astype(BF16), wc_ref[...])
    return x_ref[...] + _dot(m.astype(BF16), wo_ref[...])


def _merge_kernel(*refs):
    refs[-1][...] = _merged_residual(*refs[:-1])


def _merge_specs(bm, row_map, const):
    r512 = pl.BlockSpec((bm, 512), lambda *g: (row_map(*g), 0))
    zblk = lambda j: pl.BlockSpec((bm, D_MODEL), lambda *g: (row_map(*g), j))
    return [r512, r512, r512, pl.BlockSpec((bm, 512), lambda *g: (row_map(*g), Z_SU)),
            zblk(Z_ZA), zblk(Z_ZB), zblk(Z_ZC), pl.BlockSpec((bm, D_MODEL), lambda *g: (row_map(*g), 0)),
            const((1, S5_WIDTH)), const((S5_WIDTH, S5_WIDTH)),
            const((GLA_V, D_MODEL)), const((ML_W, D_MODEL)), const((S5_WIDTH, D_MODEL)),
            const((D_MODEL, D_MODEL))]


def _merge(ya, yb, ypre, z, x, d, wg, wa, wb, wc, wo, *, bm):
    n = x.shape[0]
    const = lambda shape: pl.BlockSpec(shape, lambda i: (0, 0))
    return pl.pallas_call(
        _merge_kernel, grid=(n // bm,),
        in_specs=_merge_specs(bm, lambda i: i, const),
        out_specs=pl.BlockSpec((bm, D_MODEL), lambda i: (i, 0)),
        out_shape=jax.ShapeDtypeStruct((n, D_MODEL), F32),
        compiler_params=_cparams(("arbitrary",)), name="merge",
    )(ya, yb, ypre, z, z, z, z, x, d, wg, wa, wb, wc, wo)


def _cross_attend(x, g_ref, wq_ref, mk_ref, mv_ref, wo_ref):
    q = _dot(_rms_rows(x, g_ref[...]).astype(BF16), wq_ref[...])
    heads = []
    for h in range(X_HEADS):
        sl = slice(h * X_DH, (h + 1) * X_DH)
        s = _dot_nt(q[:, sl].astype(BF16), mk_ref[:, sl].astype(BF16)) * (X_DH ** -0.5)
        s = s - jnp.max(s, axis=-1, keepdims=True)
        p = jnp.exp(s)
        p = p / jnp.sum(p, axis=-1, keepdims=True)
        heads.append(_dot(p.astype(BF16), mv_ref[:, sl].astype(BF16)))
    o = jnp.concatenate(heads, axis=-1)
    return x + _dot(o.astype(BF16), wo_ref[...])


N_MERGE_IN = 14


def _merge_cross_kernel(*refs):
    x1 = _merged_residual(*refs[:N_MERGE_IN])
    refs[-1][...] = _cross_attend(x1, *refs[N_MERGE_IN:-1])


def _merge_cross(ya, yb, ypre, z, x, d, wg, wa, wb, wc, wo, g, wq, mem_k, mem_v, wco,
                 *, layer, batch, seq, bm):
    nt = seq // bm
    const = lambda shape: pl.BlockSpec(shape, lambda b, t: (0, 0), pipeline_mode=pl.Buffered(1))
    mspec = pl.BlockSpec((None, MEM_LEN, D_MODEL), lambda b, t: (layer, b, 0))
    xspec = pl.BlockSpec((bm, D_MODEL), lambda b, t: (b * nt + t, 0))
    return pl.pallas_call(
        _merge_cross_kernel, grid=(batch, nt),
        in_specs=_merge_specs(bm, lambda b, t: b * nt + t, const)
        + [const((1, D_MODEL)), const((D_MODEL, D_MODEL)), mspec, mspec, const((D_MODEL, D_MODEL))],
        out_specs=xspec, out_shape=jax.ShapeDtypeStruct(x.shape, F32),
        compiler_params=_cparams(("arbitrary", "arbitrary")), name="merge_cross",
    )(ya, yb, ypre, z, z, z, z, x, d, wg, wa, wb, wc, wo, g, wq, mem_k, mem_v, wco)


def _memory_kv_kernel(x_ref, g_ref, wk_ref, wv_ref, k_ref, v_ref, hn_ref):
    @pl.when(pl.program_id(1) == 0)
    def _():
        hn_ref[...] = _rms_rows(x_ref[...], g_ref[...]).astype(BF16)
    k_ref[...] = _dot(hn_ref[...], wk_ref[...])
    v_ref[...] = _dot(hn_ref[...], wv_ref[...])


def _memory_kv(mem, g, wk, wv, *, bn=512):
    rows, d = mem.shape
    depth = wk.shape[0]
    wspec = pl.BlockSpec((None, d, bn), lambda l, j: (l, 0, j))
    ospec = pl.BlockSpec((None, rows, bn), lambda l, j: (l, 0, j))
    out = jax.ShapeDtypeStruct((depth, rows, d), F32)
    return pl.pallas_call(
        _memory_kv_kernel, grid=(depth, d // bn),
        in_specs=[pl.BlockSpec((rows, d), lambda l, j: (0, 0)),
                  pl.BlockSpec((None, 1, d), lambda l, j: (l, 0, 0)), wspec, wspec],
        out_specs=[ospec, ospec], out_shape=[out, out],
        scratch_shapes=[pltpu.VMEM((rows, d), BF16)],
        compiler_params=_cparams(("arbitrary", "arbitrary")), name="memory_kv",
    )(mem, g, wk, wv)


def _cross_decode_kernel(x_ref, g_ref, wq_ref, mk_ref, mv_ref, wo_ref, o_ref, *, bb):
    x = x_ref[...]
    q = _dot(_rms_rows(x, g_ref[...]).astype(BF16), wq_ref[...])
    rows = []
    for b in range(bb):
        q4 = jnp.concatenate([q[b:b + 1, h * X_DH:(h + 1) * X_DH] for h in range(X_HEADS)], axis=0)
        s = jnp.sum(mk_ref[0, b] * q4[None], axis=-1, keepdims=True) * (X_DH ** -0.5)
        p = jnp.exp(s - jnp.max(s, axis=0, keepdims=True))
        p = p / jnp.sum(p, axis=0, keepdims=True)
        o4 = jnp.sum(p * mv_ref[0, b], axis=0)
        rows.append(jnp.concatenate([o4[h:h + 1, :] for h in range(X_HEADS)], axis=1))
    att = jnp.concatenate(rows, axis=0)
    o_ref[...] = x + _dot(att.astype(BF16), wo_ref[...])


def _cross_decode(x, g, wq, cache_k, cache_v, wo, *, layer, bb):
    rows = x.shape[0]
    xspec = pl.BlockSpec((bb, D_MODEL), lambda i: (i, 0))
    const = lambda shape: pl.BlockSpec(shape, lambda i: (0, 0))
    mspec = pl.BlockSpec((1, bb, MEM_LEN, X_HEADS, X_DH), lambda i: (layer, i, 0, 0, 0))
    return pl.pallas_call(
        functools.partial(_cross_decode_kernel, bb=bb), grid=(rows // bb,),
        in_specs=[xspec, const((1, D_MODEL)), const((D_MODEL, D_MODEL)), mspec, mspec,
                  const((D_MODEL, D_MODEL))],
        out_specs=xspec, out_shape=jax.ShapeDtypeStruct(x.shape, F32),
        compiler_params=_cparams(("arbitrary",)), name="cross_decode",
    )(x, g, wq, cache_k, cache_v, wo)


def _ffn_prompt_kernel(x_ref, g_ref, wup_ref, cw_ref, cb_ref, wdn_ref, gf_ref,
                       o_ref, st_ref, hn_ref, halo_ref, up_ref, *, tt, final_norm):
    t = pl.program_id(1)

    @pl.when(t == 0)
    def _():
        halo_ref[...] = jnp.zeros_like(halo_ref)

    x = x_ref[...]
    hn_ref[...] = _rms_rows(x, g_ref[...]).astype(BF16)
    def project(c):
        slot = c % 2
        for part, col0 in enumerate((c * FF_CHUNK, D_FF + c * FF_CHUNK)):
            cols = slice(col0, col0 + FF_CHUNK)
            lanes = slice(part * FF_CHUNK, (part + 1) * FF_CHUNK)
            up = _dot(hn_ref[...], wup_ref[:, cols])
            up_ref[slot, 6:8, lanes] = halo_ref[0:2, cols]
            up_ref[slot, 8:8 + tt, lanes] = up
            halo_ref[0:2, cols] = up[tt - 2:tt, :]

    def gated(c):
        slot = c % 2
        a_cols = slice(c * FF_CHUNK, (c + 1) * FF_CHUNK)
        g_cols = slice(D_FF + c * FF_CHUNK, D_FF + (c + 1) * FF_CHUNK)
        cw = jnp.concatenate([cw_ref[:, a_cols], cw_ref[:, g_cols]], axis=1)
        cb = jnp.concatenate([cb_ref[:, a_cols], cb_ref[:, g_cols]], axis=1)
        cv = (up_ref[slot, 6:6 + tt, :] * cw[0:1] + up_ref[slot, 7:7 + tt, :] * cw[1:2]
              + up_ref[slot, 8:8 + tt, :] * cw[2:3] + cb)
        gt = cv[:, FF_CHUNK:]
        return (cv[:, :FF_CHUNK] * (gt * _sigmoid(gt))).astype(BF16)

    acc = x
    project(0)
    for c in range(N_FF_CHUNKS):
        if c + 1 < N_FF_CHUNKS:
            project(c + 1)
        acc = acc + _dot(gated(c), wdn_ref[c * FF_CHUNK:(c + 1) * FF_CHUNK, :])
    if final_norm:
        acc = _rms_rows(acc, gf_ref[...])
    o_ref[...] = acc

    @pl.when(t == pl.num_programs(1) - 1)
    def _():
        st_ref[0] = halo_ref[0:2, :]


def _ffn_prompt(x, g, wup, cw, cb, wdn, gf, *, batch, seq, tt, final_norm):
    nt = seq // tt
    xspec = pl.BlockSpec((tt, D_MODEL), lambda b, t: (b * nt + t, 0))
    const = lambda shape: pl.BlockSpec(shape, lambda b, t: (0, 0), pipeline_mode=pl.Buffered(1))
    return pl.pallas_call(
        functools.partial(_ffn_prompt_kernel, tt=tt, final_norm=final_norm), grid=(batch, nt),
        in_specs=[xspec, const((1, D_MODEL)), const((D_MODEL, 2 * D_FF)), const((CONV_W, 2 * D_FF)),
                  const((1, 2 * D_FF)), const((D_FF, D_MODEL)), const((1, D_MODEL))],
        out_specs=[xspec, pl.BlockSpec((1, CONV_W - 1, 2 * D_FF), lambda b, t: (b, 0, 0))],
        out_shape=[jax.ShapeDtypeStruct(x.shape, F32),
                   jax.ShapeDtypeStruct((batch, CONV_W - 1, 2 * D_FF), F32)],
        scratch_shapes=[pltpu.VMEM((tt, D_MODEL), BF16), pltpu.VMEM((8, 2 * D_FF), F32),
                        pltpu.VMEM((2, tt + 8, 2 * FF_CHUNK), F32)],
        compiler_params=_cparams(("arbitrary", "arbitrary")), name="ffn_prompt",
    )(x, g, wup, cw, cb, wdn, gf)


def _ffn_decode_kernel(x_ref, g_ref, wa_ref, wg_ref, cwa_ref, cwg_ref, cba_ref, cbg_ref,
                       b0a_ref, b0g_ref, b1a_ref, b1g_ref, wdn_ref, gf_ref,
                       o_ref, upa_ref, upg_ref, hn_ref, acc_ref, *, final_norm):
    c = pl.program_id(0)

    @pl.when(c == 0)
    def _():
        x = x_ref[...]
        hn_ref[...] = _rms_rows(x, g_ref[...]).astype(BF16)
        acc_ref[...] = x

    def conv(w_ref, cw_ref, cb_ref, b0_ref, b1_ref, up_ref):
        up = _dot(hn_ref[...], w_ref[...])
        up_ref[...] = up
        return b0_ref[...] * cw_ref[0:1, :] + b1_ref[...] * cw_ref[1:2, :] + up * cw_ref[2:3, :] + cb_ref[...]

    a = conv(wa_ref, cwa_ref, cba_ref, b0a_ref, b1a_ref, upa_ref)
    gt = conv(wg_ref, cwg_ref, cbg_ref, b0g_ref, b1g_ref, upg_ref)
    act = a * (gt * _sigmoid(gt))
    acc_ref[...] += _dot(act.astype(BF16), wdn_ref[...])

    @pl.when(c == pl.num_programs(0) - 1)
    def _():
        acc = acc_ref[...]
        if final_norm:
            acc = _rms_rows(acc, gf_ref[...])
        o_ref[...] = acc


def _ffn_decode(x, g, wup, cw, cb, wdn, gf, buf0, buf1, *, final_norm):
    rows = x.shape[0]
    fc = FF_CHUNK
    full = lambda shape: pl.BlockSpec(shape, lambda c: (0, 0))
    a_col = lambda r: pl.BlockSpec((r, fc), lambda c: (0, c))
    g_col = lambda r: pl.BlockSpec((r, fc), lambda c: (0, N_FF_CHUNKS + c))
    return pl.pallas_call(
        functools.partial(_ffn_decode_kernel, final_norm=final_norm), grid=(N_FF_CHUNKS,),
        in_specs=[full((rows, D_MODEL)), full((1, D_MODEL)),
                  a_col(D_MODEL), g_col(D_MODEL), a_col(CONV_W), g_col(CONV_W), a_col(1), g_col(1),
                  a_col(rows), g_col(rows), a_col(rows), g_col(rows),
                  pl.BlockSpec((fc, D_MODEL), lambda c: (c, 0)), full((1, D_MODEL))],
        out_specs=[full((rows, D_MODEL)), a_col(rows), a_col(rows)],
        out_shape=[jax.ShapeDtypeStruct(x.shape, F32),
                   jax.ShapeDtypeStruct((rows, D_FF), F32),
                   jax.ShapeDtypeStruct((rows, D_FF), F32)],
        scratch_shapes=[pltpu.VMEM((rows, D_MODEL), BF16), pltpu.VMEM((rows, D_MODEL), F32)],
        compiler_params=_cparams(("arbitrary",)), name="ffn_decode",
    )(x, g, wup, wup, cw, cw, cb, cb, buf0, buf0, buf1, buf1, wdn, gf)


def _layer_weights(P, l):
    w_in = P['w_in'][l]
    w_main = jnp.concatenate([w_in[:, 0:1536], w_in[:, 1552:3600], w_in[:, 3608:7192]], axis=1).astype(BF16)
    w_small = jnp.concatenate([w_in[:, 1536:1552], w_in[:, 3600:3608],
                               jnp.zeros((D_MODEL, LANES - GLA_LOWRANK - 2 * ML_HEADS), F32)], axis=1).astype(BF16)
    wa = jnp.zeros((LANES, GLA_QK), F32).at[:GLA_LOWRANK].set(P['w_gla_alpha'][l]).astype(BF16)
    gbias = jnp.zeros((1, LANES), F32)
    gbias = gbias.at[0, SM_MI:SM_MI + ML_HEADS].set(P['b_mlstm_i'][l])
    gbias = gbias.at[0, SM_MF:SM_MF + ML_HEADS].set(P['b_mlstm_f'][l])
    eye = jnp.eye(S5_GB, dtype=F32)
    w_re = jnp.einsum('ngpc,gh->ngchp', P['s5_b_re'][l].reshape(S5_NB, S5_GB, S5_P, S5_GROUP), eye)
    w_im = jnp.einsum('ngpc,gh->ngchp', P['s5_b_im'][l].reshape(S5_NB, S5_GB, S5_P, S5_GROUP), eye)
    w_blk = jnp.concatenate([w_re.reshape(S5_NB, S5_UB, S5_MB), w_im.reshape(S5_NB, S5_UB, S5_MB)],
                            axis=-1).astype(BF16)
    ct_re = jnp.einsum('ngcp,gh->ngchp', P['s5_c_re'][l].reshape(S5_NB, S5_GB, S5_GROUP, S5_P), eye)
    ct_im = jnp.einsum('ngcp,gh->ngchp', P['s5_c_im'][l].reshape(S5_NB, S5_GB, S5_GROUP, S5_P), eye)
    ct_re = ct_re.reshape(S5_NB, S5_UB, S5_MB)
    ct_im = ct_im.reshape(S5_NB, S5_UB, S5_MB)
    lam_re = P['s5_lam_re'][l].reshape(S5_NB, 1, S5_MB)
    lam_im = P['s5_lam_im'][l].reshape(S5_NB, 1, S5_MB)
    log_dt = jnp.broadcast_to(P['s5_log_dt'][l][:, None], (S5_GROUPS, S5_P)).reshape(S5_NB, 1, S5_MB)
    par, cfold = _s5_prep(lam_re, lam_im, log_dt, ct_re, ct_im)
    row = lambda a: a.reshape(1, -1)
    return dict(
        norm_mix=row(P['norm_mix'][l]), w_main=w_main, w_small=w_small, wa=wa,
        ba=row(P['b_gla_alpha'][l]), gla_norm=row(P['gla_head_norm'][l]), gbias=gbias,
        ml_norm=row(P['mlstm_head_norm'][l]), w_blk=w_blk, ct_re=ct_re.astype(BF16),
        ct_im=ct_im.astype(BF16), par=par, cfold=cfold, s5_d=row(P['s5_d'][l]),
        w_glu=P['s5_w_glu'][l].astype(BF16), w_a=P['w_branch_a'][l].astype(BF16),
        w_b=P['w_branch_b'][l].astype(BF16), w_c=P['w_branch_c'][l].astype(BF16),
        w_out=P['w_out'][l].astype(BF16), norm_cross=row(P['norm_cross'][l]),
        w_cq=P['w_cq'][l].astype(BF16), w_co=P['w_co'][l].astype(BF16),
        norm_ffn=row(P['norm_ffn'][l]), w_up=P['w_ffn_up'][l].astype(BF16),
        conv_w=P['ffn_conv_w'][l], conv_b=row(P['ffn_conv_b'][l]),
        w_down=P['w_ffn_down'][l].astype(BF16), norm_final=row(P['norm_final']),
    )


def _prompt_trunk(x_prompt, mem_k, mem_v, W, *, bm_in, tt_mix, gla_group, ml_group, tt_s5, bm, tt_ffn):
    batch, seq, _ = x_prompt.shape
    depth = len(W)
    x = x_prompt.reshape(batch * seq, D_MODEL)
    outs = []
    for l, w in enumerate(W):
        z, zs = _norm_matmul(x, w['norm_mix'], w['w_main'], w['w_small'], bm=bm_in, bn=1024, out_dtype=BF16)
        z3 = z.reshape(batch, seq, Z_MAIN)
        zs3 = zs.reshape(batch, seq, LANES)
        ya, st = _gla_prompt(z3, zs3, w['wa'], w['ba'], w['gla_norm'], tt=tt_mix, group=gla_group)
        yb, c, n, m = _mlstm_prompt(z3, zs3, w['gbias'], w['ml_norm'], tt=tt_mix, group=ml_group)
        ypre, hfin = _s5_prompt(z3, w['w_blk'], w['cfold'], w['par'], batch=batch, seq=seq, tt=tt_s5)
        x = _merge_cross(ya.reshape(batch * seq, GLA_V), yb.reshape(batch * seq, ML_W),
                         ypre.reshape(batch * seq, S5_WIDTH), z, x, w['s5_d'], w['w_glu'],
                         w['w_a'], w['w_b'], w['w_c'], w['w_out'],
                         w['norm_cross'], w['w_cq'], mem_k, mem_v, w['w_co'],
                         layer=l, batch=batch, seq=seq, bm=bm)
        x, conv = _ffn_prompt(x, w['norm_ffn'], w['w_up'], w['conv_w'], w['conv_b'], w['w_down'],
                              w['norm_final'], batch=batch, seq=seq, tt=tt_ffn,
                              final_norm=(l == depth - 1))
        st4 = st.reshape(batch, GLA_HEADS, GLA_DV, GLA_HEADS, GLA_DK)
        gla = jnp.stack([st4[:, h, :, h, :] for h in range(GLA_HEADS)], axis=1).transpose(0, 1, 3, 2)
        h4 = hfin.reshape(S5_NB, batch, 2, S5_GB, S5_P)
        s5_re = h4[:, :, 0].transpose(1, 0, 2, 3).reshape(batch, S5_GROUPS, S5_P)
        s5_im = h4[:, :, 1].transpose(1, 0, 2, 3).reshape(batch, S5_GROUPS, S5_P)
        outs.append((gla, c.reshape(batch, ML_HEADS, ML_D, ML_D), n[:, :ML_HEADS, :],
                     m[:, :ML_HEADS, 0], s5_re, s5_im, conv))
    stacked = [jnp.stack([outs[l][i] for l in range(depth)]) for i in range(7)]
    return x.reshape(batch, seq, D_MODEL), stacked


def _sample_trunk(x_sample, cache_k, cache_v, states, W):
    rows = x_sample.shape[0]
    depth = len(W)
    x = x_sample.reshape(rows, D_MODEL)
    s_gla, s_c, s_n, s_m, s_re, s_im, s_conv = states
    outs = []
    gla_all = c_all = None
    for l, w in enumerate(W):
        z, zs = _norm_matmul(x, w['norm_mix'], w['w_main'], w['w_small'], bm=rows, bn=1024)
        ya, gla_all = _gla_decode(z, zs, w['wa'], w['ba'], w['gla_norm'], s_gla, gla_all, layer=l, bb=16)
        m_pad = jnp.pad(s_m[l], ((0, 0), (0, LANES - ML_HEADS)))
        yb, c_all, n, m = _mlstm_decode(z, zs, w['gbias'], w['ml_norm'], s_c, s_n, m_pad, c_all,
                                        layer=l, bb=16)
        ypre, h_re, h_im = _s5_decode(z, w['w_blk'], w['ct_re'], w['ct_im'], w['par'],
                                      s_re[l].reshape(rows, S5_MODES), s_im[l].reshape(rows, S5_MODES))
        x = _merge(ya, yb, ypre, z, x, w['s5_d'], w['w_glu'], w['w_a'], w['w_b'], w['w_c'], w['w_out'],
                   bm=rows)
        x = _cross_decode(x, w['norm_cross'], w['w_cq'], cache_k, cache_v, w['w_co'], layer=l, bb=8)
        x, up_a, up_g = _ffn_decode(x, w['norm_ffn'], w['w_up'], w['conv_w'], w['conv_b'], w['w_down'],
                                    w['norm_final'], s_conv[l][:, 0, :], s_conv[l][:, 1, :],
                                    final_norm=(l == depth - 1))
        conv = jnp.stack([s_conv[l][:, 1, :], jnp.concatenate([up_a, up_g], axis=1)], axis=1)
        outs.append((n, m[:, :ML_HEADS], h_re.reshape(rows, S5_GROUPS, S5_P),
                     h_im.reshape(rows, S5_GROUPS, S5_P), conv))
    stacked = [jnp.stack([outs[l][i] for l in range(depth)]) for i in range(5)]
    return x.reshape(rows, 1, D_MODEL), [gla_all, c_all] + stacked


def kernel(x_prompt, x_sample, mem_prompt, cache_mem_k, cache_mem_v, state_gla, state_mlstm_c, state_mlstm_n, state_mlstm_m, state_s5_re, state_s5_im, state_ffn_conv, norm_mix, w_in, w_gla_alpha, b_gla_alpha, gla_head_norm, b_mlstm_i, b_mlstm_f, mlstm_head_norm, s5_lam_re, s5_lam_im, s5_log_dt, s5_b_re, s5_b_im, s5_c_re, s5_c_im, s5_d, s5_w_glu, w_branch_a, w_branch_b, w_branch_c, w_out, norm_cross, norm_mem, w_cq, w_ck, w_cv, w_co, norm_ffn, w_ffn_up, ffn_conv_w, ffn_conv_b, w_ffn_down, norm_final):
    P = dict(norm_mix=norm_mix, w_in=w_in, w_gla_alpha=w_gla_alpha, b_gla_alpha=b_gla_alpha,
             gla_head_norm=gla_head_norm, b_mlstm_i=b_mlstm_i, b_mlstm_f=b_mlstm_f,
             mlstm_head_norm=mlstm_head_norm, s5_lam_re=s5_lam_re, s5_lam_im=s5_lam_im,
             s5_log_dt=s5_log_dt, s5_b_re=s5_b_re, s5_b_im=s5_b_im, s5_c_re=s5_c_re,
             s5_c_im=s5_c_im, s5_d=s5_d, s5_w_glu=s5_w_glu, w_branch_a=w_branch_a,
             w_branch_b=w_branch_b, w_branch_c=w_branch_c, w_out=w_out, norm_cross=norm_cross,
             w_cq=w_cq, w_co=w_co, norm_ffn=norm_ffn, w_ffn_up=w_ffn_up, ffn_conv_w=ffn_conv_w,
             ffn_conv_b=ffn_conv_b, w_ffn_down=w_ffn_down, norm_final=norm_final)
    depth = w_in.shape[0]
    W = [_layer_weights(P, l) for l in range(depth)]
    batch, mem_len, _ = mem_prompt.shape
    mem2 = mem_prompt.reshape(batch * mem_len, D_MODEL)
    mem_k, mem_v = _memory_kv(mem2, norm_mem.reshape(depth, 1, D_MODEL), w_ck.astype(BF16), w_cv.astype(BF16))
    y_prompt, p_states = _prompt_trunk(x_prompt, mem_k, mem_v, W, bm_in=2048, tt_mix=256, gla_group=2, ml_group=2, tt_s5=256, bm=512,
                                       tt_ffn=512)
    p_mem_k = mem_k.reshape(depth, batch, mem_len, X_HEADS, X_DH)
    p_mem_v = mem_v.reshape(depth, batch, mem_len, X_HEADS, X_DH)
    y_sample, s_states = _sample_trunk(
        x_sample, cache_mem_k, cache_mem_v,
        (state_gla, state_mlstm_c, state_mlstm_n, state_mlstm_m, state_s5_re, state_s5_im, state_ffn_conv), W)
    return (y_prompt, y_sample, *p_states, p_mem_k, p_mem_v, *s_states)
```

```python
import functools

import jax
import jax.numpy as jnp
from jax import lax
from jax.experimental import pallas as pl
from jax.experimental.pallas import tpu as pltpu

F32 = jnp.float32
BF16 = jnp.bfloat16

D_MODEL = 1024
GLA_HEADS, GLA_DK, GLA_DV = 4, 64, 128
GLA_QK = GLA_HEADS * GLA_DK
GLA_V = GLA_HEADS * GLA_DV
GLA_LOWRANK = 16
GLA_TAU = 16.0
ML_HEADS, ML_D = 4, 128
ML_W = ML_HEADS * ML_D
S5_GROUP, S5_GROUPS, S5_P = 16, 32, 64
S5_WIDTH = S5_GROUP * S5_GROUPS
S5_MODES = S5_GROUPS * S5_P
S5_NB = 4
S5_GB = S5_GROUPS // S5_NB
S5_MB = S5_MODES // S5_NB
S5_UB = S5_WIDTH // S5_NB
MEM_LEN = 256
X_HEADS = 4
X_DH = D_MODEL // X_HEADS
D_FF = 2816
CONV_W = 3
EPS = 1e-6

CHUNK = 256
LANES = 128
FF_CHUNK = 256
N_FF_CHUNKS = D_FF // FF_CHUNK

Z_MAIN = 7168
Z_QK, Z_GV, Z_GR, Z_MQ, Z_MK, Z_MV, Z_MO, Z_SU = range(8)
Z_ZA, Z_ZB, Z_ZC = 4, 5, 6
SM_MI = GLA_LOWRANK
SM_MF = GLA_LOWRANK + ML_HEADS

GLA_SAFE_DECAY = 80.0

VMEM_LIMIT = 56 * 1024 * 1024


def _cparams(sem):
    return pltpu.CompilerParams(dimension_semantics=sem, vmem_limit_bytes=VMEM_LIMIT)


def _dot(a, b):
    return jnp.dot(a, b, preferred_element_type=F32)


def _dot_nt(a, b):
    return lax.dot_general(a, b, (((1,), (1,)), ((), ())), preferred_element_type=F32)


def _dot_tn(a, b):
    return lax.dot_general(a, b, (((0,), (0,)), ((), ())), preferred_element_type=F32)


def _sigmoid(x):
    return 1.0 / (1.0 + jnp.exp(-x))


def _log_sigmoid(x):
    return jnp.minimum(x, 0.0) - jnp.log(1.0 + jnp.exp(-jnp.abs(x)))


def _gelu_tanh(x):
    return 0.5 * x * (1.0 + jnp.tanh(0.7978845608028654 * (x + 0.044715 * x * x * x)))


def _rms_rows(x, g):
    r = lax.rsqrt(jnp.mean(x * x, axis=-1, keepdims=True) + EPS)
    return (x * r) * g


def _dot_exact01(a01, x):
    hi = x.astype(BF16)
    r1 = x - hi.astype(F32)
    mid = r1.astype(BF16)
    lo = (r1 - mid.astype(F32)).astype(BF16)
    return _dot(a01, hi) + _dot(a01, mid) + _dot(a01, lo)


def _dot_exact01_rhs(x, b01):
    hi = x.astype(BF16)
    r1 = x - hi.astype(F32)
    mid = r1.astype(BF16)
    lo = (r1 - mid.astype(F32)).astype(BF16)
    return _dot(hi, b01) + _dot(mid, b01) + _dot(lo, b01)


def _row_to_col(row, n):
    eye = (lax.broadcasted_iota(jnp.int32, (n, n), 0) == lax.broadcasted_iota(jnp.int32, (n, n), 1))
    return jnp.sum(jnp.where(eye, jnp.broadcast_to(row, (n, n)), 0.0), axis=1, keepdims=True)


def _norm_mm_small_kernel(x_ref, g_ref, w_ref, ws_ref, o_ref, os_ref, hn_ref):
    @pl.when(pl.program_id(1) == 0)
    def _():
        hn = _rms_rows(x_ref[...], g_ref[...]).astype(BF16)
        hn_ref[...] = hn
        os_ref[...] = _dot(hn, ws_ref[...])
    o_ref[...] = _dot(hn_ref[...], w_ref[...]).astype(o_ref.dtype)


def _norm_matmul(x, g, w, w_small, *, bm, bn, out_dtype=F32):
    n, d = x.shape
    c = w.shape[1]
    grid = (n // bm, c // bn)
    x_spec = pl.BlockSpec((bm, d), lambda i, j: (i, 0))
    g_spec = pl.BlockSpec((1, d), lambda i, j: (0, 0))
    w_spec = pl.BlockSpec((d, bn), lambda i, j: (0, j))
    o_spec = pl.BlockSpec((bm, bn), lambda i, j: (i, j))
    scratch = [pltpu.VMEM((bm, d), BF16)]
    cs = w_small.shape[1]
    return pl.pallas_call(
        _norm_mm_small_kernel, grid=grid,
        in_specs=[x_spec, g_spec, w_spec, pl.BlockSpec((d, cs), lambda i, j: (0, 0))],
        out_specs=[o_spec, pl.BlockSpec((bm, cs), lambda i, j: (i, 0))],
        out_shape=[jax.ShapeDtypeStruct((n, c), out_dtype), jax.ShapeDtypeStruct((n, cs), F32)],
        scratch_shapes=scratch,
        compiler_params=_cparams(("arbitrary", "arbitrary")), name="norm_matmul_in",
    )(x, g, w, w_small)


def _head_norm(oh, gn_h):
    r = lax.rsqrt(jnp.mean(oh * oh, axis=-1, keepdims=True) + EPS)
    return oh * r * gn_h


def _gla_log_decay(small, wa_ref, ba_ref):
    a_pre = _dot(small.astype(BF16), wa_ref[...]) + ba_ref[...]
    return _log_sigmoid(a_pre) * (1.0 / GLA_TAU)


def _gla_prompt_kernel(qk_ref, v_ref, gr_ref, sm_ref, wa_ref, ba_ref, gn_ref,
                       y_ref, st_ref, state_ref, b_ref, k_ref, *, n_chunks, group):
    t = pl.program_id(1)
    L = CHUNK

    @pl.when(t == 0)
    def _():
        state_ref[...] = jnp.zeros_like(state_ref)

    ri = lax.broadcasted_iota(jnp.int32, (L, L), 0)
    ci = lax.broadcasted_iota(jnp.int32, (L, L), 1)
    tril = jnp.where(ri >= ci, 1.0, 0.0).astype(BF16)
    rw = lax.broadcasted_iota(jnp.int32, (L, GLA_HEADS * L), 0)
    cw = lax.broadcasted_iota(jnp.int32, (L, GLA_HEADS * L), 1)
    causal_wide = (cw % L) <= rw
    kr = lax.broadcasted_iota(jnp.int32, (GLA_HEADS * L, GLA_QK), 0)
    kc = lax.broadcasted_iota(jnp.int32, (GLA_HEADS * L, GLA_QK), 1)
    kk_mask = (kr // L) == (kc // GLA_DK)
    vr = lax.broadcasted_iota(jnp.int32, (GLA_HEADS * L, GLA_V), 0)
    vc = lax.broadcasted_iota(jnp.int32, (GLA_HEADS * L, GLA_V), 1)
    vv_mask = (vr // L) == (vc // GLA_DV)
    sr = lax.broadcasted_iota(jnp.int32, (GLA_V, GLA_QK), 0)
    sc = lax.broadcasted_iota(jnp.int32, (GLA_V, GLA_QK), 1)
    st_mask = (sr // GLA_DV) == (sc // GLA_DK)
    hr = lax.broadcasted_iota(jnp.int32, (GLA_QK, LANES), 0)
    hc = lax.broadcasted_iota(jnp.int32, (GLA_QK, LANES), 1)
    head_ones = jnp.where((hr // GLA_DK) == hc, 1.0, 0.0).astype(BF16)
    pr = lax.broadcasted_iota(jnp.int32, (LANES, GLA_HEADS * L), 0)
    pc = lax.broadcasted_iota(jnp.int32, (LANES, GLA_HEADS * L), 1)

    def one(e, c):
        r0 = pl.multiple_of(c * L, L)
        rows = pl.ds(r0, L)
        qk = qk_ref[e, rows, :].astype(F32)
        q = qk[:, :GLA_QK] * (GLA_DK ** -0.5)
        k = qk[:, GLA_QK:]
        v16 = v_ref[e, rows, :].astype(BF16)
        g = _gla_log_decay(sm_ref[e, rows, :], wa_ref, ba_ref)
        b = _dot_exact01(tril, g)
        b_end = b[L - 1:L, :]
        qe = q * jnp.exp(b)
        k_dec = k * jnp.exp(b_end - b)
        qe16 = qe.astype(BF16)

        def fast_att(_):
            k_til = k * jnp.exp(-b)
            kk = jnp.where(kk_mask, jnp.concatenate([k_til] * GLA_HEADS, axis=0), 0.0)
            return _dot_nt(qe16, kk.astype(BF16))

        def direct_att(_):
            b_ref[...] = b
            k_ref[...] = k

            def col(j, acc):
                kj = k_ref[pl.ds(j, 1), :]
                bj = b_ref[pl.ds(j, 1), :]
                prod = q * kj * jnp.exp(jnp.minimum(b - bj, 0.0))
                red = _dot(prod.astype(BF16), head_ones)
                place = jnp.where((pc == pr * L + j) & (pr < GLA_HEADS), 1.0, 0.0).astype(BF16)
                return acc + _dot(red.astype(BF16), place)

            return lax.fori_loop(0, L, col, jnp.zeros((L, GLA_HEADS * L), F32))

        safe = jnp.max(-b_end) <= GLA_SAFE_DECAY
        att = lax.cond(safe, fast_att, direct_att, 0)
        att = jnp.where(causal_wide, att, 0.0)

        vv = jnp.where(vv_mask, jnp.concatenate([v16] * GLA_HEADS, axis=0), jnp.zeros((), BF16))
        st = state_ref[e]
        o = _dot(att.astype(BF16), vv) + _dot_nt(qe16, st.astype(BF16))
        upd = _dot_tn(v16, k_dec.astype(BF16))
        st_new = jnp.where(st_mask, st * jnp.exp(b_end) + upd, 0.0)

        gr = gr_ref[e, rows, :].astype(F32)
        ys = []
        for h in range(GLA_HEADS):
            sl = slice(h * GLA_DV, (h + 1) * GLA_DV)
            gate = gr[:, sl]
            ys.append((_head_norm(o[:, sl], gn_ref[:, sl]) * (gate * _sigmoid(gate))).astype(BF16))
        return jnp.concatenate(ys, axis=1), st_new

    def chunk(c, carry):
        res = [one(e, c) for e in range(group)]
        rows = pl.ds(pl.multiple_of(c * L, L), L)
        y_ref[:, rows, :] = jnp.stack([r[0] for r in res])
        state_ref[...] = jnp.stack([r[1] for r in res])
        return carry

    lax.fori_loop(0, n_chunks, chunk, 0)

    @pl.when(t == pl.num_programs(1) - 1)
    def _():
        st_ref[...] = state_ref[...]


def _gla_prompt(z3, zs3, wa, ba, gn, *, tt, group):
    batch, seq, _ = z3.shape
    blk = lambda j: pl.BlockSpec((group, tt, 512), lambda b, t: (b, t, j))
    const = lambda shape: pl.BlockSpec(shape, lambda b, t: (0,) * len(shape))
    return pl.pallas_call(
        functools.partial(_gla_prompt_kernel, n_chunks=tt // CHUNK, group=group),
        grid=(batch // group, seq // tt),
        in_specs=[blk(Z_QK), blk(Z_GV), blk(Z_GR),
                  pl.BlockSpec((group, tt, LANES), lambda b, t: (b, t, 0)),
                  const((LANES, GLA_QK)), const((1, GLA_QK)), const((1, GLA_V))],
        out_specs=[pl.BlockSpec((group, tt, GLA_V), lambda b, t: (b, t, 0)),
                   pl.BlockSpec((group, GLA_V, GLA_QK), lambda b, t: (b, 0, 0))],
        out_shape=[jax.ShapeDtypeStruct((batch, seq, GLA_V), BF16),
                   jax.ShapeDtypeStruct((batch, GLA_V, GLA_QK), F32)],
        scratch_shapes=[pltpu.VMEM((group, GLA_V, GLA_QK), F32), pltpu.VMEM((CHUNK, GLA_QK), F32),
                        pltpu.VMEM((CHUNK, GLA_QK), F32)],
        compiler_params=_cparams(("arbitrary", "arbitrary")), name="gla_prompt",
    )(z3, z3, z3, zs3, wa, ba, gn)


def _mlstm_prompt_kernel(q_ref, k_ref, v_ref, og_ref, sm_ref, gb_ref, gn_ref,
                         y_ref, c_out, n_out, m_out, c_ref, n_ref, m_ref, *, n_chunks, group):
    t = pl.program_id(1)
    L = CHUNK

    @pl.when(t == 0)
    def _():
        c_ref[...] = jnp.zeros_like(c_ref)
        n_ref[...] = jnp.zeros_like(n_ref)
        m_ref[...] = jnp.zeros_like(m_ref)

    ri = lax.broadcasted_iota(jnp.int32, (L, L), 0)
    ci = lax.broadcasted_iota(jnp.int32, (L, L), 1)
    lower = ri >= ci
    tril = jnp.where(lower, 1.0, 0.0).astype(BF16)
    triu = jnp.where(ri <= ci, 1.0, 0.0).astype(BF16)
    sr = lax.broadcasted_iota(jnp.int32, (LANES, ML_W), 0)
    sh = lax.broadcasted_iota(jnp.int32, (LANES, ML_W), 1) // ML_D
    sel_i = jnp.where(sr == SM_MI + sh, 1.0, 0.0).astype(BF16)
    sel_f = jnp.where(sr == SM_MF + sh, 1.0, 0.0).astype(BF16)

    def one(e, c):
        r0 = pl.multiple_of(c * L, L)
        rows = pl.ds(r0, L)
        sm = sm_ref[e, rows, :] + gb_ref[...]
        lf_all = _log_sigmoid(sm)
        sm_t = sm.T
        b_cols = _dot_exact01(tril, lf_all)
        b_rows = _dot_exact01_rhs(_log_sigmoid(sm_t), triu)
        b_wide = _dot_exact01_rhs(b_cols, sel_f)
        li_wide = _dot_exact01_rhs(sm, sel_i)
        ys, cs, ns, ms = [], [], [], []
        for h in range(ML_HEADS):
            sl = slice(h * ML_D, (h + 1) * ML_D)
            b_tok = b_wide[:, sl]
            li_tok = li_wide[:, sl]
            li_row = sm_t[SM_MI + h:SM_MI + h + 1, :]
            b_row = b_rows[SM_MF + h:SM_MF + h + 1, :]
            m_prev = m_ref[e, h:h + 1, :]
            w = jnp.where(lower, jnp.concatenate([b_tok] * (L // LANES), axis=1) - b_row + li_row, -jnp.inf)
            m_tok = jnp.maximum(b_tok + m_prev, jnp.max(w, axis=1, keepdims=True))
            a_inter = jnp.exp(b_tok + m_prev - m_tok)
            qh = q_ref[e, rows, sl].astype(F32) * (ML_D ** -0.5)
            kh16 = k_ref[e, rows, sl].astype(BF16)
            kh = kh16.astype(F32)
            qh16, vh16 = qh.astype(BF16), v_ref[e, rows, sl].astype(BF16)
            s = _dot_nt(qh16, kh16) * jnp.exp(w - jnp.concatenate([m_tok] * (L // LANES), axis=1))
            c_h = c_ref[e, sl, :]
            n_h = n_ref[e, h:h + 1, :]
            s16 = s.astype(BF16)
            row_sum = jnp.sum(s, axis=1, keepdims=True)
            q_n = _dot_nt(qh16, jnp.broadcast_to(n_h, (LANES, ML_D)).astype(BF16))
            num = _dot(s16, vh16) + a_inter * _dot(qh16, c_h.astype(BF16))
            den = row_sum + a_inter * q_n
            hh = num / jnp.maximum(jnp.abs(den), jnp.exp(-m_tok))
            og = og_ref[e, rows, sl].astype(F32)
            ys.append((_head_norm(hh, gn_ref[:, sl]) * _sigmoid(og)).astype(BF16))
            b_end = b_tok[L - 1:L, :]
            w_end = b_end - b_tok + li_tok
            m_new = jnp.maximum(b_end + m_prev, jnp.max(w_end, axis=0, keepdims=True))
            e_inter = jnp.exp(b_end + m_prev - m_new)
            kd = kh * jnp.exp(w_end - m_new)
            cs.append(e_inter * c_h + _dot_tn(kd.astype(BF16), vh16))
            ns.append(e_inter * n_h + jnp.sum(kd, axis=0, keepdims=True))
            ms.append(m_new)
        pad = [jnp.zeros((8 - ML_HEADS, LANES), F32)]
        return (jnp.concatenate(ys, axis=1), jnp.concatenate(cs, axis=0),
                jnp.concatenate(ns + pad, axis=0), jnp.concatenate(ms + pad, axis=0))

    def chunk(c, carry):
        res = [one(e, c) for e in range(group)]
        rows = pl.ds(pl.multiple_of(c * L, L), L)
        y_ref[:, rows, :] = jnp.stack([r[0] for r in res])
        c_ref[...] = jnp.stack([r[1] for r in res])
        n_ref[...] = jnp.stack([r[2] for r in res])
        m_ref[...] = jnp.stack([r[3] for r in res])
        return carry

    lax.fori_loop(0, n_chunks, chunk, 0)

    @pl.when(t == pl.num_programs(1) - 1)
    def _():
        c_out[...] = c_ref[...]
        n_out[...] = n_ref[...]
        m_out[...] = m_ref[...]


def _mlstm_prompt(z3, zs3, gbias, gn, *, tt, group):
    batch, seq, _ = z3.shape
    blk = lambda j: pl.BlockSpec((group, tt, 512), lambda b, t: (b, t, j))
    const = lambda shape: pl.BlockSpec(shape, lambda b, t: (0,) * len(shape))
    state = lambda shape: pl.BlockSpec((group,) + shape, lambda b, t: (b, 0, 0))
    return pl.pallas_call(
        functools.partial(_mlstm_prompt_kernel, n_chunks=tt // CHUNK, group=group),
        grid=(batch // group, seq // tt),
        in_specs=[blk(Z_MQ), blk(Z_MK), blk(Z_MV), blk(Z_MO),
                  pl.BlockSpec((group, tt, LANES), lambda b, t: (b, t, 0)),
                  const((1, LANES)), const((1, ML_W))],
        out_specs=[pl.BlockSpec((group, tt, ML_W), lambda b, t: (b, t, 0)),
                   state((ML_W, ML_D)), state((8, LANES)), state((8, LANES))],
        out_shape=[jax.ShapeDtypeStruct((batch, seq, ML_W), BF16),
                   jax.ShapeDtypeStruct((batch, ML_W, ML_D), F32),
                   jax.ShapeDtypeStruct((batch, 8, LANES), F32),
                   jax.ShapeDtypeStruct((batch, 8, LANES), F32)],
        scratch_shapes=[pltpu.VMEM((group, ML_W, ML_D), F32), pltpu.VMEM((group, 8, LANES), F32),
                        pltpu.VMEM((group, 8, LANES), F32)],
        compiler_params=_cparams(("arbitrary", "arbitrary")), name="mlstm_prompt",
    )(z3, z3, z3, z3, zs3, gbias, gn)


def _s5_prep_kernel(lre_ref, lim_ref, ldt_ref, ctr_ref, cti_ref, par_ref, cf_ref):
    lam_re = lre_ref[0]
    lam_im = lim_ref[0]
    dt = jnp.exp(ldt_ref[0])
    mag = jnp.exp(lam_re * dt)
    lb_re = mag * jnp.cos(lam_im * dt)
    lb_im = mag * jnp.sin(lam_im * dt)
    nr = lb_re - 1.0
    den = lam_re * lam_re + lam_im * lam_im
    f_re = (nr * lam_re + lb_im * lam_im) / den
    f_im = (lb_im * lam_re - nr * lam_im) / den
    par_ref[0] = jnp.concatenate([lb_re, lb_im, f_re, f_im, jnp.zeros((4, S5_MB), F32)], axis=0)
    ct_re = ctr_ref[0]
    ct_im = cti_ref[0]
    cf_ref[0, :, :S5_MB] = (ct_re * f_re - ct_im * f_im).astype(BF16)
    cf_ref[0, :, S5_MB:] = (-(ct_re * f_im + ct_im * f_re)).astype(BF16)


def _s5_prep(lam_re, lam_im, log_dt, ct_re, ct_im):
    vec = pl.BlockSpec((1, 1, S5_MB), lambda n: (n, 0, 0))
    mat = pl.BlockSpec((1, S5_UB, S5_MB), lambda n: (n, 0, 0))
    return pl.pallas_call(
        _s5_prep_kernel, grid=(S5_NB,),
        in_specs=[vec, vec, vec, mat, mat],
        out_specs=[pl.BlockSpec((1, 8, S5_MB), lambda n: (n, 0, 0)),
                   pl.BlockSpec((1, S5_UB, 2 * S5_MB), lambda n: (n, 0, 0))],
        out_shape=[jax.ShapeDtypeStruct((S5_NB, 8, S5_MB), F32),
                   jax.ShapeDtypeStruct((S5_NB, S5_UB, 2 * S5_MB), BF16)],
        compiler_params=_cparams(("arbitrary",)), name="s5_prep",
    )(lam_re, lam_im, log_dt, ct_re, ct_im)


def _s5_prompt_kernel(u_ref, w_ref, cf_ref, par_ref, y_ref, h_out, x_ref, h_ref, ub_ref, ut_ref, *, batch, tt):
    t_blk = pl.program_id(1)

    @pl.when(t_blk == 0)
    def _():
        h_ref[...] = jnp.zeros_like(h_ref)

    nlb = S5_MB // LANES
    ub_ref[...] = u_ref[...].astype(F32).reshape(batch * tt, S5_UB)

    def interleave(t, carry):
        ut_ref[pl.ds(pl.multiple_of(t * batch, batch), batch), :] = ub_ref[pl.ds(t, batch, stride=tt), :]
        return carry

    lax.fori_loop(0, tt, interleave, 0, unroll=8)
    x = _dot(ut_ref[...].astype(BF16), w_ref[0])
    for j in range(2 * nlb):
        x_ref[j] = x[:, j * LANES:(j + 1) * LANES]

    par = par_ref[0]
    lr = [jnp.broadcast_to(par[0:1, j * LANES:(j + 1) * LANES], (batch, LANES)) for j in range(nlb)]
    li = [jnp.broadcast_to(par[1:2, j * LANES:(j + 1) * LANES], (batch, LANES)) for j in range(nlb)]

    def step(t, carry):
        rows = pl.ds(pl.multiple_of(t * batch, batch), batch)
        new = []
        for j in range(nlb):
            hr, hi = carry[j], carry[nlb + j]
            nr = lr[j] * hr - li[j] * hi + x_ref[j, rows, :]
            ni = lr[j] * hi + li[j] * hr + x_ref[nlb + j, rows, :]
            x_ref[j, rows, :] = nr
            x_ref[nlb + j, rows, :] = ni
            new.append((nr, ni))
        return tuple(n[0] for n in new) + tuple(n[1] for n in new)

    h0 = tuple(h_ref[:, j * LANES:(j + 1) * LANES] for j in range(2 * nlb))
    hfin = lax.fori_loop(0, tt, step, h0, unroll=8)
    for j in range(2 * nlb):
        h_ref[:, j * LANES:(j + 1) * LANES] = hfin[j]
    hr = jnp.concatenate(hfin[:nlb], axis=1)
    hi = jnp.concatenate(hfin[nlb:], axis=1)

    hall = jnp.concatenate([x_ref[j].astype(BF16) for j in range(2 * nlb)], axis=1)
    ut_ref[...] = _dot_nt(hall, cf_ref[0])
    for b in range(batch):
        y_ref[b] = ut_ref[pl.ds(b, tt, stride=batch), :]

    @pl.when(t_blk == pl.num_programs(1) - 1)
    def _():
        fr = par[2:3, :]
        fi = par[3:4, :]
        h_out[0, :, :S5_MB] = fr * hr - fi * hi
        h_out[0, :, S5_MB:] = fr * hi + fi * hr


def _s5_prompt(z3, w_blk, cfold, par, *, batch, seq, tt):
    nt = seq // tt
    u_col0 = Z_SU * 512 // S5_UB
    return pl.pallas_call(
        functools.partial(_s5_prompt_kernel, batch=batch, tt=tt),
        grid=(S5_NB, nt),
        in_specs=[pl.BlockSpec((batch, tt, S5_UB), lambda n, t: (0, t, u_col0 + n)),
                  pl.BlockSpec((1, S5_UB, 2 * S5_MB), lambda n, t: (n, 0, 0)),
                  pl.BlockSpec((1, S5_UB, 2 * S5_MB), lambda n, t: (n, 0, 0)),
                  pl.BlockSpec((1, 8, S5_MB), lambda n, t: (n, 0, 0))],
        out_specs=[pl.BlockSpec((batch, tt, S5_UB), lambda n, t: (0, t, n)),
                   pl.BlockSpec((1, batch, 2 * S5_MB), lambda n, t: (n, 0, 0))],
        out_shape=[jax.ShapeDtypeStruct((batch, seq, S5_WIDTH), F32),
                   jax.ShapeDtypeStruct((S5_NB, batch, 2 * S5_MB), F32)],
        scratch_shapes=[pltpu.VMEM((2 * S5_MB // LANES, batch * tt, LANES), F32),
                        pltpu.VMEM((batch, 2 * S5_MB), F32),
                        pltpu.VMEM((batch * tt, S5_UB), F32), pltpu.VMEM((batch * tt, S5_UB), F32)],
        compiler_params=_cparams(("arbitrary", "arbitrary")), name="s5_prompt",
    )(z3, w_blk, cfold, par)


def _s5_decode_kernel(u_ref, w_ref, ctr_ref, cti_ref, par_ref, hr_ref, hi_ref, y_ref, hro_ref, hio_ref):
    x = _dot(u_ref[...].astype(BF16), w_ref[0])
    xr, xi = x[:, :S5_MB], x[:, S5_MB:]
    par = par_ref[0]
    lr, li, fr, fi = par[0:1, :], par[1:2, :], par[2:3, :], par[3:4, :]
    h0r, h0i = hr_ref[...], hi_ref[...]
    hr = lr * h0r - li * h0i + (fr * xr - fi * xi)
    hi = lr * h0i + li * h0r + (fr * xi + fi * xr)
    hro_ref[...] = hr
    hio_ref[...] = hi
    y_ref[...] = _dot_nt(hr.astype(BF16), ctr_ref[0]) - _dot_nt(hi.astype(BF16), cti_ref[0])


def _s5_decode(z, w_blk, ct_re, ct_im, par, h_re, h_im):
    rows = z.shape[0]
    u_col0 = Z_SU * 512 // S5_UB
    hspec = pl.BlockSpec((rows, S5_MB), lambda n: (0, n))
    mat = pl.BlockSpec((1, S5_UB, S5_MB), lambda n: (n, 0, 0))
    return pl.pallas_call(
        _s5_decode_kernel, grid=(S5_NB,),
        in_specs=[pl.BlockSpec((rows, S5_UB), lambda n: (0, u_col0 + n)),
                  pl.BlockSpec((1, S5_UB, 2 * S5_MB), lambda n: (n, 0, 0)),
                  mat, mat,
                  pl.BlockSpec((1, 8, S5_MB), lambda n: (n, 0, 0)), hspec, hspec],
        out_specs=[pl.BlockSpec((rows, S5_UB), lambda n: (0, n)), hspec, hspec],
        out_shape=[jax.ShapeDtypeStruct((rows, S5_WIDTH), F32),
                   jax.ShapeDtypeStruct((rows, S5_MODES), F32),
                   jax.ShapeDtypeStruct((rows, S5_MODES), F32)],
        compiler_params=_cparams(("arbitrary",)), name="s5_decode",
    )(z, w_blk, ct_re, ct_im, par, h_re, h_im)


def _gla_decode_kernel(qk_ref, v_ref, gr_ref, sm_ref, wa_ref, ba_ref, gn_ref, s_ref,
                       y_ref, so_ref, *, bb):
    eg = jnp.exp(_gla_log_decay(sm_ref[...], wa_ref, ba_ref))
    qk = qk_ref[...]
    q = qk[:, :GLA_QK] * (GLA_DK ** -0.5)
    k = qk[:, GLA_QK:]
    v = v_ref[...]

    def per_row(b):
        row = slice(b, b + 1)
        s_old = s_ref[b].reshape(GLA_QK, GLA_DV)
        v_rows = jnp.concatenate(
            [jnp.broadcast_to(v[row, h * GLA_DV:(h + 1) * GLA_DV], (GLA_DK, GLA_DV)) for h in range(GLA_HEADS)],
            axis=0)
        s_new = _row_to_col(eg[row], GLA_QK) * s_old + _row_to_col(k[row], GLA_QK) * v_rows
        qs = (_row_to_col(q[row], GLA_QK) * s_new).reshape(GLA_HEADS, GLA_DK, GLA_DV)
        o4 = jnp.sum(qs, axis=1)
        o_row = jnp.concatenate([o4[h:h + 1, :] for h in range(GLA_HEADS)], axis=1)
        return s_new.reshape(GLA_HEADS, GLA_DK, GLA_DV), o_row

    res = [per_row(b) for b in range(bb)]
    so_ref[...] = jnp.stack([r[0] for r in res])
    o = jnp.concatenate([r[1] for r in res], axis=0)
    gr = gr_ref[...]
    for h in range(GLA_HEADS):
        vs = slice(h * GLA_DV, (h + 1) * GLA_DV)
        gate = gr[:, vs]
        y_ref[:, vs] = (_head_norm(o[:, vs], gn_ref[:, vs]) * (gate * _sigmoid(gate))).astype(BF16)


def _skip_ref(kernel, pos):
    def wrapped(*refs):
        return kernel(*refs[:pos], *refs[pos + 1:])
    return wrapped


def _layer_out(kernel, in_specs, operands, out_specs, out_shapes, stacked_idx, prev, name, grid):
    aliases = {}
    if prev is not None:
        in_specs = in_specs + [pl.BlockSpec(memory_space=pl.ANY)]
        operands = operands + (prev,)
        aliases = {len(operands) - 1: stacked_idx}
        kernel = _skip_ref(kernel, len(operands) - 1)
    return pl.pallas_call(
        kernel, grid=grid, in_specs=in_specs, out_specs=out_specs, out_shape=out_shapes,
        input_output_aliases=aliases, compiler_params=_cparams(("arbitrary",)), name=name,
    )(*operands)


def _gla_decode(z, zs, wa, ba, gn, states, prev, *, layer, bb):
    rows = z.shape[0]
    blk = lambda j: pl.BlockSpec((bb, 512), lambda i: (i, j))
    const = lambda shape: pl.BlockSpec(shape, lambda i: (0,) * len(shape))
    sspec = pl.BlockSpec((None, bb, GLA_HEADS, GLA_DK, GLA_DV), lambda i: (layer, i, 0, 0, 0))
    return _layer_out(
        functools.partial(_gla_decode_kernel, bb=bb),
        [blk(Z_QK), blk(Z_GV), blk(Z_GR), pl.BlockSpec((bb, LANES), lambda i: (i, 0)),
         const((LANES, GLA_QK)), const((1, GLA_QK)), const((1, GLA_V)), sspec],
        (z, z, z, zs, wa, ba, gn, states),
        [pl.BlockSpec((bb, GLA_V), lambda i: (i, 0)), sspec],
        [jax.ShapeDtypeStruct((rows, GLA_V), BF16), jax.ShapeDtypeStruct(states.shape, F32)],
        1, prev, "gla_decode", (rows // bb,))


def _mlstm_decode_kernel(q_ref, k_ref, v_ref, og_ref, sm_ref, gb_ref, gn_ref, c_ref, n_ref, m_ref,
                         y_ref, co_ref, no_ref, mo_ref, *, bb):
    li_blk = sm_ref[...] + gb_ref[...]
    lf_blk = _log_sigmoid(li_blk)
    m_blk = m_ref[...]
    q_blk = q_ref[...] * (ML_D ** -0.5)
    k_blk = k_ref[...]
    v_blk = v_ref[...]
    lane = lax.broadcasted_iota(jnp.int32, (1, LANES), 1)

    def per_row(b):
        row = slice(b, b + 1)
        li_r, lf_r, m_r = li_blk[row], lf_blk[row], m_blk[row]
        q_all, k_all, v_all = q_blk[row], k_blk[row], v_blk[row]
        n_all = n_ref[b]
        m_new_row = jnp.zeros((1, LANES), F32)
        h_heads, n_heads, c_heads = [], [], []
        for h in range(ML_HEADS):
            sl = slice(h * ML_D, (h + 1) * ML_D)
            li = li_r[:, SM_MI + h:SM_MI + h + 1]
            lf = lf_r[:, SM_MF + h:SM_MF + h + 1]
            m_prev = m_r[:, h:h + 1]
            m_new = jnp.maximum(lf + m_prev, li)
            a = jnp.exp(lf + m_prev - m_new)
            e = jnp.exp(li - m_new)
            k_r = k_all[:, sl]
            q_r = q_all[:, sl]
            c_new = a * c_ref[b, h] + (_row_to_col(k_r, ML_D) * e) * v_all[:, sl]
            n_new = a * n_all[h:h + 1, :] + e * k_r
            c_heads.append(c_new)
            n_heads.append(n_new)
            num = jnp.sum(_row_to_col(q_r, ML_D) * c_new, axis=0, keepdims=True)
            den = jnp.sum(q_r * n_new, axis=1, keepdims=True)
            h_heads.append(num / jnp.maximum(jnp.abs(den), jnp.exp(-m_new)))
            m_new_row = jnp.where(lane == h, m_new, m_new_row)
        return (jnp.stack(c_heads), jnp.concatenate(n_heads, axis=0),
                jnp.concatenate(h_heads, axis=1), m_new_row)

    res = [per_row(b) for b in range(bb)]
    co_ref[...] = jnp.stack([r[0] for r in res])
    no_ref[...] = jnp.stack([r[1] for r in res])
    mo_ref[...] = jnp.concatenate([r[3] for r in res], axis=0)
    hh = jnp.concatenate([r[2] for r in res], axis=0)
    og = og_ref[...]
    for h in range(ML_HEADS):
        sl = slice(h * ML_D, (h + 1) * ML_D)
        y_ref[:, sl] = (_head_norm(hh[:, sl], gn_ref[:, sl]) * _sigmoid(og[:, sl])).astype(BF16)


def _mlstm_decode(z, zs, gbias, gn, c_all, n_all, m_pad, prev, *, layer, bb):
    rows = z.shape[0]
    blk = lambda j: pl.BlockSpec((bb, 512), lambda i: (i, j))
    const = lambda shape: pl.BlockSpec(shape, lambda i: (0,) * len(shape))
    nblock = (bb, ML_HEADS, ML_D)
    cspec = pl.BlockSpec((None, bb, ML_HEADS, ML_D, ML_D), lambda i: (layer, i, 0, 0, 0))
    mspec = pl.BlockSpec((bb, LANES), lambda i: (i, 0))
    return _layer_out(
        functools.partial(_mlstm_decode_kernel, bb=bb),
        [blk(Z_MQ), blk(Z_MK), blk(Z_MV), blk(Z_MO), mspec, const((1, LANES)), const((1, ML_W)),
         cspec, pl.BlockSpec((None,) + nblock, lambda i: (layer, i, 0, 0)), mspec],
        (z, z, z, z, zs, gbias, gn, c_all, n_all, m_pad),
        [pl.BlockSpec((bb, ML_W), lambda i: (i, 0)), cspec, pl.BlockSpec(nblock, lambda i: (i, 0, 0)), mspec],
        [jax.ShapeDtypeStruct((rows, ML_W), BF16), jax.ShapeDtypeStruct(c_all.shape, F32),
         jax.ShapeDtypeStruct(n_all.shape[1:], F32), jax.ShapeDtypeStruct((rows, LANES), F32)],
        1, prev, "mlstm_decode", (rows // bb,))


def _merged_residual(ya_ref, yb_ref, yp_ref, u_ref, za_ref, zb_ref, zc_ref, x_ref,
                     d_ref, wg_ref, wa_ref, wb_ref, wc_ref, wo_ref):
    yc = _gelu_tanh(yp_ref[...] + d_ref[...] * u_ref[...].astype(F32))
    yc = yc * _sigmoid(_dot(yc.astype(BF16), wg_ref[...]))
    m = _sigmoid(za_ref[...].astype(F32)) * _dot(ya_ref[...], wa_ref[...])
    m = m + _sigmoid(zb_ref[...].astype(F32)) * _dot(yb_ref[...], wb_ref[...])
    m = m + _sigmoid(zc_ref[...].astype(F32)) * _dot(yc.astype(BF16), wc_ref[...])
    return x_ref[...] + _dot(m.astype(BF16), wo_ref[...])


def _merge_kernel(*refs):
    refs[-1][...] = _merged_residual(*refs[:-1])


def _merge_specs(bm, row_map, const):
    r512 = pl.BlockSpec((bm, 512), lambda *g: (row_map(*g), 0))
    zblk = lambda j: pl.BlockSpec((bm, D_MODEL), lambda *g: (row_map(*g), j))
    return [r512, r512, r512, pl.BlockSpec((bm, 512), lambda *g: (row_map(*g), Z_SU)),
            zblk(Z_ZA), zblk(Z_ZB), zblk(Z_ZC), pl.BlockSpec((bm, D_MODEL), lambda *g: (row_map(*g), 0)),
            const((1, S5_WIDTH)), const((S5_WIDTH, S5_WIDTH)),
            const((GLA_V, D_MODEL)), const((ML_W, D_MODEL)), const((S5_WIDTH, D_MODEL)),
            const((D_MODEL, D_MODEL))]


def _merge(ya, yb, ypre, z, x, d, wg, wa, wb, wc, wo, *, bm):
    n = x.shape[0]
    const = lambda shape: pl.BlockSpec(shape, lambda i: (0, 0))
    return pl.pallas_call(
        _merge_kernel, grid=(n // bm,),
        in_specs=_merge_specs(bm, lambda i: i, const),
        out_specs=pl.BlockSpec((bm, D_MODEL), lambda i: (i, 0)),
        out_shape=jax.ShapeDtypeStruct((n, D_MODEL), F32),
        compiler_params=_cparams(("arbitrary",)), name="merge",
    )(ya, yb, ypre, z, z, z, z, x, d, wg, wa, wb, wc, wo)


def _cross_attend(x, g_ref, wq_ref, mk_ref, mv_ref, wo_ref):
    q = _dot(_rms_rows(x, g_ref[...]).astype(BF16), wq_ref[...])
    heads = []
    for h in range(X_HEADS):
        sl = slice(h * X_DH, (h + 1) * X_DH)
        s = _dot_nt(q[:, sl].astype(BF16), mk_ref[:, sl].astype(BF16)) * (X_DH ** -0.5)
        s = s - jnp.max(s, axis=-1, keepdims=True)
        p = jnp.exp(s)
        p = p / jnp.sum(p, axis=-1, keepdims=True)
        heads.append(_dot(p.astype(BF16), mv_ref[:, sl].astype(BF16)))
    o = jnp.concatenate(heads, axis=-1)
    return x + _dot(o.astype(BF16), wo_ref[...])


N_MERGE_IN = 14


def _merge_cross_kernel(*refs):
    x1 = _merged_residual(*refs[:N_MERGE_IN])
    refs[-1][...] = _cross_attend(x1, *refs[N_MERGE_IN:-1])


def _merge_cross(ya, yb, ypre, z, x, d, wg, wa, wb, wc, wo, g, wq, mem_k, mem_v, wco,
                 *, layer, batch, seq, bm):
    nt = seq // bm
    const = lambda shape: pl.BlockSpec(shape, lambda b, t: (0, 0), pipeline_mode=pl.Buffered(1))
    mspec = pl.BlockSpec((None, MEM_LEN, D_MODEL), lambda b, t: (layer, b, 0))
    xspec = pl.BlockSpec((bm, D_MODEL), lambda b, t: (b * nt + t, 0))
    return pl.pallas_call(
        _merge_cross_kernel, grid=(batch, nt),
        in_specs=_merge_specs(bm, lambda b, t: b * nt + t, const)
        + [const((1, D_MODEL)), const((D_MODEL, D_MODEL)), mspec, mspec, const((D_MODEL, D_MODEL))],
        out_specs=xspec, out_shape=jax.ShapeDtypeStruct(x.shape, F32),
        compiler_params=_cparams(("arbitrary", "arbitrary")), name="merge_cross",
    )(ya, yb, ypre, z, z, z, z, x, d, wg, wa, wb, wc, wo, g, wq, mem_k, mem_v, wco)


def _memory_kv_kernel(x_ref, g_ref, wk_ref, wv_ref, k_ref, v_ref, hn_ref):
    @pl.when(pl.program_id(1) == 0)
    def _():
        hn_ref[...] = _rms_rows(x_ref[...], g_ref[...]).astype(BF16)
    k_ref[...] = _dot(hn_ref[...], wk_ref[...])
    v_ref[...] = _dot(hn_ref[...], wv_ref[...])


def _memory_kv(mem, g, wk, wv, *, bn=512):
    rows, d = mem.shape
    depth = wk.shape[0]
    wspec = pl.BlockSpec((None, d, bn), lambda l, j: (l, 0, j))
    ospec = pl.BlockSpec((None, rows, bn), lambda l, j: (l, 0, j))
    out = jax.ShapeDtypeStruct((depth, rows, d), F32)
    return pl.pallas_call(
        _memory_kv_kernel, grid=(depth, d // bn),
        in_specs=[pl.BlockSpec((rows, d), lambda l, j: (0, 0)),
                  pl.BlockSpec((None, 1, d), lambda l, j: (l, 0, 0)), wspec, wspec],
        out_specs=[ospec, ospec], out_shape=[out, out],
        scratch_shapes=[pltpu.VMEM((rows, d), BF16)],
        compiler_params=_cparams(("arbitrary", "arbitrary")), name="memory_kv",
    )(mem, g, wk, wv)


def _cross_decode_kernel(x_ref, g_ref, wq_ref, mk_ref, mv_ref, wo_ref, o_ref, *, bb):
    x = x_ref[...]
    q = _dot(_rms_rows(x, g_ref[...]).astype(BF16), wq_ref[...])
    rows = []
    for b in range(bb):
        q4 = jnp.concatenate([q[b:b + 1, h * X_DH:(h + 1) * X_DH] for h in range(X_HEADS)], axis=0)
        s = jnp.sum(mk_ref[0, b] * q4[None], axis=-1, keepdims=True) * (X_DH ** -0.5)
        p = jnp.exp(s - jnp.max(s, axis=0, keepdims=True))
        p = p / jnp.sum(p, axis=0, keepdims=True)
        o4 = jnp.sum(p * mv_ref[0, b], axis=0)
        rows.append(jnp.concatenate([o4[h:h + 1, :] for h in range(X_HEADS)], axis=1))
    att = jnp.concatenate(rows, axis=0)
    o_ref[...] = x + _dot(att.astype(BF16), wo_ref[...])


def _cross_decode(x, g, wq, cache_k, cache_v, wo, *, layer, bb):
    rows = x.shape[0]
    xspec = pl.BlockSpec((bb, D_MODEL), lambda i: (i, 0))
    const = lambda shape: pl.BlockSpec(shape, lambda i: (0, 0))
    mspec = pl.BlockSpec((1, bb, MEM_LEN, X_HEADS, X_DH), lambda i: (layer, i, 0, 0, 0))
    return pl.pallas_call(
        functools.partial(_cross_decode_kernel, bb=bb), grid=(rows // bb,),
        in_specs=[xspec, const((1, D_MODEL)), const((D_MODEL, D_MODEL)), mspec, mspec,
                  const((D_MODEL, D_MODEL))],
        out_specs=xspec, out_shape=jax.ShapeDtypeStruct(x.shape, F32),
        compiler_params=_cparams(("arbitrary",)), name="cross_decode",
    )(x, g, wq, cache_k, cache_v, wo)


def _ffn_prompt_kernel(x_ref, g_ref, wup_ref, cw_ref, cb_ref, wdn_ref, gf_ref,
                       o_ref, st_ref, hn_ref, halo_ref, up_ref, *, tt, final_norm):
    t = pl.program_id(1)

    @pl.when(t == 0)
    def _():
        halo_ref[...] = jnp.zeros_like(halo_ref)

    x = x_ref[...]
    hn_ref[...] = _rms_rows(x, g_ref[...]).astype(BF16)
    def project(c):
        slot = c % 2
        for part, col0 in enumerate((c * FF_CHUNK, D_FF + c * FF_CHUNK)):
            cols = slice(col0, col0 + FF_CHUNK)
            lanes = slice(part * FF_CHUNK, (part + 1) * FF_CHUNK)
            up = _dot(hn_ref[...], wup_ref[:, cols])
            up_ref[slot, 6:8, lanes] = halo_ref[0:2, cols]
            up_ref[slot, 8:8 + tt, lanes] = up
            halo_ref[0:2, cols] = up[tt - 2:tt, :]

    def gated(c):
        slot = c % 2
        a_cols = slice(c * FF_CHUNK, (c + 1) * FF_CHUNK)
        g_cols = slice(D_FF + c * FF_CHUNK, D_FF + (c + 1) * FF_CHUNK)
        cw = jnp.concatenate([cw_ref[:, a_cols], cw_ref[:, g_cols]], axis=1)
        cb = jnp.concatenate([cb_ref[:, a_cols], cb_ref[:, g_cols]], axis=1)
        cv = (up_ref[slot, 6:6 + tt, :] * cw[0:1] + up_ref[slot, 7:7 + tt, :] * cw[1:2]
              + up_ref[slot, 8:8 + tt, :] * cw[2:3] + cb)
        gt = cv[:, FF_CHUNK:]
        return (cv[:, :FF_CHUNK] * (gt * _sigmoid(gt))).astype(BF16)

    acc = x
    project(0)
    for c in range(N_FF_CHUNKS):
        if c + 1 < N_FF_CHUNKS:
            project(c + 1)
        acc = acc + _dot(gated(c), wdn_ref[c * FF_CHUNK:(c + 1) * FF_CHUNK, :])
    if final_norm:
        acc = _rms_rows(acc, gf_ref[...])
    o_ref[...] = acc

    @pl.when(t == pl.num_programs(1) - 1)
    def _():
        st_ref[0] = halo_ref[0:2, :]


def _ffn_prompt(x, g, wup, cw, cb, wdn, gf, *, batch, seq, tt, final_norm):
    nt = seq // tt
    xspec = pl.BlockSpec((tt, D_MODEL), lambda b, t: (b * nt + t, 0))
    const = lambda shape: pl.BlockSpec(shape, lambda b, t: (0, 0), pipeline_mode=pl.Buffered(1))
    return pl.pallas_call(
        functools.partial(_ffn_prompt_kernel, tt=tt, final_norm=final_norm), grid=(batch, nt),
        in_specs=[xspec, const((1, D_MODEL)), const((D_MODEL, 2 * D_FF)), const((CONV_W, 2 * D_FF)),
                  const((1, 2 * D_FF)), const((D_FF, D_MODEL)), const((1, D_MODEL))],
        out_specs=[xspec, pl.BlockSpec((1, CONV_W - 1, 2 * D_FF), lambda b, t: (b, 0, 0))],
        out_shape=[jax.ShapeDtypeStruct(x.shape, F32),
                   jax.ShapeDtypeStruct((batch, CONV_W - 1, 2 * D_FF), F32)],
        scratch_shapes=[pltpu.VMEM((tt, D_MODEL), BF16), pltpu.VMEM((8, 2 * D_FF), F32),
                        pltpu.VMEM((2, tt + 8, 2 * FF_CHUNK), F32)],
        compiler_params=_cparams(("arbitrary", "arbitrary")), name="ffn_prompt",
    )(x, g, wup, cw, cb, wdn, gf)


def _ffn_decode_kernel(x_ref, g_ref, wa_ref, wg_ref, cwa_ref, cwg_ref, cba_ref, cbg_ref,
                       b0a_ref, b0g_ref, b1a_ref, b1g_ref, wdn_ref, gf_ref,
                       o_ref, upa_ref, upg_ref, hn_ref, acc_ref, *, final_norm):
    c = pl.program_id(0)

    @pl.when(c == 0)
    def _():
        x = x_ref[...]
        hn_ref[...] = _rms_rows(x, g_ref[...]).astype(BF16)
        acc_ref[...] = x

    def conv(w_ref, cw_ref, cb_ref, b0_ref, b1_ref, up_ref):
        up = _dot(hn_ref[...], w_ref[...])
        up_ref[...] = up
        return b0_ref[...] * cw_ref[0:1, :] + b1_ref[...] * cw_ref[1:2, :] + up * cw_ref[2:3, :] + cb_ref[...]

    a = conv(wa_ref, cwa_ref, cba_ref, b0a_ref, b1a_ref, upa_ref)
    gt = conv(wg_ref, cwg_ref, cbg_ref, b0g_ref, b1g_ref, upg_ref)
    act = a * (gt * _sigmoid(gt))
    acc_ref[...] += _dot(act.astype(BF16), wdn_ref[...])

    @pl.when(c == pl.num_programs(0) - 1)
    def _():
        acc = acc_ref[...]
        if final_norm:
            acc = _rms_rows(acc, gf_ref[...])
        o_ref[...] = acc


def _ffn_decode(x, g, wup, cw, cb, wdn, gf, buf0, buf1, *, final_norm):
    rows = x.shape[0]
    fc = FF_CHUNK
    full = lambda shape: pl.BlockSpec(shape, lambda c: (0, 0))
    a_col = lambda r: pl.BlockSpec((r, fc), lambda c: (0, c))
    g_col = lambda r: pl.BlockSpec((r, fc), lambda c: (0, N_FF_CHUNKS + c))
    return pl.pallas_call(
        functools.partial(_ffn_decode_kernel, final_norm=final_norm), grid=(N_FF_CHUNKS,),
        in_specs=[full((rows, D_MODEL)), full((1, D_MODEL)),
                  a_col(D_MODEL), g_col(D_MODEL), a_col(CONV_W), g_col(CONV_W), a_col(1), g_col(1),
                  a_col(rows), g_col(rows), a_col(rows), g_col(rows),
                  pl.BlockSpec((fc, D_MODEL), lambda c: (c, 0)), full((1, D_MODEL))],
        out_specs=[full((rows, D_MODEL)), a_col(rows), a_col(rows)],
        out_shape=[jax.ShapeDtypeStruct(x.shape, F32),
                   jax.ShapeDtypeStruct((rows, D_FF), F32),
                   jax.ShapeDtypeStruct((rows, D_FF), F32)],
        scratch_shapes=[pltpu.VMEM((rows, D_MODEL), BF16), pltpu.VMEM((rows, D_MODEL), F32)],
        compiler_params=_cparams(("arbitrary",)), name="ffn_decode",
    )(x, g, wup, wup, cw, cw, cb, cb, buf0, buf0, buf1, buf1, wdn, gf)


def _layer_weights(P, l):
    w_in = P['w_in'][l]
    w_main = jnp.concatenate([w_in[:, 0:1536], w_in[:, 1552:3600], w_in[:, 3608:7192]], axis=1).astype(BF16)
    w_small = jnp.concatenate([w_in[:, 1536:1552], w_in[:, 3600:3608],
                               jnp.zeros((D_MODEL, LANES - GLA_LOWRANK - 2 * ML_HEADS), F32)], axis=1).astype(BF16)
    wa = jnp.zeros((LANES, GLA_QK), F32).at[:GLA_LOWRANK].set(P['w_gla_alpha'][l]).astype(BF16)
    gbias = jnp.zeros((1, LANES), F32)
    gbias = gbias.at[0, SM_MI:SM_MI + ML_HEADS].set(P['b_mlstm_i'][l])
    gbias = gbias.at[0, SM_MF:SM_MF + ML_HEADS].set(P['b_mlstm_f'][l])
    eye = jnp.eye(S5_GB, dtype=F32)
    w_re = jnp.einsum('ngpc,gh->ngchp', P['s5_b_re'][l].reshape(S5_NB, S5_GB, S5_P, S5_GROUP), eye)
    w_im = jnp.einsum('ngpc,gh->ngchp', P['s5_b_im'][l].reshape(S5_NB, S5_GB, S5_P, S5_GROUP), eye)
    w_blk = jnp.concatenate([w_re.reshape(S5_NB, S5_UB, S5_MB), w_im.reshape(S5_NB, S5_UB, S5_MB)],
                            axis=-1).astype(BF16)
    ct_re = jnp.einsum('ngcp,gh->ngchp', P['s5_c_re'][l].reshape(S5_NB, S5_GB, S5_GROUP, S5_P), eye)
    ct_im = jnp.einsum('ngcp,gh->ngchp', P['s5_c_im'][l].reshape(S5_NB, S5_GB, S5_GROUP, S5_P), eye)
    ct_re = ct_re.reshape(S5_NB, S5_UB, S5_MB)
    ct_im = ct_im.reshape(S5_NB, S5_UB, S5_MB)
    lam_re = P['s5_lam_re'][l].reshape(S5_NB, 1, S5_MB)
    lam_im = P['s5_lam_im'][l].reshape(S5_NB, 1, S5_MB)
    log_dt = jnp.broadcast_to(P['s5_log_dt'][l][:, None], (S5_GROUPS, S5_P)).reshape(S5_NB, 1, S5_MB)
    par, cfold = _s5_prep(lam_re, lam_im, log_dt, ct_re, ct_im)
    row = lambda a: a.reshape(1, -1)
    return dict(
        norm_mix=row(P['norm_mix'][l]), w_main=w_main, w_small=w_small, wa=wa,
        ba=row(P['b_gla_alpha'][l]), gla_norm=row(P['gla_head_norm'][l]), gbias=gbias,
        ml_norm=row(P['mlstm_head_norm'][l]), w_blk=w_blk, ct_re=ct_re.astype(BF16),
        ct_im=ct_im.astype(BF16), par=par, cfold=cfold, s5_d=row(P['s5_d'][l]),
        w_glu=P['s5_w_glu'][l].astype(BF16), w_a=P['w_branch_a'][l].astype(BF16),
        w_b=P['w_branch_b'][l].astype(BF16), w_c=P['w_branch_c'][l].astype(BF16),
        w_out=P['w_out'][l].astype(BF16), norm_cross=row(P['norm_cross'][l]),
        w_cq=P['w_cq'][l].astype(BF16), w_co=P['w_co'][l].astype(BF16),
        norm_ffn=row(P['norm_ffn'][l]), w_up=P['w_ffn_up'][l].astype(BF16),
        conv_w=P['ffn_conv_w'][l], conv_b=row(P['ffn_conv_b'][l]),
        w_down=P['w_ffn_down'][l].astype(BF16), norm_final=row(P['norm_final']),
    )


def _prompt_trunk(x_prompt, mem_k, mem_v, W, *, bm_in, tt_mix, gla_group, ml_group, tt_s5, bm, tt_ffn):
    batch, seq, _ = x_prompt.shape
    depth = len(W)
    x = x_prompt.reshape(batch * seq, D_MODEL)
    outs = []
    for l, w in enumerate(W):
        z, zs = _norm_matmul(x, w['norm_mix'], w['w_main'], w['w_small'], bm=bm_in, bn=1024, out_dtype=BF16)
        z3 = z.reshape(batch, seq, Z_MAIN)
        zs3 = zs.reshape(batch, seq, LANES)
        ya, st = _gla_prompt(z3, zs3, w['wa'], w['ba'], w['gla_norm'], tt=tt_mix, group=gla_group)
        yb, c, n, m = _mlstm_prompt(z3, zs3, w['gbias'], w['ml_norm'], tt=tt_mix, group=ml_group)
        ypre, hfin = _s5_prompt(z3, w['w_blk'], w['cfold'], w['par'], batch=batch, seq=seq, tt=tt_s5)
        x = _merge_cross(ya.reshape(batch * seq, GLA_V), yb.reshape(batch * seq, ML_W),
                         ypre.reshape(batch * seq, S5_WIDTH), z, x, w['s5_d'], w['w_glu'],
                         w['w_a'], w['w_b'], w['w_c'], w['w_out'],
                         w['norm_cross'], w['w_cq'], mem_k, mem_v, w['w_co'],
                         layer=l, batch=batch, seq=seq, bm=bm)
        x, conv = _ffn_prompt(x, w['norm_ffn'], w['w_up'], w['conv_w'], w['conv_b'], w['w_down'],
                              w['norm_final'], batch=batch, seq=seq, tt=tt_ffn,
                              final_norm=(l == depth - 1))
        st4 = st.reshape(batch, GLA_HEADS, GLA_DV, GLA_HEADS, GLA_DK)
        gla = jnp.stack([st4[:, h, :, h, :] for h in range(GLA_HEADS)], axis=1).transpose(0, 1, 3, 2)
        h4 = hfin.reshape(S5_NB, batch, 2, S5_GB, S5_P)
        s5_re = h4[:, :, 0].transpose(1, 0, 2, 3).reshape(batch, S5_GROUPS, S5_P)
        s5_im = h4[:, :, 1].transpose(1, 0, 2, 3).reshape(batch, S5_GROUPS, S5_P)
        outs.append((gla, c.reshape(batch, ML_HEADS, ML_D, ML_D), n[:, :ML_HEADS, :],
                     m[:, :ML_HEADS, 0], s5_re, s5_im, conv))
    stacked = [jnp.stack([outs[l][i] for l in range(depth)]) for i in range(7)]
    return x.reshape(batch, seq, D_MODEL), stacked


def _sample_trunk(x_sample, cache_k, cache_v, states, W):
    rows = x_sample.shape[0]
    depth = len(W)
    x = x_sample.reshape(rows, D_MODEL)
    s_gla, s_c, s_n, s_m, s_re, s_im, s_conv = states
    outs = []
    gla_all = c_all = None
    for l, w in enumerate(W):
        z, zs = _norm_matmul(x, w['norm_mix'], w['w_main'], w['w_small'], bm=rows, bn=1024)
        ya, gla_all = _gla_decode(z, zs, w['wa'], w['ba'], w['gla_norm'], s_gla, gla_all, layer=l, bb=16)
        m_pad = jnp.pad(s_m[l], ((0, 0), (0, LANES - ML_HEADS)))
        yb, c_all, n, m = _mlstm_decode(z, zs, w['gbias'], w['ml_norm'], s_c, s_n, m_pad, c_all,
                                        layer=l, bb=16)
        ypre, h_re, h_im = _s5_decode(z, w['w_blk'], w['ct_re'], w['ct_im'], w['par'],
                                      s_re[l].reshape(rows, S5_MODES), s_im[l].reshape(rows, S5_MODES))
        x = _merge(ya, yb, ypre, z, x, w['s5_d'], w['w_glu'], w['w_a'], w['w_b'], w['w_c'], w['w_out'],
                   bm=rows)
        x = _cross_decode(x, w['norm_cross'], w['w_cq'], cache_k, cache_v, w['w_co'], layer=l, bb=8)
        x, up_a, up_g = _ffn_decode(x, w['norm_ffn'], w['w_up'], w['conv_w'], w['conv_b'], w['w_down'],
                                    w['norm_final'], s_conv[l][:, 0, :], s_conv[l][:, 1, :],
                                    final_norm=(l == depth - 1))
        conv = jnp.stack([s_conv[l][:, 1, :], jnp.concatenate([up_a, up_g], axis=1)], axis=1)
        outs.append((n, m[:, :ML_HEADS], h_re.reshape(rows, S5_GROUPS, S5_P),
                     h_im.reshape(rows, S5_GROUPS, S5_P), conv))
    stacked = [jnp.stack([outs[l][i] for l in range(depth)]) for i in range(5)]
    return x.reshape(rows, 1, D_MODEL), [gla_all, c_all] + stacked


def kernel(x_prompt, x_sample, mem_prompt, cache_mem_k, cache_mem_v, state_gla, state_mlstm_c, state_mlstm_n, state_mlstm_m, state_s5_re, state_s5_im, state_ffn_conv, norm_mix, w_in, w_gla_alpha, b_gla_alpha, gla_head_norm, b_mlstm_i, b_mlstm_f, mlstm_head_norm, s5_lam_re, s5_lam_im, s5_log_dt, s5_b_re, s5_b_im, s5_c_re, s5_c_im, s5_d, s5_w_glu, w_branch_a, w_branch_b, w_branch_c, w_out, norm_cross, norm_mem, w_cq, w_ck, w_cv, w_co, norm_ffn, w_ffn_up, ffn_conv_w, ffn_conv_b, w_ffn_down, norm_final):
    P = dict(norm_mix=norm_mix, w_in=w_in, w_gla_alpha=w_gla_alpha, b_gla_alpha=b_gla_alpha,
             gla_head_norm=gla_head_norm, b_mlstm_i=b_mlstm_i, b_mlstm_f=b_mlstm_f,
             mlstm_head_norm=mlstm_head_norm, s5_lam_re=s5_lam_re, s5_lam_im=s5_lam_im,
             s5_log_dt=s5_log_dt, s5_b_re=s5_b_re, s5_b_im=s5_b_im, s5_c_re=s5_c_re,
             s5_c_im=s5_c_im, s5_d=s5_d, s5_w_glu=s5_w_glu, w_branch_a=w_branch_a,
             w_branch_b=w_branch_b, w_branch_c=w_branch_c, w_out=w_out, norm_cross=norm_cross,
             w_cq=w_cq, w_co=w_co, norm_ffn=norm_ffn, w_ffn_up=w_ffn_up, ffn_conv_w=ffn_conv_w,
             ffn_conv_b=ffn_conv_b, w_ffn_down=w_ffn_down, norm_final=norm_final)
    depth = w_in.shape[0]
    W = [_layer_weights(P, l) for l in range(depth)]
    batch, mem_len, _ = mem_prompt.shape
    mem2 = mem_prompt.reshape(batch * mem_len, D_MODEL)
    mem_k, mem_v = _memory_kv(mem2, norm_mem.reshape(depth, 1, D_MODEL), w_ck.astype(BF16), w_cv.astype(BF16))
    y_prompt, p_states = _prompt_trunk(x_prompt, mem_k, mem_v, W, bm_in=2048, tt_mix=256, gla_group=4, ml_group=4, tt_s5=256, bm=512,
                                       tt_ffn=512)
    p_mem_k = mem_k.reshape(depth, batch, mem_len, X_HEADS, X_DH)
    p_mem_v = mem_v.reshape(depth, batch, mem_len, X_HEADS, X_DH)
    y_sample, s_states = _sample_trunk(
        x_sample, cache_mem_k, cache_mem_v,
        (state_gla, state_mlstm_c, state_mlstm_n, state_mlstm_m, state_s5_re, state_s5_im, state_ffn_conv), W)
    return (y_prompt, y_sample, *p_states, p_mem_k, p_mem_v, *s_states)
```

```python
import functools

import jax
import jax.numpy as jnp
from jax import lax
from jax.experimental import pallas as pl
from jax.experimental.pallas import tpu as pltpu

F32 = jnp.float32
BF16 = jnp.bfloat16

D_MODEL = 1024
GLA_HEADS, GLA_DK, GLA_DV = 4, 64, 128
GLA_QK = GLA_HEADS * GLA_DK
GLA_V = GLA_HEADS * GLA_DV
GLA_LOWRANK = 16
GLA_TAU = 16.0
ML_HEADS, ML_D = 4, 128
ML_W = ML_HEADS * ML_D
S5_GROUP, S5_GROUPS, S5_P = 16, 32, 64
S5_WIDTH = S5_GROUP * S5_GROUPS
S5_MODES = S5_GROUPS * S5_P
S5_NB = 4
S5_GB = S5_GROUPS // S5_NB
S5_MB = S5_MODES // S5_NB
S5_UB = S5_WIDTH // S5_NB
MEM_LEN = 256
X_HEADS = 4
X_DH = D_MODEL // X_HEADS
D_FF = 2816
CONV_W = 3
EPS = 1e-6

CHUNK = 256
LANES = 128
FF_CHUNK = 256
N_FF_CHUNKS = D_FF // FF_CHUNK

Z_MAIN = 7168
Z_QK, Z_GV, Z_GR, Z_MQ, Z_MK, Z_MV, Z_MO, Z_SU = range(8)
Z_ZA, Z_ZB, Z_ZC = 4, 5, 6
SM_MI = GLA_LOWRANK
SM_MF = GLA_LOWRANK + ML_HEADS

GLA_SAFE_DECAY = 80.0

VMEM_LIMIT = 56 * 1024 * 1024


def _cparams(sem):
    return pltpu.CompilerParams(dimension_semantics=sem, vmem_limit_bytes=VMEM_LIMIT)


def _dot(a, b):
    return jnp.dot(a, b, preferred_element_type=F32)


def _dot_nt(a, b):
    return lax.dot_general(a, b, (((1,), (1,)), ((), ())), preferred_element_type=F32)


def _dot_tn(a, b):
    return lax.dot_general(a, b, (((0,), (0,)), ((), ())), preferred_element_type=F32)


def _sigmoid(x):
    return 1.0 / (1.0 + jnp.exp(-x))


def _log_sigmoid(x):
    return jnp.minimum(x, 0.0) - jnp.log(1.0 + jnp.exp(-jnp.abs(x)))


def _gelu_tanh(x):
    return 0.5 * x * (1.0 + jnp.tanh(0.7978845608028654 * (x + 0.044715 * x * x * x)))


def _rms_rows(x, g):
    r = lax.rsqrt(jnp.mean(x * x, axis=-1, keepdims=True) + EPS)
    return (x * r) * g


def _dot_exact01(a01, x):
    hi = x.astype(BF16)
    r1 = x - hi.astype(F32)
    mid = r1.astype(BF16)
    lo = (r1 - mid.astype(F32)).astype(BF16)
    return _dot(a01, hi) + _dot(a01, mid) + _dot(a01, lo)


def _dot_exact01_rhs(x, b01):
    hi = x.astype(BF16)
    r1 = x - hi.astype(F32)
    mid = r1.astype(BF16)
    lo = (r1 - mid.astype(F32)).astype(BF16)
    return _dot(hi, b01) + _dot(mid, b01) + _dot(lo, b01)


def _row_to_col(row, n):
    eye = (lax.broadcasted_iota(jnp.int32, (n, n), 0) == lax.broadcasted_iota(jnp.int32, (n, n), 1))
    return jnp.sum(jnp.where(eye, jnp.broadcast_to(row, (n, n)), 0.0), axis=1, keepdims=True)


def _norm_mm_small_kernel(x_ref, g_ref, w_ref, ws_ref, o_ref, os_ref, hn_ref):
    @pl.when(pl.program_id(1) == 0)
    def _():
        hn = _rms_rows(x_ref[...], g_ref[...]).astype(BF16)
        hn_ref[...] = hn
        os_ref[...] = _dot(hn, ws_ref[...])
    o_ref[...] = _dot(hn_ref[...], w_ref[...]).astype(o_ref.dtype)


def _norm_matmul(x, g, w, w_small, *, bm, bn, out_dtype=F32):
    n, d = x.shape
    c = w.shape[1]
    grid = (n // bm, c // bn)
    x_spec = pl.BlockSpec((bm, d), lambda i, j: (i, 0))
    g_spec = pl.BlockSpec((1, d), lambda i, j: (0, 0))
    w_spec = pl.BlockSpec((d, bn), lambda i, j: (0, j))
    o_spec = pl.BlockSpec((bm, bn), lambda i, j: (i, j))
    scratch = [pltpu.VMEM((bm, d), BF16)]
    cs = w_small.shape[1]
    return pl.pallas_call(
        _norm_mm_small_kernel, grid=grid,
        in_specs=[x_spec, g_spec, w_spec, pl.BlockSpec((d, cs), lambda i, j: (0, 0))],
        out_specs=[o_spec, pl.BlockSpec((bm, cs), lambda i, j: (i, 0))],
        out_shape=[jax.ShapeDtypeStruct((n, c), out_dtype), jax.ShapeDtypeStruct((n, cs), F32)],
        scratch_shapes=scratch,
        compiler_params=_cparams(("arbitrary", "arbitrary")), name="norm_matmul_in",
    )(x, g, w, w_small)


def _head_norm(oh, gn_h):
    r = lax.rsqrt(jnp.mean(oh * oh, axis=-1, keepdims=True) + EPS)
    return oh * r * gn_h


def _gla_log_decay(small, wa_ref, ba_ref):
    a_pre = _dot(small.astype(BF16), wa_ref[...]) + ba_ref[...]
    return _log_sigmoid(a_pre) * (1.0 / GLA_TAU)


def _gla_prompt_kernel(qk_ref, v_ref, gr_ref, sm_ref, wa_ref, ba_ref, gn_ref,
                       y_ref, st_ref, state_ref, b_ref, k_ref, *, n_chunks, group):
    t = pl.program_id(1)
    L = CHUNK

    @pl.when(t == 0)
    def _():
        state_ref[...] = jnp.zeros_like(state_ref)

    ri = lax.broadcasted_iota(jnp.int32, (L, L), 0)
    ci = lax.broadcasted_iota(jnp.int32, (L, L), 1)
    tril = jnp.where(ri >= ci, 1.0, 0.0).astype(BF16)
    rw = lax.broadcasted_iota(jnp.int32, (L, GLA_HEADS * L), 0)
    cw = lax.broadcasted_iota(jnp.int32, (L, GLA_HEADS * L), 1)
    causal_wide = (cw % L) <= rw
    kr = lax.broadcasted_iota(jnp.int32, (GLA_HEADS * L, GLA_QK), 0)
    kc = lax.broadcasted_iota(jnp.int32, (GLA_HEADS * L, GLA_QK), 1)
    kk_mask = (kr // L) == (kc // GLA_DK)
    vr = lax.broadcasted_iota(jnp.int32, (GLA_HEADS * L, GLA_V), 0)
    vc = lax.broadcasted_iota(jnp.int32, (GLA_HEADS * L, GLA_V), 1)
    vv_mask = (vr // L) == (vc // GLA_DV)
    sr = lax.broadcasted_iota(jnp.int32, (GLA_V, GLA_QK), 0)
    sc = lax.broadcasted_iota(jnp.int32, (GLA_V, GLA_QK), 1)
    st_mask = (sr // GLA_DV) == (sc // GLA_DK)
    hr = lax.broadcasted_iota(jnp.int32, (GLA_QK, LANES), 0)
    hc = lax.broadcasted_iota(jnp.int32, (GLA_QK, LANES), 1)
    head_ones = jnp.where((hr // GLA_DK) == hc, 1.0, 0.0).astype(BF16)
    pr = lax.broadcasted_iota(jnp.int32, (LANES, GLA_HEADS * L), 0)
    pc = lax.broadcasted_iota(jnp.int32, (LANES, GLA_HEADS * L), 1)

    def one(e, c):
        r0 = pl.multiple_of(c * L, L)
        rows = pl.ds(r0, L)
        qk = qk_ref[e, rows, :].astype(F32)
        q = qk[:, :GLA_QK] * (GLA_DK ** -0.5)
        k = qk[:, GLA_QK:]
        v16 = v_ref[e, rows, :].astype(BF16)
        g = _gla_log_decay(sm_ref[e, rows, :], wa_ref, ba_ref)
        b = _dot_exact01(tril, g)
        b_end = b[L - 1:L, :]
        qe = q * jnp.exp(b)
        k_dec = k * jnp.exp(b_end - b)
        qe16 = qe.astype(BF16)

        def fast_att(_):
            k_til = k * jnp.exp(-b)
            kk = jnp.where(kk_mask, jnp.concatenate([k_til] * GLA_HEADS, axis=0), 0.0)
            return _dot_nt(qe16, kk.astype(BF16))

        def direct_att(_):
            b_ref[...] = b
            k_ref[...] = k

            def col(j, acc):
                kj = k_ref[pl.ds(j, 1), :]
                bj = b_ref[pl.ds(j, 1), :]
                prod = q * kj * jnp.exp(jnp.minimum(b - bj, 0.0))
                red = _dot(prod.astype(BF16), head_ones)
                place = jnp.where((pc == pr * L + j) & (pr < GLA_HEADS), 1.0, 0.0).astype(BF16)
                return acc + _dot(red.astype(BF16), place)

            return lax.fori_loop(0, L, col, jnp.zeros((L, GLA_HEADS * L), F32))

        safe = jnp.max(-b_end) <= GLA_SAFE_DECAY
        att = lax.cond(safe, fast_att, direct_att, 0)
        att = jnp.where(causal_wide, att, 0.0)

        vv = jnp.where(vv_mask, jnp.concatenate([v16] * GLA_HEADS, axis=0), jnp.zeros((), BF16))
        st = state_ref[e]
        o = _dot(att.astype(BF16), vv) + _dot_nt(qe16, st.astype(BF16))
        upd = _dot_tn(v16, k_dec.astype(BF16))
        st_new = jnp.where(st_mask, st * jnp.exp(b_end) + upd, 0.0)

        gr = gr_ref[e, rows, :].astype(F32)
        ys = []
        for h in range(GLA_HEADS):
            sl = slice(h * GLA_DV, (h + 1) * GLA_DV)
            gate = gr[:, sl]
            ys.append((_head_norm(o[:, sl], gn_ref[:, sl]) * (gate * _sigmoid(gate))).astype(BF16))
        return jnp.concatenate(ys, axis=1), st_new

    def chunk(c, carry):
        res = [one(e, c) for e in range(group)]
        rows = pl.ds(pl.multiple_of(c * L, L), L)
        y_ref[:, rows, :] = jnp.stack([r[0] for r in res])
        state_ref[...] = jnp.stack([r[1] for r in res])
        return carry

    lax.fori_loop(0, n_chunks, chunk, 0)

    @pl.when(t == pl.num_programs(1) - 1)
    def _():
        st_ref[...] = state_ref[...]


def _gla_prompt(z3, zs3, wa, ba, gn, *, tt, group):
    batch, seq, _ = z3.shape
    blk = lambda j: pl.BlockSpec((group, tt, 512), lambda b, t: (b, t, j))
    const = lambda shape: pl.BlockSpec(shape, lambda b, t: (0,) * len(shape))
    return pl.pallas_call(
        functools.partial(_gla_prompt_kernel, n_chunks=tt // CHUNK, group=group),
        grid=(batch // group, seq // tt),
        in_specs=[blk(Z_QK), blk(Z_GV), blk(Z_GR),
                  pl.BlockSpec((group, tt, LANES), lambda b, t: (b, t, 0)),
                  const((LANES, GLA_QK)), const((1, GLA_QK)), const((1, GLA_V))],
        out_specs=[pl.BlockSpec((group, tt, GLA_V), lambda b, t: (b, t, 0)),
                   pl.BlockSpec((group, GLA_V, GLA_QK), lambda b, t: (b, 0, 0))],
        out_shape=[jax.ShapeDtypeStruct((batch, seq, GLA_V), BF16),
                   jax.ShapeDtypeStruct((batch, GLA_V, GLA_QK), F32)],
        scratch_shapes=[pltpu.VMEM((group, GLA_V, GLA_QK), F32), pltpu.VMEM((CHUNK, GLA_QK), F32),
                        pltpu.VMEM((CHUNK, GLA_QK), F32)],
        compiler_params=_cparams(("arbitrary", "arbitrary")), name="gla_prompt",
    )(z3, z3, z3, zs3, wa, ba, gn)


def _mlstm_prompt_kernel(q_ref, k_ref, v_ref, og_ref, sm_ref, gb_ref, gn_ref,
                         y_ref, c_out, n_out, m_out, c_ref, n_ref, m_ref, *, n_chunks, group):
    t = pl.program_id(1)
    L = CHUNK

    @pl.when(t == 0)
    def _():
        c_ref[...] = jnp.zeros_like(c_ref)
        n_ref[...] = jnp.zeros_like(n_ref)
        m_ref[...] = jnp.zeros_like(m_ref)

    ri = lax.broadcasted_iota(jnp.int32, (L, L), 0)
    ci = lax.broadcasted_iota(jnp.int32, (L, L), 1)
    lower = ri >= ci
    tril = jnp.where(lower, 1.0, 0.0).astype(BF16)
    triu = jnp.where(ri <= ci, 1.0, 0.0).astype(BF16)
    sr = lax.broadcasted_iota(jnp.int32, (LANES, ML_W), 0)
    sh = lax.broadcasted_iota(jnp.int32, (LANES, ML_W), 1) // ML_D
    sel_i = jnp.where(sr == SM_MI + sh, 1.0, 0.0).astype(BF16)
    sel_f = jnp.where(sr == SM_MF + sh, 1.0, 0.0).astype(BF16)

    def one(e, c):
        r0 = pl.multiple_of(c * L, L)
        rows = pl.ds(r0, L)
        sm = sm_ref[e, rows, :] + gb_ref[...]
        lf_all = _log_sigmoid(sm)
        sm_t = sm.T
        b_cols = _dot_exact01(tril, lf_all)
        b_rows = _dot_exact01_rhs(_log_sigmoid(sm_t), triu)
        b_wide = _dot_exact01_rhs(b_cols, sel_f)
        li_wide = _dot_exact01_rhs(sm, sel_i)
        ys, cs, ns, ms = [], [], [], []
        for h in range(ML_HEADS):
            sl = slice(h * ML_D, (h + 1) * ML_D)
            b_tok = b_wide[:, sl]
            li_tok = li_wide[:, sl]
            li_row = sm_t[SM_MI + h:SM_MI + h + 1, :]
            b_row = b_rows[SM_MF + h:SM_MF + h + 1, :]
            m_prev = m_ref[e, h:h + 1, :]
            w = jnp.where(lower, jnp.concatenate([b_tok] * (L // LANES), axis=1) - b_row + li_row, -jnp.inf)
            m_tok = jnp.maximum(b_tok + m_prev, jnp.max(w, axis=1, keepdims=True))
            a_inter = jnp.exp(b_tok + m_prev - m_tok)
            qh = q_ref[e, rows, sl].astype(F32) * (ML_D ** -0.5)
            kh16 = k_ref[e, rows, sl].astype(BF16)
            kh = kh16.astype(F32)
            qh16, vh16 = qh.astype(BF16), v_ref[e, rows, sl].astype(BF16)
            s = _dot_nt(qh16, kh16) * jnp.exp(w - jnp.concatenate([m_tok] * (L // LANES), axis=1))
            c_h = c_ref[e, sl, :]
            n_h = n_ref[e, h:h + 1, :]
            s16 = s.astype(BF16)
            row_sum = jnp.sum(s, axis=1, keepdims=True)
            q_n = _dot_nt(qh16, jnp.broadcast_to(n_h, (LANES, ML_D)).astype(BF16))
            num = _dot(s16, vh16) + a_inter * _dot(qh16, c_h.astype(BF16))
            den = row_sum + a_inter * q_n
            hh = num / jnp.maximum(jnp.abs(den), jnp.exp(-m_tok))
            og = og_ref[e, rows, sl].astype(F32)
            ys.append((_head_norm(hh, gn_ref[:, sl]) * _sigmoid(og)).astype(BF16))
            b_end = b_tok[L - 1:L, :]
            w_end = b_end - b_tok + li_tok
            m_new = jnp.maximum(b_end + m_prev, jnp.max(w_end, axis=0, keepdims=True))
            e_inter = jnp.exp(b_end + m_prev - m_new)
            kd = kh * jnp.exp(w_end - m_new)
            cs.append(e_inter * c_h + _dot_tn(kd.astype(BF16), vh16))
            ns.append(e_inter * n_h + jnp.sum(kd, axis=0, keepdims=True))
            ms.append(m_new)
        pad = [jnp.zeros((8 - ML_HEADS, LANES), F32)]
        return (jnp.concatenate(ys, axis=1), jnp.concatenate(cs, axis=0),
                jnp.concatenate(ns + pad, axis=0), jnp.concatenate(ms + pad, axis=0))

    def chunk(c, carry):
        res = [one(e, c) for e in range(group)]
        rows = pl.ds(pl.multiple_of(c * L, L), L)
        y_ref[:, rows, :] = jnp.stack([r[0] for r in res])
        c_ref[...] = jnp.stack([r[1] for r in res])
        n_ref[...] = jnp.stack([r[2] for r in res])
        m_ref[...] = jnp.stack([r[3] for r in res])
        return carry

    lax.fori_loop(0, n_chunks, chunk, 0)

    @pl.when(t == pl.num_programs(1) - 1)
    def _():
        c_out[...] = c_ref[...]
        n_out[...] = n_ref[...]
        m_out[...] = m_ref[...]


def _mlstm_prompt(z3, zs3, gbias, gn, *, tt, group):
    batch, seq, _ = z3.shape
    blk = lambda j: pl.BlockSpec((group, tt, 512), lambda b, t: (b, t, j))
    const = lambda shape: pl.BlockSpec(shape, lambda b, t: (0,) * len(shape))
    state = lambda shape: pl.BlockSpec((group,) + shape, lambda b, t: (b, 0, 0))
    return pl.pallas_call(
        functools.partial(_mlstm_prompt_kernel, n_chunks=tt // CHUNK, group=group),
        grid=(batch // group, seq // tt),
        in_specs=[blk(Z_MQ), blk(Z_MK), blk(Z_MV), blk(Z_MO),
                  pl.BlockSpec((group, tt, LANES), lambda b, t: (b, t, 0)),
                  const((1, LANES)), const((1, ML_W))],
        out_specs=[pl.BlockSpec((group, tt, ML_W), lambda b, t: (b, t, 0)),
                   state((ML_W, ML_D)), state((8, LANES)), state((8, LANES))],
        out_shape=[jax.ShapeDtypeStruct((batch, seq, ML_W), BF16),
                   jax.ShapeDtypeStruct((batch, ML_W, ML_D), F32),
                   jax.ShapeDtypeStruct((batch, 8, LANES), F32),
                   jax.ShapeDtypeStruct((batch, 8, LANES), F32)],
        scratch_shapes=[pltpu.VMEM((group, ML_W, ML_D), F32), pltpu.VMEM((group, 8, LANES), F32),
                        pltpu.VMEM((group, 8, LANES), F32)],
        compiler_params=_cparams(("arbitrary", "arbitrary")), name="mlstm_prompt",
    )(z3, z3, z3, z3, zs3, gbias, gn)


def _s5_prep_kernel(lre_ref, lim_ref, ldt_ref, ctr_ref, cti_ref, par_ref, cf_ref):
    lam_re = lre_ref[0]
    lam_im = lim_ref[0]
    dt = jnp.exp(ldt_ref[0])
    mag = jnp.exp(lam_re * dt)
    lb_re = mag * jnp.cos(lam_im * dt)
    lb_im = mag * jnp.sin(lam_im * dt)
    nr = lb_re - 1.0
    den = lam_re * lam_re + lam_im * lam_im
    f_re = (nr * lam_re + lb_im * lam_im) / den
    f_im = (lb_im * lam_re - nr * lam_im) / den
    par_ref[0] = jnp.concatenate([lb_re, lb_im, f_re, f_im, jnp.zeros((4, S5_MB), F32)], axis=0)
    ct_re = ctr_ref[0]
    ct_im = cti_ref[0]
    cf_ref[0, :, :S5_MB] = (ct_re * f_re - ct_im * f_im).astype(BF16)
    cf_ref[0, :, S5_MB:] = (-(ct_re * f_im + ct_im * f_re)).astype(BF16)


def _s5_prep(lam_re, lam_im, log_dt, ct_re, ct_im):
    vec = pl.BlockSpec((1, 1, S5_MB), lambda n: (n, 0, 0))
    mat = pl.BlockSpec((1, S5_UB, S5_MB), lambda n: (n, 0, 0))
    return pl.pallas_call(
        _s5_prep_kernel, grid=(S5_NB,),
        in_specs=[vec, vec, vec, mat, mat],
        out_specs=[pl.BlockSpec((1, 8, S5_MB), lambda n: (n, 0, 0)),
                   pl.BlockSpec((1, S5_UB, 2 * S5_MB), lambda n: (n, 0, 0))],
        out_shape=[jax.ShapeDtypeStruct((S5_NB, 8, S5_MB), F32),
                   jax.ShapeDtypeStruct((S5_NB, S5_UB, 2 * S5_MB), BF16)],
        compiler_params=_cparams(("arbitrary",)), name="s5_prep",
    )(lam_re, lam_im, log_dt, ct_re, ct_im)


def _s5_prompt_kernel(u_ref, w_ref, cf_ref, par_ref, y_ref, h_out, x_ref, h_ref, ub_ref, ut_ref, *, batch, tt):
    t_blk = pl.program_id(1)

    @pl.when(t_blk == 0)
    def _():
        h_ref[...] = jnp.zeros_like(h_ref)

    nlb = S5_MB // LANES
    ub_ref[...] = u_ref[...].astype(F32).reshape(batch * tt, S5_UB)

    def interleave(t, carry):
        ut_ref[pl.ds(pl.multiple_of(t * batch, batch), batch), :] = ub_ref[pl.ds(t, batch, stride=tt), :]
        return carry

    lax.fori_loop(0, tt, interleave, 0, unroll=8)
    x = _dot(ut_ref[...].astype(BF16), w_ref[0])
    for j in range(2 * nlb):
        x_ref[j] = x[:, j * LANES:(j + 1) * LANES]

    par = par_ref[0]
    lr = [jnp.broadcast_to(par[0:1, j * LANES:(j + 1) * LANES], (batch, LANES)) for j in range(nlb)]
    li = [jnp.broadcast_to(par[1:2, j * LANES:(j + 1) * LANES], (batch, LANES)) for j in range(nlb)]

    def step(t, carry):
        rows = pl.ds(pl.multiple_of(t * batch, batch), batch)
        new = []
        for j in range(nlb):
            hr, hi = carry[j], carry[nlb + j]
            nr = lr[j] * hr - li[j] * hi + x_ref[j, rows, :]
            ni = lr[j] * hi + li[j] * hr + x_ref[nlb + j, rows, :]
            x_ref[j, rows, :] = nr
            x_ref[nlb + j, rows, :] = ni
            new.append((nr, ni))
        return tuple(n[0] for n in new) + tuple(n[1] for n in new)

    h0 = tuple(h_ref[:, j * LANES:(j + 1) * LANES] for j in range(2 * nlb))
    hfin = lax.fori_loop(0, tt, step, h0, unroll=8)
    for j in range(2 * nlb):
        h_ref[:, j * LANES:(j + 1) * LANES] = hfin[j]
    hr = jnp.concatenate(hfin[:nlb], axis=1)
    hi = jnp.concatenate(hfin[nlb:], axis=1)

    hall = jnp.concatenate([x_ref[j].astype(BF16) for j in range(2 * nlb)], axis=1)
    ut_ref[...] = _dot_nt(hall, cf_ref[0])
    for b in range(batch):
        y_ref[b] = ut_ref[pl.ds(b, tt, stride=batch), :]

    @pl.when(t_blk == pl.num_programs(1) - 1)
    def _():
        fr = par[2:3, :]
        fi = par[3:4, :]
        h_out[0, :, :S5_MB] = fr * hr - fi * hi
        h_out[0, :, S5_MB:] = fr * hi + fi * hr


def _s5_prompt(z3, w_blk, cfold, par, *, batch, seq, tt):
    nt = seq // tt
    u_col0 = Z_SU * 512 // S5_UB
    return pl.pallas_call(
        functools.partial(_s5_prompt_kernel, batch=batch, tt=tt),
        grid=(S5_NB, nt),
        in_specs=[pl.BlockSpec((batch, tt, S5_UB), lambda n, t: (0, t, u_col0 + n)),
                  pl.BlockSpec((1, S5_UB, 2 * S5_MB), lambda n, t: (n, 0, 0)),
                  pl.BlockSpec((1, S5_UB, 2 * S5_MB), lambda n, t: (n, 0, 0)),
                  pl.BlockSpec((1, 8, S5_MB), lambda n, t: (n, 0, 0))],
        out_specs=[pl.BlockSpec((batch, tt, S5_UB), lambda n, t: (0, t, n)),
                   pl.BlockSpec((1, batch, 2 * S5_MB), lambda n, t: (n, 0, 0))],
        out_shape=[jax.ShapeDtypeStruct((batch, seq, S5_WIDTH), F32),
                   jax.ShapeDtypeStruct((S5_NB, batch, 2 * S5_MB), F32)],
        scratch_shapes=[pltpu.VMEM((2 * S5_MB // LANES, batch * tt, LANES), F32),
                        pltpu.VMEM((batch, 2 * S5_MB), F32),
                        pltpu.VMEM((batch * tt, S5_UB), F32), pltpu.VMEM((batch * tt, S5_UB), F32)],
        compiler_params=_cparams(("arbitrary", "arbitrary")), name="s5_prompt",
    )(z3, w_blk, cfold, par)


def _s5_decode_kernel(u_ref, w_ref, ctr_ref, cti_ref, par_ref, hr_ref, hi_ref, y_ref, hro_ref, hio_ref):
    x = _dot(u_ref[...].astype(BF16), w_ref[0])
    xr, xi = x[:, :S5_MB], x[:, S5_MB:]
    par = par_ref[0]
    lr, li, fr, fi = par[0:1, :], par[1:2, :], par[2:3, :], par[3:4, :]
    h0r, h0i = hr_ref[...], hi_ref[...]
    hr = lr * h0r - li * h0i + (fr * xr - fi * xi)
    hi = lr * h0i + li * h0r + (fr * xi + fi * xr)
    hro_ref[...] = hr
    hio_ref[...] = hi
    y_ref[...] = _dot_nt(hr.astype(BF16), ctr_ref[0]) - _dot_nt(hi.astype(BF16), cti_ref[0])


def _s5_decode(z, w_blk, ct_re, ct_im, par, h_re, h_im):
    rows = z.shape[0]
    u_col0 = Z_SU * 512 // S5_UB
    hspec = pl.BlockSpec((rows, S5_MB), lambda n: (0, n))
    mat = pl.BlockSpec((1, S5_UB, S5_MB), lambda n: (n, 0, 0))
    return pl.pallas_call(
        _s5_decode_kernel, grid=(S5_NB,),
        in_specs=[pl.BlockSpec((rows, S5_UB), lambda n: (0, u_col0 + n)),
                  pl.BlockSpec((1, S5_UB, 2 * S5_MB), lambda n: (n, 0, 0)),
                  mat, mat,
                  pl.BlockSpec((1, 8, S5_MB), lambda n: (n, 0, 0)), hspec, hspec],
        out_specs=[pl.BlockSpec((rows, S5_UB), lambda n: (0, n)), hspec, hspec],
        out_shape=[jax.ShapeDtypeStruct((rows, S5_WIDTH), F32),
                   jax.ShapeDtypeStruct((rows, S5_MODES), F32),
                   jax.ShapeDtypeStruct((rows, S5_MODES), F32)],
        compiler_params=_cparams(("arbitrary",)), name="s5_decode",
    )(z, w_blk, ct_re, ct_im, par, h_re, h_im)


def _gla_decode_kernel(qk_ref, v_ref, gr_ref, sm_ref, wa_ref, ba_ref, gn_ref, s_ref,
                       y_ref, so_ref, *, bb, fill_layer):
    eg = jnp.exp(_gla_log_decay(sm_ref[...], wa_ref, ba_ref))
    qk = qk_ref[...]
    q = qk[:, :GLA_QK] * (GLA_DK ** -0.5)
    k = qk[:, GLA_QK:]
    v = v_ref[...]

    def per_row(b):
        row = slice(b, b + 1)
        s_old = s_ref[b].reshape(GLA_QK, GLA_DV)
        v_rows = jnp.concatenate(
            [jnp.broadcast_to(v[row, h * GLA_DV:(h + 1) * GLA_DV], (GLA_DK, GLA_DV)) for h in range(GLA_HEADS)],
            axis=0)
        s_new = _row_to_col(eg[row], GLA_QK) * s_old + _row_to_col(k[row], GLA_QK) * v_rows
        qs = (_row_to_col(q[row], GLA_QK) * s_new).reshape(GLA_HEADS, GLA_DK, GLA_DV)
        o4 = jnp.sum(qs, axis=1)
        o_row = jnp.concatenate([o4[h:h + 1, :] for h in range(GLA_HEADS)], axis=1)
        return s_new.reshape(GLA_HEADS, GLA_DK, GLA_DV), o_row

    res = [per_row(b) for b in range(bb)]
    _store_layer_state(so_ref, jnp.stack([r[0] for r in res]), fill_layer)
    o = jnp.concatenate([r[1] for r in res], axis=0)
    gr = gr_ref[...]
    for h in range(GLA_HEADS):
        vs = slice(h * GLA_DV, (h + 1) * GLA_DV)
        gate = gr[:, vs]
        y_ref[:, vs] = (_head_norm(o[:, vs], gn_ref[:, vs]) * (gate * _sigmoid(gate))).astype(BF16)


def _store_layer_state(out_ref, new, layer):
    if layer is None:
        out_ref[...] = new
    else:
        for l in range(out_ref.shape[0]):
            out_ref[l] = new if l == layer else jnp.zeros_like(new)


def _skip_ref(kernel, pos):
    def wrapped(*refs):
        return kernel(*refs[:pos], *refs[pos + 1:])
    return wrapped


def _layer_out(kernel, in_specs, operands, out_specs, out_shapes, stacked_idx, prev, name, grid):
    aliases = {}
    if prev is not None:
        in_specs = in_specs + [pl.BlockSpec(memory_space=pl.ANY)]
        operands = operands + (prev,)
        aliases = {len(operands) - 1: stacked_idx}
        kernel = _skip_ref(kernel, len(operands) - 1)
    return pl.pallas_call(
        kernel, grid=grid, in_specs=in_specs, out_specs=out_specs, out_shape=out_shapes,
        input_output_aliases=aliases, compiler_params=_cparams(("arbitrary",)), name=name,
    )(*operands)


def _gla_decode(z, zs, wa, ba, gn, states, prev, *, layer, bb):
    rows = z.shape[0]
    blk = lambda j: pl.BlockSpec((bb, 512), lambda i: (i, j))
    const = lambda shape: pl.BlockSpec(shape, lambda i: (0,) * len(shape))
    sblock = (bb, GLA_HEADS, GLA_DK, GLA_DV)
    sspec = pl.BlockSpec((None,) + sblock, lambda i: (layer, i, 0, 0, 0))
    first = prev is None
    out_sspec = pl.BlockSpec((states.shape[0],) + sblock, lambda i: (0, i, 0, 0, 0)) if first else sspec
    return _layer_out(
        functools.partial(_gla_decode_kernel, bb=bb, fill_layer=layer if first else None),
        [blk(Z_QK), blk(Z_GV), blk(Z_GR), pl.BlockSpec((bb, LANES), lambda i: (i, 0)),
         const((LANES, GLA_QK)), const((1, GLA_QK)), const((1, GLA_V)), sspec],
        (z, z, z, zs, wa, ba, gn, states),
        [pl.BlockSpec((bb, GLA_V), lambda i: (i, 0)), out_sspec],
        [jax.ShapeDtypeStruct((rows, GLA_V), BF16), jax.ShapeDtypeStruct(states.shape, F32)],
        1, prev, "gla_decode", (rows // bb,))


def _mlstm_decode_kernel(q_ref, k_ref, v_ref, og_ref, sm_ref, gb_ref, gn_ref, c_ref, n_ref, m_ref,
                         y_ref, co_ref, no_ref, mo_ref, *, bb, fill_layer):
    li_blk = sm_ref[...] + gb_ref[...]
    lf_blk = _log_sigmoid(li_blk)
    m_blk = m_ref[...]
    q_blk = q_ref[...] * (ML_D ** -0.5)
    k_blk = k_ref[...]
    v_blk = v_ref[...]
    lane = lax.broadcasted_iota(jnp.int32, (1, LANES), 1)

    def per_row(b):
        row = slice(b, b + 1)
        li_r, lf_r, m_r = li_blk[row], lf_blk[row], m_blk[row]
        q_all, k_all, v_all = q_blk[row], k_blk[row], v_blk[row]
        n_all = n_ref[b]
        m_new_row = jnp.zeros((1, LANES), F32)
        h_heads, n_heads, c_heads = [], [], []
        for h in range(ML_HEADS):
            sl = slice(h * ML_D, (h + 1) * ML_D)
            li = li_r[:, SM_MI + h:SM_MI + h + 1]
            lf = lf_r[:, SM_MF + h:SM_MF + h + 1]
            m_prev = m_r[:, h:h + 1]
            m_new = jnp.maximum(lf + m_prev, li)
            a = jnp.exp(lf + m_prev - m_new)
            e = jnp.exp(li - m_new)
            k_r = k_all[:, sl]
            q_r = q_all[:, sl]
            c_new = a * c_ref[b, h] + (_row_to_col(k_r, ML_D) * e) * v_all[:, sl]
            n_new = a * n_all[h:h + 1, :] + e * k_r
            c_heads.append(c_new)
            n_heads.append(n_new)
            num = jnp.sum(_row_to_col(q_r, ML_D) * c_new, axis=0, keepdims=True)
            den = jnp.sum(q_r * n_new, axis=1, keepdims=True)
            h_heads.append(num / jnp.maximum(jnp.abs(den), jnp.exp(-m_new)))
            m_new_row = jnp.where(lane == h, m_new, m_new_row)
        return (jnp.stack(c_heads), jnp.concatenate(n_heads, axis=0),
                jnp.concatenate(h_heads, axis=1), m_new_row)

    res = [per_row(b) for b in range(bb)]
    _store_layer_state(co_ref, jnp.stack([r[0] for r in res]), fill_layer)
    no_ref[...] = jnp.stack([r[1] for r in res])
    mo_ref[...] = jnp.concatenate([r[3] for r in res], axis=0)
    hh = jnp.concatenate([r[2] for r in res], axis=0)
    og = og_ref[...]
    for h in range(ML_HEADS):
        sl = slice(h * ML_D, (h + 1) * ML_D)
        y_ref[:, sl] = (_head_norm(hh[:, sl], gn_ref[:, sl]) * _sigmoid(og[:, sl])).astype(BF16)


def _mlstm_decode(z, zs, gbias, gn, c_all, n_all, m_pad, prev, *, layer, bb):
    rows = z.shape[0]
    blk = lambda j: pl.BlockSpec((bb, 512), lambda i: (i, j))
    const = lambda shape: pl.BlockSpec(shape, lambda i: (0,) * len(shape))
    nblock = (bb, ML_HEADS, ML_D)
    cblock = (bb, ML_HEADS, ML_D, ML_D)
    cspec = pl.BlockSpec((None,) + cblock, lambda i: (layer, i, 0, 0, 0))
    mspec = pl.BlockSpec((bb, LANES), lambda i: (i, 0))
    first = prev is None
    out_cspec = pl.BlockSpec((c_all.shape[0],) + cblock, lambda i: (0, i, 0, 0, 0)) if first else cspec
    return _layer_out(
        functools.partial(_mlstm_decode_kernel, bb=bb, fill_layer=layer if first else None),
        [blk(Z_MQ), blk(Z_MK), blk(Z_MV), blk(Z_MO), mspec, const((1, LANES)), const((1, ML_W)),
         cspec, pl.BlockSpec((None,) + nblock, lambda i: (layer, i, 0, 0)), mspec],
        (z, z, z, z, zs, gbias, gn, c_all, n_all, m_pad),
        [pl.BlockSpec((bb, ML_W), lambda i: (i, 0)), out_cspec, pl.BlockSpec(nblock, lambda i: (i, 0, 0)), mspec],
        [jax.ShapeDtypeStruct((rows, ML_W), BF16), jax.ShapeDtypeStruct(c_all.shape, F32),
         jax.ShapeDtypeStruct(n_all.shape[1:], F32), jax.ShapeDtypeStruct((rows, LANES), F32)],
        1, prev, "mlstm_decode", (rows // bb,))


def _merged_residual(ya_ref, yb_ref, yp_ref, u_ref, za_ref, zb_ref, zc_ref, x_ref,
                     d_ref, wg_ref, wa_ref, wb_ref, wc_ref, wo_ref):
    yc = _gelu_tanh(yp_ref[...] + d_ref[...] * u_ref[...].astype(F32))
    yc = yc * _sigmoid(_dot(yc.astype(BF16), wg_ref[...]))
    m = _sigmoid(za_ref[...].astype(F32)) * _dot(ya_ref[...], wa_ref[...])
    m = m + _sigmoid(zb_ref[...].astype(F32)) * _dot(yb_ref[...], wb_ref[...])
    m = m + _sigmoid(zc_ref[...].astype(F32)) * _dot(yc.astype(BF16), wc_ref[...])
    return x_ref[...] + _dot(m.astype(BF16), wo_ref[...])


def _merge_kernel(*refs):
    refs[-1][...] = _merged_residual(*refs[:-1])


def _merge_specs(bm, row_map, const):
    r512 = pl.BlockSpec((bm, 512), lambda *g: (row_map(*g), 0))
    zblk = lambda j: pl.BlockSpec((bm, D_MODEL), lambda *g: (row_map(*g), j))
    return [r512, r512, r512, pl.BlockSpec((bm, 512), lambda *g: (row_map(*g), Z_SU)),
            zblk(Z_ZA), zblk(Z_ZB), zblk(Z_ZC), pl.BlockSpec((bm, D_MODEL), lambda *g: (row_map(*g), 0)),
            const((1, S5_WIDTH)), const((S5_WIDTH, S5_WIDTH)),
            const((GLA_V, D_MODEL)), const((ML_W, D_MODEL)), const((S5_WIDTH, D_MODEL)),
            const((D_MODEL, D_MODEL))]


def _merge(ya, yb, ypre, z, x, d, wg, wa, wb, wc, wo, *, bm):
    n = x.shape[0]
    const = lambda shape: pl.BlockSpec(shape, lambda i: (0, 0))
    return pl.pallas_call(
        _merge_kernel, grid=(n // bm,),
        in_specs=_merge_specs(bm, lambda i: i, const),
        out_specs=pl.BlockSpec((bm, D_MODEL), lambda i: (i, 0)),
        out_shape=jax.ShapeDtypeStruct((n, D_MODEL), F32),
        compiler_params=_cparams(("arbitrary",)), name="merge",
    )(ya, yb, ypre, z, z, z, z, x, d, wg, wa, wb, wc, wo)


def _cross_attend(x, g_ref, wq_ref, mk_ref, mv_ref, wo_ref):
    q = _dot(_rms_rows(x, g_ref[...]).astype(BF16), wq_ref[...])
    heads = []
    for h in range(X_HEADS):
        sl = slice(h * X_DH, (h + 1) * X_DH)
        s = _dot_nt(q[:, sl].astype(BF16), mk_ref[:, sl].astype(BF16)) * (X_DH ** -0.5)
        s = s - jnp.max(s, axis=-1, keepdims=True)
        p = jnp.exp(s)
        p = p / jnp.sum(p, axis=-1, keepdims=True)
        heads.append(_dot(p.astype(BF16), mv_ref[:, sl].astype(BF16)))
    o = jnp.concatenate(heads, axis=-1)
    return x + _dot(o.astype(BF16), wo_ref[...])


N_MERGE_IN = 14


def _merge_cross_kernel(*refs):
    x1 = _merged_residual(*refs[:N_MERGE_IN])
    refs[-1][...] = _cross_attend(x1, *refs[N_MERGE_IN:-1])


def _merge_cross(ya, yb, ypre, z, x, d, wg, wa, wb, wc, wo, g, wq, mem_k, mem_v, wco,
                 *, layer, batch, seq, bm):
    nt = seq // bm
    const = lambda shape: pl.BlockSpec(shape, lambda b, t: (0, 0), pipeline_mode=pl.Buffered(1))
    mspec = pl.BlockSpec((None, MEM_LEN, D_MODEL), lambda b, t: (layer, b, 0))
    xspec = pl.BlockSpec((bm, D_MODEL), lambda b, t: (b * nt + t, 0))
    return pl.pallas_call(
        _merge_cross_kernel, grid=(batch, nt),
        in_specs=_merge_specs(bm, lambda b, t: b * nt + t, const)
        + [const((1, D_MODEL)), const((D_MODEL, D_MODEL)), mspec, mspec, const((D_MODEL, D_MODEL))],
        out_specs=xspec, out_shape=jax.ShapeDtypeStruct(x.shape, F32),
        compiler_params=_cparams(("arbitrary", "arbitrary")), name="merge_cross",
    )(ya, yb, ypre, z, z, z, z, x, d, wg, wa, wb, wc, wo, g, wq, mem_k, mem_v, wco)


def _memory_kv_kernel(x_ref, g_ref, wk_ref, wv_ref, k_ref, v_ref, hn_ref):
    @pl.when(pl.program_id(1) == 0)
    def _():
        hn_ref[...] = _rms_rows(x_ref[...], g_ref[...]).astype(BF16)
    k_ref[...] = _dot(hn_ref[...], wk_ref[...])
    v_ref[...] = _dot(hn_ref[...], wv_ref[...])


def _memory_kv(mem, g, wk, wv, *, bn=512):
    rows, d = mem.shape
    depth = wk.shape[0]
    wspec = pl.BlockSpec((None, d, bn), lambda l, j: (l, 0, j))
    ospec = pl.BlockSpec((None, rows, bn), lambda l, j: (l, 0, j))
    out = jax.ShapeDtypeStruct((depth, rows, d), F32)
    return pl.pallas_call(
        _memory_kv_kernel, grid=(depth, d // bn),
        in_specs=[pl.BlockSpec((rows, d), lambda l, j: (0, 0)),
                  pl.BlockSpec((None, 1, d), lambda l, j: (l, 0, 0)), wspec, wspec],
        out_specs=[ospec, ospec], out_shape=[out, out],
        scratch_shapes=[pltpu.VMEM((rows, d), BF16)],
        compiler_params=_cparams(("arbitrary", "arbitrary")), name="memory_kv",
    )(mem, g, wk, wv)


def _cross_decode_kernel(x_ref, g_ref, wq_ref, mk_ref, mv_ref, wo_ref, o_ref, *, bb):
    x = x_ref[...]
    q = _dot(_rms_rows(x, g_ref[...]).astype(BF16), wq_ref[...])
    rows = []
    for b in range(bb):
        q4 = jnp.concatenate([q[b:b + 1, h * X_DH:(h + 1) * X_DH] for h in range(X_HEADS)], axis=0)
        s = jnp.sum(mk_ref[0, b] * q4[None], axis=-1, keepdims=True) * (X_DH ** -0.5)
        p = jnp.exp(s - jnp.max(s, axis=0, keepdims=True))
        p = p / jnp.sum(p, axis=0, keepdims=True)
        o4 = jnp.sum(p * mv_ref[0, b], axis=0)
        rows.append(jnp.concatenate([o4[h:h + 1, :] for h in range(X_HEADS)], axis=1))
    att = jnp.concatenate(rows, axis=0)
    o_ref[...] = x + _dot(att.astype(BF16), wo_ref[...])


def _cross_decode(x, g, wq, cache_k, cache_v, wo, *, layer, bb):
    rows = x.shape[0]
    xspec = pl.BlockSpec((bb, D_MODEL), lambda i: (i, 0))
    const = lambda shape: pl.BlockSpec(shape, lambda i: (0, 0))
    mspec = pl.BlockSpec((1, bb, MEM_LEN, X_HEADS, X_DH), lambda i: (layer, i, 0, 0, 0))
    return pl.pallas_call(
        functools.partial(_cross_decode_kernel, bb=bb), grid=(rows // bb,),
        in_specs=[xspec, const((1, D_MODEL)), const((D_MODEL, D_MODEL)), mspec, mspec,
                  const((D_MODEL, D_MODEL))],
        out_specs=xspec, out_shape=jax.ShapeDtypeStruct(x.shape, F32),
        compiler_params=_cparams(("arbitrary",)), name="cross_decode",
    )(x, g, wq, cache_k, cache_v, wo)


def _ffn_prompt_kernel(x_ref, g_ref, wup_ref, cw_ref, cb_ref, wdn_ref, gf_ref,
                       o_ref, st_ref, hn_ref, halo_ref, up_ref, *, tt, final_norm):
    t = pl.program_id(1)

    @pl.when(t == 0)
    def _():
        halo_ref[...] = jnp.zeros_like(halo_ref)

    x = x_ref[...]
    hn_ref[...] = _rms_rows(x, g_ref[...]).astype(BF16)
    def project(c):
        slot = c % 2
        for part, col0 in enumerate((c * FF_CHUNK, D_FF + c * FF_CHUNK)):
            cols = slice(col0, col0 + FF_CHUNK)
            lanes = slice(part * FF_CHUNK, (part + 1) * FF_CHUNK)
            up = _dot(hn_ref[...], wup_ref[:, cols])
            up_ref[slot, 6:8, lanes] = halo_ref[0:2, cols]
            up_ref[slot, 8:8 + tt, lanes] = up
            halo_ref[0:2, cols] = up[tt - 2:tt, :]

    def gated(c):
        slot = c % 2
        a_cols = slice(c * FF_CHUNK, (c + 1) * FF_CHUNK)
        g_cols = slice(D_FF + c * FF_CHUNK, D_FF + (c + 1) * FF_CHUNK)
        cw = jnp.concatenate([cw_ref[:, a_cols], cw_ref[:, g_cols]], axis=1)
        cb = jnp.concatenate([cb_ref[:, a_cols], cb_ref[:, g_cols]], axis=1)
        cv = (up_ref[slot, 6:6 + tt, :] * cw[0:1] + up_ref[slot, 7:7 + tt, :] * cw[1:2]
              + up_ref[slot, 8:8 + tt, :] * cw[2:3] + cb)
        gt = cv[:, FF_CHUNK:]
        return (cv[:, :FF_CHUNK] * (gt * _sigmoid(gt))).astype(BF16)

    acc = x
    project(0)
    for c in range(N_FF_CHUNKS):
        if c + 1 < N_FF_CHUNKS:
            project(c + 1)
        acc = acc + _dot(gated(c), wdn_ref[c * FF_CHUNK:(c + 1) * FF_CHUNK, :])
    if final_norm:
        acc = _rms_rows(acc, gf_ref[...])
    o_ref[...] = acc

    @pl.when(t == pl.num_programs(1) - 1)
    def _():
        st_ref[0] = halo_ref[0:2, :]


def _ffn_prompt(x, g, wup, cw, cb, wdn, gf, *, batch, seq, tt, final_norm):
    nt = seq // tt
    xspec = pl.BlockSpec((tt, D_MODEL), lambda b, t: (b * nt + t, 0))
    const = lambda shape: pl.BlockSpec(shape, lambda b, t: (0, 0), pipeline_mode=pl.Buffered(1))
    return pl.pallas_call(
        functools.partial(_ffn_prompt_kernel, tt=tt, final_norm=final_norm), grid=(batch, nt),
        in_specs=[xspec, const((1, D_MODEL)), const((D_MODEL, 2 * D_FF)), const((CONV_W, 2 * D_FF)),
                  const((1, 2 * D_FF)), const((D_FF, D_MODEL)), const((1, D_MODEL))],
        out_specs=[xspec, pl.BlockSpec((1, CONV_W - 1, 2 * D_FF), lambda b, t: (b, 0, 0))],
        out_shape=[jax.ShapeDtypeStruct(x.shape, F32),
                   jax.ShapeDtypeStruct((batch, CONV_W - 1, 2 * D_FF), F32)],
        scratch_shapes=[pltpu.VMEM((tt, D_MODEL), BF16), pltpu.VMEM((8, 2 * D_FF), F32),
                        pltpu.VMEM((2, tt + 8, 2 * FF_CHUNK), F32)],
        compiler_params=_cparams(("arbitrary", "arbitrary")), name="ffn_prompt",
    )(x, g, wup, cw, cb, wdn, gf)


def _ffn_decode_kernel(x_ref, g_ref, wa_ref, wg_ref, cwa_ref, cwg_ref, cba_ref, cbg_ref,
                       b0a_ref, b0g_ref, b1a_ref, b1g_ref, wdn_ref, gf_ref,
                       o_ref, upa_ref, upg_ref, hn_ref, acc_ref, *, final_norm):
    c = pl.program_id(0)

    @pl.when(c == 0)
    def _():
        x = x_ref[...]
        hn_ref[...] = _rms_rows(x, g_ref[...]).astype(BF16)
        acc_ref[...] = x

    def conv(w_ref, cw_ref, cb_ref, b0_ref, b1_ref, up_ref):
        up = _dot(hn_ref[...], w_ref[...])
        up_ref[...] = up
        return b0_ref[...] * cw_ref[0:1, :] + b1_ref[...] * cw_ref[1:2, :] + up * cw_ref[2:3, :] + cb_ref[...]

    a = conv(wa_ref, cwa_ref, cba_ref, b0a_ref, b1a_ref, upa_ref)
    gt = conv(wg_ref, cwg_ref, cbg_ref, b0g_ref, b1g_ref, upg_ref)
    act = a * (gt * _sigmoid(gt))
    acc_ref[...] += _dot(act.astype(BF16), wdn_ref[...])

    @pl.when(c == pl.num_programs(0) - 1)
    def _():
        acc = acc_ref[...]
        if final_norm:
            acc = _rms_rows(acc, gf_ref[...])
        o_ref[...] = acc


def _ffn_decode(x, g, wup, cw, cb, wdn, gf, buf0, buf1, *, final_norm):
    rows = x.shape[0]
    fc = FF_CHUNK
    full = lambda shape: pl.BlockSpec(shape, lambda c: (0, 0))
    a_col = lambda r: pl.BlockSpec((r, fc), lambda c: (0, c))
    g_col = lambda r: pl.BlockSpec((r, fc), lambda c: (0, N_FF_CHUNKS + c))
    return pl.pallas_call(
        functools.partial(_ffn_decode_kernel, final_norm=final_norm), grid=(N_FF_CHUNKS,),
        in_specs=[full((rows, D_MODEL)), full((1, D_MODEL)),
                  a_col(D_MODEL), g_col(D_MODEL), a_col(CONV_W), g_col(CONV_W), a_col(1), g_col(1),
                  a_col(rows), g_col(rows), a_col(rows), g_col(rows),
                  pl.BlockSpec((fc, D_MODEL), lambda c: (c, 0)), full((1, D_MODEL))],
        out_specs=[full((rows, D_MODEL)), a_col(rows), a_col(rows)],
        out_shape=[jax.ShapeDtypeStruct(x.shape, F32),
                   jax.ShapeDtypeStruct((rows, D_FF), F32),
                   jax.ShapeDtypeStruct((rows, D_FF), F32)],
        scratch_shapes=[pltpu.VMEM((rows, D_MODEL), BF16), pltpu.VMEM((rows, D_MODEL), F32)],
        compiler_params=_cparams(("arbitrary",)), name="ffn_decode",
    )(x, g, wup, wup, cw, cw, cb, cb, buf0, buf0, buf1, buf1, wdn, gf)


def _layer_weights(P, l):
    w_in = P['w_in'][l]
    w_main = jnp.concatenate([w_in[:, 0:1536], w_in[:, 1552:3600], w_in[:, 3608:7192]], axis=1).astype(BF16)
    w_small = jnp.concatenate([w_in[:, 1536:1552], w_in[:, 3600:3608],
                               jnp.zeros((D_MODEL, LANES - GLA_LOWRANK - 2 * ML_HEADS), F32)], axis=1).astype(BF16)
    wa = jnp.zeros((LANES, GLA_QK), F32).at[:GLA_LOWRANK].set(P['w_gla_alpha'][l]).astype(BF16)
    gbias = jnp.zeros((1, LANES), F32)
    gbias = gbias.at[0, SM_MI:SM_MI + ML_HEADS].set(P['b_mlstm_i'][l])
    gbias = gbias.at[0, SM_MF:SM_MF + ML_HEADS].set(P['b_mlstm_f'][l])
    eye = jnp.eye(S5_GB, dtype=F32)
    w_re = jnp.einsum('ngpc,gh->ngchp', P['s5_b_re'][l].reshape(S5_NB, S5_GB, S5_P, S5_GROUP), eye)
    w_im = jnp.einsum('ngpc,gh->ngchp', P['s5_b_im'][l].reshape(S5_NB, S5_GB, S5_P, S5_GROUP), eye)
    w_blk = jnp.concatenate([w_re.reshape(S5_NB, S5_UB, S5_MB), w_im.reshape(S5_NB, S5_UB, S5_MB)],
                            axis=-1).astype(BF16)
    ct_re = jnp.einsum('ngcp,gh->ngchp', P['s5_c_re'][l].reshape(S5_NB, S5_GB, S5_GROUP, S5_P), eye)
    ct_im = jnp.einsum('ngcp,gh->ngchp', P['s5_c_im'][l].reshape(S5_NB, S5_GB, S5_GROUP, S5_P), eye)
    ct_re = ct_re.reshape(S5_NB, S5_UB, S5_MB)
    ct_im = ct_im.reshape(S5_NB, S5_UB, S5_MB)
    lam_re = P['s5_lam_re'][l].reshape(S5_NB, 1, S5_MB)
    lam_im = P['s5_lam_im'][l].reshape(S5_NB, 1, S5_MB)
    log_dt = jnp.broadcast_to(P['s5_log_dt'][l][:, None], (S5_GROUPS, S5_P)).reshape(S5_NB, 1, S5_MB)
    par, cfold = _s5_prep(lam_re, lam_im, log_dt, ct_re, ct_im)
    row = lambda a: a.reshape(1, -1)
    return dict(
        norm_mix=row(P['norm_mix'][l]), w_main=w_main, w_small=w_small, wa=wa,
        ba=row(P['b_gla_alpha'][l]), gla_norm=row(P['gla_head_norm'][l]), gbias=gbias,
        ml_norm=row(P['mlstm_head_norm'][l]), w_blk=w_blk, ct_re=ct_re.astype(BF16),
        ct_im=ct_im.astype(BF16), par=par, cfold=cfold, s5_d=row(P['s5_d'][l]),
        w_glu=P['s5_w_glu'][l].astype(BF16), w_a=P['w_branch_a'][l].astype(BF16),
        w_b=P['w_branch_b'][l].astype(BF16), w_c=P['w_branch_c'][l].astype(BF16),
        w_out=P['w_out'][l].astype(BF16), norm_cross=row(P['norm_cross'][l]),
        w_cq=P['w_cq'][l].astype(BF16), w_co=P['w_co'][l].astype(BF16),
        norm_ffn=row(P['norm_ffn'][l]), w_up=P['w_ffn_up'][l].astype(BF16),
        conv_w=P['ffn_conv_w'][l], conv_b=row(P['ffn_conv_b'][l]),
        w_down=P['w_ffn_down'][l].astype(BF16), norm_final=row(P['norm_final']),
    )


def _prompt_trunk(x_prompt, mem_k, mem_v, W, *, bm_in, tt_mix, gla_group, ml_group, tt_s5, bm, tt_ffn):
    batch, seq, _ = x_prompt.shape
    depth = len(W)
    x = x_prompt.reshape(batch * seq, D_MODEL)
    outs = []
    for l, w in enumerate(W):
        z, zs = _norm_matmul(x, w['norm_mix'], w['w_main'], w['w_small'], bm=bm_in, bn=1024, out_dtype=BF16)
        z3 = z.reshape(batch, seq, Z_MAIN)
        zs3 = zs.reshape(batch, seq, LANES)
        ya, st = _gla_prompt(z3, zs3, w['wa'], w['ba'], w['gla_norm'], tt=tt_mix, group=gla_group)
        yb, c, n, m = _mlstm_prompt(z3, zs3, w['gbias'], w['ml_norm'], tt=tt_mix, group=ml_group)
        ypre, hfin = _s5_prompt(z3, w['w_blk'], w['cfold'], w['par'], batch=batch, seq=seq, tt=tt_s5)
        x = _merge_cross(ya.reshape(batch * seq, GLA_V), yb.reshape(batch * seq, ML_W),
                         ypre.reshape(batch * seq, S5_WIDTH), z, x, w['s5_d'], w['w_glu'],
                         w['w_a'], w['w_b'], w['w_c'], w['w_out'],
                         w['norm_cross'], w['w_cq'], mem_k, mem_v, w['w_co'],
                         layer=l, batch=batch, seq=seq, bm=bm)
        x, conv = _ffn_prompt(x, w['norm_ffn'], w['w_up'], w['conv_w'], w['conv_b'], w['w_down'],
                              w['norm_final'], batch=batch, seq=seq, tt=tt_ffn,
                              final_norm=(l == depth - 1))
        st4 = st.reshape(batch, GLA_HEADS, GLA_DV, GLA_HEADS, GLA_DK)
        gla = jnp.stack([st4[:, h, :, h, :] for h in range(GLA_HEADS)], axis=1).transpose(0, 1, 3, 2)
        h4 = hfin.reshape(S5_NB, batch, 2, S5_GB, S5_P)
        s5_re = h4[:, :, 0].transpose(1, 0, 2, 3).reshape(batch, S5_GROUPS, S5_P)
        s5_im = h4[:, :, 1].transpose(1, 0, 2, 3).reshape(batch, S5_GROUPS, S5_P)
        outs.append((gla, c.reshape(batch, ML_HEADS, ML_D, ML_D), n[:, :ML_HEADS, :],
                     m[:, :ML_HEADS, 0], s5_re, s5_im, conv))
    stacked = [jnp.stack([outs[l][i] for l in range(depth)]) for i in range(7)]
    return x.reshape(batch, seq, D_MODEL), stacked


def _sample_trunk(x_sample, cache_k, cache_v, states, W):
    rows = x_sample.shape[0]
    depth = len(W)
    x = x_sample.reshape(rows, D_MODEL)
    s_gla, s_c, s_n, s_m, s_re, s_im, s_conv = states
    outs = []
    gla_all = c_all = None
    for l, w in enumerate(W):
        z, zs = _norm_matmul(x, w['norm_mix'], w['w_main'], w['w_small'], bm=rows, bn=1024)
        ya, gla_all = _gla_decode(z, zs, w['wa'], w['ba'], w['gla_norm'], s_gla, gla_all, layer=l, bb=16)
        m_pad = jnp.pad(s_m[l], ((0, 0), (0, LANES - ML_HEADS)))
        yb, c_all, n, m = _mlstm_decode(z, zs, w['gbias'], w['ml_norm'], s_c, s_n, m_pad, c_all,
                                        layer=l, bb=16)
        ypre, h_re, h_im = _s5_decode(z, w['w_blk'], w['ct_re'], w['ct_im'], w['par'],
                                      s_re[l].reshape(rows, S5_MODES), s_im[l].reshape(rows, S5_MODES))
        x = _merge(ya, yb, ypre, z, x, w['s5_d'], w['w_glu'], w['w_a'], w['w_b'], w['w_c'], w['w_out'],
                   bm=rows)
        x = _cross_decode(x, w['norm_cross'], w['w_cq'], cache_k, cache_v, w['w_co'], layer=l, bb=8)
        x, up_a, up_g = _ffn_decode(x, w['norm_ffn'], w['w_up'], w['conv_w'], w['conv_b'], w['w_down'],
                                    w['norm_final'], s_conv[l][:, 0, :], s_conv[l][:, 1, :],
                                    final_norm=(l == depth - 1))
        conv = jnp.stack([s_conv[l][:, 1, :], jnp.concatenate([up_a, up_g], axis=1)], axis=1)
        outs.append((n, m[:, :ML_HEADS], h_re.reshape(rows, S5_GROUPS, S5_P),
                     h_im.reshape(rows, S5_GROUPS, S5_P), conv))
    stacked = [jnp.stack([outs[l][i] for l in range(depth)]) for i in range(5)]
    return x.reshape(rows, 1, D_MODEL), [gla_all, c_all] + stacked


def kernel(x_prompt, x_sample, mem_prompt, cache_mem_k, cache_mem_v, state_gla, state_mlstm_c, state_mlstm_n, state_mlstm_m, state_s5_re, state_s5_im, state_ffn_conv, norm_mix, w_in, w_gla_alpha, b_gla_alpha, gla_head_norm, b_mlstm_i, b_mlstm_f, mlstm_head_norm, s5_lam_re, s5_lam_im, s5_log_dt, s5_b_re, s5_b_im, s5_c_re, s5_c_im, s5_d, s5_w_glu, w_branch_a, w_branch_b, w_branch_c, w_out, norm_cross, norm_mem, w_cq, w_ck, w_cv, w_co, norm_ffn, w_ffn_up, ffn_conv_w, ffn_conv_b, w_ffn_down, norm_final):
    P = dict(norm_mix=norm_mix, w_in=w_in, w_gla_alpha=w_gla_alpha, b_gla_alpha=b_gla_alpha,
             gla_head_norm=gla_head_norm, b_mlstm_i=b_mlstm_i, b_mlstm_f=b_mlstm_f,
             mlstm_head_norm=mlstm_head_norm, s5_lam_re=s5_lam_re, s5_lam_im=s5_lam_im,
             s5_log_dt=s5_log_dt, s5_b_re=s5_b_re, s5_b_im=s5_b_im, s5_c_re=s5_c_re,
             s5_c_im=s5_c_im, s5_d=s5_d, s5_w_glu=s5_w_glu, w_branch_a=w_branch_a,
             w_branch_b=w_branch_b, w_branch_c=w_branch_c, w_out=w_out, norm_cross=norm_cross,
             w_cq=w_cq, w_co=w_co, norm_ffn=norm_ffn, w_ffn_up=w_ffn_up, ffn_conv_w=ffn_conv_w,
             ffn_conv_b=ffn_conv_b, w_ffn_down=w_ffn_down, norm_final=norm_final)
    depth = w_in.shape[0]
    W = [_layer_weights(P, l) for l in range(depth)]
    batch, mem_len, _ = mem_prompt.shape
    mem2 = mem_prompt.reshape(batch * mem_len, D_MODEL)
    mem_k, mem_v = _memory_kv(mem2, norm_mem.reshape(depth, 1, D_MODEL), w_ck.astype(BF16), w_cv.astype(BF16))
    y_prompt, p_states = _prompt_trunk(x_prompt, mem_k, mem_v, W, bm_in=2048, tt_mix=256, gla_group=4, ml_group=4, tt_s5=256, bm=512,
                                       tt_ffn=512)
    p_mem_k = mem_k.reshape(depth, batch, mem_len, X_HEADS, X_DH)
    p_mem_v = mem_v.reshape(depth, batch, mem_len, X_HEADS, X_DH)
    y_sample, s_states = _sample_trunk(
        x_sample, cache_mem_k, cache_mem_v,
        (state_gla, state_mlstm_c, state_mlstm_n, state_mlstm_m, state_s5_re, state_s5_im, state_ffn_conv), W)
    return (y_prompt, y_sample, *p_states, p_mem_k, p_mem_v, *s_states)
```

```python
import functools

import jax
import jax.numpy as jnp
from jax import lax
from jax.experimental import pallas as pl
from jax.experimental.pallas import tpu as pltpu

F32 = jnp.float32
BF16 = jnp.bfloat16

D_MODEL = 1024
GLA_HEADS, GLA_DK, GLA_DV = 4, 64, 128
GLA_QK = GLA_HEADS * GLA_DK
GLA_V = GLA_HEADS * GLA_DV
GLA_LOWRANK = 16
GLA_TAU = 16.0
ML_HEADS, ML_D = 4, 128
ML_W = ML_HEADS * ML_D
S5_GROUP, S5_GROUPS, S5_P = 16, 32, 64
S5_WIDTH = S5_GROUP * S5_GROUPS
S5_MODES = S5_GROUPS * S5_P
S5_NB = 4
S5_GB = S5_GROUPS // S5_NB
S5_MB = S5_MODES // S5_NB
S5_UB = S5_WIDTH // S5_NB
MEM_LEN = 256
X_HEADS = 4
X_DH = D_MODEL // X_HEADS
D_FF = 2816
CONV_W = 3
EPS = 1e-6

CHUNK = 256
LANES = 128
FF_CHUNK = 256
N_FF_CHUNKS = D_FF // FF_CHUNK
Z_MAIN = 7168
Z_QK, Z_GV, Z_GR, Z_MQ, Z_MK, Z_MV, Z_MO, Z_SU = range(8)
Z_ZA, Z_ZB, Z_ZC = 4, 5, 6
SM_MI = GLA_LOWRANK
SM_MF = GLA_LOWRANK + ML_HEADS

GLA_SAFE_DECAY = 80.0

VMEM_LIMIT = 56 * 1024 * 1024


def _cparams(sem):
    return pltpu.CompilerParams(dimension_semantics=sem, vmem_limit_bytes=VMEM_LIMIT)


def _dot(a, b):
    return jnp.dot(a, b, preferred_element_type=F32)


def _dot_nt(a, b):
    return lax.dot_general(a, b, (((1,), (1,)), ((), ())), preferred_element_type=F32)


def _dot_tn(a, b):
    return lax.dot_general(a, b, (((0,), (0,)), ((), ())), preferred_element_type=F32)


def _sigmoid(x):
    return 1.0 / (1.0 + jnp.exp(-x))


def _log_sigmoid(x):
    return jnp.minimum(x, 0.0) - jnp.log(1.0 + jnp.exp(-jnp.abs(x)))


def _gelu_tanh(x):
    return 0.5 * x * (1.0 + jnp.tanh(0.7978845608028654 * (x + 0.044715 * x * x * x)))


def _rms_rows(x, g):
    r = lax.rsqrt(jnp.mean(x * x, axis=-1, keepdims=True) + EPS)
    return (x * r) * g


def _dot_exact01(a01, x):
    hi = x.astype(BF16)
    r1 = x - hi.astype(F32)
    mid = r1.astype(BF16)
    lo = (r1 - mid.astype(F32)).astype(BF16)
    return _dot(a01, hi) + _dot(a01, mid) + _dot(a01, lo)


def _dot_exact01_rhs(x, b01):
    hi = x.astype(BF16)
    r1 = x - hi.astype(F32)
    mid = r1.astype(BF16)
    lo = (r1 - mid.astype(F32)).astype(BF16)
    return _dot(hi, b01) + _dot(mid, b01) + _dot(lo, b01)


def _row_to_col(row, n):
    eye = (lax.broadcasted_iota(jnp.int32, (n, n), 0) == lax.broadcasted_iota(jnp.int32, (n, n), 1))
    return jnp.sum(jnp.where(eye, jnp.broadcast_to(row, (n, n)), 0.0), axis=1, keepdims=True)


def _norm_mm_small_kernel(x_ref, g_ref, w_ref, ws_ref, o_ref, os_ref, hn_ref):
    @pl.when(pl.program_id(1) == 0)
    def _():
        hn = _rms_rows(x_ref[...], g_ref[...]).astype(BF16)
        hn_ref[...] = hn
        os_ref[...] = _dot(hn, ws_ref[...])
    o_ref[...] = _dot(hn_ref[...], w_ref[...]).astype(o_ref.dtype)


def _norm_matmul(x, g, w, w_small, *, bm, bn, out_dtype=F32):
    n, d = x.shape
    c = w.shape[1]
    grid = (n // bm, c // bn)
    x_spec = pl.BlockSpec((bm, d), lambda i, j: (i, 0))
    g_spec = pl.BlockSpec((1, d), lambda i, j: (0, 0))
    w_spec = pl.BlockSpec((d, bn), lambda i, j: (0, j))
    o_spec = pl.BlockSpec((bm, bn), lambda i, j: (i, j))
    scratch = [pltpu.VMEM((bm, d), BF16)]
    cs = w_small.shape[1]
    return pl.pallas_call(
        _norm_mm_small_kernel, grid=grid,
        in_specs=[x_spec, g_spec, w_spec, pl.BlockSpec((d, cs), lambda i, j: (0, 0))],
        out_specs=[o_spec, pl.BlockSpec((bm, cs), lambda i, j: (i, 0))],
        out_shape=[jax.ShapeDtypeStruct((n, c), out_dtype), jax.ShapeDtypeStruct((n, cs), F32)],
        scratch_shapes=scratch,
        compiler_params=_cparams(("arbitrary", "arbitrary")), name="norm_matmul_in",
    )(x, g, w, w_small)


def _head_norm(oh, gn_h):
    r = lax.rsqrt(jnp.mean(oh * oh, axis=-1, keepdims=True) + EPS)
    return oh * r * gn_h


def _gla_log_decay(small, wa_ref, ba_ref):
    a_pre = _dot(small.astype(BF16), wa_ref[...]) + ba_ref[...]
    return _log_sigmoid(a_pre) * (1.0 / GLA_TAU)


def _gla_prompt_kernel(qk_ref, v_ref, gr_ref, sm_ref, wa_ref, ba_ref, gn_ref,
                       y_ref, st_ref, state_ref, b_ref, k_ref, *, n_chunks, group):
    t = pl.program_id(1)
    L = CHUNK

    @pl.when(t == 0)
    def _():
        state_ref[...] = jnp.zeros_like(state_ref)

    ri = lax.broadcasted_iota(jnp.int32, (L, L), 0)
    ci = lax.broadcasted_iota(jnp.int32, (L, L), 1)
    tril = jnp.where(ri >= ci, 1.0, 0.0).astype(BF16)
    rw = lax.broadcasted_iota(jnp.int32, (L, GLA_HEADS * L), 0)
    cw = lax.broadcasted_iota(jnp.int32, (L, GLA_HEADS * L), 1)
    causal_wide = (cw % L) <= rw
    kr = lax.broadcasted_iota(jnp.int32, (GLA_HEADS * L, GLA_QK), 0)
    kc = lax.broadcasted_iota(jnp.int32, (GLA_HEADS * L, GLA_QK), 1)
    kk_mask = (kr // L) == (kc // GLA_DK)
    vr = lax.broadcasted_iota(jnp.int32, (GLA_HEADS * L, GLA_V), 0)
    vc = lax.broadcasted_iota(jnp.int32, (GLA_HEADS * L, GLA_V), 1)
    vv_mask = (vr // L) == (vc // GLA_DV)
    sr = lax.broadcasted_iota(jnp.int32, (GLA_V, GLA_QK), 0)
    sc = lax.broadcasted_iota(jnp.int32, (GLA_V, GLA_QK), 1)
    st_mask = (sr // GLA_DV) == (sc // GLA_DK)
    hr = lax.broadcasted_iota(jnp.int32, (GLA_QK, LANES), 0)
    hc = lax.broadcasted_iota(jnp.int32, (GLA_QK, LANES), 1)
    head_ones = jnp.where((hr // GLA_DK) == hc, 1.0, 0.0).astype(BF16)
    pr = lax.broadcasted_iota(jnp.int32, (LANES, GLA_HEADS * L), 0)
    pc = lax.broadcasted_iota(jnp.int32, (LANES, GLA_HEADS * L), 1)

    def one(e, c):
        r0 = pl.multiple_of(c * L, L)
        rows = pl.ds(r0, L)
        qk = qk_ref[e, rows, :].astype(F32)
        q = qk[:, :GLA_QK] * (GLA_DK ** -0.5)
        k = qk[:, GLA_QK:]
        v16 = v_ref[e, rows, :].astype(BF16)
        g = _gla_log_decay(sm_ref[e, rows, :], wa_ref, ba_ref)
        b = _dot_exact01(tril, g)
        b_end = b[L - 1:L, :]
        qe = q * jnp.exp(b)
        k_dec = k * jnp.exp(b_end - b)
        qe16 = qe.astype(BF16)

        def fast_att(_):
            k_til = k * jnp.exp(-b)
            kk = jnp.where(kk_mask, jnp.concatenate([k_til] * GLA_HEADS, axis=0), 0.0)
            return _dot_nt(qe16, kk.astype(BF16))

        def direct_att(_):
            b_ref[...] = b
            k_ref[...] = k

            def col(j, acc):
                kj = k_ref[pl.ds(j, 1), :]
                bj = b_ref[pl.ds(j, 1), :]
                prod = q * kj * jnp.exp(jnp.minimum(b - bj, 0.0))
                red = _dot(prod.astype(BF16), head_ones)
                place = jnp.where((pc == pr * L + j) & (pr < GLA_HEADS), 1.0, 0.0).astype(BF16)
                return acc + _dot(red.astype(BF16), place)

            return lax.fori_loop(0, L, col, jnp.zeros((L, GLA_HEADS * L), F32))

        safe = jnp.max(-b_end) <= GLA_SAFE_DECAY
        att = lax.cond(safe, fast_att, direct_att, 0)
        att = jnp.where(causal_wide, att, 0.0)

        vv = jnp.where(vv_mask, jnp.concatenate([v16] * GLA_HEADS, axis=0), jnp.zeros((), BF16))
        st = state_ref[e]
        o = _dot(att.astype(BF16), vv) + _dot_nt(qe16, st.astype(BF16))
        upd = _dot_tn(v16, k_dec.astype(BF16))
        st_new = jnp.where(st_mask, st * jnp.exp(b_end) + upd, 0.0)

        gr = gr_ref[e, rows, :].astype(F32)
        ys = []
        for h in range(GLA_HEADS):
            sl = slice(h * GLA_DV, (h + 1) * GLA_DV)
            gate = gr[:, sl]
            ys.append((_head_norm(o[:, sl], gn_ref[:, sl]) * (gate * _sigmoid(gate))).astype(BF16))
        return jnp.concatenate(ys, axis=1), st_new

    def chunk(c, carry):
        res = [one(e, c) for e in range(group)]
        rows = pl.ds(pl.multiple_of(c * L, L), L)
        y_ref[:, rows, :] = jnp.stack([r[0] for r in res])
        state_ref[...] = jnp.stack([r[1] for r in res])
        return carry

    lax.fori_loop(0, n_chunks, chunk, 0)

    @pl.when(t == pl.num_programs(1) - 1)
    def _():
        st_ref[...] = state_ref[...]


def _gla_prompt(z3, zs3, wa, ba, gn, *, tt, group):
    batch, seq, _ = z3.shape
    blk = lambda j: pl.BlockSpec((group, tt, 512), lambda b, t: (b, t, j))
    const = lambda shape: pl.BlockSpec(shape, lambda b, t: (0,) * len(shape))
    return pl.pallas_call(
        functools.partial(_gla_prompt_kernel, n_chunks=tt // CHUNK, group=group),
        grid=(batch // group, seq // tt),
        in_specs=[blk(Z_QK), blk(Z_GV), blk(Z_GR),
                  pl.BlockSpec((group, tt, LANES), lambda b, t: (b, t, 0)),
                  const((LANES, GLA_QK)), const((1, GLA_QK)), const((1, GLA_V))],
        out_specs=[pl.BlockSpec((group, tt, GLA_V), lambda b, t: (b, t, 0)),
                   pl.BlockSpec((group, GLA_V, GLA_QK), lambda b, t: (b, 0, 0))],
        out_shape=[jax.ShapeDtypeStruct((batch, seq, GLA_V), BF16),
                   jax.ShapeDtypeStruct((batch, GLA_V, GLA_QK), F32)],
        scratch_shapes=[pltpu.VMEM((group, GLA_V, GLA_QK), F32), pltpu.VMEM((CHUNK, GLA_QK), F32),
                        pltpu.VMEM((CHUNK, GLA_QK), F32)],
        compiler_params=_cparams(("arbitrary", "arbitrary")), name="gla_prompt",
    )(z3, z3, z3, zs3, wa, ba, gn)


def _mlstm_prompt_kernel(q_ref, k_ref, v_ref, og_ref, sm_ref, gb_ref, gn_ref,
                         y_ref, c_out, n_out, m_out, c_ref, n_ref, m_ref, *, n_chunks, group):
    t = pl.program_id(1)
    L = CHUNK

    @pl.when(t == 0)
    def _():
        c_ref[...] = jnp.zeros_like(c_ref)
        n_ref[...] = jnp.zeros_like(n_ref)
        m_ref[...] = jnp.zeros_like(m_ref)

    ri = lax.broadcasted_iota(jnp.int32, (L, L), 0)
    ci = lax.broadcasted_iota(jnp.int32, (L, L), 1)
    lower = ri >= ci
    tril = jnp.where(lower, 1.0, 0.0).astype(BF16)
    triu = jnp.where(ri <= ci, 1.0, 0.0).astype(BF16)
    sr = lax.broadcasted_iota(jnp.int32, (LANES, ML_W), 0)
    sh = lax.broadcasted_iota(jnp.int32, (LANES, ML_W), 1) // ML_D
    sel_i = jnp.where(sr == SM_MI + sh, 1.0, 0.0).astype(BF16)
    sel_f = jnp.where(sr == SM_MF + sh, 1.0, 0.0).astype(BF16)

    def one(e, c):
        r0 = pl.multiple_of(c * L, L)
        rows = pl.ds(r0, L)
        sm = sm_ref[e, rows, :] + gb_ref[...]
        lf_all = _log_sigmoid(sm)
        sm_t = sm.T
        b_cols = _dot_exact01(tril, lf_all)
        b_rows = _dot_exact01_rhs(_log_sigmoid(sm_t), triu)
        b_wide = _dot_exact01_rhs(b_cols, sel_f)
        li_wide = _dot_exact01_rhs(sm, sel_i)
        ys, cs, ns, ms = [], [], [], []
        for h in range(ML_HEADS):
            sl = slice(h * ML_D, (h + 1) * ML_D)
            b_tok = b_wide[:, sl]
            li_tok = li_wide[:, sl]
            li_row = sm_t[SM_MI + h:SM_MI + h + 1, :]
            b_row = b_rows[SM_MF + h:SM_MF + h + 1, :]
            m_prev = m_ref[e, h:h + 1, :]
            w = jnp.where(lower, jnp.concatenate([b_tok] * (L // LANES), axis=1) - b_row + li_row, -jnp.inf)
            m_tok = jnp.maximum(b_tok + m_prev, jnp.max(w, axis=1, keepdims=True))
            a_inter = jnp.exp(b_tok + m_prev - m_tok)
            qh = q_ref[e, rows, sl].astype(F32) * (ML_D ** -0.5)
            kh16 = k_ref[e, rows, sl].astype(BF16)
            kh = kh16.astype(F32)
            qh16, vh16 = qh.astype(BF16), v_ref[e, rows, sl].astype(BF16)
            s = _dot_nt(qh16, kh16) * jnp.exp(w - jnp.concatenate([m_tok] * (L // LANES), axis=1))
            c_h = c_ref[e, sl, :]
            n_h = n_ref[e, h:h + 1, :]
            s16 = s.astype(BF16)
            row_sum = jnp.sum(s, axis=1, keepdims=True)
            q_n = _dot_nt(qh16, jnp.broadcast_to(n_h, (LANES, ML_D)).astype(BF16))
            num = _dot(s16, vh16) + a_inter * _dot(qh16, c_h.astype(BF16))
            den = row_sum + a_inter * q_n
            hh = num / jnp.maximum(jnp.abs(den), jnp.exp(-m_tok))
            og = og_ref[e, rows, sl].astype(F32)
            ys.append((_head_norm(hh, gn_ref[:, sl]) * _sigmoid(og)).astype(BF16))
            b_end = b_tok[L - 1:L, :]
            w_end = b_end - b_tok + li_tok
            m_new = jnp.maximum(b_end + m_prev, jnp.max(w_end, axis=0, keepdims=True))
            e_inter = jnp.exp(b_end + m_prev - m_new)
            kd = kh * jnp.exp(w_end - m_new)
            cs.append(e_inter * c_h + _dot_tn(kd.astype(BF16), vh16))
            ns.append(e_inter * n_h + jnp.sum(kd, axis=0, keepdims=True))
            ms.append(m_new)
        pad = [jnp.zeros((8 - ML_HEADS, LANES), F32)]
        return (jnp.concatenate(ys, axis=1), jnp.concatenate(cs, axis=0),
                jnp.concatenate(ns + pad, axis=0), jnp.concatenate(ms + pad, axis=0))

    def chunk(c, carry):
        res = [one(e, c) for e in range(group)]
        rows = pl.ds(pl.multiple_of(c * L, L), L)
        y_ref[:, rows, :] = jnp.stack([r[0] for r in res])
        c_ref[...] = jnp.stack([r[1] for r in res])
        n_ref[...] = jnp.stack([r[2] for r in res])
        m_ref[...] = jnp.stack([r[3] for r in res])
        return carry

    lax.fori_loop(0, n_chunks, chunk, 0)

    @pl.when(t == pl.num_programs(1) - 1)
    def _():
        c_out[...] = c_ref[...]
        n_out[...] = n_ref[...]
        m_out[...] = m_ref[...]


def _mlstm_prompt(z3, zs3, gbias, gn, *, tt, group):
    batch, seq, _ = z3.shape
    blk = lambda j: pl.BlockSpec((group, tt, 512), lambda b, t: (b, t, j))
    const = lambda shape: pl.BlockSpec(shape, lambda b, t: (0,) * len(shape))
    state = lambda shape: pl.BlockSpec((group,) + shape, lambda b, t: (b, 0, 0))
    return pl.pallas_call(
        functools.partial(_mlstm_prompt_kernel, n_chunks=tt // CHUNK, group=group),
        grid=(batch // group, seq // tt),
        in_specs=[blk(Z_MQ), blk(Z_MK), blk(Z_MV), blk(Z_MO),
                  pl.BlockSpec((group, tt, LANES), lambda b, t: (b, t, 0)),
                  const((1, LANES)), const((1, ML_W))],
        out_specs=[pl.BlockSpec((group, tt, ML_W), lambda b, t: (b, t, 0)),
                   state((ML_W, ML_D)), state((8, LANES)), state((8, LANES))],
        out_shape=[jax.ShapeDtypeStruct((batch, seq, ML_W), BF16),
                   jax.ShapeDtypeStruct((batch, ML_W, ML_D), F32),
                   jax.ShapeDtypeStruct((batch, 8, LANES), F32),
                   jax.ShapeDtypeStruct((batch, 8, LANES), F32)],
        scratch_shapes=[pltpu.VMEM((group, ML_W, ML_D), F32), pltpu.VMEM((group, 8, LANES), F32),
                        pltpu.VMEM((group, 8, LANES), F32)],
        compiler_params=_cparams(("arbitrary", "arbitrary")), name="mlstm_prompt",
    )(z3, z3, z3, z3, zs3, gbias, gn)


def _s5_prep_kernel(lre_ref, lim_ref, ldt_ref, ctr_ref, cti_ref, par_ref, cf_ref):
    lam_re = lre_ref[0]
    lam_im = lim_ref[0]
    dt = jnp.exp(ldt_ref[0])
    mag = jnp.exp(lam_re * dt)
    lb_re = mag * jnp.cos(lam_im * dt)
    lb_im = mag * jnp.sin(lam_im * dt)
    nr = lb_re - 1.0
    den = lam_re * lam_re + lam_im * lam_im
    f_re = (nr * lam_re + lb_im * lam_im) / den
    f_im = (lb_im * lam_re - nr * lam_im) / den
    par_ref[0] = jnp.concatenate([lb_re, lb_im, f_re, f_im, jnp.zeros((4, S5_MB), F32)], axis=0)
    ct_re = ctr_ref[0]
    ct_im = cti_ref[0]
    cf_ref[0, :, :S5_MB] = (ct_re * f_re - ct_im * f_im).astype(BF16)
    cf_ref[0, :, S5_MB:] = (-(ct_re * f_im + ct_im * f_re)).astype(BF16)


def _s5_prep(lam_re, lam_im, log_dt, ct_re, ct_im):
    vec = pl.BlockSpec((1, 1, S5_MB), lambda n: (n, 0, 0))
    mat = pl.BlockSpec((1, S5_UB, S5_MB), lambda n: (n, 0, 0))
    return pl.pallas_call(
        _s5_prep_kernel, grid=(S5_NB,),
        in_specs=[vec, vec, vec, mat, mat],
        out_specs=[pl.BlockSpec((1, 8, S5_MB), lambda n: (n, 0, 0)),
                   pl.BlockSpec((1, S5_UB, 2 * S5_MB), lambda n: (n, 0, 0))],
        out_shape=[jax.ShapeDtypeStruct((S5_NB, 8, S5_MB), F32),
                   jax.ShapeDtypeStruct((S5_NB, S5_UB, 2 * S5_MB), BF16)],
        compiler_params=_cparams(("arbitrary",)), name="s5_prep",
    )(lam_re, lam_im, log_dt, ct_re, ct_im)


def _s5_prompt_kernel(u_ref, w_ref, cf_ref, par_ref, y_ref, h_out, x_ref, h_ref, ub_ref, ut_ref, *, batch, tt):
    t_blk = pl.program_id(1)

    @pl.when(t_blk == 0)
    def _():
        h_ref[...] = jnp.zeros_like(h_ref)

    nlb = S5_MB // LANES
    ub_ref[...] = u_ref[...].astype(F32).reshape(batch * tt, S5_UB)

    def interleave(t, carry):
        ut_ref[pl.ds(pl.multiple_of(t * batch, batch), batch), :] = ub_ref[pl.ds(t, batch, stride=tt), :]
        return carry

    lax.fori_loop(0, tt, interleave, 0, unroll=8)
    x = _dot(ut_ref[...].astype(BF16), w_ref[0])
    for j in range(2 * nlb):
        x_ref[j] = x[:, j * LANES:(j + 1) * LANES]

    par = par_ref[0]
    lr = [jnp.broadcast_to(par[0:1, j * LANES:(j + 1) * LANES], (batch, LANES)) for j in range(nlb)]
    li = [jnp.broadcast_to(par[1:2, j * LANES:(j + 1) * LANES], (batch, LANES)) for j in range(nlb)]

    def step(t, carry):
        rows = pl.ds(pl.multiple_of(t * batch, batch), batch)
        new = []
        for j in range(nlb):
            hr, hi = carry[j], carry[nlb + j]
            nr = lr[j] * hr - li[j] * hi + x_ref[j, rows, :]
            ni = lr[j] * hi + li[j] * hr + x_ref[nlb + j, rows, :]
            x_ref[j, rows, :] = nr
            x_ref[nlb + j, rows, :] = ni
            new.append((nr, ni))
        return tuple(n[0] for n in new) + tuple(n[1] for n in new)

    h0 = tuple(h_ref[:, j * LANES:(j + 1) * LANES] for j in range(2 * nlb))
    hfin = lax.fori_loop(0, tt, step, h0, unroll=8)
    for j in range(2 * nlb):
        h_ref[:, j * LANES:(j + 1) * LANES] = hfin[j]
    hr = jnp.concatenate(hfin[:nlb], axis=1)
    hi = jnp.concatenate(hfin[nlb:], axis=1)

    hall = jnp.concatenate([x_ref[j].astype(BF16) for j in range(2 * nlb)], axis=1)
    ut_ref[...] = _dot_nt(hall, cf_ref[0])
    for b in range(batch):
        y_ref[b] = ut_ref[pl.ds(b, tt, stride=batch), :]

    @pl.when(t_blk == pl.num_programs(1) - 1)
    def _():
        fr = par[2:3, :]
        fi = par[3:4, :]
        h_out[0, :, :S5_MB] = fr * hr - fi * hi
        h_out[0, :, S5_MB:] = fr * hi + fi * hr


def _s5_prompt(z3, w_blk, cfold, par, *, batch, seq, tt):
    nt = seq // tt
    u_col0 = Z_SU * 512 // S5_UB
    return pl.pallas_call(
        functools.partial(_s5_prompt_kernel, batch=batch, tt=tt),
        grid=(S5_NB, nt),
        in_specs=[pl.BlockSpec((batch, tt, S5_UB), lambda n, t: (0, t, u_col0 + n)),
                  pl.BlockSpec((1, S5_UB, 2 * S5_MB), lambda n, t: (n, 0, 0)),
                  pl.BlockSpec((1, S5_UB, 2 * S5_MB), lambda n, t: (n, 0, 0)),
                  pl.BlockSpec((1, 8, S5_MB), lambda n, t: (n, 0, 0))],
        out_specs=[pl.BlockSpec((batch, tt, S5_UB), lambda n, t: (0, t, n)),
                   pl.BlockSpec((1, batch, 2 * S5_MB), lambda n, t: (n, 0, 0))],
        out_shape=[jax.ShapeDtypeStruct((batch, seq, S5_WIDTH), F32),
                   jax.ShapeDtypeStruct((S5_NB, batch, 2 * S5_MB), F32)],
        scratch_shapes=[pltpu.VMEM((2 * S5_MB // LANES, batch * tt, LANES), F32),
                        pltpu.VMEM((batch, 2 * S5_MB), F32),
                        pltpu.VMEM((batch * tt, S5_UB), F32), pltpu.VMEM((batch * tt, S5_UB), F32)],
        compiler_params=_cparams(("arbitrary", "arbitrary")), name="s5_prompt",
    )(z3, w_blk, cfold, par)


def _s5_decode_kernel(u_ref, w_ref, ctr_ref, cti_ref, par_ref, hr_ref, hi_ref, y_ref, hro_ref, hio_ref):
    x = _dot(u_ref[...].astype(BF16), w_ref[0])
    xr, xi = x[:, :S5_MB], x[:, S5_MB:]
    par = par_ref[0]
    lr, li, fr, fi = par[0:1, :], par[1:2, :], par[2:3, :], par[3:4, :]
    h0r, h0i = hr_ref[...], hi_ref[...]
    hr = lr * h0r - li * h0i + (fr * xr - fi * xi)
    hi = lr * h0i + li * h0r + (fr * xi + fi * xr)
    hro_ref[...] = hr
    hio_ref[...] = hi
    y_ref[...] = _dot_nt(hr.astype(BF16), ctr_ref[0]) - _dot_nt(hi.astype(BF16), cti_ref[0])


def _s5_decode(z, w_blk, ct_re, ct_im, par, h_re, h_im):
    rows = z.shape[0]
    u_col0 = Z_SU * 512 // S5_UB
    hspec = pl.BlockSpec((rows, S5_MB), lambda n: (0, n))
    mat = pl.BlockSpec((1, S5_UB, S5_MB), lambda n: (n, 0, 0))
    return pl.pallas_call(
        _s5_decode_kernel, grid=(S5_NB,),
        in_specs=[pl.BlockSpec((rows, S5_UB), lambda n: (0, u_col0 + n)),
                  pl.BlockSpec((1, S5_UB, 2 * S5_MB), lambda n: (n, 0, 0)),
                  mat, mat,
                  pl.BlockSpec((1, 8, S5_MB), lambda n: (n, 0, 0)), hspec, hspec],
        out_specs=[pl.BlockSpec((rows, S5_UB), lambda n: (0, n)), hspec, hspec],
        out_shape=[jax.ShapeDtypeStruct((rows, S5_WIDTH), F32),
                   jax.ShapeDtypeStruct((rows, S5_MODES), F32),
                   jax.ShapeDtypeStruct((rows, S5_MODES), F32)],
        compiler_params=_cparams(("arbitrary",)), name="s5_decode",
    )(z, w_blk, ct_re, ct_im, par, h_re, h_im)


def _gla_decode_kernel(qk_ref, v_ref, gr_ref, sm_ref, wa_ref, ba_ref, gn_ref, s_ref,
                       y_ref, so_ref, *, bb, fill_layer):
    eg = jnp.exp(_gla_log_decay(sm_ref[...], wa_ref, ba_ref))
    qk = qk_ref[...]
    q = qk[:, :GLA_QK] * (GLA_DK ** -0.5)
    k = qk[:, GLA_QK:]
    v = v_ref[...]

    def per_row(b):
        row = slice(b, b + 1)
        s_old = s_ref[b].reshape(GLA_QK, GLA_DV)
        v_rows = jnp.concatenate(
            [jnp.broadcast_to(v[row, h * GLA_DV:(h + 1) * GLA_DV], (GLA_DK, GLA_DV)) for h in range(GLA_HEADS)],
            axis=0)
        s_new = _row_to_col(eg[row], GLA_QK) * s_old + _row_to_col(k[row], GLA_QK) * v_rows
        qs = (_row_to_col(q[row], GLA_QK) * s_new).reshape(GLA_HEADS, GLA_DK, GLA_DV)
        o4 = jnp.sum(qs, axis=1)
        o_row = jnp.concatenate([o4[h:h + 1, :] for h in range(GLA_HEADS)], axis=1)
        return s_new.reshape(GLA_HEADS, GLA_DK, GLA_DV), o_row

    res = [per_row(b) for b in range(bb)]
    _store_layer_state(so_ref, jnp.stack([r[0] for r in res]), fill_layer)
    o = jnp.concatenate([r[1] for r in res], axis=0)
    gr = gr_ref[...]
    for h in range(GLA_HEADS):
        vs = slice(h * GLA_DV, (h + 1) * GLA_DV)
        gate = gr[:, vs]
        y_ref[:, vs] = (_head_norm(o[:, vs], gn_ref[:, vs]) * (gate * _sigmoid(gate))).astype(BF16)


def _store_layer_state(out_ref, new, layer):
    if layer is None:
        out_ref[...] = new
    else:
        for l in range(out_ref.shape[0]):
            out_ref[l] = new if l == layer else jnp.zeros_like(new)


def _skip_ref(kernel, pos):
    def wrapped(*refs):
        return kernel(*refs[:pos], *refs[pos + 1:])
    return wrapped


def _layer_out(kernel, in_specs, operands, out_specs, out_shapes, stacked_idx, prev, name, grid):
    aliases = {}
    if prev is not None:
        in_specs = in_specs + [pl.BlockSpec(memory_space=pl.ANY)]
        operands = operands + (prev,)
        aliases = {len(operands) - 1: stacked_idx}
        kernel = _skip_ref(kernel, len(operands) - 1)
    return pl.pallas_call(
        kernel, grid=grid, in_specs=in_specs, out_specs=out_specs, out_shape=out_shapes,
        input_output_aliases=aliases, compiler_params=_cparams(("arbitrary",)), name=name,
    )(*operands)


def _gla_decode(z, zs, wa, ba, gn, states, prev, *, layer, bb):
    rows = z.shape[0]
    blk = lambda j: pl.BlockSpec((bb, 512), lambda i: (i, j))
    const = lambda shape: pl.BlockSpec(shape, lambda i: (0,) * len(shape))
    sblock = (bb, GLA_HEADS, GLA_DK, GLA_DV)
    sspec = pl.BlockSpec((None,) + sblock, lambda i: (layer, i, 0, 0, 0))
    first = prev is None
    out_sspec = pl.BlockSpec((states.shape[0],) + sblock, lambda i: (0, i, 0, 0, 0)) if first else sspec
    return _layer_out(
        functools.partial(_gla_decode_kernel, bb=bb, fill_layer=layer if first else None),
        [blk(Z_QK), blk(Z_GV), blk(Z_GR), pl.BlockSpec((bb, LANES), lambda i: (i, 0)),
         const((LANES, GLA_QK)), const((1, GLA_QK)), const((1, GLA_V)), sspec],
        (z, z, z, zs, wa, ba, gn, states),
        [pl.BlockSpec((bb, GLA_V), lambda i: (i, 0)), out_sspec],
        [jax.ShapeDtypeStruct((rows, GLA_V), BF16), jax.ShapeDtypeStruct(states.shape, F32)],
        1, prev, "gla_decode", (rows // bb,))


def _mlstm_decode_kernel(q_ref, k_ref, v_ref, og_ref, sm_ref, gb_ref, gn_ref, c_ref, n_ref, m_ref,
                         y_ref, co_ref, no_ref, mo_ref, *, bb, fill_layer):
    li_blk = sm_ref[...] + gb_ref[...]
    lf_blk = _log_sigmoid(li_blk)
    m_blk = m_ref[...]
    q_blk = q_ref[...] * (ML_D ** -0.5)
    k_blk = k_ref[...]
    v_blk = v_ref[...]
    lane = lax.broadcasted_iota(jnp.int32, (1, LANES), 1)

    def per_row(b):
        row = slice(b, b + 1)
        li_r, lf_r, m_r = li_blk[row], lf_blk[row], m_blk[row]
        q_all, k_all, v_all = q_blk[row], k_blk[row], v_blk[row]
        n_all = n_ref[b]
        m_new_row = jnp.zeros((1, LANES), F32)
        h_heads, n_heads, c_heads = [], [], []
        for h in range(ML_HEADS):
            sl = slice(h * ML_D, (h + 1) * ML_D)
            li = li_r[:, SM_MI + h:SM_MI + h + 1]
            lf = lf_r[:, SM_MF + h:SM_MF + h + 1]
            m_prev = m_r[:, h:h + 1]
            m_new = jnp.maximum(lf + m_prev, li)
            a = jnp.exp(lf + m_prev - m_new)
            e = jnp.exp(li - m_new)
            k_r = k_all[:, sl]
            q_r = q_all[:, sl]
            c_new = a * c_ref[b, h] + (_row_to_col(k_r, ML_D) * e) * v_all[:, sl]
            n_new = a * n_all[h:h + 1, :] + e * k_r
            c_heads.append(c_new)
            n_heads.append(n_new)
            num = jnp.sum(_row_to_col(q_r, ML_D) * c_new, axis=0, keepdims=True)
            den = jnp.sum(q_r * n_new, axis=1, keepdims=True)
            h_heads.append(num / jnp.maximum(jnp.abs(den), jnp.exp(-m_new)))
            m_new_row = jnp.where(lane == h, m_new, m_new_row)
        return (jnp.stack(c_heads), jnp.concatenate(n_heads, axis=0),
                jnp.concatenate(h_heads, axis=1), m_new_row)

    res = [per_row(b) for b in range(bb)]
    _store_layer_state(co_ref, jnp.stack([r[0] for r in res]), fill_layer)
    no_ref[...] = jnp.stack([r[1] for r in res])
    mo_ref[...] = jnp.concatenate([r[3] for r in res], axis=0)
    hh = jnp.concatenate([r[2] for r in res], axis=0)
    og = og_ref[...]
    for h in range(ML_HEADS):
        sl = slice(h * ML_D, (h + 1) * ML_D)
        y_ref[:, sl] = (_head_norm(hh[:, sl], gn_ref[:, sl]) * _sigmoid(og[:, sl])).astype(BF16)


def _mlstm_decode(z, zs, gbias, gn, c_all, n_all, m_pad, prev, *, layer, bb):
    rows = z.shape[0]
    blk = lambda j: pl.BlockSpec((bb, 512), lambda i: (i, j))
    const = lambda shape: pl.BlockSpec(shape, lambda i: (0,) * len(shape))
    nblock = (bb, ML_HEADS, ML_D)
    cblock = (bb, ML_HEADS, ML_D, ML_D)
    cspec = pl.BlockSpec((None,) + cblock, lambda i: (layer, i, 0, 0, 0))
    mspec = pl.BlockSpec((bb, LANES), lambda i: (i, 0))
    first = prev is None
    out_cspec = pl.BlockSpec((c_all.shape[0],) + cblock, lambda i: (0, i, 0, 0, 0)) if first else cspec
    return _layer_out(
        functools.partial(_mlstm_decode_kernel, bb=bb, fill_layer=layer if first else None),
        [blk(Z_MQ), blk(Z_MK), blk(Z_MV), blk(Z_MO), mspec, const((1, LANES)), const((1, ML_W)),
         cspec, pl.BlockSpec((None,) + nblock, lambda i: (layer, i, 0, 0)), mspec],
        (z, z, z, z, zs, gbias, gn, c_all, n_all, m_pad),
        [pl.BlockSpec((bb, ML_W), lambda i: (i, 0)), out_cspec, pl.BlockSpec(nblock, lambda i: (i, 0, 0)), mspec],
        [jax.ShapeDtypeStruct((rows, ML_W), BF16), jax.ShapeDtypeStruct(c_all.shape, F32),
         jax.ShapeDtypeStruct(n_all.shape[1:], F32), jax.ShapeDtypeStruct((rows, LANES), F32)],
        1, prev, "mlstm_decode", (rows // bb,))


def _merged_residual(ya_ref, yb_ref, yp_ref, u_ref, za_ref, zb_ref, zc_ref, x_ref,
                     d_ref, wg_ref, wa_ref, wb_ref, wc_ref, wo_ref):
    yc = _gelu_tanh(yp_ref[...] + d_ref[...] * u_ref[...].astype(F32))
    yc = yc * _sigmoid(_dot(yc.astype(BF16), wg_ref[...]))
    m = _sigmoid(za_ref[...].astype(F32)) * _dot(ya_ref[...], wa_ref[...])
    m = m + _sigmoid(zb_ref[...].astype(F32)) * _dot(yb_ref[...], wb_ref[...])
    m = m + _sigmoid(zc_ref[...].astype(F32)) * _dot(yc.astype(BF16), wc_ref[...])
    return x_ref[...] + _dot(m.astype(BF16), wo_ref[...])


def _merge_kernel(*refs):
    refs[-1][...] = _merged_residual(*refs[:-1])


def _merge_specs(bm, row_map, const):
    r512 = pl.BlockSpec((bm, 512), lambda *g: (row_map(*g), 0))
    zblk = lambda j: pl.BlockSpec((bm, D_MODEL), lambda *g: (row_map(*g), j))
    return [r512, r512, r512, pl.BlockSpec((bm, 512), lambda *g: (row_map(*g), Z_SU)),
            zblk(Z_ZA), zblk(Z_ZB), zblk(Z_ZC), pl.BlockSpec((bm, D_MODEL), lambda *g: (row_map(*g), 0)),
            const((1, S5_WIDTH)), const((S5_WIDTH, S5_WIDTH)),
            const((GLA_V, D_MODEL)), const((ML_W, D_MODEL)), const((S5_WIDTH, D_MODEL)),
            const((D_MODEL, D_MODEL))]


def _merge(ya, yb, ypre, z, x, d, wg, wa, wb, wc, wo, *, bm):
    n = x.shape[0]
    const = lambda shape: pl.BlockSpec(shape, lambda i: (0, 0))
    return pl.pallas_call(
        _merge_kernel, grid=(n // bm,),
        in_specs=_merge_specs(bm, lambda i: i, const),
        out_specs=pl.BlockSpec((bm, D_MODEL), lambda i: (i, 0)),
        out_shape=jax.ShapeDtypeStruct((n, D_MODEL), F32),
        compiler_params=_cparams(("arbitrary",)), name="merge",
    )(ya, yb, ypre, z, z, z, z, x, d, wg, wa, wb, wc, wo)


def _cross_attend(x, g_ref, wq_ref, mk_ref, mv_ref, wo_ref):
    q = _dot(_rms_rows(x, g_ref[...]).astype(BF16), wq_ref[...])
    heads = []
    for h in range(X_HEADS):
        sl = slice(h * X_DH, (h + 1) * X_DH)
        s = _dot_nt(q[:, sl].astype(BF16), mk_ref[:, sl].astype(BF16)) * (X_DH ** -0.5)
        s = s - jnp.max(s, axis=-1, keepdims=True)
        p = jnp.exp(s)
        p = p / jnp.sum(p, axis=-1, keepdims=True)
        heads.append(_dot(p.astype(BF16), mv_ref[:, sl].astype(BF16)))
    o = jnp.concatenate(heads, axis=-1)
    return x + _dot(o.astype(BF16), wo_ref[...])


N_MERGE_IN = 14


def _merge_cross_kernel(*refs):
    x1 = _merged_residual(*refs[:N_MERGE_IN])
    refs[-1][...] = _cross_attend(x1, *refs[N_MERGE_IN:-1])


def _merge_cross(ya, yb, ypre, z, x, d, wg, wa, wb, wc, wo, g, wq, mem_k, mem_v, wco,
                 *, layer, batch, seq, bm):
    nt = seq // bm
    const = lambda shape: pl.BlockSpec(shape, lambda b, t: (0, 0), pipeline_mode=pl.Buffered(1))
    mspec = pl.BlockSpec((None, MEM_LEN, D_MODEL), lambda b, t: (layer, b, 0))
    xspec = pl.BlockSpec((bm, D_MODEL), lambda b, t: (b * nt + t, 0))
    return pl.pallas_call(
        _merge_cross_kernel, grid=(batch, nt),
        in_specs=_merge_specs(bm, lambda b, t: b * nt + t, const)
        + [const((1, D_MODEL)), const((D_MODEL, D_MODEL)), mspec, mspec, const((D_MODEL, D_MODEL))],
        out_specs=xspec, out_shape=jax.ShapeDtypeStruct(x.shape, F32),
        compiler_params=_cparams(("arbitrary", "arbitrary")), name="merge_cross",
    )(ya, yb, ypre, z, z, z, z, x, d, wg, wa, wb, wc, wo, g, wq, mem_k, mem_v, wco)


def _memory_kv_kernel(x_ref, g_ref, wk_ref, wv_ref, k_ref, v_ref, hn_ref):
    @pl.when(pl.program_id(1) == 0)
    def _():
        hn_ref[...] = _rms_rows(x_ref[...], g_ref[...]).astype(BF16)
    k_ref[...] = _dot(hn_ref[...], wk_ref[...])
    v_ref[...] = _dot(hn_ref[...], wv_ref[...])


def _memory_kv(mem, g, wk, wv, *, bn=512):
    rows, d = mem.shape
    depth = wk.shape[0]
    wspec = pl.BlockSpec((None, d, bn), lambda l, j: (l, 0, j))
    ospec = pl.BlockSpec((None, rows, bn), lambda l, j: (l, 0, j))
    out = jax.ShapeDtypeStruct((depth, rows, d), F32)
    return pl.pallas_call(
        _memory_kv_kernel, grid=(depth, d // bn),
        in_specs=[pl.BlockSpec((rows, d), lambda l, j: (0, 0)),
                  pl.BlockSpec((None, 1, d), lambda l, j: (l, 0, 0)), wspec, wspec],
        out_specs=[ospec, ospec], out_shape=[out, out],
        scratch_shapes=[pltpu.VMEM((rows, d), BF16)],
        compiler_params=_cparams(("arbitrary", "arbitrary")), name="memory_kv",
    )(mem, g, wk, wv)


def _cross_decode_kernel(x_ref, g_ref, wq_ref, mk_ref, mv_ref, wo_ref, o_ref, *, bb):
    x = x_ref[...]
    q = _dot(_rms_rows(x, g_ref[...]).astype(BF16), wq_ref[...])
    rows = []
    for b in range(bb):
        q4 = jnp.concatenate([q[b:b + 1, h * X_DH:(h + 1) * X_DH] for h in range(X_HEADS)], axis=0)
        s = jnp.sum(mk_ref[0, b] * q4[None], axis=-1, keepdims=True) * (X_DH ** -0.5)
        p = jnp.exp(s - jnp.max(s, axis=0, keepdims=True))
        p = p / jnp.sum(p, axis=0, keepdims=True)
        o4 = jnp.sum(p * mv_ref[0, b], axis=0)
        rows.append(jnp.concatenate([o4[h:h + 1, :] for h in range(X_HEADS)], axis=1))
    att = jnp.concatenate(rows, axis=0)
    o_ref[...] = x + _dot(att.astype(BF16), wo_ref[...])


def _cross_decode(x, g, wq, cache_k, cache_v, wo, *, layer, bb):
    rows = x.shape[0]
    xspec = pl.BlockSpec((bb, D_MODEL), lambda i: (i, 0))
    const = lambda shape: pl.BlockSpec(shape, lambda i: (0, 0))
    mspec = pl.BlockSpec((1, bb, MEM_LEN, X_HEADS, X_DH), lambda i: (layer, i, 0, 0, 0))
    return pl.pallas_call(
        functools.partial(_cross_decode_kernel, bb=bb), grid=(rows // bb,),
        in_specs=[xspec, const((1, D_MODEL)), const((D_MODEL, D_MODEL)), mspec, mspec,
                  const((D_MODEL, D_MODEL))],
        out_specs=xspec, out_shape=jax.ShapeDtypeStruct(x.shape, F32),
        compiler_params=_cparams(("arbitrary",)), name="cross_decode",
    )(x, g, wq, cache_k, cache_v, wo)


def _ffn_prompt_kernel(x_ref, g_ref, wup_ref, cw_ref, cb_ref, wdn_ref, gf_ref,
                       o_ref, st_ref, hn_ref, halo_ref, up_ref, act_ref, *, tt, final_norm):
    t = pl.program_id(1)

    @pl.when(t == 0)
    def _():
        halo_ref[...] = jnp.zeros_like(halo_ref)

    x = x_ref[...]
    hn_ref[...] = _rms_rows(x, g_ref[...]).astype(BF16)
    def project(c):
        slot = c % 2
        for part, col0 in enumerate((c * FF_CHUNK, D_FF + c * FF_CHUNK)):
            cols = slice(col0, col0 + FF_CHUNK)
            lanes = slice(part * FF_CHUNK, (part + 1) * FF_CHUNK)
            up = _dot(hn_ref[...], wup_ref[:, cols])
            up_ref[slot, 6:8, lanes] = halo_ref[0:2, cols]
            up_ref[slot, 8:8 + tt, lanes] = up
            halo_ref[0:2, cols] = up[tt - 2:tt, :]

    def gated(c):
        slot = c % 2
        a_cols = slice(c * FF_CHUNK, (c + 1) * FF_CHUNK)
        g_cols = slice(D_FF + c * FF_CHUNK, D_FF + (c + 1) * FF_CHUNK)
        cw = jnp.concatenate([cw_ref[:, a_cols], cw_ref[:, g_cols]], axis=1)
        cb = jnp.concatenate([cb_ref[:, a_cols], cb_ref[:, g_cols]], axis=1)
        cv = (up_ref[slot, 6:6 + tt, :] * cw[0:1] + up_ref[slot, 7:7 + tt, :] * cw[1:2]
              + up_ref[slot, 8:8 + tt, :] * cw[2:3] + cb)
        gt = cv[:, FF_CHUNK:]
        return (cv[:, :FF_CHUNK] * (gt * _sigmoid(gt))).astype(BF16)

    project(0)
    for c in range(N_FF_CHUNKS):
        if c + 1 < N_FF_CHUNKS:
            project(c + 1)
        act_ref[:, c * FF_CHUNK:(c + 1) * FF_CHUNK] = gated(c)
    acc = x + _dot(act_ref[...], wdn_ref[...])
    if final_norm:
        acc = _rms_rows(acc, gf_ref[...])
    o_ref[...] = acc

    @pl.when(t == pl.num_programs(1) - 1)
    def _():
        st_ref[0] = halo_ref[0:2, :]


def _ffn_prompt(x, g, wup, cw, cb, wdn, gf, *, batch, seq, tt, final_norm):
    nt = seq // tt
    xspec = pl.BlockSpec((tt, D_MODEL), lambda b, t: (b * nt + t, 0))
    const = lambda shape: pl.BlockSpec(shape, lambda b, t: (0, 0), pipeline_mode=pl.Buffered(1))
    return pl.pallas_call(
        functools.partial(_ffn_prompt_kernel, tt=tt, final_norm=final_norm), grid=(batch, nt),
        in_specs=[xspec, const((1, D_MODEL)), const((D_MODEL, 2 * D_FF)), const((CONV_W, 2 * D_FF)),
                  const((1, 2 * D_FF)), const((D_FF, D_MODEL)), const((1, D_MODEL))],
        out_specs=[xspec, pl.BlockSpec((1, CONV_W - 1, 2 * D_FF), lambda b, t: (b, 0, 0))],
        out_shape=[jax.ShapeDtypeStruct(x.shape, F32),
                   jax.ShapeDtypeStruct((batch, CONV_W - 1, 2 * D_FF), F32)],
        scratch_shapes=[pltpu.VMEM((tt, D_MODEL), BF16), pltpu.VMEM((8, 2 * D_FF), F32),
                        pltpu.VMEM((2, tt + 8, 2 * FF_CHUNK), F32), pltpu.VMEM((tt, D_FF), BF16)],
        compiler_params=_cparams(("arbitrary", "arbitrary")), name="ffn_prompt",
    )(x, g, wup, cw, cb, wdn, gf)


def _ffn_decode_kernel(x_ref, g_ref, wa_ref, wg_ref, cwa_ref, cwg_ref, cba_ref, cbg_ref,
                       b0a_ref, b0g_ref, b1a_ref, b1g_ref, wdn_ref, gf_ref,
                       o_ref, upa_ref, upg_ref, hn_ref, acc_ref, *, final_norm):
    c = pl.program_id(0)

    @pl.when(c == 0)
    def _():
        x = x_ref[...]
        hn_ref[...] = _rms_rows(x, g_ref[...]).astype(BF16)
        acc_ref[...] = x

    def conv(w_ref, cw_ref, cb_ref, b0_ref, b1_ref, up_ref):
        up = _dot(hn_ref[...], w_ref[...])
        up_ref[...] = up
        return b0_ref[...] * cw_ref[0:1, :] + b1_ref[...] * cw_ref[1:2, :] + up * cw_ref[2:3, :] + cb_ref[...]

    a = conv(wa_ref, cwa_ref, cba_ref, b0a_ref, b1a_ref, upa_ref)
    gt = conv(wg_ref, cwg_ref, cbg_ref, b0g_ref, b1g_ref, upg_ref)
    act = a * (gt * _sigmoid(gt))
    acc_ref[...] += _dot(act.astype(BF16), wdn_ref[...])

    @pl.when(c == pl.num_programs(0) - 1)
    def _():
        acc = acc_ref[...]
        if final_norm:
            acc = _rms_rows(acc, gf_ref[...])
        o_ref[...] = acc


def _ffn_decode(x, g, wup, cw, cb, wdn, gf, buf0, buf1, *, final_norm):
    rows = x.shape[0]
    fc = FF_CHUNK
    full = lambda shape: pl.BlockSpec(shape, lambda c: (0, 0))
    a_col = lambda r: pl.BlockSpec((r, fc), lambda c: (0, c))
    g_col = lambda r: pl.BlockSpec((r, fc), lambda c: (0, N_FF_CHUNKS + c))
    return pl.pallas_call(
        functools.partial(_ffn_decode_kernel, final_norm=final_norm), grid=(N_FF_CHUNKS,),
        in_specs=[full((rows, D_MODEL)), full((1, D_MODEL)),
                  a_col(D_MODEL), g_col(D_MODEL), a_col(CONV_W), g_col(CONV_W), a_col(1), g_col(1),
                  a_col(rows), g_col(rows), a_col(rows), g_col(rows),
                  pl.BlockSpec((fc, D_MODEL), lambda c: (c, 0)), full((1, D_MODEL))],
        out_specs=[full((rows, D_MODEL)), a_col(rows), a_col(rows)],
        out_shape=[jax.ShapeDtypeStruct(x.shape, F32),
                   jax.ShapeDtypeStruct((rows, D_FF), F32),
                   jax.ShapeDtypeStruct((rows, D_FF), F32)],
        scratch_shapes=[pltpu.VMEM((rows, D_MODEL), BF16), pltpu.VMEM((rows, D_MODEL), F32)],
        compiler_params=_cparams(("arbitrary",)), name="ffn_decode",
    )(x, g, wup, wup, cw, cw, cb, cb, buf0, buf0, buf1, buf1, wdn, gf)


def _layer_weights(P, l):
    w_in = P['w_in'][l]
    w_main = jnp.concatenate([w_in[:, 0:1536], w_in[:, 1552:3600], w_in[:, 3608:7192]], axis=1).astype(BF16)
    w_small = jnp.concatenate([w_in[:, 1536:1552], w_in[:, 3600:3608],
                               jnp.zeros((D_MODEL, LANES - GLA_LOWRANK - 2 * ML_HEADS), F32)], axis=1).astype(BF16)
    wa = jnp.zeros((LANES, GLA_QK), F32).at[:GLA_LOWRANK].set(P['w_gla_alpha'][l]).astype(BF16)
    gbias = jnp.zeros((1, LANES), F32)
    gbias = gbias.at[0, SM_MI:SM_MI + ML_HEADS].set(P['b_mlstm_i'][l])
    gbias = gbias.at[0, SM_MF:SM_MF + ML_HEADS].set(P['b_mlstm_f'][l])
    eye = jnp.eye(S5_GB, dtype=F32)
    w_re = jnp.einsum('ngpc,gh->ngchp', P['s5_b_re'][l].reshape(S5_NB, S5_GB, S5_P, S5_GROUP), eye)
    w_im = jnp.einsum('ngpc,gh->ngchp', P['s5_b_im'][l].reshape(S5_NB, S5_GB, S5_P, S5_GROUP), eye)
    w_blk = jnp.concatenate([w_re.reshape(S5_NB, S5_UB, S5_MB), w_im.reshape(S5_NB, S5_UB, S5_MB)],
                            axis=-1).astype(BF16)
    ct_re = jnp.einsum('ngcp,gh->ngchp', P['s5_c_re'][l].reshape(S5_NB, S5_GB, S5_GROUP, S5_P), eye)
    ct_im = jnp.einsum('ngcp,gh->ngchp', P['s5_c_im'][l].reshape(S5_NB, S5_GB, S5_GROUP, S5_P), eye)
    ct_re = ct_re.reshape(S5_NB, S5_UB, S5_MB)
    ct_im = ct_im.reshape(S5_NB, S5_UB, S5_MB)
    lam_re = P['s5_lam_re'][l].reshape(S5_NB, 1, S5_MB)
    lam_im = P['s5_lam_im'][l].reshape(S5_NB, 1, S5_MB)
    log_dt = jnp.broadcast_to(P['s5_log_dt'][l][:, None], (S5_GROUPS, S5_P)).reshape(S5_NB, 1, S5_MB)
    par, cfold = _s5_prep(lam_re, lam_im, log_dt, ct_re, ct_im)
    row = lambda a: a.reshape(1, -1)
    return dict(
        norm_mix=row(P['norm_mix'][l]), w_main=w_main, w_small=w_small, wa=wa,
        ba=row(P['b_gla_alpha'][l]), gla_norm=row(P['gla_head_norm'][l]), gbias=gbias,
        ml_norm=row(P['mlstm_head_norm'][l]), w_blk=w_blk, ct_re=ct_re.astype(BF16),
        ct_im=ct_im.astype(BF16), par=par, cfold=cfold, s5_d=row(P['s5_d'][l]),
        w_glu=P['s5_w_glu'][l].astype(BF16), w_a=P['w_branch_a'][l].astype(BF16),
        w_b=P['w_branch_b'][l].astype(BF16), w_c=P['w_branch_c'][l].astype(BF16),
        w_out=P['w_out'][l].astype(BF16), norm_cross=row(P['norm_cross'][l]),
        w_cq=P['w_cq'][l].astype(BF16), w_co=P['w_co'][l].astype(BF16),
        norm_ffn=row(P['norm_ffn'][l]), w_up=P['w_ffn_up'][l].astype(BF16),
        conv_w=P['ffn_conv_w'][l], conv_b=row(P['ffn_conv_b'][l]),
        w_down=P['w_ffn_down'][l].astype(BF16), norm_final=row(P['norm_final']),
    )


def _prompt_trunk(x_prompt, mem_k, mem_v, W, *, bm_in, tt_mix, gla_group, ml_group, tt_s5, bm, tt_ffn):
    batch, seq, _ = x_prompt.shape
    depth = len(W)
    x = x_prompt.reshape(batch * seq, D_MODEL)
    outs = []
    for l, w in enumerate(W):
        z, zs = _norm_matmul(x, w['norm_mix'], w['w_main'], w['w_small'], bm=bm_in, bn=1024, out_dtype=BF16)
        z3 = z.reshape(batch, seq, Z_MAIN)
        zs3 = zs.reshape(batch, seq, LANES)
        ya, st = _gla_prompt(z3, zs3, w['wa'], w['ba'], w['gla_norm'], tt=tt_mix, group=gla_group)
        yb, c, n, m = _mlstm_prompt(z3, zs3, w['gbias'], w['ml_norm'], tt=tt_mix, group=ml_group)
        ypre, hfin = _s5_prompt(z3, w['w_blk'], w['cfold'], w['par'], batch=batch, seq=seq, tt=tt_s5)
        x = _merge_cross(ya.reshape(batch * seq, GLA_V), yb.reshape(batch * seq, ML_W),
                         ypre.reshape(batch * seq, S5_WIDTH), z, x, w['s5_d'], w['w_glu'],
                         w['w_a'], w['w_b'], w['w_c'], w['w_out'],
                         w['norm_cross'], w['w_cq'], mem_k, mem_v, w['w_co'],
                         layer=l, batch=batch, seq=seq, bm=bm)
        x, conv = _ffn_prompt(x, w['norm_ffn'], w['w_up'], w['conv_w'], w['conv_b'], w['w_down'],
                              w['norm_final'], batch=batch, seq=seq, tt=tt_ffn,
                              final_norm=(l == depth - 1))
        st4 = st.reshape(batch, GLA_HEADS, GLA_DV, GLA_HEADS, GLA_DK)
        gla = jnp.stack([st4[:, h, :, h, :] for h in range(GLA_HEADS)], axis=1).transpose(0, 1, 3, 2)
        h4 = hfin.reshape(S5_NB, batch, 2, S5_GB, S5_P)
        s5_re = h4[:, :, 0].transpose(1, 0, 2, 3).reshape(batch, S5_GROUPS, S5_P)
        s5_im = h4[:, :, 1].transpose(1, 0, 2, 3).reshape(batch, S5_GROUPS, S5_P)
        outs.append((gla, c.reshape(batch, ML_HEADS, ML_D, ML_D), n[:, :ML_HEADS, :],
                     m[:, :ML_HEADS, 0], s5_re, s5_im, conv))
    stacked = [jnp.stack([outs[l][i] for l in range(depth)]) for i in range(7)]
    return x.reshape(batch, seq, D_MODEL), stacked


def _sample_trunk(x_sample, cache_k, cache_v, states, W):
    rows = x_sample.shape[0]
    depth = len(W)
    x = x_sample.reshape(rows, D_MODEL)
    s_gla, s_c, s_n, s_m, s_re, s_im, s_conv = states
    outs = []
    gla_all = c_all = None
    for l, w in enumerate(W):
        z, zs = _norm_matmul(x, w['norm_mix'], w['w_main'], w['w_small'], bm=rows, bn=1024)
        ya, gla_all = _gla_decode(z, zs, w['wa'], w['ba'], w['gla_norm'], s_gla, gla_all, layer=l, bb=16)
        m_pad = jnp.pad(s_m[l], ((0, 0), (0, LANES - ML_HEADS)))
        yb, c_all, n, m = _mlstm_decode(z, zs, w['gbias'], w['ml_norm'], s_c, s_n, m_pad, c_all,
                                        layer=l, bb=16)
        ypre, h_re, h_im = _s5_decode(z, w['w_blk'], w['ct_re'], w['ct_im'], w['par'],
                                      s_re[l].reshape(rows, S5_MODES), s_im[l].reshape(rows, S5_MODES))
        x = _merge(ya, yb, ypre, z, x, w['s5_d'], w['w_glu'], w['w_a'], w['w_b'], w['w_c'], w['w_out'],
                   bm=rows)
        x = _cross_decode(x, w['norm_cross'], w['w_cq'], cache_k, cache_v, w['w_co'], layer=l, bb=8)
        x, up_a, up_g = _ffn_decode(x, w['norm_ffn'], w['w_up'], w['conv_w'], w['conv_b'], w['w_down'],
                                    w['norm_final'], s_conv[l][:, 0, :], s_conv[l][:, 1, :],
                                    final_norm=(l == depth - 1))
        conv = jnp.stack([s_conv[l][:, 1, :], jnp.concatenate([up_a, up_g], axis=1)], axis=1)
        outs.append((n, m[:, :ML_HEADS], h_re.reshape(rows, S5_GROUPS, S5_P),
                     h_im.reshape(rows, S5_GROUPS, S5_P), conv))
    stacked = [jnp.stack([outs[l][i] for l in range(depth)]) for i in range(5)]
    return x.reshape(rows, 1, D_MODEL), [gla_all, c_all] + stacked


def kernel(x_prompt, x_sample, mem_prompt, cache_mem_k, cache_mem_v, state_gla, state_mlstm_c, state_mlstm_n, state_mlstm_m, state_s5_re, state_s5_im, state_ffn_conv, norm_mix, w_in, w_gla_alpha, b_gla_alpha, gla_head_norm, b_mlstm_i, b_mlstm_f, mlstm_head_norm, s5_lam_re, s5_lam_im, s5_log_dt, s5_b_re, s5_b_im, s5_c_re, s5_c_im, s5_d, s5_w_glu, w_branch_a, w_branch_b, w_branch_c, w_out, norm_cross, norm_mem, w_cq, w_ck, w_cv, w_co, norm_ffn, w_ffn_up, ffn_conv_w, ffn_conv_b, w_ffn_down, norm_final):
    P = dict(norm_mix=norm_mix, w_in=w_in, w_gla_alpha=w_gla_alpha, b_gla_alpha=b_gla_alpha,
             gla_head_norm=gla_head_norm, b_mlstm_i=b_mlstm_i, b_mlstm_f=b_mlstm_f,
             mlstm_head_norm=mlstm_head_norm, s5_lam_re=s5_lam_re, s5_lam_im=s5_lam_im,
             s5_log_dt=s5_log_dt, s5_b_re=s5_b_re, s5_b_im=s5_b_im, s5_c_re=s5_c_re,
             s5_c_im=s5_c_im, s5_d=s5_d, s5_w_glu=s5_w_glu, w_branch_a=w_branch_a,
             w_branch_b=w_branch_b, w_branch_c=w_branch_c, w_out=w_out, norm_cross=norm_cross,
             w_cq=w_cq, w_co=w_co, norm_ffn=norm_ffn, w_ffn_up=w_ffn_up, ffn_conv_w=ffn_conv_w,
             ffn_conv_b=ffn_conv_b, w_ffn_down=w_ffn_down, norm_final=norm_final)
    depth = w_in.shape[0]
    W = [_layer_weights(P, l) for l in range(depth)]
    batch, mem_len, _ = mem_prompt.shape
    mem2 = mem_prompt.reshape(batch * mem_len, D_MODEL)
    mem_k, mem_v = _memory_kv(mem2, norm_mem.reshape(depth, 1, D_MODEL), w_ck.astype(BF16), w_cv.astype(BF16))
    y_prompt, p_states = _prompt_trunk(x_prompt, mem_k, mem_v, W, bm_in=2048, tt_mix=256, gla_group=4, ml_group=4, tt_s5=256, bm=512,
                                       tt_ffn=512)
    p_mem_k = mem_k.reshape(depth, batch, mem_len, X_HEADS, X_DH)
    p_mem_v = mem_v.reshape(depth, batch, mem_len, X_HEADS, X_DH)
    y_sample, s_states = _sample_trunk(
        x_sample, cache_mem_k, cache_mem_v,
        (state_gla, state_mlstm_c, state_mlstm_n, state_mlstm_m, state_s5_re, state_s5_im, state_ffn_conv), W)
    return (y_prompt, y_sample, *p_states, p_mem_k, p_mem_v, *s_states)
```

```python
import functools

import jax
import jax.numpy as jnp
from jax import lax
from jax.experimental import pallas as pl
from jax.experimental.pallas import tpu as pltpu

F32 = jnp.float32
BF16 = jnp.bfloat16

D_MODEL = 1024
GLA_HEADS, GLA_DK, GLA_DV = 4, 64, 128
GLA_QK = GLA_HEADS * GLA_DK
GLA_V = GLA_HEADS * GLA_DV
GLA_LOWRANK = 16
GLA_TAU = 16.0
ML_HEADS, ML_D = 4, 128
ML_W = ML_HEADS * ML_D
S5_GROUP, S5_GROUPS, S5_P = 16, 32, 64
S5_WIDTH = S5_GROUP * S5_GROUPS
S5_MODES = S5_GROUPS * S5_P
S5_NB = 4
S5_GB = S5_GROUPS // S5_NB
S5_MB = S5_MODES // S5_NB
S5_UB = S5_WIDTH // S5_NB
MEM_LEN = 256
X_HEADS = 4
X_DH = D_MODEL // X_HEADS
D_FF = 2816
CONV_W = 3
EPS = 1e-6

CHUNK = 256
LANES = 128
FF_CHUNK = 256
N_FF_CHUNKS = D_FF // FF_CHUNK
FF_ROW_SLAB = 128
Z_MAIN = 7168
Z_QK, Z_GV, Z_GR, Z_MQ, Z_MK, Z_MV, Z_MO, Z_SU = range(8)
Z_ZA, Z_ZB, Z_ZC = 4, 5, 6
SM_MI = GLA_LOWRANK
SM_MF = GLA_LOWRANK + ML_HEADS

GLA_SAFE_DECAY = 80.0

VMEM_LIMIT = 56 * 1024 * 1024


def _cparams(sem):
    return pltpu.CompilerParams(dimension_semantics=sem, vmem_limit_bytes=VMEM_LIMIT)


def _dot(a, b):
    return jnp.dot(a, b, preferred_element_type=F32)


def _dot_nt(a, b):
    return lax.dot_general(a, b, (((1,), (1,)), ((), ())), preferred_element_type=F32)


def _dot_tn(a, b):
    return lax.dot_general(a, b, (((0,), (0,)), ((), ())), preferred_element_type=F32)


def _sigmoid(x):
    return 1.0 / (1.0 + jnp.exp(-x))


def _log_sigmoid(x):
    return jnp.minimum(x, 0.0) - jnp.log(1.0 + jnp.exp(-jnp.abs(x)))


def _gelu_tanh(x):
    return 0.5 * x * (1.0 + jnp.tanh(0.7978845608028654 * (x + 0.044715 * x * x * x)))


def _rms_rows(x, g):
    r = lax.rsqrt(jnp.mean(x * x, axis=-1, keepdims=True) + EPS)
    return (x * r) * g


def _dot_exact01(a01, x):
    hi = x.astype(BF16)
    r1 = x - hi.astype(F32)
    mid = r1.astype(BF16)
    lo = (r1 - mid.astype(F32)).astype(BF16)
    return _dot(a01, hi) + _dot(a01, mid) + _dot(a01, lo)


def _dot_exact01_rhs(x, b01):
    hi = x.astype(BF16)
    r1 = x - hi.astype(F32)
    mid = r1.astype(BF16)
    lo = (r1 - mid.astype(F32)).astype(BF16)
    return _dot(hi, b01) + _dot(mid, b01) + _dot(lo, b01)


def _row_to_col(row, n):
    eye = (lax.broadcasted_iota(jnp.int32, (n, n), 0) == lax.broadcasted_iota(jnp.int32, (n, n), 1))
    return jnp.sum(jnp.where(eye, jnp.broadcast_to(row, (n, n)), 0.0), axis=1, keepdims=True)


def _norm_mm_small_kernel(x_ref, g_ref, w_ref, ws_ref, o_ref, os_ref, hn_ref):
    @pl.when(pl.program_id(1) == 0)
    def _():
        hn = _rms_rows(x_ref[...], g_ref[...]).astype(BF16)
        hn_ref[...] = hn
        os_ref[...] = _dot(hn, ws_ref[...])
    o_ref[...] = _dot(hn_ref[...], w_ref[...]).astype(o_ref.dtype)


def _norm_matmul(x, g, w, w_small, *, bm, bn, out_dtype=F32):
    n, d = x.shape
    c = w.shape[1]
    grid = (n // bm, c // bn)
    x_spec = pl.BlockSpec((bm, d), lambda i, j: (i, 0))
    g_spec = pl.BlockSpec((1, d), lambda i, j: (0, 0))
    w_spec = pl.BlockSpec((d, bn), lambda i, j: (0, j))
    o_spec = pl.BlockSpec((bm, bn), lambda i, j: (i, j))
    scratch = [pltpu.VMEM((bm, d), BF16)]
    cs = w_small.shape[1]
    return pl.pallas_call(
        _norm_mm_small_kernel, grid=grid,
        in_specs=[x_spec, g_spec, w_spec, pl.BlockSpec((d, cs), lambda i, j: (0, 0))],
        out_specs=[o_spec, pl.BlockSpec((bm, cs), lambda i, j: (i, 0))],
        out_shape=[jax.ShapeDtypeStruct((n, c), out_dtype), jax.ShapeDtypeStruct((n, cs), F32)],
        scratch_shapes=scratch,
        compiler_params=_cparams(("arbitrary", "arbitrary")), name="norm_matmul_in",
    )(x, g, w, w_small)


def _head_norm(oh, gn_h):
    r = lax.rsqrt(jnp.mean(oh * oh, axis=-1, keepdims=True) + EPS)
    return oh * r * gn_h


def _gla_log_decay(small, wa_ref, ba_ref):
    a_pre = _dot(small.astype(BF16), wa_ref[...]) + ba_ref[...]
    return _log_sigmoid(a_pre) * (1.0 / GLA_TAU)


def _gla_prompt_kernel(qk_ref, v_ref, gr_ref, sm_ref, wa_ref, ba_ref, gn_ref,
                       y_ref, st_ref, state_ref, b_ref, k_ref, *, n_chunks, group):
    t = pl.program_id(1)
    L = CHUNK

    @pl.when(t == 0)
    def _():
        state_ref[...] = jnp.zeros_like(state_ref)

    ri = lax.broadcasted_iota(jnp.int32, (L, L), 0)
    ci = lax.broadcasted_iota(jnp.int32, (L, L), 1)
    tril = jnp.where(ri >= ci, 1.0, 0.0).astype(BF16)
    rw = lax.broadcasted_iota(jnp.int32, (L, GLA_HEADS * L), 0)
    cw = lax.broadcasted_iota(jnp.int32, (L, GLA_HEADS * L), 1)
    causal_wide = (cw % L) <= rw
    kr = lax.broadcasted_iota(jnp.int32, (GLA_HEADS * L, GLA_QK), 0)
    kc = lax.broadcasted_iota(jnp.int32, (GLA_HEADS * L, GLA_QK), 1)
    kk_mask = (kr // L) == (kc // GLA_DK)
    vr = lax.broadcasted_iota(jnp.int32, (GLA_HEADS * L, GLA_V), 0)
    vc = lax.broadcasted_iota(jnp.int32, (GLA_HEADS * L, GLA_V), 1)
    vv_mask = (vr // L) == (vc // GLA_DV)
    sr = lax.broadcasted_iota(jnp.int32, (GLA_V, GLA_QK), 0)
    sc = lax.broadcasted_iota(jnp.int32, (GLA_V, GLA_QK), 1)
    st_mask = (sr // GLA_DV) == (sc // GLA_DK)
    hr = lax.broadcasted_iota(jnp.int32, (GLA_QK, LANES), 0)
    hc = lax.broadcasted_iota(jnp.int32, (GLA_QK, LANES), 1)
    head_ones = jnp.where((hr // GLA_DK) == hc, 1.0, 0.0).astype(BF16)
    pr = lax.broadcasted_iota(jnp.int32, (LANES, GLA_HEADS * L), 0)
    pc = lax.broadcasted_iota(jnp.int32, (LANES, GLA_HEADS * L), 1)

    def one(e, c):
        r0 = pl.multiple_of(c * L, L)
        rows = pl.ds(r0, L)
        qk = qk_ref[e, rows, :].astype(F32)
        q = qk[:, :GLA_QK] * (GLA_DK ** -0.5)
        k = qk[:, GLA_QK:]
        v16 = v_ref[e, rows, :].astype(BF16)
        g = _gla_log_decay(sm_ref[e, rows, :], wa_ref, ba_ref)
        b = _dot_exact01(tril, g)
        b_end = b[L - 1:L, :]
        qe = q * jnp.exp(b)
        k_dec = k * jnp.exp(b_end - b)
        qe16 = qe.astype(BF16)

        def fast_att(_):
            k_til = k * jnp.exp(-b)
            kk = jnp.where(kk_mask, jnp.concatenate([k_til] * GLA_HEADS, axis=0), 0.0)
            return _dot_nt(qe16, kk.astype(BF16))

        def direct_att(_):
            b_ref[...] = b
            k_ref[...] = k

            def col(j, acc):
                kj = k_ref[pl.ds(j, 1), :]
                bj = b_ref[pl.ds(j, 1), :]
                prod = q * kj * jnp.exp(jnp.minimum(b - bj, 0.0))
                red = _dot(prod.astype(BF16), head_ones)
                place = jnp.where((pc == pr * L + j) & (pr < GLA_HEADS), 1.0, 0.0).astype(BF16)
                return acc + _dot(red.astype(BF16), place)

            return lax.fori_loop(0, L, col, jnp.zeros((L, GLA_HEADS * L), F32))

        safe = jnp.max(-b_end) <= GLA_SAFE_DECAY
        att = lax.cond(safe, fast_att, direct_att, 0)
        att = jnp.where(causal_wide, att, 0.0)

        vv = jnp.where(vv_mask, jnp.concatenate([v16] * GLA_HEADS, axis=0), jnp.zeros((), BF16))
        st = state_ref[e]
        o = _dot(att.astype(BF16), vv) + _dot_nt(qe16, st.astype(BF16))
        upd = _dot_tn(v16, k_dec.astype(BF16))
        st_new = jnp.where(st_mask, st * jnp.exp(b_end) + upd, 0.0)

        gr = gr_ref[e, rows, :].astype(F32)
        ys = []
        for h in range(GLA_HEADS):
            sl = slice(h * GLA_DV, (h + 1) * GLA_DV)
            gate = gr[:, sl]
            ys.append((_head_norm(o[:, sl], gn_ref[:, sl]) * (gate * _sigmoid(gate))).astype(BF16))
        return jnp.concatenate(ys, axis=1), st_new

    def chunk(c, carry):
        res = [one(e, c) for e in range(group)]
        rows = pl.ds(pl.multiple_of(c * L, L), L)
        y_ref[:, rows, :] = jnp.stack([r[0] for r in res])
        state_ref[...] = jnp.stack([r[1] for r in res])
        return carry

    lax.fori_loop(0, n_chunks, chunk, 0)

    @pl.when(t == pl.num_programs(1) - 1)
    def _():
        st_ref[...] = state_ref[...]


def _gla_prompt(z3, zs3, wa, ba, gn, *, tt, group):
    batch, seq, _ = z3.shape
    blk = lambda j: pl.BlockSpec((group, tt, 512), lambda b, t: (b, t, j))
    const = lambda shape: pl.BlockSpec(shape, lambda b, t: (0,) * len(shape))
    return pl.pallas_call(
        functools.partial(_gla_prompt_kernel, n_chunks=tt // CHUNK, group=group),
        grid=(batch // group, seq // tt),
        in_specs=[blk(Z_QK), blk(Z_GV), blk(Z_GR),
                  pl.BlockSpec((group, tt, LANES), lambda b, t: (b, t, 0)),
                  const((LANES, GLA_QK)), const((1, GLA_QK)), const((1, GLA_V))],
        out_specs=[pl.BlockSpec((group, tt, GLA_V), lambda b, t: (b, t, 0)),
                   pl.BlockSpec((group, GLA_V, GLA_QK), lambda b, t: (b, 0, 0))],
        out_shape=[jax.ShapeDtypeStruct((batch, seq, GLA_V), BF16),
                   jax.ShapeDtypeStruct((batch, GLA_V, GLA_QK), F32)],
        scratch_shapes=[pltpu.VMEM((group, GLA_V, GLA_QK), F32), pltpu.VMEM((CHUNK, GLA_QK), F32),
                        pltpu.VMEM((CHUNK, GLA_QK), F32)],
        compiler_params=_cparams(("arbitrary", "arbitrary")), name="gla_prompt",
    )(z3, z3, z3, zs3, wa, ba, gn)


def _mlstm_prompt_kernel(q_ref, k_ref, v_ref, og_ref, sm_ref, gb_ref, gn_ref,
                         y_ref, c_out, n_out, m_out, c_ref, n_ref, m_ref, *, n_chunks, group):
    t = pl.program_id(1)
    L = CHUNK

    @pl.when(t == 0)
    def _():
        c_ref[...] = jnp.zeros_like(c_ref)
        n_ref[...] = jnp.zeros_like(n_ref)
        m_ref[...] = jnp.zeros_like(m_ref)

    ri = lax.broadcasted_iota(jnp.int32, (L, L), 0)
    ci = lax.broadcasted_iota(jnp.int32, (L, L), 1)
    lower = ri >= ci
    tril = jnp.where(lower, 1.0, 0.0).astype(BF16)
    triu = jnp.where(ri <= ci, 1.0, 0.0).astype(BF16)
    sr = lax.broadcasted_iota(jnp.int32, (LANES, ML_W), 0)
    sh = lax.broadcasted_iota(jnp.int32, (LANES, ML_W), 1) // ML_D
    sel_i = jnp.where(sr == SM_MI + sh, 1.0, 0.0).astype(BF16)
    sel_f = jnp.where(sr == SM_MF + sh, 1.0, 0.0).astype(BF16)

    def one(e, c):
        r0 = pl.multiple_of(c * L, L)
        rows = pl.ds(r0, L)
        sm = sm_ref[e, rows, :] + gb_ref[...]
        lf_all = _log_sigmoid(sm)
        sm_t = sm.T
        b_cols = _dot_exact01(tril, lf_all)
        b_rows = _dot_exact01_rhs(_log_sigmoid(sm_t), triu)
        b_wide = _dot_exact01_rhs(b_cols, sel_f)
        li_wide = _dot_exact01_rhs(sm, sel_i)
        ys, cs, ns, ms = [], [], [], []
        for h in range(ML_HEADS):
            sl = slice(h * ML_D, (h + 1) * ML_D)
            b_tok = b_wide[:, sl]
            li_tok = li_wide[:, sl]
            li_row = sm_t[SM_MI + h:SM_MI + h + 1, :]
            b_row = b_rows[SM_MF + h:SM_MF + h + 1, :]
            m_prev = m_ref[e, h:h + 1, :]
            w = jnp.where(lower, jnp.concatenate([b_tok] * (L // LANES), axis=1) - b_row + li_row, -jnp.inf)
            m_tok = jnp.maximum(b_tok + m_prev, jnp.max(w, axis=1, keepdims=True))
            a_inter = jnp.exp(b_tok + m_prev - m_tok)
            qh = q_ref[e, rows, sl].astype(F32) * (ML_D ** -0.5)
            kh16 = k_ref[e, rows, sl].astype(BF16)
            kh = kh16.astype(F32)
            qh16, vh16 = qh.astype(BF16), v_ref[e, rows, sl].astype(BF16)
            s = _dot_nt(qh16, kh16) * jnp.exp(w - jnp.concatenate([m_tok] * (L // LANES), axis=1))
            c_h = c_ref[e, sl, :]
            n_h = n_ref[e, h:h + 1, :]
            s16 = s.astype(BF16)
            row_sum = jnp.sum(s, axis=1, keepdims=True)
            q_n = _dot_nt(qh16, jnp.broadcast_to(n_h, (LANES, ML_D)).astype(BF16))
            num = _dot(s16, vh16) + a_inter * _dot(qh16, c_h.astype(BF16))
            den = row_sum + a_inter * q_n
            hh = num / jnp.maximum(jnp.abs(den), jnp.exp(-m_tok))
            og = og_ref[e, rows, sl].astype(F32)
            ys.append((_head_norm(hh, gn_ref[:, sl]) * _sigmoid(og)).astype(BF16))
            b_end = b_tok[L - 1:L, :]
            w_end = b_end - b_tok + li_tok
            m_new = jnp.maximum(b_end + m_prev, jnp.max(w_end, axis=0, keepdims=True))
            e_inter = jnp.exp(b_end + m_prev - m_new)
            kd = kh * jnp.exp(w_end - m_new)
            cs.append(e_inter * c_h + _dot_tn(kd.astype(BF16), vh16))
            ns.append(e_inter * n_h + jnp.sum(kd, axis=0, keepdims=True))
            ms.append(m_new)
        pad = [jnp.zeros((8 - ML_HEADS, LANES), F32)]
        return (jnp.concatenate(ys, axis=1), jnp.concatenate(cs, axis=0),
                jnp.concatenate(ns + pad, axis=0), jnp.concatenate(ms + pad, axis=0))

    def chunk(c, carry):
        res = [one(e, c) for e in range(group)]
        rows = pl.ds(pl.multiple_of(c * L, L), L)
        y_ref[:, rows, :] = jnp.stack([r[0] for r in res])
        c_ref[...] = jnp.stack([r[1] for r in res])
        n_ref[...] = jnp.stack([r[2] for r in res])
        m_ref[...] = jnp.stack([r[3] for r in res])
        return carry

    lax.fori_loop(0, n_chunks, chunk, 0)

    @pl.when(t == pl.num_programs(1) - 1)
    def _():
        c_out[...] = c_ref[...]
        n_out[...] = n_ref[...]
        m_out[...] = m_ref[...]


def _mlstm_prompt(z3, zs3, gbias, gn, *, tt, group):
    batch, seq, _ = z3.shape
    blk = lambda j: pl.BlockSpec((group, tt, 512), lambda b, t: (b, t, j))
    const = lambda shape: pl.BlockSpec(shape, lambda b, t: (0,) * len(shape))
    state = lambda shape: pl.BlockSpec((group,) + shape, lambda b, t: (b, 0, 0))
    return pl.pallas_call(
        functools.partial(_mlstm_prompt_kernel, n_chunks=tt // CHUNK, group=group),
        grid=(batch // group, seq // tt),
        in_specs=[blk(Z_MQ), blk(Z_MK), blk(Z_MV), blk(Z_MO),
                  pl.BlockSpec((group, tt, LANES), lambda b, t: (b, t, 0)),
                  const((1, LANES)), const((1, ML_W))],
        out_specs=[pl.BlockSpec((group, tt, ML_W), lambda b, t: (b, t, 0)),
                   state((ML_W, ML_D)), state((8, LANES)), state((8, LANES))],
        out_shape=[jax.ShapeDtypeStruct((batch, seq, ML_W), BF16),
                   jax.ShapeDtypeStruct((batch, ML_W, ML_D), F32),
                   jax.ShapeDtypeStruct((batch, 8, LANES), F32),
                   jax.ShapeDtypeStruct((batch, 8, LANES), F32)],
        scratch_shapes=[pltpu.VMEM((group, ML_W, ML_D), F32), pltpu.VMEM((group, 8, LANES), F32),
                        pltpu.VMEM((group, 8, LANES), F32)],
        compiler_params=_cparams(("arbitrary", "arbitrary")), name="mlstm_prompt",
    )(z3, z3, z3, z3, zs3, gbias, gn)


def _s5_prep_kernel(lre_ref, lim_ref, ldt_ref, ctr_ref, cti_ref, par_ref, cf_ref):
    lam_re = lre_ref[0]
    lam_im = lim_ref[0]
    dt = jnp.exp(ldt_ref[0])
    mag = jnp.exp(lam_re * dt)
    lb_re = mag * jnp.cos(lam_im * dt)
    lb_im = mag * jnp.sin(lam_im * dt)
    nr = lb_re - 1.0
    den = lam_re * lam_re + lam_im * lam_im
    f_re = (nr * lam_re + lb_im * lam_im) / den
    f_im = (lb_im * lam_re - nr * lam_im) / den
    par_ref[0] = jnp.concatenate([lb_re, lb_im, f_re, f_im, jnp.zeros((4, S5_MB), F32)], axis=0)
    ct_re = ctr_ref[0]
    ct_im = cti_ref[0]
    cf_ref[0, :, :S5_MB] = (ct_re * f_re - ct_im * f_im).astype(BF16)
    cf_ref[0, :, S5_MB:] = (-(ct_re * f_im + ct_im * f_re)).astype(BF16)


def _s5_prep(lam_re, lam_im, log_dt, ct_re, ct_im):
    vec = pl.BlockSpec((1, 1, S5_MB), lambda n: (n, 0, 0))
    mat = pl.BlockSpec((1, S5_UB, S5_MB), lambda n: (n, 0, 0))
    return pl.pallas_call(
        _s5_prep_kernel, grid=(S5_NB,),
        in_specs=[vec, vec, vec, mat, mat],
        out_specs=[pl.BlockSpec((1, 8, S5_MB), lambda n: (n, 0, 0)),
                   pl.BlockSpec((1, S5_UB, 2 * S5_MB), lambda n: (n, 0, 0))],
        out_shape=[jax.ShapeDtypeStruct((S5_NB, 8, S5_MB), F32),
                   jax.ShapeDtypeStruct((S5_NB, S5_UB, 2 * S5_MB), BF16)],
        compiler_params=_cparams(("arbitrary",)), name="s5_prep",
    )(lam_re, lam_im, log_dt, ct_re, ct_im)


def _s5_prompt_kernel(u_ref, w_ref, cf_ref, par_ref, y_ref, h_out, x_ref, h_ref, ub_ref, ut_ref, *, batch, tt):
    t_blk = pl.program_id(1)

    @pl.when(t_blk == 0)
    def _():
        h_ref[...] = jnp.zeros_like(h_ref)

    nlb = S5_MB // LANES
    ub_ref[...] = u_ref[...].astype(F32).reshape(batch * tt, S5_UB)

    def interleave(t, carry):
        ut_ref[pl.ds(pl.multiple_of(t * batch, batch), batch), :] = ub_ref[pl.ds(t, batch, stride=tt), :]
        return carry

    lax.fori_loop(0, tt, interleave, 0, unroll=8)
    x = _dot(ut_ref[...].astype(BF16), w_ref[0])
    for j in range(2 * nlb):
        x_ref[j] = x[:, j * LANES:(j + 1) * LANES]

    par = par_ref[0]
    lr = [jnp.broadcast_to(par[0:1, j * LANES:(j + 1) * LANES], (batch, LANES)) for j in range(nlb)]
    li = [jnp.broadcast_to(par[1:2, j * LANES:(j + 1) * LANES], (batch, LANES)) for j in range(nlb)]

    def step(t, carry):
        rows = pl.ds(pl.multiple_of(t * batch, batch), batch)
        new = []
        for j in range(nlb):
            hr, hi = carry[j], carry[nlb + j]
            nr = lr[j] * hr - li[j] * hi + x_ref[j, rows, :]
            ni = lr[j] * hi + li[j] * hr + x_ref[nlb + j, rows, :]
            x_ref[j, rows, :] = nr
            x_ref[nlb + j, rows, :] = ni
            new.append((nr, ni))
        return tuple(n[0] for n in new) + tuple(n[1] for n in new)

    h0 = tuple(h_ref[:, j * LANES:(j + 1) * LANES] for j in range(2 * nlb))
    hfin = lax.fori_loop(0, tt, step, h0, unroll=8)
    for j in range(2 * nlb):
        h_ref[:, j * LANES:(j + 1) * LANES] = hfin[j]
    hr = jnp.concatenate(hfin[:nlb], axis=1)
    hi = jnp.concatenate(hfin[nlb:], axis=1)

    hall = jnp.concatenate([x_ref[j].astype(BF16) for j in range(2 * nlb)], axis=1)
    ut_ref[...] = _dot_nt(hall, cf_ref[0])
    for b in range(batch):
        y_ref[b] = ut_ref[pl.ds(b, tt, stride=batch), :]

    @pl.when(t_blk == pl.num_programs(1) - 1)
    def _():
        fr = par[2:3, :]
        fi = par[3:4, :]
        h_out[0, :, :S5_MB] = fr * hr - fi * hi
        h_out[0, :, S5_MB:] = fr * hi + fi * hr


def _s5_prompt(z3, w_blk, cfold, par, *, batch, seq, tt):
    nt = seq // tt
    u_col0 = Z_SU * 512 // S5_UB
    return pl.pallas_call(
        functools.partial(_s5_prompt_kernel, batch=batch, tt=tt),
        grid=(S5_NB, nt),
        in_specs=[pl.BlockSpec((batch, tt, S5_UB), lambda n, t: (0, t, u_col0 + n)),
                  pl.BlockSpec((1, S5_UB, 2 * S5_MB), lambda n, t: (n, 0, 0)),
                  pl.BlockSpec((1, S5_UB, 2 * S5_MB), lambda n, t: (n, 0, 0)),
                  pl.BlockSpec((1, 8, S5_MB), lambda n, t: (n, 0, 0))],
        out_specs=[pl.BlockSpec((batch, tt, S5_UB), lambda n, t: (0, t, n)),
                   pl.BlockSpec((1, batch, 2 * S5_MB), lambda n, t: (n, 0, 0))],
        out_shape=[jax.ShapeDtypeStruct((batch, seq, S5_WIDTH), F32),
                   jax.ShapeDtypeStruct((S5_NB, batch, 2 * S5_MB), F32)],
        scratch_shapes=[pltpu.VMEM((2 * S5_MB // LANES, batch * tt, LANES), F32),
                        pltpu.VMEM((batch, 2 * S5_MB), F32),
                        pltpu.VMEM((batch * tt, S5_UB), F32), pltpu.VMEM((batch * tt, S5_UB), F32)],
        compiler_params=_cparams(("arbitrary", "arbitrary")), name="s5_prompt",
    )(z3, w_blk, cfold, par)


def _s5_decode_kernel(u_ref, w_ref, ctr_ref, cti_ref, par_ref, hr_ref, hi_ref, y_ref, hro_ref, hio_ref):
    x = _dot(u_ref[...].astype(BF16), w_ref[0])
    xr, xi = x[:, :S5_MB], x[:, S5_MB:]
    par = par_ref[0]
    lr, li, fr, fi = par[0:1, :], par[1:2, :], par[2:3, :], par[3:4, :]
    h0r, h0i = hr_ref[...], hi_ref[...]
    hr = lr * h0r - li * h0i + (fr * xr - fi * xi)
    hi = lr * h0i + li * h0r + (fr * xi + fi * xr)
    hro_ref[...] = hr
    hio_ref[...] = hi
    y_ref[...] = _dot_nt(hr.astype(BF16), ctr_ref[0]) - _dot_nt(hi.astype(BF16), cti_ref[0])


def _s5_decode(z, w_blk, ct_re, ct_im, par, h_re, h_im):
    rows = z.shape[0]
    u_col0 = Z_SU * 512 // S5_UB
    hspec = pl.BlockSpec((rows, S5_MB), lambda n: (0, n))
    mat = pl.BlockSpec((1, S5_UB, S5_MB), lambda n: (n, 0, 0))
    return pl.pallas_call(
        _s5_decode_kernel, grid=(S5_NB,),
        in_specs=[pl.BlockSpec((rows, S5_UB), lambda n: (0, u_col0 + n)),
                  pl.BlockSpec((1, S5_UB, 2 * S5_MB), lambda n: (n, 0, 0)),
                  mat, mat,
                  pl.BlockSpec((1, 8, S5_MB), lambda n: (n, 0, 0)), hspec, hspec],
        out_specs=[pl.BlockSpec((rows, S5_UB), lambda n: (0, n)), hspec, hspec],
        out_shape=[jax.ShapeDtypeStruct((rows, S5_WIDTH), F32),
                   jax.ShapeDtypeStruct((rows, S5_MODES), F32),
                   jax.ShapeDtypeStruct((rows, S5_MODES), F32)],
        compiler_params=_cparams(("arbitrary",)), name="s5_decode",
    )(z, w_blk, ct_re, ct_im, par, h_re, h_im)


def _gla_decode_kernel(qk_ref, v_ref, gr_ref, sm_ref, wa_ref, ba_ref, gn_ref, s_ref,
                       y_ref, so_ref, *, bb, fill_layer):
    eg = jnp.exp(_gla_log_decay(sm_ref[...], wa_ref, ba_ref))
    qk = qk_ref[...]
    q = qk[:, :GLA_QK] * (GLA_DK ** -0.5)
    k = qk[:, GLA_QK:]
    v = v_ref[...]

    def per_row(b):
        row = slice(b, b + 1)
        s_old = s_ref[b].reshape(GLA_QK, GLA_DV)
        v_rows = jnp.concatenate(
            [jnp.broadcast_to(v[row, h * GLA_DV:(h + 1) * GLA_DV], (GLA_DK, GLA_DV)) for h in range(GLA_HEADS)],
            axis=0)
        s_new = _row_to_col(eg[row], GLA_QK) * s_old + _row_to_col(k[row], GLA_QK) * v_rows
        qs = (_row_to_col(q[row], GLA_QK) * s_new).reshape(GLA_HEADS, GLA_DK, GLA_DV)
        o4 = jnp.sum(qs, axis=1)
        o_row = jnp.concatenate([o4[h:h + 1, :] for h in range(GLA_HEADS)], axis=1)
        return s_new.reshape(GLA_HEADS, GLA_DK, GLA_DV), o_row

    res = [per_row(b) for b in range(bb)]
    _store_layer_state(so_ref, jnp.stack([r[0] for r in res]), fill_layer)
    o = jnp.concatenate([r[1] for r in res], axis=0)
    gr = gr_ref[...]
    for h in range(GLA_HEADS):
        vs = slice(h * GLA_DV, (h + 1) * GLA_DV)
        gate = gr[:, vs]
        y_ref[:, vs] = (_head_norm(o[:, vs], gn_ref[:, vs]) * (gate * _sigmoid(gate))).astype(BF16)


def _store_layer_state(out_ref, new, layer):
    if layer is None:
        out_ref[...] = new
    else:
        for l in range(out_ref.shape[0]):
            out_ref[l] = new if l == layer else jnp.zeros_like(new)


def _skip_ref(kernel, pos):
    def wrapped(*refs):
        return kernel(*refs[:pos], *refs[pos + 1:])
    return wrapped


def _layer_out(kernel, in_specs, operands, out_specs, out_shapes, stacked_idx, prev, name, grid):
    aliases = {}
    if prev is not None:
        in_specs = in_specs + [pl.BlockSpec(memory_space=pl.ANY)]
        operands = operands + (prev,)
        aliases = {len(operands) - 1: stacked_idx}
        kernel = _skip_ref(kernel, len(operands) - 1)
    return pl.pallas_call(
        kernel, grid=grid, in_specs=in_specs, out_specs=out_specs, out_shape=out_shapes,
        input_output_aliases=aliases, compiler_params=_cparams(("arbitrary",)), name=name,
    )(*operands)


def _gla_decode(z, zs, wa, ba, gn, states, prev, *, layer, bb):
    rows = z.shape[0]
    blk = lambda j: pl.BlockSpec((bb, 512), lambda i: (i, j))
    const = lambda shape: pl.BlockSpec(shape, lambda i: (0,) * len(shape))
    sblock = (bb, GLA_HEADS, GLA_DK, GLA_DV)
    sspec = pl.BlockSpec((None,) + sblock, lambda i: (layer, i, 0, 0, 0))
    first = prev is None
    out_sspec = pl.BlockSpec((states.shape[0],) + sblock, lambda i: (0, i, 0, 0, 0)) if first else sspec
    return _layer_out(
        functools.partial(_gla_decode_kernel, bb=bb, fill_layer=layer if first else None),
        [blk(Z_QK), blk(Z_GV), blk(Z_GR), pl.BlockSpec((bb, LANES), lambda i: (i, 0)),
         const((LANES, GLA_QK)), const((1, GLA_QK)), const((1, GLA_V)), sspec],
        (z, z, z, zs, wa, ba, gn, states),
        [pl.BlockSpec((bb, GLA_V), lambda i: (i, 0)), out_sspec],
        [jax.ShapeDtypeStruct((rows, GLA_V), BF16), jax.ShapeDtypeStruct(states.shape, F32)],
        1, prev, "gla_decode", (rows // bb,))


def _mlstm_decode_kernel(q_ref, k_ref, v_ref, og_ref, sm_ref, gb_ref, gn_ref, c_ref, n_ref, m_ref,
                         y_ref, co_ref, no_ref, mo_ref, *, bb, fill_layer):
    li_blk = sm_ref[...] + gb_ref[...]
    lf_blk = _log_sigmoid(li_blk)
    m_blk = m_ref[...]
    q_blk = q_ref[...] * (ML_D ** -0.5)
    k_blk = k_ref[...]
    v_blk = v_ref[...]
    lane = lax.broadcasted_iota(jnp.int32, (1, LANES), 1)

    def per_row(b):
        row = slice(b, b + 1)
        li_r, lf_r, m_r = li_blk[row], lf_blk[row], m_blk[row]
        q_all, k_all, v_all = q_blk[row], k_blk[row], v_blk[row]
        n_all = n_ref[b]
        m_new_row = jnp.zeros((1, LANES), F32)
        h_heads, n_heads, c_heads = [], [], []
        for h in range(ML_HEADS):
            sl = slice(h * ML_D, (h + 1) * ML_D)
            li = li_r[:, SM_MI + h:SM_MI + h + 1]
            lf = lf_r[:, SM_MF + h:SM_MF + h + 1]
            m_prev = m_r[:, h:h + 1]
            m_new = jnp.maximum(lf + m_prev, li)
            a = jnp.exp(lf + m_prev - m_new)
            e = jnp.exp(li - m_new)
            k_r = k_all[:, sl]
            q_r = q_all[:, sl]
            c_new = a * c_ref[b, h] + (_row_to_col(k_r, ML_D) * e) * v_all[:, sl]
            n_new = a * n_all[h:h + 1, :] + e * k_r
            c_heads.append(c_new)
            n_heads.append(n_new)
            num = jnp.sum(_row_to_col(q_r, ML_D) * c_new, axis=0, keepdims=True)
            den = jnp.sum(q_r * n_new, axis=1, keepdims=True)
            h_heads.append(num / jnp.maximum(jnp.abs(den), jnp.exp(-m_new)))
            m_new_row = jnp.where(lane == h, m_new, m_new_row)
        return (jnp.stack(c_heads), jnp.concatenate(n_heads, axis=0),
                jnp.concatenate(h_heads, axis=1), m_new_row)

    res = [per_row(b) for b in range(bb)]
    _store_layer_state(co_ref, jnp.stack([r[0] for r in res]), fill_layer)
    no_ref[...] = jnp.stack([r[1] for r in res])
    mo_ref[...] = jnp.concatenate([r[3] for r in res], axis=0)
    hh = jnp.concatenate([r[2] for r in res], axis=0)
    og = og_ref[...]
    for h in range(ML_HEADS):
        sl = slice(h * ML_D, (h + 1) * ML_D)
        y_ref[:, sl] = (_head_norm(hh[:, sl], gn_ref[:, sl]) * _sigmoid(og[:, sl])).astype(BF16)


def _mlstm_decode(z, zs, gbias, gn, c_all, n_all, m_pad, prev, *, layer, bb):
    rows = z.shape[0]
    blk = lambda j: pl.BlockSpec((bb, 512), lambda i: (i, j))
    const = lambda shape: pl.BlockSpec(shape, lambda i: (0,) * len(shape))
    nblock = (bb, ML_HEADS, ML_D)
    cblock = (bb, ML_HEADS, ML_D, ML_D)
    cspec = pl.BlockSpec((None,) + cblock, lambda i: (layer, i, 0, 0, 0))
    mspec = pl.BlockSpec((bb, LANES), lambda i: (i, 0))
    first = prev is None
    out_cspec = pl.BlockSpec((c_all.shape[0],) + cblock, lambda i: (0, i, 0, 0, 0)) if first else cspec
    return _layer_out(
        functools.partial(_mlstm_decode_kernel, bb=bb, fill_layer=layer if first else None),
        [blk(Z_MQ), blk(Z_MK), blk(Z_MV), blk(Z_MO), mspec, const((1, LANES)), const((1, ML_W)),
         cspec, pl.BlockSpec((None,) + nblock, lambda i: (layer, i, 0, 0)), mspec],
        (z, z, z, z, zs, gbias, gn, c_all, n_all, m_pad),
        [pl.BlockSpec((bb, ML_W), lambda i: (i, 0)), out_cspec, pl.BlockSpec(nblock, lambda i: (i, 0, 0)), mspec],
        [jax.ShapeDtypeStruct((rows, ML_W), BF16), jax.ShapeDtypeStruct(c_all.shape, F32),
         jax.ShapeDtypeStruct(n_all.shape[1:], F32), jax.ShapeDtypeStruct((rows, LANES), F32)],
        1, prev, "mlstm_decode", (rows // bb,))


def _merged_residual(ya_ref, yb_ref, yp_ref, u_ref, za_ref, zb_ref, zc_ref, x_ref,
                     d_ref, wg_ref, wa_ref, wb_ref, wc_ref, wo_ref):
    yc = _gelu_tanh(yp_ref[...] + d_ref[...] * u_ref[...].astype(F32))
    yc = yc * _sigmoid(_dot(yc.astype(BF16), wg_ref[...]))
    m = _sigmoid(za_ref[...].astype(F32)) * _dot(ya_ref[...], wa_ref[...])
    m = m + _sigmoid(zb_ref[...].astype(F32)) * _dot(yb_ref[...], wb_ref[...])
    m = m + _sigmoid(zc_ref[...].astype(F32)) * _dot(yc.astype(BF16), wc_ref[...])
    return x_ref[...] + _dot(m.astype(BF16), wo_ref[...])


def _merge_kernel(*refs):
    refs[-1][...] = _merged_residual(*refs[:-1])


def _merge_specs(bm, row_map, const):
    r512 = pl.BlockSpec((bm, 512), lambda *g: (row_map(*g), 0))
    zblk = lambda j: pl.BlockSpec((bm, D_MODEL), lambda *g: (row_map(*g), j))
    return [r512, r512, r512, pl.BlockSpec((bm, 512), lambda *g: (row_map(*g), Z_SU)),
            zblk(Z_ZA), zblk(Z_ZB), zblk(Z_ZC), pl.BlockSpec((bm, D_MODEL), lambda *g: (row_map(*g), 0)),
            const((1, S5_WIDTH)), const((S5_WIDTH, S5_WIDTH)),
            const((GLA_V, D_MODEL)), const((ML_W, D_MODEL)), const((S5_WIDTH, D_MODEL)),
            const((D_MODEL, D_MODEL))]


def _merge(ya, yb, ypre, z, x, d, wg, wa, wb, wc, wo, *, bm):
    n = x.shape[0]
    const = lambda shape: pl.BlockSpec(shape, lambda i: (0, 0))
    return pl.pallas_call(
        _merge_kernel, grid=(n // bm,),
        in_specs=_merge_specs(bm, lambda i: i, const),
        out_specs=pl.BlockSpec((bm, D_MODEL), lambda i: (i, 0)),
        out_shape=jax.ShapeDtypeStruct((n, D_MODEL), F32),
        compiler_params=_cparams(("arbitrary",)), name="merge",
    )(ya, yb, ypre, z, z, z, z, x, d, wg, wa, wb, wc, wo)


def _cross_attend(x, g_ref, wq_ref, mk_ref, mv_ref, wo_ref):
    q = _dot(_rms_rows(x, g_ref[...]).astype(BF16), wq_ref[...])
    heads = []
    for h in range(X_HEADS):
        sl = slice(h * X_DH, (h + 1) * X_DH)
        s = _dot_nt(q[:, sl].astype(BF16), mk_ref[:, sl].astype(BF16)) * (X_DH ** -0.5)
        s = s - jnp.max(s, axis=-1, keepdims=True)
        p = jnp.exp(s)
        p = p / jnp.sum(p, axis=-1, keepdims=True)
        heads.append(_dot(p.astype(BF16), mv_ref[:, sl].astype(BF16)))
    o = jnp.concatenate(heads, axis=-1)
    return x + _dot(o.astype(BF16), wo_ref[...])


N_MERGE_IN = 14


def _merge_cross_kernel(*refs):
    x1 = _merged_residual(*refs[:N_MERGE_IN])
    refs[-1][...] = _cross_attend(x1, *refs[N_MERGE_IN:-1])


def _merge_cross(ya, yb, ypre, z, x, d, wg, wa, wb, wc, wo, g, wq, mem_k, mem_v, wco,
                 *, layer, batch, seq, bm):
    nt = seq // bm
    const = lambda shape: pl.BlockSpec(shape, lambda b, t: (0, 0), pipeline_mode=pl.Buffered(1))
    mspec = pl.BlockSpec((None, MEM_LEN, D_MODEL), lambda b, t: (layer, b, 0))
    xspec = pl.BlockSpec((bm, D_MODEL), lambda b, t: (b * nt + t, 0))
    return pl.pallas_call(
        _merge_cross_kernel, grid=(batch, nt),
        in_specs=_merge_specs(bm, lambda b, t: b * nt + t, const)
        + [const((1, D_MODEL)), const((D_MODEL, D_MODEL)), mspec, mspec, const((D_MODEL, D_MODEL))],
        out_specs=xspec, out_shape=jax.ShapeDtypeStruct(x.shape, F32),
        compiler_params=_cparams(("arbitrary", "arbitrary")), name="merge_cross",
    )(ya, yb, ypre, z, z, z, z, x, d, wg, wa, wb, wc, wo, g, wq, mem_k, mem_v, wco)


def _memory_kv_kernel(x_ref, g_ref, wk_ref, wv_ref, k_ref, v_ref, hn_ref):
    @pl.when(pl.program_id(1) == 0)
    def _():
        hn_ref[...] = _rms_rows(x_ref[...], g_ref[...]).astype(BF16)
    k_ref[...] = _dot(hn_ref[...], wk_ref[...])
    v_ref[...] = _dot(hn_ref[...], wv_ref[...])


def _memory_kv(mem, g, wk, wv, *, bn=512):
    rows, d = mem.shape
    depth = wk.shape[0]
    wspec = pl.BlockSpec((None, d, bn), lambda l, j: (l, 0, j))
    ospec = pl.BlockSpec((None, rows, bn), lambda l, j: (l, 0, j))
    out = jax.ShapeDtypeStruct((depth, rows, d), F32)
    return pl.pallas_call(
        _memory_kv_kernel, grid=(depth, d // bn),
        in_specs=[pl.BlockSpec((rows, d), lambda l, j: (0, 0)),
                  pl.BlockSpec((None, 1, d), lambda l, j: (l, 0, 0)), wspec, wspec],
        out_specs=[ospec, ospec], out_shape=[out, out],
        scratch_shapes=[pltpu.VMEM((rows, d), BF16)],
        compiler_params=_cparams(("arbitrary", "arbitrary")), name="memory_kv",
    )(mem, g, wk, wv)


def _cross_decode_kernel(x_ref, g_ref, wq_ref, mk_ref, mv_ref, wo_ref, o_ref, *, bb):
    x = x_ref[...]
    q = _dot(_rms_rows(x, g_ref[...]).astype(BF16), wq_ref[...])
    rows = []
    for b in range(bb):
        q4 = jnp.concatenate([q[b:b + 1, h * X_DH:(h + 1) * X_DH] for h in range(X_HEADS)], axis=0)
        s = jnp.sum(mk_ref[0, b] * q4[None], axis=-1, keepdims=True) * (X_DH ** -0.5)
        p = jnp.exp(s - jnp.max(s, axis=0, keepdims=True))
        p = p / jnp.sum(p, axis=0, keepdims=True)
        o4 = jnp.sum(p * mv_ref[0, b], axis=0)
        rows.append(jnp.concatenate([o4[h:h + 1, :] for h in range(X_HEADS)], axis=1))
    att = jnp.concatenate(rows, axis=0)
    o_ref[...] = x + _dot(att.astype(BF16), wo_ref[...])


def _cross_decode(x, g, wq, cache_k, cache_v, wo, *, layer, bb):
    rows = x.shape[0]
    xspec = pl.BlockSpec((bb, D_MODEL), lambda i: (i, 0))
    const = lambda shape: pl.BlockSpec(shape, lambda i: (0, 0))
    mspec = pl.BlockSpec((1, bb, MEM_LEN, X_HEADS, X_DH), lambda i: (layer, i, 0, 0, 0))
    return pl.pallas_call(
        functools.partial(_cross_decode_kernel, bb=bb), grid=(rows // bb,),
        in_specs=[xspec, const((1, D_MODEL)), const((D_MODEL, D_MODEL)), mspec, mspec,
                  const((D_MODEL, D_MODEL))],
        out_specs=xspec, out_shape=jax.ShapeDtypeStruct(x.shape, F32),
        compiler_params=_cparams(("arbitrary",)), name="cross_decode",
    )(x, g, wq, cache_k, cache_v, wo)


def _ffn_prompt_kernel(x_ref, g_ref, wup_ref, cw_ref, cb_ref, wdn_ref, gf_ref,
                       o_ref, st_ref, hn_ref, halo_ref, up_ref, act_ref, *, tt, final_norm):
    t = pl.program_id(1)

    @pl.when(t == 0)
    def _():
        halo_ref[...] = jnp.zeros_like(halo_ref)

    x = x_ref[...]
    hn_ref[...] = _rms_rows(x, g_ref[...]).astype(BF16)
    def project(c):
        slot = c % 2
        for part, col0 in enumerate((c * FF_CHUNK, D_FF + c * FF_CHUNK)):
            cols = slice(col0, col0 + FF_CHUNK)
            lanes = slice(part * FF_CHUNK, (part + 1) * FF_CHUNK)
            up = _dot(hn_ref[...], wup_ref[:, cols])
            up_ref[slot, 6:8, lanes] = halo_ref[0:2, cols]
            up_ref[slot, 8:8 + tt, lanes] = up
            halo_ref[0:2, cols] = up[tt - 2:tt, :]

    def gated(c):
        slot = c % 2
        a_cols = slice(c * FF_CHUNK, (c + 1) * FF_CHUNK)
        g_cols = slice(D_FF + c * FF_CHUNK, D_FF + (c + 1) * FF_CHUNK)
        cw = jnp.concatenate([cw_ref[:, a_cols], cw_ref[:, g_cols]], axis=1)
        cb = jnp.concatenate([cb_ref[:, a_cols], cb_ref[:, g_cols]], axis=1)
        for r0 in range(0, tt, FF_ROW_SLAB):
            cv = (up_ref[slot, 6 + r0:6 + r0 + FF_ROW_SLAB, :] * cw[0:1]
                  + up_ref[slot, 7 + r0:7 + r0 + FF_ROW_SLAB, :] * cw[1:2]
                  + up_ref[slot, 8 + r0:8 + r0 + FF_ROW_SLAB, :] * cw[2:3] + cb)
            gt = cv[:, FF_CHUNK:]
            act_ref[r0:r0 + FF_ROW_SLAB, a_cols] = (cv[:, :FF_CHUNK] * (gt * _sigmoid(gt))).astype(BF16)

    project(0)
    for c in range(N_FF_CHUNKS):
        if c + 1 < N_FF_CHUNKS:
            project(c + 1)
        gated(c)
    acc = x + _dot(act_ref[...], wdn_ref[...])
    if final_norm:
        acc = _rms_rows(acc, gf_ref[...])
    o_ref[...] = acc

    @pl.when(t == pl.num_programs(1) - 1)
    def _():
        st_ref[0] = halo_ref[0:2, :]


def _ffn_prompt(x, g, wup, cw, cb, wdn, gf, *, batch, seq, tt, final_norm):
    nt = seq // tt
    xspec = pl.BlockSpec((tt, D_MODEL), lambda b, t: (b * nt + t, 0))
    const = lambda shape: pl.BlockSpec(shape, lambda b, t: (0, 0), pipeline_mode=pl.Buffered(1))
    return pl.pallas_call(
        functools.partial(_ffn_prompt_kernel, tt=tt, final_norm=final_norm), grid=(batch, nt),
        in_specs=[xspec, const((1, D_MODEL)), const((D_MODEL, 2 * D_FF)), const((CONV_W, 2 * D_FF)),
                  const((1, 2 * D_FF)), const((D_FF, D_MODEL)), const((1, D_MODEL))],
        out_specs=[xspec, pl.BlockSpec((1, CONV_W - 1, 2 * D_FF), lambda b, t: (b, 0, 0))],
        out_shape=[jax.ShapeDtypeStruct(x.shape, F32),
                   jax.ShapeDtypeStruct((batch, CONV_W - 1, 2 * D_FF), F32)],
        scratch_shapes=[pltpu.VMEM((tt, D_MODEL), BF16), pltpu.VMEM((8, 2 * D_FF), F32),
                        pltpu.VMEM((2, tt + 8, 2 * FF_CHUNK), F32), pltpu.VMEM((tt, D_FF), BF16)],
        compiler_params=_cparams(("arbitrary", "arbitrary")), name="ffn_prompt",
    )(x, g, wup, cw, cb, wdn, gf)


def _ffn_decode_kernel(x_ref, g_ref, wa_ref, wg_ref, cwa_ref, cwg_ref, cba_ref, cbg_ref,
                       b0a_ref, b0g_ref, b1a_ref, b1g_ref, wdn_ref, gf_ref,
                       o_ref, upa_ref, upg_ref, hn_ref, acc_ref, *, final_norm):
    c = pl.program_id(0)

    @pl.when(c == 0)
    def _():
        x = x_ref[...]
        hn_ref[...] = _rms_rows(x, g_ref[...]).astype(BF16)
        acc_ref[...] = x

    def conv(w_ref, cw_ref, cb_ref, b0_ref, b1_ref, up_ref):
        up = _dot(hn_ref[...], w_ref[...])
        up_ref[...] = up
        return b0_ref[...] * cw_ref[0:1, :] + b1_ref[...] * cw_ref[1:2, :] + up * cw_ref[2:3, :] + cb_ref[...]

    a = conv(wa_ref, cwa_ref, cba_ref, b0a_ref, b1a_ref, upa_ref)
    gt = conv(wg_ref, cwg_ref, cbg_ref, b0g_ref, b1g_ref, upg_ref)
    act = a * (gt * _sigmoid(gt))
    acc_ref[...] += _dot(act.astype(BF16), wdn_ref[...])

    @pl.when(c == pl.num_programs(0) - 1)
    def _():
        acc = acc_ref[...]
        if final_norm:
            acc = _rms_rows(acc, gf_ref[...])
        o_ref[...] = acc


def _ffn_decode(x, g, wup, cw, cb, wdn, gf, buf0, buf1, *, final_norm):
    rows = x.shape[0]
    fc = FF_CHUNK
    full = lambda shape: pl.BlockSpec(shape, lambda c: (0, 0))
    a_col = lambda r: pl.BlockSpec((r, fc), lambda c: (0, c))
    g_col = lambda r: pl.BlockSpec((r, fc), lambda c: (0, N_FF_CHUNKS + c))
    return pl.pallas_call(
        functools.partial(_ffn_decode_kernel, final_norm=final_norm), grid=(N_FF_CHUNKS,),
        in_specs=[full((rows, D_MODEL)), full((1, D_MODEL)),
                  a_col(D_MODEL), g_col(D_MODEL), a_col(CONV_W), g_col(CONV_W), a_col(1), g_col(1),
                  a_col(rows), g_col(rows), a_col(rows), g_col(rows),
                  pl.BlockSpec((fc, D_MODEL), lambda c: (c, 0)), full((1, D_MODEL))],
        out_specs=[full((rows, D_MODEL)), a_col(rows), a_col(rows)],
        out_shape=[jax.ShapeDtypeStruct(x.shape, F32),
                   jax.ShapeDtypeStruct((rows, D_FF), F32),
                   jax.ShapeDtypeStruct((rows, D_FF), F32)],
        scratch_shapes=[pltpu.VMEM((rows, D_MODEL), BF16), pltpu.VMEM((rows, D_MODEL), F32)],
        compiler_params=_cparams(("arbitrary",)), name="ffn_decode",
    )(x, g, wup, wup, cw, cw, cb, cb, buf0, buf0, buf1, buf1, wdn, gf)


def _layer_weights(P, l):
    w_in = P['w_in'][l]
    w_main = jnp.concatenate([w_in[:, 0:1536], w_in[:, 1552:3600], w_in[:, 3608:7192]], axis=1).astype(BF16)
    w_small = jnp.concatenate([w_in[:, 1536:1552], w_in[:, 3600:3608],
                               jnp.zeros((D_MODEL, LANES - GLA_LOWRANK - 2 * ML_HEADS), F32)], axis=1).astype(BF16)
    wa = jnp.zeros((LANES, GLA_QK), F32).at[:GLA_LOWRANK].set(P['w_gla_alpha'][l]).astype(BF16)
    gbias = jnp.zeros((1, LANES), F32)
    gbias = gbias.at[0, SM_MI:SM_MI + ML_HEADS].set(P['b_mlstm_i'][l])
    gbias = gbias.at[0, SM_MF:SM_MF + ML_HEADS].set(P['b_mlstm_f'][l])
    eye = jnp.eye(S5_GB, dtype=F32)
    w_re = jnp.einsum('ngpc,gh->ngchp', P['s5_b_re'][l].reshape(S5_NB, S5_GB, S5_P, S5_GROUP), eye)
    w_im = jnp.einsum('ngpc,gh->ngchp', P['s5_b_im'][l].reshape(S5_NB, S5_GB, S5_P, S5_GROUP), eye)
    w_blk = jnp.concatenate([w_re.reshape(S5_NB, S5_UB, S5_MB), w_im.reshape(S5_NB, S5_UB, S5_MB)],
                            axis=-1).astype(BF16)
    ct_re = jnp.einsum('ngcp,gh->ngchp', P['s5_c_re'][l].reshape(S5_NB, S5_GB, S5_GROUP, S5_P), eye)
    ct_im = jnp.einsum('ngcp,gh->ngchp', P['s5_c_im'][l].reshape(S5_NB, S5_GB, S5_GROUP, S5_P), eye)
    ct_re = ct_re.reshape(S5_NB, S5_UB, S5_MB)
    ct_im = ct_im.reshape(S5_NB, S5_UB, S5_MB)
    lam_re = P['s5_lam_re'][l].reshape(S5_NB, 1, S5_MB)
    lam_im = P['s5_lam_im'][l].reshape(S5_NB, 1, S5_MB)
    log_dt = jnp.broadcast_to(P['s5_log_dt'][l][:, None], (S5_GROUPS, S5_P)).reshape(S5_NB, 1, S5_MB)
    par, cfold = _s5_prep(lam_re, lam_im, log_dt, ct_re, ct_im)
    row = lambda a: a.reshape(1, -1)
    return dict(
        norm_mix=row(P['norm_mix'][l]), w_main=w_main, w_small=w_small, wa=wa,
        ba=row(P['b_gla_alpha'][l]), gla_norm=row(P['gla_head_norm'][l]), gbias=gbias,
        ml_norm=row(P['mlstm_head_norm'][l]), w_blk=w_blk, ct_re=ct_re.astype(BF16),
        ct_im=ct_im.astype(BF16), par=par, cfold=cfold, s5_d=row(P['s5_d'][l]),
        w_glu=P['s5_w_glu'][l].astype(BF16), w_a=P['w_branch_a'][l].astype(BF16),
        w_b=P['w_branch_b'][l].astype(BF16), w_c=P['w_branch_c'][l].astype(BF16),
        w_out=P['w_out'][l].astype(BF16), norm_cross=row(P['norm_cross'][l]),
        w_cq=P['w_cq'][l].astype(BF16), w_co=P['w_co'][l].astype(BF16),
        norm_ffn=row(P['norm_ffn'][l]), w_up=P['w_ffn_up'][l].astype(BF16),
        conv_w=P['ffn_conv_w'][l], conv_b=row(P['ffn_conv_b'][l]),
        w_down=P['w_ffn_down'][l].astype(BF16), norm_final=row(P['norm_final']),
    )


def _prompt_trunk(x_prompt, mem_k, mem_v, W, *, bm_in, tt_mix, gla_group, ml_group, tt_s5, bm, tt_ffn):
    batch, seq, _ = x_prompt.shape
    depth = len(W)
    x = x_prompt.reshape(batch * seq, D_MODEL)
    outs = []
    for l, w in enumerate(W):
        z, zs = _norm_matmul(x, w['norm_mix'], w['w_main'], w['w_small'], bm=bm_in, bn=1024, out_dtype=BF16)
        z3 = z.reshape(batch, seq, Z_MAIN)
        zs3 = zs.reshape(batch, seq, LANES)
        ya, st = _gla_prompt(z3, zs3, w['wa'], w['ba'], w['gla_norm'], tt=tt_mix, group=gla_group)
        yb, c, n, m = _mlstm_prompt(z3, zs3, w['gbias'], w['ml_norm'], tt=tt_mix, group=ml_group)
        ypre, hfin = _s5_prompt(z3, w['w_blk'], w['cfold'], w['par'], batch=batch, seq=seq, tt=tt_s5)
        x = _merge_cross(ya.reshape(batch * seq, GLA_V), yb.reshape(batch * seq, ML_W),
                         ypre.reshape(batch * seq, S5_WIDTH), z, x, w['s5_d'], w['w_glu'],
                         w['w_a'], w['w_b'], w['w_c'], w['w_out'],
                         w['norm_cross'], w['w_cq'], mem_k, mem_v, w['w_co'],
                         layer=l, batch=batch, seq=seq, bm=bm)
        x, conv = _ffn_prompt(x, w['norm_ffn'], w['w_up'], w['conv_w'], w['conv_b'], w['w_down'],
                              w['norm_final'], batch=batch, seq=seq, tt=tt_ffn,
                              final_norm=(l == depth - 1))
        st4 = st.reshape(batch, GLA_HEADS, GLA_DV, GLA_HEADS, GLA_DK)
        gla = jnp.stack([st4[:, h, :, h, :] for h in range(GLA_HEADS)], axis=1).transpose(0, 1, 3, 2)
        h4 = hfin.reshape(S5_NB, batch, 2, S5_GB, S5_P)
        s5_re = h4[:, :, 0].transpose(1, 0, 2, 3).reshape(batch, S5_GROUPS, S5_P)
        s5_im = h4[:, :, 1].transpose(1, 0, 2, 3).reshape(batch, S5_GROUPS, S5_P)
        outs.append((gla, c.reshape(batch, ML_HEADS, ML_D, ML_D), n[:, :ML_HEADS, :],
                     m[:, :ML_HEADS, 0], s5_re, s5_im, conv))
    stacked = [jnp.stack([outs[l][i] for l in range(depth)]) for i in range(7)]
    return x.reshape(batch, seq, D_MODEL), stacked


def _sample_trunk(x_sample, cache_k, cache_v, states, W):
    rows = x_sample.shape[0]
    depth = len(W)
    x = x_sample.reshape(rows, D_MODEL)
    s_gla, s_c, s_n, s_m, s_re, s_im, s_conv = states
    outs = []
    gla_all = c_all = None
    for l, w in enumerate(W):
        z, zs = _norm_matmul(x, w['norm_mix'], w['w_main'], w['w_small'], bm=rows, bn=1024)
        ya, gla_all = _gla_decode(z, zs, w['wa'], w['ba'], w['gla_norm'], s_gla, gla_all, layer=l, bb=16)
        m_pad = jnp.pad(s_m[l], ((0, 0), (0, LANES - ML_HEADS)))
        yb, c_all, n, m = _mlstm_decode(z, zs, w['gbias'], w['ml_norm'], s_c, s_n, m_pad, c_all,
                                        layer=l, bb=16)
        ypre, h_re, h_im = _s5_decode(z, w['w_blk'], w['ct_re'], w['ct_im'], w['par'],
                                      s_re[l].reshape(rows, S5_MODES), s_im[l].reshape(rows, S5_MODES))
        x = _merge(ya, yb, ypre, z, x, w['s5_d'], w['w_glu'], w['w_a'], w['w_b'], w['w_c'], w['w_out'],
                   bm=rows)
        x = _cross_decode(x, w['norm_cross'], w['w_cq'], cache_k, cache_v, w['w_co'], layer=l, bb=8)
        x, up_a, up_g = _ffn_decode(x, w['norm_ffn'], w['w_up'], w['conv_w'], w['conv_b'], w['w_down'],
                                    w['norm_final'], s_conv[l][:, 0, :], s_conv[l][:, 1, :],
                                    final_norm=(l == depth - 1))
        conv = jnp.stack([s_conv[l][:, 1, :], jnp.concatenate([up_a, up_g], axis=1)], axis=1)
        outs.append((n, m[:, :ML_HEADS], h_re.reshape(rows, S5_GROUPS, S5_P),
                     h_im.reshape(rows, S5_GROUPS, S5_P), conv))
    stacked = [jnp.stack([outs[l][i] for l in range(depth)]) for i in range(5)]
    return x.reshape(rows, 1, D_MODEL), [gla_all, c_all] + stacked


def kernel(x_prompt, x_sample, mem_prompt, cache_mem_k, cache_mem_v, state_gla, state_mlstm_c, state_mlstm_n, state_mlstm_m, state_s5_re, state_s5_im, state_ffn_conv, norm_mix, w_in, w_gla_alpha, b_gla_alpha, gla_head_norm, b_mlstm_i, b_mlstm_f, mlstm_head_norm, s5_lam_re, s5_lam_im, s5_log_dt, s5_b_re, s5_b_im, s5_c_re, s5_c_im, s5_d, s5_w_glu, w_branch_a, w_branch_b, w_branch_c, w_out, norm_cross, norm_mem, w_cq, w_ck, w_cv, w_co, norm_ffn, w_ffn_up, ffn_conv_w, ffn_conv_b, w_ffn_down, norm_final):
    P = dict(norm_mix=norm_mix, w_in=w_in, w_gla_alpha=w_gla_alpha, b_gla_alpha=b_gla_alpha,
             gla_head_norm=gla_head_norm, b_mlstm_i=b_mlstm_i, b_mlstm_f=b_mlstm_f,
             mlstm_head_norm=mlstm_head_norm, s5_lam_re=s5_lam_re, s5_lam_im=s5_lam_im,
             s5_log_dt=s5_log_dt, s5_b_re=s5_b_re, s5_b_im=s5_b_im, s5_c_re=s5_c_re,
             s5_c_im=s5_c_im, s5_d=s5_d, s5_w_glu=s5_w_glu, w_branch_a=w_branch_a,
             w_branch_b=w_branch_b, w_branch_c=w_branch_c, w_out=w_out, norm_cross=norm_cross,
             w_cq=w_cq, w_co=w_co, norm_ffn=norm_ffn, w_ffn_up=w_ffn_up, ffn_conv_w=ffn_conv_w,
             ffn_conv_b=ffn_conv_b, w_ffn_down=w_ffn_down, norm_final=norm_final)
    depth = w_in.shape[0]
    W = [_layer_weights(P, l) for l in range(depth)]
    batch, mem_len, _ = mem_prompt.shape
    mem2 = mem_prompt.reshape(batch * mem_len, D_MODEL)
    mem_k, mem_v = _memory_kv(mem2, norm_mem.reshape(depth, 1, D_MODEL), w_ck.astype(BF16), w_cv.astype(BF16))
    y_prompt, p_states = _prompt_trunk(x_prompt, mem_k, mem_v, W, bm_in=2048, tt_mix=256, gla_group=4, ml_group=4, tt_s5=256, bm=512,
                                       tt_ffn=512)
    p_mem_k = mem_k.reshape(depth, batch, mem_len, X_HEADS, X_DH)
    p_mem_v = mem_v.reshape(depth, batch, mem_len, X_HEADS, X_DH)
    y_sample, s_states = _sample_trunk(
        x_sample, cache_mem_k, cache_mem_v,
        (state_gla, state_mlstm_c, state_mlstm_n, state_mlstm_m, state_s5_re, state_s5_im, state_ffn_conv), W)
    return (y_prompt, y_sample, *p_states, p_mem_k, p_mem_v, *s_states)
```
